```python
import math
import jax
import jax.numpy as jnp
from jax import lax
import numpy as np

D_MODEL = 1024
BATCH = 8
SEQ = 2048
DEPTH = 1
DEC_BATCH = 128
DEC_SEQ = 4
PAST_LEN = 2048
PAGE_SIZE = 128

HEAD_DIM = 64
N_ATTN_HEADS = D_MODEL // (2 * HEAD_DIM)
N_GDN_HEADS = D_MODEL // (2 * HEAD_DIM)
ATTN_WIDTH = N_ATTN_HEADS * HEAD_DIM
GDN_WIDTH = N_GDN_HEADS * HEAD_DIM
DILATED_PATTERNS = ((128, 1), (512, 4), (2048, 16))
MAX_WINDOW = 2048
T5_BUCKETS = 32
T5_MAX_EXACT = T5_BUCKETS // 2
T5_MAX_DIST = MAX_WINDOW
CONV_WIDTH = 4
GDN_CHUNK = 64
D_FF = 2816
IN_COLS = 3 * ATTN_WIDTH + 3 * GDN_WIDTH + GDN_WIDTH + 2 * N_GDN_HEADS
LN_EPS = 1e-5
RMS_EPS = 1e-6

kernel_name = 'hymba_dilated_gdn_macaron_deepnorm_step'


def _layernorm(x, g, b):
    xf = x.astype(jnp.float32)
    mu = jnp.mean(xf, axis=-1, keepdims=True)
    xc = xf - mu
    var = jnp.mean(xc * xc, axis=-1, keepdims=True)
    return (xc * lax.rsqrt(var + LN_EPS) * g.astype(jnp.float32) + b.astype(jnp.float32)).astype(x.dtype)


def _swiglu(x, w_gate, w_up, w_down):
    return (jax.nn.silu(x @ w_gate) * (x @ w_up)) @ w_down


def _l2norm(x):
    return x * lax.rsqrt(jnp.sum(x * x, axis=-1, keepdims=True) + RMS_EPS)


def _t5_bias(dist, rel_bias):
    n = jnp.maximum(dist, 0)
    nf = jnp.maximum(n, 1).astype(jnp.float32)
    large = T5_MAX_EXACT + (jnp.log(nf / T5_MAX_EXACT) / math.log(T5_MAX_DIST / T5_MAX_EXACT)
                            * (T5_BUCKETS - T5_MAX_EXACT)).astype(jnp.int32)
    large = jnp.minimum(large, T5_BUCKETS - 1)
    bucket = jnp.where(n < T5_MAX_EXACT, n, large)
    return rel_bias[bucket].astype(jnp.float32)


def _split_projection(h):
    B, L, _ = h.shape
    sizes = [ATTN_WIDTH, ATTN_WIDTH, ATTN_WIDTH, 3 * GDN_WIDTH, GDN_WIDTH, N_GDN_HEADS, N_GDN_HEADS]
    offs = [int(o) for o in np.cumsum(sizes)[:-1]]
    aq, ak, av, g_qkv, z, b, a = jnp.split(h, offs, axis=-1)
    heads = lambda t: t.reshape(B, L, N_ATTN_HEADS, HEAD_DIM)
    return heads(aq), heads(ak), heads(av), g_qkv, z, b, a


def _dilated_prompt(q, k, v, rel_bias, window, dilation):
    B, S, H, E = q.shape
    nb = window // dilation
    L = S // dilation
    nblk = -(-L // nb)
    lp = nblk * nb

    def classes(t):
        t = t.astype(jnp.float32).reshape(B, L, dilation, H, E).transpose(0, 2, 1, 3, 4)
        t = jnp.pad(t, ((0, 0), (0, 0), (0, lp - L), (0, 0), (0, 0)))
        return t.reshape(B, dilation, nblk, nb, H, E)

    qc = classes(q) * HEAD_DIM ** -0.5
    kc = classes(k)
    vc = classes(v)
    prev = lambda t: jnp.pad(t, ((0, 0), (0, 0), (1, 0), (0, 0), (0, 0), (0, 0)))[:, :, :-1]
    kk = jnp.concatenate([prev(kc), kc], axis=3)
    vv = jnp.concatenate([prev(vc), vc], axis=3)
    logits = jnp.einsum('brnqhe,brnkhe->brnhqk', qc, kk)
    qi = jnp.arange(nb)[:, None]
    ki = jnp.arange(2 * nb)[None, :]
    dist = qi + nb - ki
    key_cls = jnp.arange(nblk)[:, None, None] * nb - nb + ki[None]
    valid = (dist >= 0) & (dist <= nb) & (key_cls >= 0)
    bias = _t5_bias(dist * dilation, rel_bias).transpose(2, 0, 1)
    logits = jnp.where(valid[:, None], logits + bias, -jnp.inf)
    m = jnp.max(logits, axis=-1, keepdims=True)
    p = jnp.exp(logits - m)
    s = jnp.sum(p, axis=-1)
    o = jnp.einsum('brnhqk,brnkhe->brnqhe', p, vv) / s.transpose(0, 1, 2, 4, 3)[..., None]
    lse = (m[..., 0] + jnp.log(s)).transpose(0, 1, 2, 4, 3)

    def unclass(t):
        t = t.reshape((B, dilation, lp) + t.shape[4:])[:, :, :L]
        t = jnp.moveaxis(t, 1, 2)
        return t.reshape((B, S) + t.shape[3:])

    return unclass(o), unclass(lse)


def _dilated_sample(q, k_all, v_all, rel_bias, window, dilation):
    T = q.shape[1]
    w_past = k_all.shape[1] - T
    offs = jnp.arange(window // dilation + 1) * dilation
    idx = w_past + jnp.arange(T)[:, None] - offs[None, :]
    valid = idx >= 0
    idx = jnp.maximum(idx, 0)
    kg = k_all[:, idx].astype(jnp.float32)
    vg = v_all[:, idx].astype(jnp.float32)
    logits = jnp.einsum('bthe,btkhe->bthk', q.astype(jnp.float32) * HEAD_DIM ** -0.5, kg)
    logits = logits + _t5_bias(offs, rel_bias).T
    logits = jnp.where(valid[None, :, None, :], logits, -jnp.inf)
    m = jnp.max(logits, axis=-1, keepdims=True)
    p = jnp.exp(logits - m)
    s = jnp.sum(p, axis=-1)
    o = jnp.einsum('bthk,btkhe->bthe', p, vg) / s[..., None]
    return o, m[..., 0] + jnp.log(s)


def _merge_dilations(outs, lses):
    w = jax.nn.softmax(jnp.stack(lses), axis=0)
    return jnp.sum(w[..., None] * jnp.stack(outs), axis=0)


def _gdn_chunked(q, k, v, g, beta, s0, chunk):
    B, L, H, DK = q.shape
    DV = v.shape[-1]
    n = -(-L // chunk)
    pad = n * chunk - L
    padt = lambda t: jnp.pad(t, ((0, 0), (0, pad)) + ((0, 0),) * (t.ndim - 2))
    q, k, v, g, beta = [padt(t) for t in (q, k, v, g, beta)]
    ch4 = lambda t: t.reshape(B, n, chunk, H, t.shape[-1]).transpose(1, 0, 3, 2, 4)
    ch3 = lambda t: t.reshape(B, n, chunk, H).transpose(1, 0, 3, 2)
    q, k, v = ch4(q), ch4(k), ch4(v)
    g, beta = ch3(g), ch3(beta)
    gc = jnp.cumsum(g, axis=-1)
    eye = jnp.eye(chunk, dtype=jnp.float32)
    tril = jnp.tril(jnp.ones((chunk, chunk), dtype=bool))
    strict = tril & ~jnp.eye(chunk, dtype=bool)
    diff = gc[..., :, None] - gc[..., None, :]
    decay = jnp.where(tril, jnp.exp(jnp.where(tril, diff, 0.0)), 0.0)
    kb = k * beta[..., None]
    a = jnp.where(strict, jnp.einsum('nbhcd,nbhsd->nbhcs', kb, k) * decay, 0.0)
    t_inv = lax.linalg.triangular_solve(eye + a, jnp.broadcast_to(eye, a.shape),
                                        left_side=True, lower=True, unit_diagonal=True)
    u = jnp.einsum('nbhcs,nbhse->nbhce', t_inv, v * beta[..., None])
    w = jnp.einsum('nbhcs,nbhsd->nbhcd', t_inv, kb * jnp.exp(gc)[..., None])
    intra = jnp.where(tril, jnp.einsum('nbhcd,nbhsd->nbhcs', q, k) * decay, 0.0)
    qd = q * jnp.exp(gc)[..., None]
    kd = k * jnp.exp(gc[..., -1:] - gc)[..., None]
    glast = jnp.exp(gc[..., -1])

    def step(s, xs):
        u_i, w_i, intra_i, qd_i, kd_i, gl_i = xs
        v_new = u_i - jnp.einsum('bhcd,bhde->bhce', w_i, s)
        o_i = jnp.einsum('bhcd,bhde->bhce', qd_i, s) + jnp.einsum('bhcs,bhse->bhce', intra_i, v_new)
        s = s * gl_i[..., None, None] + jnp.einsum('bhcd,bhce->bhde', kd_i, v_new)
        return s, o_i

    s_final, o = lax.scan(step, s0, (u, w, intra, qd, kd, glast))
    o = o.transpose(1, 0, 3, 2, 4).reshape(B, n * chunk, H, DV)[:, :L]
    return o, s_final


def _gdn_branch(qkv_raw, z, b, a, conv_buf, s0, conv_w, a_log, dt_bias, norm_w):
    B, L, _ = qkv_raw.shape
    xp = jnp.concatenate([conv_buf.astype(qkv_raw.dtype), qkv_raw], axis=1)
    conv = sum(xp[:, j:j + L] * conv_w[j] for j in range(CONV_WIDTH))
    new_buf = xp[:, L:]
    qkv = jax.nn.silu(conv.astype(jnp.float32))
    q, k, v = [t.reshape(B, L, N_GDN_HEADS, HEAD_DIM) for t in jnp.split(qkv, 3, axis=-1)]
    q = _l2norm(q) * HEAD_DIM ** -0.5
    k = _l2norm(k)
    beta = jax.nn.sigmoid(b.astype(jnp.float32))
    g = -jnp.exp(a_log.astype(jnp.float32)) * jax.nn.softplus(a.astype(jnp.float32) + dt_bias.astype(jnp.float32))
    o, s_new = _gdn_chunked(q, k, v, g, beta, s0.astype(jnp.float32), min(GDN_CHUNK, L))
    o = o * lax.rsqrt(jnp.mean(o * o, axis=-1, keepdims=True) + RMS_EPS) * norm_w.astype(jnp.float32)
    o = o * jax.nn.silu(z.astype(jnp.float32).reshape(B, L, N_GDN_HEADS, HEAD_DIM))
    return o.reshape(B, L, GDN_WIDTH), s_new, new_buf


def _mixer_prompt(xn, w_in, rel_bias, conv_w, a_log, dt_bias, norm_w):
    B, L, _ = xn.shape
    aq, ak, av, g_qkv, z, b, a = _split_projection(xn @ w_in)
    res = [_dilated_prompt(aq, ak, av, rel_bias, wd, dl) for (wd, dl) in DILATED_PATTERNS]
    attn = _merge_dilations([r[0] for r in res], [r[1] for r in res])
    conv0 = jnp.zeros((B, CONV_WIDTH - 1, 3 * GDN_WIDTH), xn.dtype)
    s0 = jnp.zeros((B, N_GDN_HEADS, HEAD_DIM, HEAD_DIM), jnp.float32)
    gdn, s_new, conv_new = _gdn_branch(g_qkv, z, b, a, conv0, s0, conv_w, a_log, dt_bias, norm_w)
    heads = jnp.concatenate([attn.reshape(B, L, ATTN_WIDTH).astype(xn.dtype), gdn.astype(xn.dtype)], axis=-1)
    wp = min(MAX_WINDOW, L)
    return heads, (ak[:, L - wp:], av[:, L - wp:], s_new, conv_new)


def _mixer_sample(xn, k_past, v_past, s_past, conv_past, w_in, rel_bias, conv_w, a_log, dt_bias, norm_w):
    B, L, _ = xn.shape
    aq, ak, av, g_qkv, z, b, a = _split_projection(xn @ w_in)
    k_all = jnp.concatenate([k_past.astype(ak.dtype), ak], axis=1)
    v_all = jnp.concatenate([v_past.astype(av.dtype), av], axis=1)
    res = [_dilated_sample(aq, k_all, v_all, rel_bias, wd, dl) for (wd, dl) in DILATED_PATTERNS]
    attn = _merge_dilations([r[0] for r in res], [r[1] for r in res])
    gdn, s_new, conv_new = _gdn_branch(g_qkv, z, b, a, conv_past, s_past, conv_w, a_log, dt_bias, norm_w)
    heads = jnp.concatenate([attn.reshape(B, L, ATTN_WIDTH).astype(xn.dtype), gdn.astype(xn.dtype)], axis=-1)
    return heads, (ak, av, s_new, conv_new)


def _macaron_layer(x, mixer, alpha, ln1_g, ln1_b, ln2_g, ln2_b, ln3_g, ln3_b,
                   f1_gate, f1_up, f1_down, f2_gate, f2_up, f2_down, w_out):
    x = _layernorm(alpha * x + 0.5 * _swiglu(x, f1_gate, f1_up, f1_down), ln1_g, ln1_b)
    heads, states = mixer(x)
    x = _layernorm(alpha * x + heads @ w_out, ln2_g, ln2_b)
    x = _layernorm(alpha * x + 0.5 * _swiglu(x, f2_gate, f2_up, f2_down), ln3_g, ln3_b)
    return x, states


def setup_inputs(seed: int = 0) -> dict:
    key = jax.random.key(seed)
    ks = jax.random.split(key, 32)
    f32 = jnp.float32
    w_buf = min(MAX_WINDOW, PAST_LEN)
    out_scale = (8.0 * DEPTH) ** -0.25

    def dense(k, shape, fan_in, scale=1.0):
        return jax.random.normal(k, shape, f32) * (scale * fan_in ** -0.5)

    def gain(k, shape):
        return 1.0 + 0.05 * jax.random.normal(k, shape, f32)

    def small(k, shape):
        return 0.02 * jax.random.normal(k, shape, f32)

    dt = jnp.exp(jax.random.uniform(ks[24], (DEPTH, N_GDN_HEADS), f32, math.log(1e-3), math.log(1e-1)))
    return {
        'x_prompt': jax.random.normal(ks[0], (BATCH, SEQ, D_MODEL), f32),
        'x_sample': jax.random.normal(ks[1], (DEC_BATCH, DEC_SEQ, D_MODEL), f32),
        'cache_attn_k': jax.random.normal(ks[2], (DEPTH, DEC_BATCH, w_buf, N_ATTN_HEADS, HEAD_DIM), f32),
        'cache_attn_v': jax.random.normal(ks[3], (DEPTH, DEC_BATCH, w_buf, N_ATTN_HEADS, HEAD_DIM), f32),
        'state_gdn': 0.1 * jax.random.normal(ks[4], (DEPTH, DEC_BATCH, N_GDN_HEADS, HEAD_DIM, HEAD_DIM), f32),
        'state_conv': jax.random.normal(ks[5], (DEPTH, DEC_BATCH, CONV_WIDTH - 1, 3 * GDN_WIDTH), f32),
        'rel_bias': 0.5 * jax.random.normal(ks[6], (T5_BUCKETS, N_ATTN_HEADS), f32),
        'ln1_g': gain(ks[7], (DEPTH, D_MODEL)),
        'ln1_b': small(ks[8], (DEPTH, D_MODEL)),
        'ffn1_w_gate': dense(ks[9], (DEPTH, D_MODEL, D_FF), D_MODEL),
        'ffn1_w_up': dense(ks[10], (DEPTH, D_MODEL, D_FF), D_MODEL),
        'ffn1_w_down': dense(ks[11], (DEPTH, D_FF, D_MODEL), D_FF, out_scale),
        'w_in': dense(ks[12], (DEPTH, D_MODEL, IN_COLS), D_MODEL),
        'w_out': dense(ks[13], (DEPTH, D_MODEL, D_MODEL), D_MODEL, out_scale),
        'gdn_conv_w': dense(ks[14], (DEPTH, CONV_WIDTH, 3 * GDN_WIDTH), CONV_WIDTH),
        'gdn_a_log': jnp.log(jax.random.uniform(ks[15], (DEPTH, N_GDN_HEADS), f32, 1.0, 16.0)),
        'gdn_dt_bias': dt + jnp.log(-jnp.expm1(-dt)),
        'gdn_norm_w': gain(ks[16], (DEPTH, HEAD_DIM)),
        'ln2_g': gain(ks[17], (DEPTH, D_MODEL)),
        'ln2_b': small(ks[18], (DEPTH, D_MODEL)),
        'ffn2_w_gate': dense(ks[19], (DEPTH, D_MODEL, D_FF), D_MODEL),
        'ffn2_w_up': dense(ks[20], (DEPTH, D_MODEL, D_FF), D_MODEL),
        'ffn2_w_down': dense(ks[21], (DEPTH, D_FF, D_MODEL), D_FF, out_scale),
        'ln3_g': gain(ks[22], (DEPTH, D_MODEL)),
        'ln3_b': small(ks[23], (DEPTH, D_MODEL)),
    }


def reference(x_prompt, x_sample, cache_attn_k, cache_attn_v, state_gdn, state_conv, rel_bias,
              ln1_g, ln1_b, ffn1_w_gate, ffn1_w_up, ffn1_w_down, w_in, w_out,
              gdn_conv_w, gdn_a_log, gdn_dt_bias, gdn_norm_w, ln2_g, ln2_b,
              ffn2_w_gate, ffn2_w_up, ffn2_w_down, ln3_g, ln3_b):
    alpha = (2.0 * DEPTH) ** 0.25
    yp, ys = x_prompt, x_sample
    collected = [[] for _ in range(8)]
    for layer in range(DEPTH):
        common = (ln1_g[layer], ln1_b[layer], ln2_g[layer], ln2_b[layer], ln3_g[layer], ln3_b[layer],
                  ffn1_w_gate[layer], ffn1_w_up[layer], ffn1_w_down[layer],
                  ffn2_w_gate[layer], ffn2_w_up[layer], ffn2_w_down[layer], w_out[layer])
        mix_w = (w_in[layer], rel_bias, gdn_conv_w[layer], gdn_a_log[layer], gdn_dt_bias[layer], gdn_norm_w[layer])
        yp, st_p = _macaron_layer(yp, lambda xn: _mixer_prompt(xn, *mix_w), alpha, *common)
        ys, st_s = _macaron_layer(
            ys, lambda xn: _mixer_sample(xn, cache_attn_k[layer], cache_attn_v[layer],
                                         state_gdn[layer], state_conv[layer], *mix_w), alpha, *common)
        for lst, st in zip(collected, st_p + st_s):
            lst.append(st)
    k_p, v_p, sg_p, sc_p, k_s, v_s, sg_s, sc_s = [jnp.stack(t, axis=0) for t in collected]
    return (yp, ys, k_p, v_p, sg_p, sc_p, k_s, v_s, sg_s, sc_s)
```

```python
import functools
import math

import numpy as np
import jax
import jax.numpy as jnp
from jax import lax
from jax.experimental import pallas as pl
from jax.experimental.pallas import tpu as pltpu

F32 = jnp.float32
BF16 = jnp.bfloat16

HEAD_DIM = 64
N_HEADS = 8
HEAD_WIDTH = N_HEADS * HEAD_DIM
DILATED_PATTERNS = ((128, 1), (512, 4), (2048, 16))
KEYS_PER_BLOCK = 128
T5_BUCKETS = 32
T5_MAX_EXACT = 16
T5_MAX_DIST = 2048
CONV_WIDTH = 4
GDN_CHUNK = 64
LN_EPS = 1e-5
RMS_EPS = 1e-6
NEG_BIG = -1e30

LANES = 128
SUBLANES = 8
TOKEN_TILE = 256
GDN_TOKEN_BLOCK = 256
SAMPLE_T_PAD = 8
VMEM_LIMIT = 56 * 1024 * 1024


def _resident(shape):
    nd = len(shape)
    return pl.BlockSpec(shape, lambda *_: (0,) * nd, pipeline_mode=pl.Buffered(1))


def _layernorm(y, g, b):
    mu = jnp.mean(y, axis=-1, keepdims=True)
    yc = y - mu
    var = jnp.mean(yc * yc, axis=-1, keepdims=True)
    return yc * lax.rsqrt(var + LN_EPS) * g + b


def _swiglu(xb, wg_ref, wu_ref, wd_ref, h_scr):
    d_ff = wg_ref.shape[1]
    step = 2 * LANES
    assert d_ff % step == 0
    for c in range(d_ff // step):
        sl = slice(c * step, (c + 1) * step)
        gate = jnp.dot(xb, wg_ref[:, sl], preferred_element_type=F32)
        up = jnp.dot(xb, wu_ref[:, sl], preferred_element_type=F32)
        h_scr[:, sl] = (jax.nn.silu(gate) * up).astype(BF16)
    return jnp.dot(h_scr[...], wd_ref[...], preferred_element_type=F32)


def _pre_kernel(x_ref, wg_ref, wu_ref, wd_ref, g_ref, b_ref, win_ref,
                x1_ref, aq_ref, ak_ref, av_ref, gq_ref, z_ref, ba_ref, h_scr, *, alpha):
    x = x_ref[...]
    ff = _swiglu(x.astype(BF16), wg_ref, wu_ref, wd_ref, h_scr)
    x1 = _layernorm(alpha * x + 0.5 * ff, g_ref[...], b_ref[...])
    x1_ref[...] = x1
    xb = x1.astype(BF16)
    col = 0
    for ref in (aq_ref, ak_ref, av_ref, gq_ref, z_ref, ba_ref):
        width = ref.shape[1]
        ref[...] = jnp.dot(xb, win_ref[:, col:col + width], preferred_element_type=F32)
        col += width


def _pre_call(x2d, wg, wu, wd, g, b, win, alpha):
    m, d = x2d.shape
    d_ff = wg.shape[1]
    tm = TOKEN_TILE
    assert m % tm == 0
    widths = (HEAD_WIDTH, HEAD_WIDTH, HEAD_WIDTH, 3 * HEAD_WIDTH, HEAD_WIDTH, LANES)
    assert sum(widths) == win.shape[1]
    tile = lambda w: pl.BlockSpec((tm, w), lambda i: (i, 0))
    return pl.pallas_call(
        functools.partial(_pre_kernel, alpha=alpha),
        grid=(m // tm,),
        in_specs=[tile(d), _resident(wg.shape), _resident(wu.shape), _resident(wd.shape),
                  _resident(g.shape), _resident(b.shape), _resident(win.shape)],
        out_specs=[tile(d)] + [tile(w) for w in widths],
        out_shape=[jax.ShapeDtypeStruct((m, d), F32)] + [jax.ShapeDtypeStruct((m, w), F32) for w in widths],
        scratch_shapes=[pltpu.VMEM((tm, d_ff), BF16)],
        compiler_params=pltpu.CompilerParams(dimension_semantics=("parallel",), vmem_limit_bytes=VMEM_LIMIT),
        name="pre_ffn_proj",
    )(x2d, wg, wu, wd, g, b, win)


def _post_kernel(attn_ref, gdn_ref, x1_ref, wo_ref, g2_ref, b2_ref, wg_ref, wu_ref, wd_ref, g3_ref, b3_ref,
                 y_ref, h_scr, *, alpha):
    mix = jnp.dot(attn_ref[...].astype(BF16), wo_ref[0:HEAD_WIDTH, :], preferred_element_type=F32)
    mix = mix + jnp.dot(gdn_ref[...].astype(BF16), wo_ref[HEAD_WIDTH:2 * HEAD_WIDTH, :],
                        preferred_element_type=F32)
    x2 = _layernorm(alpha * x1_ref[...] + mix, g2_ref[...], b2_ref[...])
    ff = _swiglu(x2.astype(BF16), wg_ref, wu_ref, wd_ref, h_scr)
    y_ref[...] = _layernorm(alpha * x2 + 0.5 * ff, g3_ref[...], b3_ref[...])


def _post_call(attn, gdn, x1, wo, g2, b2, wg, wu, wd, g3, b3, alpha):
    m, d = x1.shape
    tm = TOKEN_TILE
    assert m % tm == 0
    tile = lambda w: pl.BlockSpec((tm, w), lambda i: (i, 0))
    consts = (wo, g2, b2, wg, wu, wd, g3, b3)
    return pl.pallas_call(
        functools.partial(_post_kernel, alpha=alpha),
        grid=(m // tm,),
        in_specs=[tile(HEAD_WIDTH), tile(HEAD_WIDTH), tile(d)] + [_resident(c.shape) for c in consts],
        out_specs=tile(d),
        out_shape=jax.ShapeDtypeStruct((m, d), F32),
        scratch_shapes=[pltpu.VMEM((tm, wg.shape[1]), BF16)],
        compiler_params=pltpu.CompilerParams(dimension_semantics=("parallel",), vmem_limit_bytes=VMEM_LIMIT),
        name="post_out_ffn",
    )(attn, gdn, x1, *consts)


def _t5_bucket_np(dist):
    n = np.maximum(dist, 0)
    nf = np.maximum(n, 1).astype(np.float32)
    large = T5_MAX_EXACT + (np.log(nf / np.float32(T5_MAX_EXACT)) / np.float32(math.log(T5_MAX_DIST / T5_MAX_EXACT))
                            * np.float32(T5_BUCKETS - T5_MAX_EXACT)).astype(np.int32)
    large = np.minimum(large, T5_BUCKETS - 1)
    return np.where(n < T5_MAX_EXACT, n, large).astype(np.int32)


def _t5_table_kernel(rb_ref, bk_ref, out_ref):
    bk = bk_ref[0]
    for h in range(N_HEADS):
        acc = jnp.full(bk.shape, NEG_BIG, F32)
        for b in range(T5_BUCKETS):
            acc = jnp.where(bk == b, rb_ref[b, h], acc)
        out_ref[0, h] = acc


def _t5_table_call(rel_bias, buckets):
    p, r, c = buckets.shape
    return pl.pallas_call(
        _t5_table_kernel,
        grid=(p,),
        in_specs=[pl.BlockSpec(memory_space=pltpu.SMEM),
                  pl.BlockSpec((1, r, c), lambda i: (i, 0, 0))],
        out_specs=pl.BlockSpec((1, N_HEADS, r, c), lambda i: (i, 0, 0, 0)),
        out_shape=jax.ShapeDtypeStruct((p, N_HEADS, r, c), F32),
        compiler_params=pltpu.CompilerParams(dimension_semantics=("parallel",)),
        name="t5_bias_table",
    )(rel_bias, buckets)


def _prompt_buckets():
    nb = KEYS_PER_BLOCK
    qi = np.arange(nb)[:, None]
    ki = np.arange(2 * nb)[None, :]
    dist = qi + nb - ki
    valid = (dist >= 0) & (dist <= nb)
    out = []
    for window, dil in DILATED_PATTERNS:
        assert window // dil == nb
        out.append(np.where(valid, _t5_bucket_np(dist * dil), -1))
    return np.stack(out).astype(np.int32)


def _nt_dot(a, b):
    return lax.dot_general(a, b, (((1,), (1,)), ((), ())), preferred_element_type=F32)


def _attn_prompt_kernel(q_ref, k_ref, v_ref, tab_ref, o_ref, og_scr, *, seq):
    nb = KEYS_PER_BLOCK
    hd = HEAD_DIM
    n_pat = len(DILATED_PATTERNS)
    heads = LANES // hd
    for g, (window, dil) in enumerate(DILATED_PATTERNS):
        n_blocks = seq // (dil * nb)

        def block(start, with_prev, g=g, dil=dil):
            rows = pl.ds(start, nb, stride=dil) if dil > 1 else pl.ds(start, nb)
            q2 = q_ref[0, rows, :] * hd ** -0.5
            k2 = k_ref[0, rows, :]
            v2 = v_ref[0, rows, :]
            if with_prev:
                rows_p = (pl.ds(start - dil * nb, nb, stride=dil) if dil > 1 else pl.ds(start - nb, nb))
                kp2 = k_ref[0, rows_p, :]
                vp2 = v_ref[0, rows_p, :]
            for hh in range(heads):
                ln = slice(hd * hh, hd * (hh + 1))
                qb = q2[:, ln].astype(BF16)
                s_c = _nt_dot(qb, k2[:, ln].astype(BF16)) + tab_ref[g, hh, :, nb:2 * nb]
                m = jnp.max(s_c, axis=-1, keepdims=True)
                if with_prev:
                    s_p = _nt_dot(qb, kp2[:, ln].astype(BF16)) + tab_ref[g, hh, :, 0:nb]
                    m = jnp.maximum(m, jnp.max(s_p, axis=-1, keepdims=True))
                p_c = jnp.exp(s_c - m)
                den = jnp.sum(p_c, axis=-1, keepdims=True)
                acc = jnp.dot(p_c.astype(BF16), v2[:, ln].astype(BF16), preferred_element_type=F32)
                if with_prev:
                    p_p = jnp.exp(s_p - m)
                    den = den + jnp.sum(p_p, axis=-1, keepdims=True)
                    acc = acc + jnp.dot(p_p.astype(BF16), vp2[:, ln].astype(BF16), preferred_element_type=F32)
                lse = m + jnp.log(den)
                og_scr[g * heads + hh, rows, :] = jnp.concatenate(
                    [acc / den, jnp.broadcast_to(lse, (nb, hd))], axis=-1)

        def first_body(r, carry, block=block):
            block(r, False)
            return carry

        lax.fori_loop(0, dil, first_body, 0)
        if n_blocks > 1:
            def rest_body(i, carry, block=block, dil=dil, n_blocks=n_blocks):
                r = i // (n_blocks - 1)
                n = i % (n_blocks - 1) + 1
                block(r + dil * nb * n, True)
                return carry

            lax.fori_loop(0, dil * (n_blocks - 1), rest_body, 0)

    rb = 2 * nb

    def merge_body(i, carry):
        rows = pl.ds(pl.multiple_of(i * rb, rb), rb)
        res = []
        for hh in range(heads):
            tiles = [og_scr[g * heads + hh, rows, :] for g in range(n_pat)]
            lses = [t[:, hd:2 * hd] for t in tiles]
            top = functools.reduce(jnp.maximum, lses)
            ws = [jnp.exp(l - top) for l in lses]
            num = functools.reduce(lambda a, b: a + b, [w * t[:, 0:hd] for w, t in zip(ws, tiles)])
            res.append(num / functools.reduce(lambda a, b: a + b, ws))
        o_ref[0, rows, :] = jnp.concatenate(res, axis=-1)
        return carry

    lax.fori_loop(0, seq // rb, merge_body, 0)


def _attn_prompt_call(aq, ak, av, tab):
    bsz, seq, width = aq.shape
    assert width == HEAD_WIDTH and seq % (DILATED_PATTERNS[-1][1] * KEYS_PER_BLOCK) == 0
    n_pat = len(DILATED_PATTERNS)
    heads_per_step = LANES // HEAD_DIM
    qkv_spec = pl.BlockSpec((1, seq, LANES), lambda h, b: (b, 0, h))
    return pl.pallas_call(
        functools.partial(_attn_prompt_kernel, seq=seq),
        grid=(N_HEADS // heads_per_step, bsz),
        in_specs=[qkv_spec, qkv_spec, qkv_spec,
                  pl.BlockSpec((n_pat, heads_per_step, KEYS_PER_BLOCK, 2 * KEYS_PER_BLOCK),
                               lambda h, b: (0, h, 0, 0))],
        out_specs=pl.BlockSpec((1, seq, LANES), lambda h, b: (b, 0, h)),
        out_shape=jax.ShapeDtypeStruct((bsz, seq, width), F32),
        scratch_shapes=[pltpu.VMEM((n_pat * heads_per_step, seq, LANES), F32)],
        compiler_params=pltpu.CompilerParams(dimension_semantics=("parallel", "parallel"),
                                             vmem_limit_bytes=VMEM_LIMIT),
        name="attn_prompt",
    )(aq, ak, av, tab)


SAMPLE_FAR_ROWS = SUBLANES


def _sample_layout(w_buf, t_new):
    period = DILATED_PATTERNS[-1][1]
    near_rows = DILATED_PATTERNS[-2][0]
    assert w_buf % period == 0 and near_rows % period == 0 and t_new <= SAMPLE_FAR_ROWS <= period
    n_far_groups = (w_buf - near_rows) // period
    far_cols = n_far_groups * SAMPLE_FAR_ROWS
    cols = far_cols + near_rows + LANES
    row_of_col = np.full((cols,), -1, np.int64)
    c = np.arange(far_cols)
    row_of_col[:far_cols] = (c // SAMPLE_FAR_ROWS) * period + c % SAMPLE_FAR_ROWS
    row_of_col[far_cols:far_cols + near_rows] = w_buf - near_rows + np.arange(near_rows)
    row_of_col[far_cols + near_rows:far_cols + near_rows + t_new] = w_buf + np.arange(t_new)
    buckets = np.full((len(DILATED_PATTERNS), SAMPLE_T_PAD, cols), -1, np.int32)
    for g, (window, dil) in enumerate(DILATED_PATTERNS):
        for t in range(t_new):
            dist = w_buf + t - row_of_col
            ok = (row_of_col >= 0) & (dist >= 0) & (dist <= window) & (dist % dil == 0)
            assert int(ok.sum()) == min(window, w_buf + t) // dil + 1, (g, t, int(ok.sum()))
            buckets[g, t] = np.where(ok, _t5_bucket_np(dist), -1)
    return n_far_groups, far_cols, near_rows, cols, buckets


def _attn_sample_kernel(q_ref, kn_ref, vn_ref, kfar_ref, knear_ref, vfar_ref, vnear_ref, tab_ref, hm_ref,
                        o_ref, *, far_cols, near_rows):
    tp = SAMPLE_T_PAD
    q8 = q_ref[0] * HEAD_DIM ** -0.5
    hm = hm_ref[...]
    a = jnp.concatenate([q8 * hm[h:h + 1, :] for h in range(N_HEADS)], axis=0)
    width = q8.shape[1]
    pad = jnp.zeros((LANES - tp, width), F32)
    k_parts = (kfar_ref[0].reshape(far_cols, width), knear_ref[0].reshape(near_rows, width),
               jnp.concatenate([kn_ref[0], pad], axis=0))
    v_parts = (vfar_ref[0].reshape(far_cols, width), vnear_ref[0].reshape(near_rows, width),
               jnp.concatenate([vn_ref[0], pad], axis=0))
    edges = (0, far_cols, far_cols + near_rows, far_cols + near_rows + LANES)
    scores = [_nt_dot(a, kp) for kp in k_parts]
    outs, lses = [], []
    for g, (window, dil) in enumerate(DILATED_PATTERNS):
        parts = [i for i in range(3) if not (i == 0 and window <= near_rows)]
        logit = [scores[i] + tab_ref[g, :, edges[i]:edges[i + 1]] for i in parts]
        m = functools.reduce(jnp.maximum, [jnp.max(l, axis=-1, keepdims=True) for l in logit])
        ps = [jnp.exp(l - m) for l in logit]
        den = functools.reduce(lambda x, y: x + y, [jnp.sum(p, axis=-1, keepdims=True) for p in ps])
        acc = functools.reduce(lambda x, y: x + y,
                               [jnp.dot(p, v_parts[i], preferred_element_type=F32) for p, i in zip(ps, parts)])
        outs.append(acc / den)
        lses.append(m + jnp.log(den))
    top = functools.reduce(jnp.maximum, lses)
    ws = [jnp.exp(l - top) for l in lses]
    merged = functools.reduce(lambda x, y: x + y, [w * o for w, o in zip(ws, outs)])
    merged = merged / functools.reduce(lambda x, y: x + y, ws)
    res = merged[0:tp] * hm[0:1, :]
    for h in range(1, N_HEADS):
        res = res + merged[h * tp:(h + 1) * tp] * hm[h:h + 1, :]
    o_ref[0] = res


def _attn_sample_call(aq8, ak8, av8, cache_k, cache_v, tab, layout):
    n_far_groups, far_cols, near_rows, cols, _ = layout
    bsz, w_buf, n_heads, hd = cache_k.shape
    width = n_heads * hd
    period = DILATED_PATTERNS[-1][1]
    ck = cache_k.reshape(bsz, w_buf // period, period, width)
    cv = cache_v.reshape(bsz, w_buf // period, period, width)
    n_near_groups = near_rows // period
    assert n_far_groups % n_near_groups == 0
    head_mask = np.repeat(np.eye(n_heads, dtype=np.float32), hd, axis=1)
    new_spec = pl.BlockSpec((1, SAMPLE_T_PAD, width), lambda b: (b, 0, 0))
    far_spec = pl.BlockSpec((1, n_far_groups, SAMPLE_FAR_ROWS, width), lambda b: (b, 0, 0, 0))
    near_spec = pl.BlockSpec((1, n_near_groups, period, width),
                             lambda b: (b, n_far_groups // n_near_groups, 0, 0))
    return pl.pallas_call(
        functools.partial(_attn_sample_kernel, far_cols=far_cols, near_rows=near_rows),
        grid=(bsz,),
        in_specs=[new_spec, new_spec, new_spec, far_spec, near_spec, far_spec, near_spec,
                  _resident(tab.shape), _resident(head_mask.shape)],
        out_specs=new_spec,
        out_shape=jax.ShapeDtypeStruct((bsz, SAMPLE_T_PAD, width), F32),
        compiler_params=pltpu.CompilerParams(dimension_semantics=("parallel",), vmem_limit_bytes=VMEM_LIMIT),
        name="attn_sample",
    )(aq8, ak8, av8, ck, ck, cv, cv, tab, jnp.asarray(head_mask))


def _split3(x):
    h1 = x.astype(BF16)
    r1 = x - h1.astype(F32)
    h2 = r1.astype(BF16)
    h3 = (r1 - h2.astype(F32)).astype(BF16)
    return h1, h2, h3


def _head_sums(x, ones_ref):
    h1, h2, _ = _split3(x)
    return (jnp.dot(h1, ones_ref[...], preferred_element_type=F32)
            + jnp.dot(h2, ones_ref[...], preferred_element_type=F32))


def _gdn_kernel(x_ref, z_ref, ba_ref, cb_ref, s0_ref, cw_ref, hp_ref, nw_ref, ones_ref,
                o_ref, sfin_ref,
                ext_scr, q_scr, k_scr, v_scr, oc_scr, bt_scr, g_scr, s_scr, *, chunk, n_valid):
    tb = x_ref.shape[1]
    hw = HEAD_WIDTH
    hd = HEAD_DIM
    t = pl.program_id(1)
    hist = SUBLANES

    @pl.when(t == 0)
    def _():
        ext_scr[0:hist, :] = cb_ref[0]
        s_scr[...] = s0_ref[0]

    ext_scr[hist:hist + tb, :] = x_ref[0]
    conv = jnp.zeros((tb, 3 * hw), F32)
    for j in range(CONV_WIDTH):
        off = hist - (CONV_WIDTH - 1) + j
        conv = conv + ext_scr[off:off + tb, :] * cw_ref[j:j + 1, :]
    ext_scr[0:hist, :] = ext_scr[tb:tb + hist, :]
    act = jax.nn.silu(conv)
    q = act[:, 0:hw]
    k = act[:, hw:2 * hw]
    q_scr[...] = q * lax.rsqrt(_head_sums(q * q, ones_ref) + RMS_EPS) * hd ** -0.5
    k_scr[...] = k * lax.rsqrt(_head_sums(k * k, ones_ref) + RMS_EPS)
    v_scr[...] = act[:, 2 * hw:3 * hw]

    ba = ba_ref[0]
    row = t * tb + lax.broadcasted_iota(jnp.int32, ba.shape, 0)
    live = row < n_valid
    bt_scr[...] = jnp.where(live, jax.nn.sigmoid(ba), 0.0)
    g_scr[...] = jnp.where(live, -jnp.exp(hp_ref[0:1, :]) * jax.nn.softplus(ba + hp_ref[1:2, :]), 0.0)

    ri = lax.broadcasted_iota(jnp.int32, (chunk, chunk), 0)
    ci = lax.broadcasted_iota(jnp.int32, (chunk, chunk), 1)
    tril = ri >= ci
    strict = ri > ci
    eye_c = (ri == ci).astype(F32)
    tril_b = tril.astype(BF16)
    eye_l = (lax.broadcasted_iota(jnp.int32, (LANES, LANES), 0)
             == lax.broadcasted_iota(jnp.int32, (LANES, LANES), 1)).astype(BF16)
    n_doublings = int(math.log2(chunk)) - 1
    assert 2 ** (n_doublings + 1) == chunk

    def chunk_body(ci_, carry):
        rows = pl.ds(pl.multiple_of(ci_ * chunk, chunk), chunk)
        g3 = _split3(g_scr[rows, :])
        gc = functools.reduce(lambda a, b: a + b,
                              [jnp.dot(tril_b, p, preferred_element_type=F32) for p in g3])
        gct = functools.reduce(lambda a, b: a + b, [_nt_dot(eye_l, p) for p in _split3(gc)])
        e_gc = jnp.exp(gc)
        g_last = gc[chunk - 1:chunk, :]
        e_rem = jnp.exp(g_last - gc)
        e_last = jnp.exp(g_last)
        bt = bt_scr[rows, :]
        for h in range(N_HEADS):
            hl = pl.ds(hd * h, hd)
            gl = N_HEADS + h
            qh = q_scr[rows, hl]
            kh = k_scr[rows, hl]
            vh = v_scr[rows, hl]
            beta = bt[:, h:h + 1]
            diff = gc[:, gl:gl + 1] - gct[gl:gl + 1, :]
            decay = jnp.where(tril, jnp.exp(jnp.where(tril, diff, 0.0)), 0.0)
            kb = kh * beta
            a = jnp.where(strict, _nt_dot(kb, kh) * decay, 0.0)
            t_inv = eye_c - a
            pw = a
            for _ in range(n_doublings):
                pw = jnp.dot(pw, pw, preferred_element_type=F32)
                t_inv = jnp.dot(t_inv, eye_c + pw, preferred_element_type=F32)
            u = jnp.dot(t_inv, vh * beta, preferred_element_type=F32)
            w = jnp.dot(t_inv, kb * e_gc[:, gl:gl + 1], preferred_element_type=F32)
            intra = jnp.where(tril, _nt_dot(qh, kh) * decay, 0.0)
            s = s_scr[h]
            v_new = u - jnp.dot(w, s, preferred_element_type=F32)
            o_h = (jnp.dot(qh * e_gc[:, gl:gl + 1], s, preferred_element_type=F32)
                   + jnp.dot(intra, v_new, preferred_element_type=F32))
            kd = kh * e_rem[:, gl:gl + 1]
            s_scr[h] = s * e_last[:, gl:gl + 1] + lax.dot_general(
                kd, v_new, (((0,), (0,)), ((), ())), preferred_element_type=F32)
            oc_scr[rows, hl] = o_h
        return carry

    lax.fori_loop(0, tb // chunk, chunk_body, 0)

    o = oc_scr[...]
    o = o * lax.rsqrt(_head_sums(o * o, ones_ref) * (1.0 / hd) + RMS_EPS) * nw_ref[...]
    o_ref[0] = o * jax.nn.silu(z_ref[0])

    @pl.when(t == pl.num_programs(1) - 1)
    def _():
        sfin_ref[0] = s_scr[...]


def _gdn_call(gq, z, ba, conv_buf8, s0, conv_w8, head_params, norm_w_row, n_valid, token_block, chunk):
    bsz, seq, w3 = gq.shape
    hw = HEAD_WIDTH
    assert w3 == 3 * hw and seq % token_block == 0 and token_block % chunk == 0
    tb = token_block
    ones_bd = np.kron(np.eye(N_HEADS, dtype=np.float32), np.ones((HEAD_DIM, HEAD_DIM), np.float32))
    blk = lambda w: pl.BlockSpec((1, tb, w), lambda b, t: (b, t, 0))
    per_b3 = lambda s: pl.BlockSpec((1,) + s, lambda b, t: (b,) + (0,) * len(s))
    return pl.pallas_call(
        functools.partial(_gdn_kernel, chunk=chunk, n_valid=n_valid),
        grid=(bsz, seq // tb),
        in_specs=[blk(w3), blk(hw), blk(LANES), per_b3((SUBLANES, w3)), per_b3((N_HEADS, HEAD_DIM, HEAD_DIM)),
                  _resident(conv_w8.shape), _resident(head_params.shape), _resident(norm_w_row.shape),
                  _resident(ones_bd.shape)],
        out_specs=[blk(hw), per_b3((N_HEADS, HEAD_DIM, HEAD_DIM))],
        out_shape=[jax.ShapeDtypeStruct((bsz, seq, hw), F32),
                   jax.ShapeDtypeStruct((bsz, N_HEADS, HEAD_DIM, HEAD_DIM), F32)],
        scratch_shapes=[pltpu.VMEM((tb + 2 * SUBLANES, w3), F32)]
                       + [pltpu.VMEM((tb, hw), F32)] * 4
                       + [pltpu.VMEM((tb, LANES), F32)] * 2
                       + [pltpu.VMEM((N_HEADS, HEAD_DIM, HEAD_DIM), F32)],
        compiler_params=pltpu.CompilerParams(dimension_semantics=("parallel", "arbitrary"),
                                             vmem_limit_bytes=VMEM_LIMIT),
        name="gdn",
    )(gq, z, ba, conv_buf8, s0, conv_w8, head_params, norm_w_row, jnp.asarray(ones_bd, BF16))


def _pad_rows(x, rows, front=False):
    extra = rows - x.shape[1]
    cfg = [(0, 0)] * x.ndim
    cfg[1] = (extra, 0) if front else (0, extra)
    return jnp.pad(x, cfg)


def kernel(x_prompt, x_sample, cache_attn_k, cache_attn_v, state_gdn, state_conv, rel_bias, ln1_g, ln1_b,
           ffn1_w_gate, ffn1_w_up, ffn1_w_down, w_in, w_out, gdn_conv_w, gdn_a_log, gdn_dt_bias, gdn_norm_w,
           ln2_g, ln2_b, ffn2_w_gate, ffn2_w_up, ffn2_w_down, ln3_g, ln3_b):
    depth = w_in.shape[0]
    alpha = (2.0 * depth) ** 0.25
    bsz, seq, d_model = x_prompt.shape
    dbsz, dseq, _ = x_sample.shape
    w_buf = cache_attn_k.shape[2]
    hw = HEAD_WIDTH
    in_cols = w_in.shape[2]
    assert in_cols == 7 * hw + 2 * N_HEADS and CONV_WIDTH - 1 <= dseq <= SAMPLE_T_PAD and seq >= CONV_WIDTH - 1

    layout = _sample_layout(w_buf, dseq)
    tab_p = _t5_table_call(rel_bias, jnp.asarray(_prompt_buckets()))
    tab_s = _t5_table_call(rel_bias, jnp.asarray(layout[4]))
    tab_s = tab_s.reshape(tab_s.shape[0], N_HEADS * SAMPLE_T_PAD, tab_s.shape[3])

    yp = x_prompt.reshape(bsz * seq, d_model)
    ys = x_sample.reshape(dbsz * dseq, d_model)
    collected = [[] for _ in range(8)]
    row = lambda v: v.reshape(1, -1)
    for layer in range(depth):
        wg1, wu1, wd1 = (w[layer].astype(BF16) for w in (ffn1_w_gate, ffn1_w_up, ffn1_w_down))
        wg2, wu2, wd2 = (w[layer].astype(BF16) for w in (ffn2_w_gate, ffn2_w_up, ffn2_w_down))
        win = jnp.pad(w_in[layer], ((0, 0), (0, 7 * hw + LANES - in_cols))).astype(BF16)
        wo = w_out[layer].astype(BF16)
        conv_w8 = jnp.pad(gdn_conv_w[layer], ((0, SUBLANES - CONV_WIDTH), (0, 0)))
        head_params = jnp.zeros((SUBLANES, LANES), F32)
        head_params = head_params.at[0, N_HEADS:2 * N_HEADS].set(gdn_a_log[layer])
        head_params = head_params.at[1, N_HEADS:2 * N_HEADS].set(gdn_dt_bias[layer])
        norm_w_row = jnp.tile(gdn_norm_w[layer], N_HEADS).reshape(1, hw)
        pre = functools.partial(_pre_call, wg=wg1, wu=wu1, wd=wd1, g=row(ln1_g[layer]), b=row(ln1_b[layer]),
                                win=win, alpha=alpha)
        post = functools.partial(_post_call, wo=wo, g2=row(ln2_g[layer]), b2=row(ln2_b[layer]), wg=wg2, wu=wu2,
                                 wd=wd2, g3=row(ln3_g[layer]), b3=row(ln3_b[layer]), alpha=alpha)

        x1, aq, ak, av, gq, z, ba = pre(yp)
        shp = lambda a: a.reshape(bsz, seq, a.shape[1])
        attn = _attn_prompt_call(shp(aq), shp(ak), shp(av), tab_p)
        gdn, s_p = _gdn_call(shp(gq), shp(z), shp(ba),
                             jnp.zeros((bsz, SUBLANES, 3 * hw), F32),
                             jnp.zeros((bsz, N_HEADS, HEAD_DIM, HEAD_DIM), F32),
                             conv_w8, head_params, norm_w_row, n_valid=seq,
                             token_block=min(GDN_TOKEN_BLOCK, seq), chunk=min(GDN_CHUNK, seq))
        yp = post(attn.reshape(bsz * seq, hw), gdn.reshape(bsz * seq, hw), x1)
        wp = min(w_buf, seq)
        heads5 = lambda a: a.reshape(bsz, seq, N_HEADS, HEAD_DIM)[:, seq - wp:]
        st_p = (heads5(ak), heads5(av), s_p, shp(gq)[:, seq - (CONV_WIDTH - 1):])

        x1s, aqs, aks, avs, gqs, zs, bas = pre(ys)
        shs = lambda a: _pad_rows(a.reshape(dbsz, dseq, a.shape[1]), SAMPLE_T_PAD)
        attn_s = _attn_sample_call(shs(aqs), shs(aks), shs(avs), cache_attn_k[layer], cache_attn_v[layer],
                                   tab_s, layout)
        gdn_s, s_s = _gdn_call(shs(gqs), shs(zs), shs(bas),
                               _pad_rows(state_conv[layer], SUBLANES, front=True), state_gdn[layer],
                               conv_w8, head_params, norm_w_row, n_valid=dseq,
                               token_block=SAMPLE_T_PAD, chunk=SAMPLE_T_PAD)
        unpad = lambda a: a[:, :dseq].reshape(dbsz * dseq, hw)
        ys = post(unpad(attn_s), unpad(gdn_s), x1s)
        heads5s = lambda a: a.reshape(dbsz, dseq, N_HEADS, HEAD_DIM)
        st_s = (heads5s(aks), heads5s(avs), s_s,
                gqs.reshape(dbsz, dseq, 3 * hw)[:, dseq - (CONV_WIDTH - 1):])
        for lst, st in zip(collected, st_p + st_s):
            lst.append(st)
    outs = [jnp.stack(t, axis=0) for t in collected]
    return (yp.reshape(bsz, seq, d_model), ys.reshape(dbsz, dseq, d_model)) + tuple(outs)
```

```python
import functools
import math

import numpy as np
import jax
import jax.numpy as jnp
from jax import lax
from jax.experimental import pallas as pl
from jax.experimental.pallas import tpu as pltpu

F32 = jnp.float32
BF16 = jnp.bfloat16

HEAD_DIM = 64
N_HEADS = 8
HEAD_WIDTH = N_HEADS * HEAD_DIM
DILATED_PATTERNS = ((128, 1), (512, 4), (2048, 16))
KEYS_PER_BLOCK = 128
T5_BUCKETS = 32
T5_MAX_EXACT = 16
T5_MAX_DIST = 2048
CONV_WIDTH = 4
GDN_CHUNK = 64
LN_EPS = 1e-5
RMS_EPS = 1e-6
NEG_BIG = -1e30

LANES = 128
SUBLANES = 8
TOKEN_TILE = 256
GDN_TOKEN_BLOCK = 256
GDN_CHUNK_GROUP = 2
GDN_SAMPLE_BATCH_BLOCK = 8
SAMPLE_T_PAD = 8
VMEM_LIMIT = 56 * 1024 * 1024


def _resident(shape):
    nd = len(shape)
    return pl.BlockSpec(shape, lambda *_: (0,) * nd, pipeline_mode=pl.Buffered(1))


def _layernorm(y, g, b):
    mu = jnp.mean(y, axis=-1, keepdims=True)
    yc = y - mu
    var = jnp.mean(yc * yc, axis=-1, keepdims=True)
    return yc * lax.rsqrt(var + LN_EPS) * g + b


def _swiglu(xb, wg_ref, wu_ref, wd_ref, h_scr):
    d_ff = wg_ref.shape[1]
    step = 2 * LANES
    assert d_ff % step == 0
    for c in range(d_ff // step):
        sl = slice(c * step, (c + 1) * step)
        gate = jnp.dot(xb, wg_ref[:, sl], preferred_element_type=F32)
        up = jnp.dot(xb, wu_ref[:, sl], preferred_element_type=F32)
        h_scr[:, sl] = (jax.nn.silu(gate) * up).astype(BF16)
    return jnp.dot(h_scr[...], wd_ref[...], preferred_element_type=F32)


def _pre_kernel(x_ref, wg_ref, wu_ref, wd_ref, g_ref, b_ref, win_ref,
                x1_ref, aq_ref, ak_ref, av_ref, gq_ref, z_ref, ba_ref, h_scr, *, alpha):
    x = x_ref[...]
    ff = _swiglu(x.astype(BF16), wg_ref, wu_ref, wd_ref, h_scr)
    x1 = _layernorm(alpha * x + 0.5 * ff, g_ref[...], b_ref[...])
    x1_ref[...] = x1
    xb = x1.astype(BF16)
    col = 0
    for ref in (aq_ref, ak_ref, av_ref, gq_ref, z_ref, ba_ref):
        width = ref.shape[1]
        ref[...] = jnp.dot(xb, win_ref[:, col:col + width], preferred_element_type=F32)
        col += width


def _pre_call(x2d, wg, wu, wd, g, b, win, alpha):
    m, d = x2d.shape
    d_ff = wg.shape[1]
    tm = TOKEN_TILE
    assert m % tm == 0
    widths = (HEAD_WIDTH, HEAD_WIDTH, HEAD_WIDTH, 3 * HEAD_WIDTH, HEAD_WIDTH, LANES)
    assert sum(widths) == win.shape[1]
    tile = lambda w: pl.BlockSpec((tm, w), lambda i: (i, 0))
    return pl.pallas_call(
        functools.partial(_pre_kernel, alpha=alpha),
        grid=(m // tm,),
        in_specs=[tile(d), _resident(wg.shape), _resident(wu.shape), _resident(wd.shape),
                  _resident(g.shape), _resident(b.shape), _resident(win.shape)],
        out_specs=[tile(d)] + [tile(w) for w in widths],
        out_shape=[jax.ShapeDtypeStruct((m, d), F32)] + [jax.ShapeDtypeStruct((m, w), F32) for w in widths],
        scratch_shapes=[pltpu.VMEM((tm, d_ff), BF16)],
        compiler_params=pltpu.CompilerParams(dimension_semantics=("parallel",), vmem_limit_bytes=VMEM_LIMIT),
        name="pre_ffn_proj",
    )(x2d, wg, wu, wd, g, b, win)


def _post_kernel(attn_ref, gdn_ref, x1_ref, wo_ref, g2_ref, b2_ref, wg_ref, wu_ref, wd_ref, g3_ref, b3_ref,
                 y_ref, h_scr, *, alpha):
    mix = jnp.dot(attn_ref[...].astype(BF16), wo_ref[0:HEAD_WIDTH, :], preferred_element_type=F32)
    mix = mix + jnp.dot(gdn_ref[...].astype(BF16), wo_ref[HEAD_WIDTH:2 * HEAD_WIDTH, :],
                        preferred_element_type=F32)
    x2 = _layernorm(alpha * x1_ref[...] + mix, g2_ref[...], b2_ref[...])
    ff = _swiglu(x2.astype(BF16), wg_ref, wu_ref, wd_ref, h_scr)
    y_ref[...] = _layernorm(alpha * x2 + 0.5 * ff, g3_ref[...], b3_ref[...])


def _post_call(attn, gdn, x1, wo, g2, b2, wg, wu, wd, g3, b3, alpha):
    m, d = x1.shape
    tm = TOKEN_TILE
    assert m % tm == 0
    tile = lambda w: pl.BlockSpec((tm, w), lambda i: (i, 0))
    consts = (wo, g2, b2, wg, wu, wd, g3, b3)
    return pl.pallas_call(
        functools.partial(_post_kernel, alpha=alpha),
        grid=(m // tm,),
        in_specs=[tile(HEAD_WIDTH), tile(HEAD_WIDTH), tile(d)] + [_resident(c.shape) for c in consts],
        out_specs=tile(d),
        out_shape=jax.ShapeDtypeStruct((m, d), F32),
        scratch_shapes=[pltpu.VMEM((tm, wg.shape[1]), BF16)],
        compiler_params=pltpu.CompilerParams(dimension_semantics=("parallel",), vmem_limit_bytes=VMEM_LIMIT),
        name="post_out_ffn",
    )(attn, gdn, x1, *consts)


def _t5_bucket_np(dist):
    n = np.maximum(dist, 0)
    nf = np.maximum(n, 1).astype(np.float32)
    large = T5_MAX_EXACT + (np.log(nf / np.float32(T5_MAX_EXACT)) / np.float32(math.log(T5_MAX_DIST / T5_MAX_EXACT))
                            * np.float32(T5_BUCKETS - T5_MAX_EXACT)).astype(np.int32)
    large = np.minimum(large, T5_BUCKETS - 1)
    return np.where(n < T5_MAX_EXACT, n, large).astype(np.int32)


def _t5_table_kernel(rb_ref, bk_ref, out_ref):
    bk = bk_ref[0]
    for h in range(N_HEADS):
        acc = jnp.full(bk.shape, NEG_BIG, F32)
        for b in range(T5_BUCKETS):
            acc = jnp.where(bk == b, rb_ref[b, h], acc)
        out_ref[0, h] = acc


def _t5_table_call(rel_bias, buckets):
    p, r, c = buckets.shape
    return pl.pallas_call(
        _t5_table_kernel,
        grid=(p,),
        in_specs=[pl.BlockSpec(memory_space=pltpu.SMEM),
                  pl.BlockSpec((1, r, c), lambda i: (i, 0, 0))],
        out_specs=pl.BlockSpec((1, N_HEADS, r, c), lambda i: (i, 0, 0, 0)),
        out_shape=jax.ShapeDtypeStruct((p, N_HEADS, r, c), F32),
        compiler_params=pltpu.CompilerParams(dimension_semantics=("parallel",)),
        name="t5_bias_table",
    )(rel_bias, buckets)


def _prompt_buckets():
    nb = KEYS_PER_BLOCK
    qi = np.arange(nb)[:, None]
    ki = np.arange(2 * nb)[None, :]
    dist = qi + nb - ki
    valid = (dist >= 0) & (dist <= nb)
    out = []
    for window, dil in DILATED_PATTERNS:
        assert window // dil == nb
        out.append(np.where(valid, _t5_bucket_np(dist * dil), -1))
    return np.stack(out).astype(np.int32)


def _nt_dot(a, b):
    return lax.dot_general(a, b, (((1,), (1,)), ((), ())), preferred_element_type=F32)


def _attn_prompt_kernel(q_ref, k_ref, v_ref, tab_ref, o_ref, og_scr, *, seq):
    nb = KEYS_PER_BLOCK
    hd = HEAD_DIM
    n_pat = len(DILATED_PATTERNS)
    heads = LANES // hd
    for g, (window, dil) in enumerate(DILATED_PATTERNS):
        n_blocks = seq // (dil * nb)

        def blocks(starts, with_prev, g=g, dil=dil):
            row_sl, qs, ks, vs, kps, vps = [], [], [], [], [], []
            for start in starts:
                rows = pl.ds(start, nb, stride=dil) if dil > 1 else pl.ds(start, nb)
                row_sl.append(rows)
                qs.append(q_ref[0, rows, :] * hd ** -0.5)
                ks.append(k_ref[0, rows, :])
                vs.append(v_ref[0, rows, :])
                if with_prev:
                    rows_p = (pl.ds(start - dil * nb, nb, stride=dil) if dil > 1 else pl.ds(start - nb, nb))
                    kps.append(k_ref[0, rows_p, :])
                    vps.append(v_ref[0, rows_p, :])
            chains = [(u, hh) for u in range(len(starts)) for hh in range(heads)]
            ln = lambda hh: slice(hd * hh, hd * (hh + 1))
            s_c, s_p, m, p_c, p_p, den, acc = {}, {}, {}, {}, {}, {}, {}
            for c in chains:
                u, hh = c
                qb = qs[u][:, ln(hh)].astype(BF16)
                s_c[c] = _nt_dot(qb, ks[u][:, ln(hh)].astype(BF16)) + tab_ref[g, hh, :, nb:2 * nb]
                if with_prev:
                    s_p[c] = _nt_dot(qb, kps[u][:, ln(hh)].astype(BF16)) + tab_ref[g, hh, :, 0:nb]
            for c in chains:
                m[c] = jnp.max(s_c[c], axis=-1, keepdims=True)
                if with_prev:
                    m[c] = jnp.maximum(m[c], jnp.max(s_p[c], axis=-1, keepdims=True))
            for c in chains:
                p_c[c] = jnp.exp(s_c[c] - m[c])
                den[c] = jnp.sum(p_c[c], axis=-1, keepdims=True)
                if with_prev:
                    p_p[c] = jnp.exp(s_p[c] - m[c])
                    den[c] = den[c] + jnp.sum(p_p[c], axis=-1, keepdims=True)
            for c in chains:
                u, hh = c
                acc[c] = jnp.dot(p_c[c].astype(BF16), vs[u][:, ln(hh)].astype(BF16), preferred_element_type=F32)
                if with_prev:
                    acc[c] = acc[c] + jnp.dot(p_p[c].astype(BF16), vps[u][:, ln(hh)].astype(BF16),
                                              preferred_element_type=F32)
            for c in chains:
                u, hh = c
                lse = m[c] + jnp.log(den[c])
                og_scr[g * heads + hh, row_sl[u], :] = jnp.concatenate(
                    [acc[c] / den[c], jnp.broadcast_to(lse, (nb, hd))], axis=-1)

        def unroll_of(count):
            return next(u for u in (4, 3, 2, 1) if count % u == 0)

        u_first = unroll_of(dil)

        def first_body(i, carry, blocks=blocks, u_first=u_first):
            blocks([i * u_first + j for j in range(u_first)], False)
            return carry

        lax.fori_loop(0, dil // u_first, first_body, 0)
        if n_blocks > 1:
            n_rest = dil * (n_blocks - 1)
            u_rest = unroll_of(n_rest)

            def rest_body(i, carry, blocks=blocks, dil=dil, n_blocks=n_blocks, u_rest=u_rest):
                starts = []
                for j in range(u_rest):
                    idx = i * u_rest + j
                    starts.append(idx // (n_blocks - 1) + dil * nb * (idx % (n_blocks - 1) + 1))
                blocks(starts, True)
                return carry

            lax.fori_loop(0, n_rest // u_rest, rest_body, 0)

    rb = 2 * nb

    def merge_body(i, carry):
        rows = pl.ds(pl.multiple_of(i * rb, rb), rb)
        res = []
        for hh in range(heads):
            tiles = [og_scr[g * heads + hh, rows, :] for g in range(n_pat)]
            lses = [t[:, hd:2 * hd] for t in tiles]
            top = functools.reduce(jnp.maximum, lses)
            ws = [jnp.exp(l - top) for l in lses]
            num = functools.reduce(lambda a, b: a + b, [w * t[:, 0:hd] for w, t in zip(ws, tiles)])
            res.append(num / functools.reduce(lambda a, b: a + b, ws))
        o_ref[0, rows, :] = jnp.concatenate(res, axis=-1)
        return carry

    lax.fori_loop(0, seq // rb, merge_body, 0)


def _attn_prompt_call(aq, ak, av, tab):
    bsz, seq, width = aq.shape
    assert width == HEAD_WIDTH and seq % (DILATED_PATTERNS[-1][1] * KEYS_PER_BLOCK) == 0
    n_pat = len(DILATED_PATTERNS)
    heads_per_step = LANES // HEAD_DIM
    qkv_spec = pl.BlockSpec((1, seq, LANES), lambda h, b: (b, 0, h))
    return pl.pallas_call(
        functools.partial(_attn_prompt_kernel, seq=seq),
        grid=(N_HEADS // heads_per_step, bsz),
        in_specs=[qkv_spec, qkv_spec, qkv_spec,
                  pl.BlockSpec((n_pat, heads_per_step, KEYS_PER_BLOCK, 2 * KEYS_PER_BLOCK),
                               lambda h, b: (0, h, 0, 0))],
        out_specs=pl.BlockSpec((1, seq, LANES), lambda h, b: (b, 0, h)),
        out_shape=jax.ShapeDtypeStruct((bsz, seq, width), F32),
        scratch_shapes=[pltpu.VMEM((n_pat * heads_per_step, seq, LANES), F32)],
        compiler_params=pltpu.CompilerParams(dimension_semantics=("parallel", "parallel"),
                                             vmem_limit_bytes=VMEM_LIMIT),
        name="attn_prompt",
    )(aq, ak, av, tab)


SAMPLE_FAR_ROWS = SUBLANES


def _sample_layout(w_buf, t_new):
    period = DILATED_PATTERNS[-1][1]
    near_rows = DILATED_PATTERNS[-2][0]
    assert w_buf % period == 0 and near_rows % period == 0 and t_new <= SAMPLE_FAR_ROWS <= period
    n_far_groups = (w_buf - near_rows) // period
    far_cols = n_far_groups * SAMPLE_FAR_ROWS
    cols = far_cols + near_rows + LANES
    row_of_col = np.full((cols,), -1, np.int64)
    c = np.arange(far_cols)
    row_of_col[:far_cols] = (c // SAMPLE_FAR_ROWS) * period + c % SAMPLE_FAR_ROWS
    row_of_col[far_cols:far_cols + near_rows] = w_buf - near_rows + np.arange(near_rows)
    row_of_col[far_cols + near_rows:far_cols + near_rows + t_new] = w_buf + np.arange(t_new)
    buckets = np.full((len(DILATED_PATTERNS), SAMPLE_T_PAD, cols), -1, np.int32)
    for g, (window, dil) in enumerate(DILATED_PATTERNS):
        for t in range(t_new):
            dist = w_buf + t - row_of_col
            ok = (row_of_col >= 0) & (dist >= 0) & (dist <= window) & (dist % dil == 0)
            assert int(ok.sum()) == min(window, w_buf + t) // dil + 1, (g, t, int(ok.sum()))
            buckets[g, t] = np.where(ok, _t5_bucket_np(dist), -1)
    return n_far_groups, far_cols, near_rows, cols, buckets


def _attn_sample_kernel(q_ref, kn_ref, vn_ref, kfar_ref, knear_ref, vfar_ref, vnear_ref, tab_ref, hm_ref,
                        o_ref, *, far_cols, near_rows):
    tp = SAMPLE_T_PAD
    q8 = q_ref[0] * HEAD_DIM ** -0.5
    hm = hm_ref[...]
    a = jnp.concatenate([q8 * hm[h:h + 1, :] for h in range(N_HEADS)], axis=0)
    width = q8.shape[1]
    pad = jnp.zeros((LANES - tp, width), F32)
    k_parts = (kfar_ref[0].reshape(far_cols, width), knear_ref[0].reshape(near_rows, width),
               jnp.concatenate([kn_ref[0], pad], axis=0))
    v_parts = (vfar_ref[0].reshape(far_cols, width), vnear_ref[0].reshape(near_rows, width),
               jnp.concatenate([vn_ref[0], pad], axis=0))
    edges = (0, far_cols, far_cols + near_rows, far_cols + near_rows + LANES)
    scores = [_nt_dot(a, kp) for kp in k_parts]
    outs, lses = [], []
    for g, (window, dil) in enumerate(DILATED_PATTERNS):
        parts = [i for i in range(3) if not (i == 0 and window <= near_rows)]
        logit = [scores[i] + tab_ref[g, :, edges[i]:edges[i + 1]] for i in parts]
        m = functools.reduce(jnp.maximum, [jnp.max(l, axis=-1, keepdims=True) for l in logit])
        ps = [jnp.exp(l - m) for l in logit]
        den = functools.reduce(lambda x, y: x + y, [jnp.sum(p, axis=-1, keepdims=True) for p in ps])
        acc = functools.reduce(lambda x, y: x + y,
                               [jnp.dot(p, v_parts[i], preferred_element_type=F32) for p, i in zip(ps, parts)])
        outs.append(acc / den)
        lses.append(m + jnp.log(den))
    top = functools.reduce(jnp.maximum, lses)
    ws = [jnp.exp(l - top) for l in lses]
    merged = functools.reduce(lambda x, y: x + y, [w * o for w, o in zip(ws, outs)])
    merged = merged / functools.reduce(lambda x, y: x + y, ws)
    res = merged[0:tp] * hm[0:1, :]
    for h in range(1, N_HEADS):
        res = res + merged[h * tp:(h + 1) * tp] * hm[h:h + 1, :]
    o_ref[0] = res


def _attn_sample_call(aq8, ak8, av8, cache_k, cache_v, tab, layout):
    n_far_groups, far_cols, near_rows, cols, _ = layout
    bsz, w_buf, n_heads, hd = cache_k.shape
    width = n_heads * hd
    period = DILATED_PATTERNS[-1][1]
    ck = cache_k.reshape(bsz, w_buf // period, period, width)
    cv = cache_v.reshape(bsz, w_buf // period, period, width)
    n_near_groups = near_rows // period
    assert n_far_groups % n_near_groups == 0
    head_mask = np.repeat(np.eye(n_heads, dtype=np.float32), hd, axis=1)
    new_spec = pl.BlockSpec((1, SAMPLE_T_PAD, width), lambda b: (b, 0, 0))
    far_spec = pl.BlockSpec((1, n_far_groups, SAMPLE_FAR_ROWS, width), lambda b: (b, 0, 0, 0))
    near_spec = pl.BlockSpec((1, n_near_groups, period, width),
                             lambda b: (b, n_far_groups // n_near_groups, 0, 0))
    return pl.pallas_call(
        functools.partial(_attn_sample_kernel, far_cols=far_cols, near_rows=near_rows),
        grid=(bsz,),
        in_specs=[new_spec, new_spec, new_spec, far_spec, near_spec, far_spec, near_spec,
                  _resident(tab.shape), _resident(head_mask.shape)],
        out_specs=new_spec,
        out_shape=jax.ShapeDtypeStruct((bsz, SAMPLE_T_PAD, width), F32),
        compiler_params=pltpu.CompilerParams(dimension_semantics=("parallel",), vmem_limit_bytes=VMEM_LIMIT),
        name="attn_sample",
    )(aq8, ak8, av8, ck, ck, cv, cv, tab, jnp.asarray(head_mask))


def _aligned(index, size):
    return index * size if isinstance(index, int) else pl.multiple_of(index * size, size)


def _split3(x):
    h1 = x.astype(BF16)
    r1 = x - h1.astype(F32)
    h2 = r1.astype(BF16)
    h3 = (r1 - h2.astype(F32)).astype(BF16)
    return h1, h2, h3


def _head_sums(x, ones_ref):
    h1, h2, _ = _split3(x)
    return (jnp.dot(h1, ones_ref[...], preferred_element_type=F32)
            + jnp.dot(h2, ones_ref[...], preferred_element_type=F32))


def _gdn_kernel(x_ref, z_ref, ba_ref, cb_ref, s0_ref, cw_ref, hp_ref, nw_ref, ones_ref,
                o_ref, sfin_ref,
                ext_scr, q_scr, k_scr, v_scr, oc_scr, bt_scr, g_scr, s_scr,
                uw_scr, in_scr, qd_scr, kd_scr, el_scr, *, chunk, n_valid, group):
    n_bb, tb = x_ref.shape[0], x_ref.shape[1]
    n_chunks = tb // chunk
    hw = HEAD_WIDTH
    hd = HEAD_DIM
    t = pl.program_id(1)
    hist = SUBLANES

    @pl.when(t == 0)
    def _():
        ext_scr[:, 0:hist, :] = cb_ref[...]
        s_scr[...] = s0_ref[...]

    for bb in range(n_bb):
        ext_scr[bb, hist:hist + tb, :] = x_ref[bb]
        conv = jnp.zeros((tb, 3 * hw), F32)
        for j in range(CONV_WIDTH):
            off = hist - (CONV_WIDTH - 1) + j
            conv = conv + ext_scr[bb, off:off + tb, :] * cw_ref[j:j + 1, :]
        ext_scr[bb, 0:hist, :] = ext_scr[bb, tb:tb + hist, :]
        act = jax.nn.silu(conv)
        q = act[:, 0:hw]
        k = act[:, hw:2 * hw]
        q_scr[bb] = q * lax.rsqrt(_head_sums(q * q, ones_ref) + RMS_EPS) * hd ** -0.5
        k_scr[bb] = k * lax.rsqrt(_head_sums(k * k, ones_ref) + RMS_EPS)
        v_scr[bb] = act[:, 2 * hw:3 * hw]

        ba = ba_ref[bb]
        row = t * tb + lax.broadcasted_iota(jnp.int32, ba.shape, 0)
        live = row < n_valid
        bt_scr[bb] = jnp.where(live, jax.nn.sigmoid(ba), 0.0)
        g_scr[bb] = jnp.where(live, -jnp.exp(hp_ref[0:1, :]) * jax.nn.softplus(ba + hp_ref[1:2, :]), 0.0)

    ri = lax.broadcasted_iota(jnp.int32, (chunk, chunk), 0)
    ci = lax.broadcasted_iota(jnp.int32, (chunk, chunk), 1)
    tril = ri >= ci
    strict = ri > ci
    eye_c = (ri == ci).astype(F32)
    tril_b = tril.astype(BF16)
    eye_l = (lax.broadcasted_iota(jnp.int32, (LANES, LANES), 0)
             == lax.broadcasted_iota(jnp.int32, (LANES, LANES), 1)).astype(BF16)
    n_doublings = int(math.log2(chunk)) - 1
    assert 2 ** (n_doublings + 1) == chunk

    dot = functools.partial(jnp.dot, preferred_element_type=F32)

    def state_free_part(gi, carry):
        probs, shared = [], {}
        for bb in range(n_bb):
            for cg in range(group):
                c_idx = gi * group + cg
                rows = pl.ds(_aligned(c_idx, chunk), chunk)
                gc = functools.reduce(lambda a, b: a + b,
                                      [dot(tril_b, p) for p in _split3(g_scr[bb, rows, :])])
                gct = functools.reduce(lambda a, b: a + b, [_nt_dot(eye_l, p) for p in _split3(gc)])
                g_last = gc[chunk - 1:chunk, :]
                el_rows = pl.ds(_aligned(c_idx, SUBLANES), SUBLANES)
                el_scr[bb, el_rows, :] = jnp.broadcast_to(jnp.exp(g_last), (SUBLANES, LANES))
                shared[bb, cg] = (rows, gc, gct, jnp.exp(gc), jnp.exp(g_last - gc), bt_scr[bb, rows, :])
                probs += [(bb, cg, h) for h in range(N_HEADS)]
        kk, qk, rhs, decay, a, t_inv, pw = {}, {}, {}, {}, {}, {}, {}
        for p in probs:
            bb, cg, h = p
            rows, gc, gct, e_gc, e_rem, bt = shared[bb, cg]
            hl = pl.ds(hd * h, hd)
            gl = N_HEADS + h
            qh = q_scr[bb, rows, hl]
            kh = k_scr[bb, rows, hl]
            beta = bt[:, h:h + 1]
            kb = kh * beta
            kk[p] = _nt_dot(kb, kh)
            qk[p] = _nt_dot(qh, kh)
            rhs[p] = jnp.concatenate([v_scr[bb, rows, hl] * beta, kb * e_gc[:, gl:gl + 1]], axis=1)
            qd_scr[bb, rows, hl] = qh * e_gc[:, gl:gl + 1]
            kd_scr[bb, rows, hl] = kh * e_rem[:, gl:gl + 1]
            diff = gc[:, gl:gl + 1] - gct[gl:gl + 1, :]
            decay[p] = jnp.where(tril, jnp.exp(jnp.where(tril, diff, 0.0)), 0.0)
        for p in probs:
            bb, cg, h = p
            rows = shared[bb, cg][0]
            a[p] = jnp.where(strict, kk[p] * decay[p], 0.0)
            in_scr[bb, rows, LANES * h:LANES * h + chunk] = jnp.where(tril, qk[p] * decay[p], 0.0)
            t_inv[p] = eye_c - a[p]
        for p in probs:
            pw[p] = dot(a[p], a[p])
        for _ in range(n_doublings - 1):
            nxt_pw, nxt_t = {}, {}
            for p in probs:
                nxt_pw[p] = dot(pw[p], pw[p])
                nxt_t[p] = dot(t_inv[p], eye_c + pw[p])
            pw, t_inv = nxt_pw, nxt_t
        for p in probs:
            t_inv[p] = dot(t_inv[p], eye_c + pw[p])
        for p in probs:
            bb, cg, h = p
            rows = shared[bb, cg][0]
            uw_scr[bb, rows, LANES * h:LANES * (h + 1)] = dot(t_inv[p], rhs[p])
        return carry

    def state_part(ci_, carry):
        rows = pl.ds(_aligned(ci_, chunk), chunk)
        el_rows = pl.ds(_aligned(ci_, SUBLANES), SUBLANES)
        probs = [(bb, h) for bb in range(n_bb) for h in range(N_HEADS)]
        s_old, uw, both, v_new, o_in, s_add = {}, {}, {}, {}, {}, {}
        for p in probs:
            bb, h = p
            s_old[p] = s_scr[bb, h]
            uw[p] = uw_scr[bb, rows, LANES * h:LANES * (h + 1)]
            lhs = jnp.concatenate([uw[p][:, hd:2 * hd], qd_scr[bb, rows, pl.ds(hd * h, hd)]], axis=0)
            both[p] = dot(lhs, s_old[p])
        for p in probs:
            bb, h = p
            v_new[p] = uw[p][:, 0:hd] - both[p][0:chunk]
            o_in[p] = dot(in_scr[bb, rows, LANES * h:LANES * h + chunk], v_new[p])
            s_add[p] = lax.dot_general(kd_scr[bb, rows, pl.ds(hd * h, hd)], v_new[p],
                                       (((0,), (0,)), ((), ())), preferred_element_type=F32)
        for p in probs:
            bb, h = p
            e_last = el_scr[bb, el_rows, :][0:1, N_HEADS + h:N_HEADS + h + 1]
            oc_scr[bb, rows, pl.ds(hd * h, hd)] = both[p][chunk:2 * chunk] + o_in[p]
            s_scr[bb, h] = s_old[p] * e_last + s_add[p]
        return carry

    assert n_chunks % group == 0
    if n_chunks == group:
        state_free_part(0, 0)
    else:
        lax.fori_loop(0, n_chunks // group, state_free_part, 0)
    if n_chunks == 1:
        state_part(0, 0)
    else:
        lax.fori_loop(0, n_chunks, state_part, 0)

    for bb in range(n_bb):
        o = oc_scr[bb]
        o = o * lax.rsqrt(_head_sums(o * o, ones_ref) * (1.0 / hd) + RMS_EPS) * nw_ref[...]
        o_ref[bb] = o * jax.nn.silu(z_ref[bb])

    @pl.when(t == pl.num_programs(1) - 1)
    def _():
        sfin_ref[...] = s_scr[...]


def _gdn_call(gq, z, ba, conv_buf8, s0, conv_w8, head_params, norm_w_row, n_valid, token_block, chunk,
              batch_block, group):
    bsz, seq, w3 = gq.shape
    hw = HEAD_WIDTH
    assert w3 == 3 * hw and seq % token_block == 0 and token_block % chunk == 0 and bsz % batch_block == 0
    tb = token_block
    nbb = batch_block
    ones_bd = np.kron(np.eye(N_HEADS, dtype=np.float32), np.ones((HEAD_DIM, HEAD_DIM), np.float32))
    blk = lambda w: pl.BlockSpec((nbb, tb, w), lambda b, t: (b, t, 0))
    per_b3 = lambda s: pl.BlockSpec((nbb,) + s, lambda b, t: (b,) + (0,) * len(s))
    return pl.pallas_call(
        functools.partial(_gdn_kernel, chunk=chunk, n_valid=n_valid, group=group),
        grid=(bsz // nbb, seq // tb),
        in_specs=[blk(w3), blk(hw), blk(LANES), per_b3((SUBLANES, w3)), per_b3((N_HEADS, HEAD_DIM, HEAD_DIM)),
                  _resident(conv_w8.shape), _resident(head_params.shape), _resident(norm_w_row.shape),
                  _resident(ones_bd.shape)],
        out_specs=[blk(hw), per_b3((N_HEADS, HEAD_DIM, HEAD_DIM))],
        out_shape=[jax.ShapeDtypeStruct((bsz, seq, hw), F32),
                   jax.ShapeDtypeStruct((bsz, N_HEADS, HEAD_DIM, HEAD_DIM), F32)],
        scratch_shapes=[pltpu.VMEM((nbb, tb + 2 * SUBLANES, w3), F32)]
                       + [pltpu.VMEM((nbb, tb, hw), F32)] * 4
                       + [pltpu.VMEM((nbb, tb, LANES), F32)] * 2
                       + [pltpu.VMEM((nbb, N_HEADS, HEAD_DIM, HEAD_DIM), F32)]
                       + [pltpu.VMEM((nbb, tb, N_HEADS * LANES), F32)] * 2
                       + [pltpu.VMEM((nbb, tb, hw), F32)] * 2
                       + [pltpu.VMEM((nbb, (tb // chunk) * SUBLANES, LANES), F32)],
        compiler_params=pltpu.CompilerParams(dimension_semantics=("parallel", "arbitrary"),
                                             vmem_limit_bytes=VMEM_LIMIT),
        name="gdn",
    )(gq, z, ba, conv_buf8, s0, conv_w8, head_params, norm_w_row, jnp.asarray(ones_bd, BF16))


def _pad_rows(x, rows, front=False):
    extra = rows - x.shape[1]
    cfg = [(0, 0)] * x.ndim
    cfg[1] = (extra, 0) if front else (0, extra)
    return jnp.pad(x, cfg)


def kernel(x_prompt, x_sample, cache_attn_k, cache_attn_v, state_gdn, state_conv, rel_bias, ln1_g, ln1_b,
           ffn1_w_gate, ffn1_w_up, ffn1_w_down, w_in, w_out, gdn_conv_w, gdn_a_log, gdn_dt_bias, gdn_norm_w,
           ln2_g, ln2_b, ffn2_w_gate, ffn2_w_up, ffn2_w_down, ln3_g, ln3_b):
    depth = w_in.shape[0]
    alpha = (2.0 * depth) ** 0.25
    bsz, seq, d_model = x_prompt.shape
    dbsz, dseq, _ = x_sample.shape
    w_buf = cache_attn_k.shape[2]
    hw = HEAD_WIDTH
    in_cols = w_in.shape[2]
    assert in_cols == 7 * hw + 2 * N_HEADS and CONV_WIDTH - 1 <= dseq <= SAMPLE_T_PAD and seq >= CONV_WIDTH - 1

    layout = _sample_layout(w_buf, dseq)
    tab_p = _t5_table_call(rel_bias, jnp.asarray(_prompt_buckets()))
    tab_s = _t5_table_call(rel_bias, jnp.asarray(layout[4]))
    tab_s = tab_s.reshape(tab_s.shape[0], N_HEADS * SAMPLE_T_PAD, tab_s.shape[3])

    yp = x_prompt.reshape(bsz * seq, d_model)
    ys = x_sample.reshape(dbsz * dseq, d_model)
    collected = [[] for _ in range(8)]
    row = lambda v: v.reshape(1, -1)
    for layer in range(depth):
        wg1, wu1, wd1 = (w[layer].astype(BF16) for w in (ffn1_w_gate, ffn1_w_up, ffn1_w_down))
        wg2, wu2, wd2 = (w[layer].astype(BF16) for w in (ffn2_w_gate, ffn2_w_up, ffn2_w_down))
        win = jnp.pad(w_in[layer], ((0, 0), (0, 7 * hw + LANES - in_cols))).astype(BF16)
        wo = w_out[layer].astype(BF16)
        conv_w8 = jnp.pad(gdn_conv_w[layer], ((0, SUBLANES - CONV_WIDTH), (0, 0)))
        head_params = jnp.zeros((SUBLANES, LANES), F32)
        head_params = head_params.at[0, N_HEADS:2 * N_HEADS].set(gdn_a_log[layer])
        head_params = head_params.at[1, N_HEADS:2 * N_HEADS].set(gdn_dt_bias[layer])
        norm_w_row = jnp.tile(gdn_norm_w[layer], N_HEADS).reshape(1, hw)
        pre = functools.partial(_pre_call, wg=wg1, wu=wu1, wd=wd1, g=row(ln1_g[layer]), b=row(ln1_b[layer]),
                                win=win, alpha=alpha)
        post = functools.partial(_post_call, wo=wo, g2=row(ln2_g[layer]), b2=row(ln2_b[layer]), wg=wg2, wu=wu2,
                                 wd=wd2, g3=row(ln3_g[layer]), b3=row(ln3_b[layer]), alpha=alpha)

        x1, aq, ak, av, gq, z, ba = pre(yp)
        shp = lambda a: a.reshape(bsz, seq, a.shape[1])
        attn = _attn_prompt_call(shp(aq), shp(ak), shp(av), tab_p)
        gdn, s_p = _gdn_call(shp(gq), shp(z), shp(ba),
                             jnp.zeros((bsz, SUBLANES, 3 * hw), F32),
                             jnp.zeros((bsz, N_HEADS, HEAD_DIM, HEAD_DIM), F32),
                             conv_w8, head_params, norm_w_row, n_valid=seq,
                             token_block=min(GDN_TOKEN_BLOCK, seq), chunk=min(GDN_CHUNK, seq),
                             batch_block=1, group=GDN_CHUNK_GROUP)
        yp = post(attn.reshape(bsz * seq, hw), gdn.reshape(bsz * seq, hw), x1)
        wp = min(w_buf, seq)
        heads5 = lambda a: a.reshape(bsz, seq, N_HEADS, HEAD_DIM)[:, seq - wp:]
        st_p = (heads5(ak), heads5(av), s_p, shp(gq)[:, seq - (CONV_WIDTH - 1):])

        x1s, aqs, aks, avs, gqs, zs, bas = pre(ys)
        shs = lambda a: _pad_rows(a.reshape(dbsz, dseq, a.shape[1]), SAMPLE_T_PAD)
        attn_s = _attn_sample_call(shs(aqs), shs(aks), shs(avs), cache_attn_k[layer], cache_attn_v[layer],
                                   tab_s, layout)
        gdn_s, s_s = _gdn_call(shs(gqs), shs(zs), shs(bas),
                               _pad_rows(state_conv[layer], SUBLANES, front=True), state_gdn[layer],
                               conv_w8, head_params, norm_w_row, n_valid=dseq,
                               token_block=SAMPLE_T_PAD, chunk=SAMPLE_T_PAD,
                               batch_block=GDN_SAMPLE_BATCH_BLOCK, group=1)
        unpad = lambda a: a[:, :dseq].reshape(dbsz * dseq, hw)
        ys = post(unpad(attn_s), unpad(gdn_s), x1s)
        heads5s = lambda a: a.reshape(dbsz, dseq, N_HEADS, HEAD_DIM)
        st_s = (heads5s(aks), heads5s(avs), s_s,
                gqs.reshape(dbsz, dseq, 3 * hw)[:, dseq - (CONV_WIDTH - 1):])
        for lst, st in zip(collected, st_p + st_s):
            lst.append(st)
    outs = [jnp.stack(t, axis=0) for t in collected]
    return (yp.reshape(bsz, seq, d_model), ys.reshape(dbsz, dseq, d_model)) + tuple(outs)
```

```python
import functools
import math

import numpy as np
import jax
import jax.numpy as jnp
from jax import lax
from jax.experimental import pallas as pl
from jax.experimental.pallas import tpu as pltpu

F32 = jnp.float32
BF16 = jnp.bfloat16

HEAD_DIM = 64
N_HEADS = 8
HEAD_WIDTH = N_HEADS * HEAD_DIM
DILATED_PATTERNS = ((128, 1), (512, 4), (2048, 16))
KEYS_PER_BLOCK = 128
T5_BUCKETS = 32
T5_MAX_EXACT = 16
T5_MAX_DIST = 2048
CONV_WIDTH = 4
GDN_CHUNK = 64
LN_EPS = 1e-5
RMS_EPS = 1e-6
NEG_BIG = -1e30

LANES = 128
SUBLANES = 8
TOKEN_TILE = 256
GDN_TOKEN_BLOCK = 256
GDN_CHUNK_GROUP = 2
GDN_SAMPLE_BATCH_BLOCK = 8
SAMPLE_T_PAD = 8
VMEM_LIMIT = 56 * 1024 * 1024


def _resident(shape):
    nd = len(shape)
    return pl.BlockSpec(shape, lambda *_: (0,) * nd, pipeline_mode=pl.Buffered(1))


def _layernorm(y, g, b):
    mu = jnp.mean(y, axis=-1, keepdims=True)
    yc = y - mu
    var = jnp.mean(yc * yc, axis=-1, keepdims=True)
    return yc * lax.rsqrt(var + LN_EPS) * g + b


def _swiglu(xb, wg_ref, wu_ref, wd_ref, h_scr):
    d_ff = wg_ref.shape[1]
    step = 2 * LANES
    assert d_ff % step == 0
    for c in range(d_ff // step):
        sl = slice(c * step, (c + 1) * step)
        gate = jnp.dot(xb, wg_ref[:, sl], preferred_element_type=F32)
        up = jnp.dot(xb, wu_ref[:, sl], preferred_element_type=F32)
        h_scr[:, sl] = (jax.nn.silu(gate) * up).astype(BF16)
    return jnp.dot(h_scr[...], wd_ref[...], preferred_element_type=F32)


def _pre_kernel(x_ref, wg_ref, wu_ref, wd_ref, g_ref, b_ref, win_ref,
                x1_ref, aq_ref, ak_ref, av_ref, gq_ref, z_ref, ba_ref, h_scr, *, alpha):
    x = x_ref[...]
    ff = _swiglu(x.astype(BF16), wg_ref, wu_ref, wd_ref, h_scr)
    x1 = _layernorm(alpha * x + 0.5 * ff, g_ref[...], b_ref[...])
    x1_ref[...] = x1
    xb = x1.astype(BF16)
    col = 0
    for ref in (aq_ref, ak_ref, av_ref, gq_ref, z_ref, ba_ref):
        width = ref.shape[1]
        ref[...] = jnp.dot(xb, win_ref[:, col:col + width], preferred_element_type=F32)
        col += width


def _pre_call(x2d, wg, wu, wd, g, b, win, alpha):
    m, d = x2d.shape
    d_ff = wg.shape[1]
    tm = TOKEN_TILE
    assert m % tm == 0
    widths = (HEAD_WIDTH, HEAD_WIDTH, HEAD_WIDTH, 3 * HEAD_WIDTH, HEAD_WIDTH, LANES)
    assert sum(widths) == win.shape[1]
    tile = lambda w: pl.BlockSpec((tm, w), lambda i: (i, 0))
    return pl.pallas_call(
        functools.partial(_pre_kernel, alpha=alpha),
        grid=(m // tm,),
        in_specs=[tile(d), _resident(wg.shape), _resident(wu.shape), _resident(wd.shape),
                  _resident(g.shape), _resident(b.shape), _resident(win.shape)],
        out_specs=[tile(d)] + [tile(w) for w in widths],
        out_shape=[jax.ShapeDtypeStruct((m, d), F32)] + [jax.ShapeDtypeStruct((m, w), F32) for w in widths],
        scratch_shapes=[pltpu.VMEM((tm, d_ff), BF16)],
        compiler_params=pltpu.CompilerParams(dimension_semantics=("parallel",), vmem_limit_bytes=VMEM_LIMIT),
        name="pre_ffn_proj",
    )(x2d, wg, wu, wd, g, b, win)


def _post_kernel(attn_ref, gdn_ref, x1_ref, wo_ref, g2_ref, b2_ref, wg_ref, wu_ref, wd_ref, g3_ref, b3_ref,
                 y_ref, h_scr, *, alpha):
    mix = jnp.dot(attn_ref[...].astype(BF16), wo_ref[0:HEAD_WIDTH, :], preferred_element_type=F32)
    mix = mix + jnp.dot(gdn_ref[...].astype(BF16), wo_ref[HEAD_WIDTH:2 * HEAD_WIDTH, :],
                        preferred_element_type=F32)
    x2 = _layernorm(alpha * x1_ref[...] + mix, g2_ref[...], b2_ref[...])
    ff = _swiglu(x2.astype(BF16), wg_ref, wu_ref, wd_ref, h_scr)
    y_ref[...] = _layernorm(alpha * x2 + 0.5 * ff, g3_ref[...], b3_ref[...])


def _post_call(attn, gdn, x1, wo, g2, b2, wg, wu, wd, g3, b3, alpha):
    m, d = x1.shape
    tm = TOKEN_TILE
    assert m % tm == 0
    tile = lambda w: pl.BlockSpec((tm, w), lambda i: (i, 0))
    consts = (wo, g2, b2, wg, wu, wd, g3, b3)
    return pl.pallas_call(
        functools.partial(_post_kernel, alpha=alpha),
        grid=(m // tm,),
        in_specs=[tile(HEAD_WIDTH), tile(HEAD_WIDTH), tile(d)] + [_resident(c.shape) for c in consts],
        out_specs=tile(d),
        out_shape=jax.ShapeDtypeStruct((m, d), F32),
        scratch_shapes=[pltpu.VMEM((tm, wg.shape[1]), BF16)],
        compiler_params=pltpu.CompilerParams(dimension_semantics=("parallel",), vmem_limit_bytes=VMEM_LIMIT),
        name="post_out_ffn",
    )(attn, gdn, x1, *consts)


def _t5_bucket_np(dist):
    n = np.maximum(dist, 0)
    nf = np.maximum(n, 1).astype(np.float32)
    large = T5_MAX_EXACT + (np.log(nf / np.float32(T5_MAX_EXACT)) / np.float32(math.log(T5_MAX_DIST / T5_MAX_EXACT))
                            * np.float32(T5_BUCKETS - T5_MAX_EXACT)).astype(np.int32)
    large = np.minimum(large, T5_BUCKETS - 1)
    return np.where(n < T5_MAX_EXACT, n, large).astype(np.int32)


def _t5_table_kernel(rb_ref, bk_ref, out_ref):
    for h in range(N_HEADS):
        bk = bk_ref[0, h % bk_ref.shape[1]]
        acc = jnp.full(bk.shape, NEG_BIG, F32)
        for b in range(T5_BUCKETS):
            acc = jnp.where(bk == b, rb_ref[b, h], acc)
        out_ref[0, h] = acc


def _t5_table_call(rel_bias, buckets):
    p, hb, r, c = buckets.shape
    assert hb in (1, N_HEADS)
    return pl.pallas_call(
        _t5_table_kernel,
        grid=(p,),
        in_specs=[pl.BlockSpec(memory_space=pltpu.SMEM),
                  pl.BlockSpec((1, hb, r, c), lambda i: (i, 0, 0, 0))],
        out_specs=pl.BlockSpec((1, N_HEADS, r, c), lambda i: (i, 0, 0, 0)),
        out_shape=jax.ShapeDtypeStruct((p, N_HEADS, r, c), F32),
        compiler_params=pltpu.CompilerParams(dimension_semantics=("parallel",)),
        name="t5_bias_table",
    )(rel_bias, buckets)


def _prompt_buckets():
    nb = KEYS_PER_BLOCK
    qi = np.arange(nb)[:, None]
    ki = np.arange(2 * nb)[None, :]
    dist = qi + nb - ki
    valid = (dist >= 0) & (dist <= nb)
    out = []
    for window, dil in DILATED_PATTERNS:
        assert window // dil == nb
        out.append(np.where(valid, _t5_bucket_np(dist * dil), -1))
    return np.stack(out).astype(np.int32)[:, None]


def _nt_dot(a, b):
    return lax.dot_general(a, b, (((1,), (1,)), ((), ())), preferred_element_type=F32)


def _attn_prompt_kernel(q_ref, k_ref, v_ref, tab_ref, o_ref, og_scr, *, seq):
    nb = KEYS_PER_BLOCK
    hd = HEAD_DIM
    n_pat = len(DILATED_PATTERNS)
    heads = LANES // hd
    for g, (window, dil) in enumerate(DILATED_PATTERNS):
        n_blocks = seq // (dil * nb)

        def blocks(starts, with_prev, g=g, dil=dil):
            row_sl, qs, ks, vs, kps, vps = [], [], [], [], [], []
            for start in starts:
                rows = pl.ds(start, nb, stride=dil) if dil > 1 else pl.ds(start, nb)
                row_sl.append(rows)
                qs.append(q_ref[0, rows, :] * hd ** -0.5)
                ks.append(k_ref[0, rows, :])
                vs.append(v_ref[0, rows, :])
                if with_prev:
                    rows_p = (pl.ds(start - dil * nb, nb, stride=dil) if dil > 1 else pl.ds(start - nb, nb))
                    kps.append(k_ref[0, rows_p, :])
                    vps.append(v_ref[0, rows_p, :])
            chains = [(u, hh) for u in range(len(starts)) for hh in range(heads)]
            ln = lambda hh: slice(hd * hh, hd * (hh + 1))
            s_c, s_p, m, p_c, p_p, den, acc = {}, {}, {}, {}, {}, {}, {}
            for c in chains:
                u, hh = c
                qb = qs[u][:, ln(hh)].astype(BF16)
                s_c[c] = _nt_dot(qb, ks[u][:, ln(hh)].astype(BF16)) + tab_ref[g, hh, :, nb:2 * nb]
                if with_prev:
                    s_p[c] = _nt_dot(qb, kps[u][:, ln(hh)].astype(BF16)) + tab_ref[g, hh, :, 0:nb]
            for c in chains:
                m[c] = jnp.max(s_c[c], axis=-1, keepdims=True)
                if with_prev:
                    m[c] = jnp.maximum(m[c], jnp.max(s_p[c], axis=-1, keepdims=True))
            for c in chains:
                p_c[c] = jnp.exp(s_c[c] - m[c])
                den[c] = jnp.sum(p_c[c], axis=-1, keepdims=True)
                if with_prev:
                    p_p[c] = jnp.exp(s_p[c] - m[c])
                    den[c] = den[c] + jnp.sum(p_p[c], axis=-1, keepdims=True)
            for c in chains:
                u, hh = c
                acc[c] = jnp.dot(p_c[c].astype(BF16), vs[u][:, ln(hh)].astype(BF16), preferred_element_type=F32)
                if with_prev:
                    acc[c] = acc[c] + jnp.dot(p_p[c].astype(BF16), vps[u][:, ln(hh)].astype(BF16),
                                              preferred_element_type=F32)
            for c in chains:
                u, hh = c
                lse = m[c] + jnp.log(den[c])
                og_scr[g * heads + hh, row_sl[u], :] = jnp.concatenate(
                    [acc[c] / den[c], jnp.broadcast_to(lse, (nb, hd))], axis=-1)

        def unroll_of(count):
            return next(u for u in (4, 3, 2, 1) if count % u == 0)

        u_first = unroll_of(dil)

        def first_body(i, carry, blocks=blocks, u_first=u_first):
            blocks([i * u_first + j for j in range(u_first)], False)
            return carry

        lax.fori_loop(0, dil // u_first, first_body, 0)
        if n_blocks > 1:
            n_rest = dil * (n_blocks - 1)
            u_rest = unroll_of(n_rest)

            def rest_body(i, carry, blocks=blocks, dil=dil, n_blocks=n_blocks, u_rest=u_rest):
                starts = []
                for j in range(u_rest):
                    idx = i * u_rest + j
                    starts.append(idx // (n_blocks - 1) + dil * nb * (idx % (n_blocks - 1) + 1))
                blocks(starts, True)
                return carry

            lax.fori_loop(0, n_rest // u_rest, rest_body, 0)

    rb = 2 * nb

    def merge_body(i, carry):
        rows = pl.ds(pl.multiple_of(i * rb, rb), rb)
        res = []
        for hh in range(heads):
            tiles = [og_scr[g * heads + hh, rows, :] for g in range(n_pat)]
            lses = [t[:, hd:2 * hd] for t in tiles]
            top = functools.reduce(jnp.maximum, lses)
            ws = [jnp.exp(l - top) for l in lses]
            num = functools.reduce(lambda a, b: a + b, [w * t[:, 0:hd] for w, t in zip(ws, tiles)])
            res.append(num / functools.reduce(lambda a, b: a + b, ws))
        o_ref[0, rows, :] = jnp.concatenate(res, axis=-1)
        return carry

    lax.fori_loop(0, seq // rb, merge_body, 0)


def _attn_prompt_call(aq, ak, av, tab):
    bsz, seq, width = aq.shape
    assert width == HEAD_WIDTH and seq % (DILATED_PATTERNS[-1][1] * KEYS_PER_BLOCK) == 0
    n_pat = len(DILATED_PATTERNS)
    heads_per_step = LANES // HEAD_DIM
    qkv_spec = pl.BlockSpec((1, seq, LANES), lambda h, b: (b, 0, h))
    return pl.pallas_call(
        functools.partial(_attn_prompt_kernel, seq=seq),
        grid=(N_HEADS // heads_per_step, bsz),
        in_specs=[qkv_spec, qkv_spec, qkv_spec,
                  pl.BlockSpec((n_pat, heads_per_step, KEYS_PER_BLOCK, 2 * KEYS_PER_BLOCK),
                               lambda h, b: (0, h, 0, 0))],
        out_specs=pl.BlockSpec((1, seq, LANES), lambda h, b: (b, 0, h)),
        out_shape=jax.ShapeDtypeStruct((bsz, seq, width), F32),
        scratch_shapes=[pltpu.VMEM((n_pat * heads_per_step, seq, LANES), F32)],
        compiler_params=pltpu.CompilerParams(dimension_semantics=("parallel", "parallel"),
                                             vmem_limit_bytes=VMEM_LIMIT),
        name="attn_prompt",
    )(aq, ak, av, tab)


SAMPLE_NEW_POSITIONS = LANES // N_HEADS


def _sample_layout(w_buf, t_new):
    period = DILATED_PATTERNS[-1][1]
    near_pos = DILATED_PATTERNS[-2][0]
    assert w_buf % period == 0 and near_pos % period == 0 and t_new <= period and near_pos <= w_buf
    assert (t_new * N_HEADS) % SUBLANES == 0 and t_new <= SAMPLE_T_PAD <= SAMPLE_NEW_POSITIONS
    n_far_groups = (w_buf - near_pos) // period
    far_pos = n_far_groups * t_new
    n_pos = far_pos + near_pos + SAMPLE_NEW_POSITIONS
    row_of_pos = np.full((n_pos,), -1, np.int64)
    c = np.arange(far_pos)
    row_of_pos[:far_pos] = (c // t_new) * period + c % t_new
    row_of_pos[far_pos:far_pos + near_pos] = w_buf - near_pos + np.arange(near_pos)
    row_of_pos[far_pos + near_pos:far_pos + near_pos + t_new] = w_buf + np.arange(t_new)
    per_pos = np.full((len(DILATED_PATTERNS), SAMPLE_T_PAD, n_pos), -1, np.int32)
    for g, (window, dil) in enumerate(DILATED_PATTERNS):
        for t in range(t_new):
            dist = w_buf + t - row_of_pos
            ok = (row_of_pos >= 0) & (dist >= 0) & (dist <= window) & (dist % dil == 0)
            assert int(ok.sum()) == min(window, w_buf + t) // dil + 1, (g, t, int(ok.sum()))
            per_pos[g, t] = np.where(ok, _t5_bucket_np(dist), -1)
    same_head = np.eye(N_HEADS, dtype=bool)
    buckets = np.where(same_head[None, :, None, None, :], per_pos[:, None, :, :, None], -1)
    buckets = buckets.reshape(len(DILATED_PATTERNS), N_HEADS, SAMPLE_T_PAD, n_pos * N_HEADS).astype(np.int32)
    return n_far_groups, far_pos * N_HEADS, near_pos * N_HEADS, buckets


def _attn_sample_kernel(q_ref, kn_ref, vn_ref, kfar_ref, knear_ref, vfar_ref, vnear_ref, tab_ref,
                        o_ref, *, far_cols, near_cols):
    tp = SAMPLE_T_PAD
    hd = HEAD_DIM
    add = lambda x, y: x + y
    q8 = q_ref[0] * hd ** -0.5
    a = jnp.concatenate([q8[:, hd * h:hd * (h + 1)] for h in range(N_HEADS)], axis=0)
    k_parts = (kfar_ref[0].reshape(far_cols, hd), knear_ref[0].reshape(near_cols, hd), kn_ref[0])
    v_parts = (vfar_ref[0].reshape(far_cols, hd), vnear_ref[0].reshape(near_cols, hd), vn_ref[0])
    edges = (0, far_cols, far_cols + near_cols, far_cols + near_cols + LANES)
    scores = [_nt_dot(a, kp) for kp in k_parts]
    ps, dens, lses, part_ids = [], [], [], []
    for g, (window, dil) in enumerate(DILATED_PATTERNS):
        parts = [i for i in range(3) if not (i == 0 and window * N_HEADS <= near_cols)]
        logit = [scores[i] + tab_ref[g, :, edges[i]:edges[i + 1]] for i in parts]
        m = functools.reduce(jnp.maximum, [jnp.max(l, axis=-1, keepdims=True) for l in logit])
        p = [jnp.exp(l - m) for l in logit]
        den = functools.reduce(add, [jnp.sum(x, axis=-1, keepdims=True) for x in p])
        ps.append(p)
        dens.append(den)
        lses.append(m + jnp.log(den))
        part_ids.append(parts)
    top = functools.reduce(jnp.maximum, lses)
    ws = [jnp.exp(l - top) for l in lses]
    w_sum = functools.reduce(add, ws)
    coef = [w / (w_sum * den) for w, den in zip(ws, dens)]
    acc = None
    for i in range(3):
        terms = [coef[g] * ps[g][part_ids[g].index(i)] for g in range(len(DILATED_PATTERNS)) if i in part_ids[g]]
        out_i = jnp.dot(functools.reduce(add, terms), v_parts[i], preferred_element_type=F32)
        acc = out_i if acc is None else acc + out_i
    o_ref[0] = jnp.concatenate([acc[h * tp:(h + 1) * tp, :] for h in range(N_HEADS)], axis=1)


def _attn_sample_call(aq8, k_new, v_new, cache_k, cache_v, tab, layout):
    n_far_groups, far_cols, near_cols, _ = layout
    bsz, w_buf, n_heads, hd = cache_k.shape
    period = DILATED_PATTERNS[-1][1]
    group_rows = period * n_heads
    ck = cache_k.reshape(bsz, w_buf // period, group_rows, hd)
    cv = cache_v.reshape(bsz, w_buf // period, group_rows, hd)
    n_near_groups = near_cols // group_rows
    assert n_far_groups % n_near_groups == 0 and far_cols % n_far_groups == 0
    q_spec = pl.BlockSpec((1, SAMPLE_T_PAD, n_heads * hd), lambda b: (b, 0, 0))
    new_spec = pl.BlockSpec((1, LANES, hd), lambda b: (b, 0, 0))
    far_spec = pl.BlockSpec((1, n_far_groups, far_cols // n_far_groups, hd), lambda b: (b, 0, 0, 0))
    near_spec = pl.BlockSpec((1, n_near_groups, group_rows, hd),
                             lambda b: (b, n_far_groups // n_near_groups, 0, 0))
    return pl.pallas_call(
        functools.partial(_attn_sample_kernel, far_cols=far_cols, near_cols=near_cols),
        grid=(bsz,),
        in_specs=[q_spec, new_spec, new_spec, far_spec, near_spec, far_spec, near_spec, _resident(tab.shape)],
        out_specs=q_spec,
        out_shape=jax.ShapeDtypeStruct((bsz, SAMPLE_T_PAD, n_heads * hd), F32),
        compiler_params=pltpu.CompilerParams(dimension_semantics=("parallel",), vmem_limit_bytes=VMEM_LIMIT),
        name="attn_sample",
    )(aq8, k_new, v_new, ck, ck, cv, cv, tab)


def _aligned(index, size):
    return index * size if isinstance(index, int) else pl.multiple_of(index * size, size)


def _split3(x):
    h1 = x.astype(BF16)
    r1 = x - h1.astype(F32)
    h2 = r1.astype(BF16)
    h3 = (r1 - h2.astype(F32)).astype(BF16)
    return h1, h2, h3


def _head_sums(x, ones_ref):
    h1, h2, _ = _split3(x)
    return (jnp.dot(h1, ones_ref[...], preferred_element_type=F32)
            + jnp.dot(h2, ones_ref[...], preferred_element_type=F32))


def _gdn_kernel(x_ref, z_ref, ba_ref, cb_ref, s0_ref, cw_ref, hp_ref, nw_ref, ones_ref,
                o_ref, sfin_ref,
                ext_scr, q_scr, k_scr, v_scr, oc_scr, bt_scr, g_scr, s_scr,
                uw_scr, in_scr, qd_scr, kd_scr, el_scr, *, chunk, n_valid, group):
    n_bb, tb = x_ref.shape[0], x_ref.shape[1]
    n_chunks = tb // chunk
    hw = HEAD_WIDTH
    hd = HEAD_DIM
    t = pl.program_id(1)
    hist = SUBLANES

    @pl.when(t == 0)
    def _():
        ext_scr[:, 0:hist, :] = cb_ref[...]
        s_scr[...] = s0_ref[...]

    for bb in range(n_bb):
        ext_scr[bb, hist:hist + tb, :] = x_ref[bb]
        conv = jnp.zeros((tb, 3 * hw), F32)
        for j in range(CONV_WIDTH):
            off = hist - (CONV_WIDTH - 1) + j
            conv = conv + ext_scr[bb, off:off + tb, :] * cw_ref[j:j + 1, :]
        ext_scr[bb, 0:hist, :] = ext_scr[bb, tb:tb + hist, :]
        act = jax.nn.silu(conv)
        q = act[:, 0:hw]
        k = act[:, hw:2 * hw]
        q_scr[bb] = q * lax.rsqrt(_head_sums(q * q, ones_ref) + RMS_EPS) * hd ** -0.5
        k_scr[bb] = k * lax.rsqrt(_head_sums(k * k, ones_ref) + RMS_EPS)
        v_scr[bb] = act[:, 2 * hw:3 * hw]

        ba = ba_ref[bb]
        row = t * tb + lax.broadcasted_iota(jnp.int32, ba.shape, 0)
        live = row < n_valid
        bt_scr[bb] = jnp.where(live, jax.nn.sigmoid(ba), 0.0)
        g_scr[bb] = jnp.where(live, -jnp.exp(hp_ref[0:1, :]) * jax.nn.softplus(ba + hp_ref[1:2, :]), 0.0)

    ri = lax.broadcasted_iota(jnp.int32, (chunk, chunk), 0)
    ci = lax.broadcasted_iota(jnp.int32, (chunk, chunk), 1)
    tril = ri >= ci
    strict = ri > ci
    eye_c = (ri == ci).astype(F32)
    tril_b = tril.astype(BF16)
    eye_l = (lax.broadcasted_iota(jnp.int32, (LANES, LANES), 0)
             == lax.broadcasted_iota(jnp.int32, (LANES, LANES), 1)).astype(BF16)
    n_doublings = int(math.log2(chunk)) - 1
    assert 2 ** (n_doublings + 1) == chunk

    dot = functools.partial(jnp.dot, preferred_element_type=F32)

    def state_free_part(gi, carry):
        probs, shared = [], {}
        for bb in range(n_bb):
            for cg in range(group):
                c_idx = gi * group + cg
                rows = pl.ds(_aligned(c_idx, chunk), chunk)
                gc = functools.reduce(lambda a, b: a + b,
                                      [dot(tril_b, p) for p in _split3(g_scr[bb, rows, :])])
                gct = functools.reduce(lambda a, b: a + b, [_nt_dot(eye_l, p) for p in _split3(gc)])
                g_last = gc[chunk - 1:chunk, :]
                el_rows = pl.ds(_aligned(c_idx, SUBLANES), SUBLANES)
                el_scr[bb, el_rows, :] = jnp.broadcast_to(jnp.exp(g_last), (SUBLANES, LANES))
                shared[bb, cg] = (rows, gc, gct, jnp.exp(gc), jnp.exp(g_last - gc), bt_scr[bb, rows, :])
                probs += [(bb, cg, h) for h in range(N_HEADS)]
        kk, qk, rhs, decay, a, t_inv, pw = {}, {}, {}, {}, {}, {}, {}
        for p in probs:
            bb, cg, h = p
            rows, gc, gct, e_gc, e_rem, bt = shared[bb, cg]
            hl = pl.ds(hd * h, hd)
            gl = N_HEADS + h
            qh = q_scr[bb, rows, hl]
            kh = k_scr[bb, rows, hl]
            beta = bt[:, h:h + 1]
            kb = kh * beta
            kk[p] = _nt_dot(kb, kh)
            qk[p] = _nt_dot(qh, kh)
            rhs[p] = jnp.concatenate([v_scr[bb, rows, hl] * beta, kb * e_gc[:, gl:gl + 1]], axis=1)
            qd_scr[bb, rows, hl] = qh * e_gc[:, gl:gl + 1]
            kd_scr[bb, rows, hl] = kh * e_rem[:, gl:gl + 1]
            diff = gc[:, gl:gl + 1] - gct[gl:gl + 1, :]
            decay[p] = jnp.where(tril, jnp.exp(jnp.where(tril, diff, 0.0)), 0.0)
        for p in probs:
            bb, cg, h = p
            rows = shared[bb, cg][0]
            a[p] = jnp.where(strict, kk[p] * decay[p], 0.0)
            in_scr[bb, rows, LANES * h:LANES * h + chunk] = jnp.where(tril, qk[p] * decay[p], 0.0)
            t_inv[p] = eye_c - a[p]
        for p in probs:
            pw[p] = dot(a[p], a[p])
        for _ in range(n_doublings - 1):
            nxt_pw, nxt_t = {}, {}
            for p in probs:
                nxt_pw[p] = dot(pw[p], pw[p])
                nxt_t[p] = dot(t_inv[p], eye_c + pw[p])
            pw, t_inv = nxt_pw, nxt_t
        for p in probs:
            t_inv[p] = dot(t_inv[p], eye_c + pw[p])
        for p in probs:
            bb, cg, h = p
            rows = shared[bb, cg][0]
            uw_scr[bb, rows, LANES * h:LANES * (h + 1)] = dot(t_inv[p], rhs[p])
        return carry

    def state_part(ci_, carry):
        rows = pl.ds(_aligned(ci_, chunk), chunk)
        el_rows = pl.ds(_aligned(ci_, SUBLANES), SUBLANES)
        probs = [(bb, h) for bb in range(n_bb) for h in range(N_HEADS)]
        s_old, uw, both, v_new, o_in, s_add = {}, {}, {}, {}, {}, {}
        for p in probs:
            bb, h = p
            s_old[p] = s_scr[bb, h]
            uw[p] = uw_scr[bb, rows, LANES * h:LANES * (h + 1)]
            lhs = jnp.concatenate([uw[p][:, hd:2 * hd], qd_scr[bb, rows, pl.ds(hd * h, hd)]], axis=0)
            both[p] = dot(lhs, s_old[p])
        for p in probs:
            bb, h = p
            v_new[p] = uw[p][:, 0:hd] - both[p][0:chunk]
            o_in[p] = dot(in_scr[bb, rows, LANES * h:LANES * h + chunk], v_new[p])
            s_add[p] = lax.dot_general(kd_scr[bb, rows, pl.ds(hd * h, hd)], v_new[p],
                                       (((0,), (0,)), ((), ())), preferred_element_type=F32)
        for p in probs:
            bb, h = p
            e_last = el_scr[bb, el_rows, :][0:1, N_HEADS + h:N_HEADS + h + 1]
            oc_scr[bb, rows, pl.ds(hd * h, hd)] = both[p][chunk:2 * chunk] + o_in[p]
            s_scr[bb, h] = s_old[p] * e_last + s_add[p]
        return carry

    assert n_chunks % group == 0
    if n_chunks == group:
        state_free_part(0, 0)
    else:
        lax.fori_loop(0, n_chunks // group, state_free_part, 0)
    if n_chunks == 1:
        state_part(0, 0)
    else:
        lax.fori_loop(0, n_chunks, state_part, 0)

    for bb in range(n_bb):
        o = oc_scr[bb]
        o = o * lax.rsqrt(_head_sums(o * o, ones_ref) * (1.0 / hd) + RMS_EPS) * nw_ref[...]
        o_ref[bb] = o * jax.nn.silu(z_ref[bb])

    @pl.when(t == pl.num_programs(1) - 1)
    def _():
        sfin_ref[...] = s_scr[...]


def _gdn_call(gq, z, ba, conv_buf8, s0, conv_w8, head_params, norm_w_row, n_valid, token_block, chunk,
              batch_block, group):
    bsz, seq, w3 = gq.shape
    hw = HEAD_WIDTH
    assert w3 == 3 * hw and seq % token_block == 0 and token_block % chunk == 0 and bsz % batch_block == 0
    tb = token_block
    nbb = batch_block
    ones_bd = np.kron(np.eye(N_HEADS, dtype=np.float32), np.ones((HEAD_DIM, HEAD_DIM), np.float32))
    blk = lambda w: pl.BlockSpec((nbb, tb, w), lambda b, t: (b, t, 0))
    per_b3 = lambda s: pl.BlockSpec((nbb,) + s, lambda b, t: (b,) + (0,) * len(s))
    return pl.pallas_call(
        functools.partial(_gdn_kernel, chunk=chunk, n_valid=n_valid, group=group),
        grid=(bsz // nbb, seq // tb),
        in_specs=[blk(w3), blk(hw), blk(LANES), per_b3((SUBLANES, w3)), per_b3((N_HEADS, HEAD_DIM, HEAD_DIM)),
                  _resident(conv_w8.shape), _resident(head_params.shape), _resident(norm_w_row.shape),
                  _resident(ones_bd.shape)],
        out_specs=[blk(hw), per_b3((N_HEADS, HEAD_DIM, HEAD_DIM))],
        out_shape=[jax.ShapeDtypeStruct((bsz, seq, hw), F32),
                   jax.ShapeDtypeStruct((bsz, N_HEADS, HEAD_DIM, HEAD_DIM), F32)],
        scratch_shapes=[pltpu.VMEM((nbb, tb + 2 * SUBLANES, w3), F32)]
                       + [pltpu.VMEM((nbb, tb, hw), F32)] * 4
                       + [pltpu.VMEM((nbb, tb, LANES), F32)] * 2
                       + [pltpu.VMEM((nbb, N_HEADS, HEAD_DIM, HEAD_DIM), F32)]
                       + [pltpu.VMEM((nbb, tb, N_HEADS * LANES), F32)] * 2
                       + [pltpu.VMEM((nbb, tb, hw), F32)] * 2
                       + [pltpu.VMEM((nbb, (tb // chunk) * SUBLANES, LANES), F32)],
        compiler_params=pltpu.CompilerParams(dimension_semantics=("parallel", "arbitrary"),
                                             vmem_limit_bytes=VMEM_LIMIT),
        name="gdn",
    )(gq, z, ba, conv_buf8, s0, conv_w8, head_params, norm_w_row, jnp.asarray(ones_bd, BF16))


def _pad_rows(x, rows, front=False):
    extra = rows - x.shape[1]
    cfg = [(0, 0)] * x.ndim
    cfg[1] = (extra, 0) if front else (0, extra)
    return jnp.pad(x, cfg)


def kernel(x_prompt, x_sample, cache_attn_k, cache_attn_v, state_gdn, state_conv, rel_bias, ln1_g, ln1_b,
           ffn1_w_gate, ffn1_w_up, ffn1_w_down, w_in, w_out, gdn_conv_w, gdn_a_log, gdn_dt_bias, gdn_norm_w,
           ln2_g, ln2_b, ffn2_w_gate, ffn2_w_up, ffn2_w_down, ln3_g, ln3_b):
    depth = w_in.shape[0]
    alpha = (2.0 * depth) ** 0.25
    bsz, seq, d_model = x_prompt.shape
    dbsz, dseq, _ = x_sample.shape
    w_buf = cache_attn_k.shape[2]
    hw = HEAD_WIDTH
    in_cols = w_in.shape[2]
    assert in_cols == 7 * hw + 2 * N_HEADS and CONV_WIDTH - 1 <= dseq <= SAMPLE_T_PAD and seq >= CONV_WIDTH - 1

    layout = _sample_layout(w_buf, dseq)
    tab_p = _t5_table_call(rel_bias, jnp.asarray(_prompt_buckets()))
    tab_s = _t5_table_call(rel_bias, jnp.asarray(layout[3]))
    tab_s = tab_s.reshape(tab_s.shape[0], N_HEADS * SAMPLE_T_PAD, tab_s.shape[3])

    yp = x_prompt.reshape(bsz * seq, d_model)
    ys = x_sample.reshape(dbsz * dseq, d_model)
    collected = [[] for _ in range(8)]
    row = lambda v: v.reshape(1, -1)
    for layer in range(depth):
        wg1, wu1, wd1 = (w[layer].astype(BF16) for w in (ffn1_w_gate, ffn1_w_up, ffn1_w_down))
        wg2, wu2, wd2 = (w[layer].astype(BF16) for w in (ffn2_w_gate, ffn2_w_up, ffn2_w_down))
        win = jnp.pad(w_in[layer], ((0, 0), (0, 7 * hw + LANES - in_cols))).astype(BF16)
        wo = w_out[layer].astype(BF16)
        conv_w8 = jnp.pad(gdn_conv_w[layer], ((0, SUBLANES - CONV_WIDTH), (0, 0)))
        head_params = jnp.zeros((SUBLANES, LANES), F32)
        head_params = head_params.at[0, N_HEADS:2 * N_HEADS].set(gdn_a_log[layer])
        head_params = head_params.at[1, N_HEADS:2 * N_HEADS].set(gdn_dt_bias[layer])
        norm_w_row = jnp.tile(gdn_norm_w[layer], N_HEADS).reshape(1, hw)
        pre = functools.partial(_pre_call, wg=wg1, wu=wu1, wd=wd1, g=row(ln1_g[layer]), b=row(ln1_b[layer]),
                                win=win, alpha=alpha)
        post = functools.partial(_post_call, wo=wo, g2=row(ln2_g[layer]), b2=row(ln2_b[layer]), wg=wg2, wu=wu2,
                                 wd=wd2, g3=row(ln3_g[layer]), b3=row(ln3_b[layer]), alpha=alpha)

        x1, aq, ak, av, gq, z, ba = pre(yp)
        shp = lambda a: a.reshape(bsz, seq, a.shape[1])
        attn = _attn_prompt_call(shp(aq), shp(ak), shp(av), tab_p)
        gdn, s_p = _gdn_call(shp(gq), shp(z), shp(ba),
                             jnp.zeros((bsz, SUBLANES, 3 * hw), F32),
                             jnp.zeros((bsz, N_HEADS, HEAD_DIM, HEAD_DIM), F32),
                             conv_w8, head_params, norm_w_row, n_valid=seq,
                             token_block=min(GDN_TOKEN_BLOCK, seq), chunk=min(GDN_CHUNK, seq),
                             batch_block=1, group=GDN_CHUNK_GROUP)
        yp = post(attn.reshape(bsz * seq, hw), gdn.reshape(bsz * seq, hw), x1)
        wp = min(w_buf, seq)
        heads5 = lambda a: a.reshape(bsz, seq, N_HEADS, HEAD_DIM)[:, seq - wp:]
        st_p = (heads5(ak), heads5(av), s_p, shp(gq)[:, seq - (CONV_WIDTH - 1):])

        x1s, aqs, aks, avs, gqs, zs, bas = pre(ys)
        shs = lambda a: _pad_rows(a.reshape(dbsz, dseq, a.shape[1]), SAMPLE_T_PAD)
        heads5s = lambda a: a.reshape(dbsz, dseq, N_HEADS, HEAD_DIM)
        new_rows = lambda a: _pad_rows(heads5s(a).reshape(dbsz, dseq * N_HEADS, HEAD_DIM), LANES)
        attn_s = _attn_sample_call(shs(aqs), new_rows(aks), new_rows(avs), cache_attn_k[layer],
                                   cache_attn_v[layer], tab_s, layout)
        gdn_s, s_s = _gdn_call(shs(gqs), shs(zs), shs(bas),
                               _pad_rows(state_conv[layer], SUBLANES, front=True), state_gdn[layer],
                               conv_w8, head_params, norm_w_row, n_valid=dseq,
                               token_block=SAMPLE_T_PAD, chunk=SAMPLE_T_PAD,
                               batch_block=GDN_SAMPLE_BATCH_BLOCK, group=1)
        unpad = lambda a: a[:, :dseq].reshape(dbsz * dseq, hw)
        ys = post(unpad(attn_s), unpad(gdn_s), x1s)
        st_s = (heads5s(aks), heads5s(avs), s_s,
                gqs.reshape(dbsz, dseq, 3 * hw)[:, dseq - (CONV_WIDTH - 1):])
        for lst, st in zip(collected, st_p + st_s):
            lst.append(st)
    outs = [jnp.stack(t, axis=0) for t in collected]
    return (yp.reshape(bsz, seq, d_model), ys.reshape(dbsz, dseq, d_model)) + tuple(outs)
```

```python
import functools
import math

import numpy as np
import jax
import jax.numpy as jnp
from jax import lax
from jax.experimental import pallas as pl
from jax.experimental.pallas import tpu as pltpu

F32 = jnp.float32
BF16 = jnp.bfloat16

HEAD_DIM = 64
N_HEADS = 8
HEAD_WIDTH = N_HEADS * HEAD_DIM
DILATED_PATTERNS = ((128, 1), (512, 4), (2048, 16))
KEYS_PER_BLOCK = 128
T5_BUCKETS = 32
T5_MAX_EXACT = 16
T5_MAX_DIST = 2048
CONV_WIDTH = 4
GDN_CHUNK = 64
LN_EPS = 1e-5
RMS_EPS = 1e-6
NEG_BIG = -1e30

LANES = 128
SUBLANES = 8
TOKEN_TILE = 256
GDN_TOKEN_BLOCK = 256
GDN_CHUNK_GROUP = 2
GDN_SAMPLE_BATCH_BLOCK = 8
SAMPLE_T_PAD = 8
VMEM_LIMIT = 56 * 1024 * 1024


def _resident(shape):
    nd = len(shape)
    return pl.BlockSpec(shape, lambda *_: (0,) * nd, pipeline_mode=pl.Buffered(1))


def _layernorm(y, g, b):
    mu = jnp.mean(y, axis=-1, keepdims=True)
    yc = y - mu
    var = jnp.mean(yc * yc, axis=-1, keepdims=True)
    return yc * lax.rsqrt(var + LN_EPS) * g + b


def _swiglu(xb, wg_ref, wu_ref, wd_ref, h_scr):
    d_ff = wg_ref.shape[1]
    step = 2 * LANES
    assert d_ff % step == 0
    for c in range(d_ff // step):
        sl = slice(c * step, (c + 1) * step)
        gate = jnp.dot(xb, wg_ref[:, sl], preferred_element_type=F32)
        up = jnp.dot(xb, wu_ref[:, sl], preferred_element_type=F32)
        h_scr[:, sl] = (jax.nn.silu(gate) * up).astype(BF16)
    return jnp.dot(h_scr[...], wd_ref[...], preferred_element_type=F32)


def _pre_kernel(x_ref, wg_ref, wu_ref, wd_ref, g_ref, b_ref, win_ref,
                x1_ref, aq_ref, ak_ref, av_ref, gq_ref, z_ref, ba_ref, h_scr, *, alpha):
    x = x_ref[...]
    ff = _swiglu(x.astype(BF16), wg_ref, wu_ref, wd_ref, h_scr)
    x1 = _layernorm(alpha * x + 0.5 * ff, g_ref[...], b_ref[...])
    x1_ref[...] = x1
    xb = x1.astype(BF16)
    col = 0
    for ref in (aq_ref, ak_ref, av_ref, gq_ref, z_ref, ba_ref):
        width = ref.shape[1]
        ref[...] = jnp.dot(xb, win_ref[:, col:col + width], preferred_element_type=F32)
        col += width


def _pre_call(x2d, wg, wu, wd, g, b, win, alpha):
    m, d = x2d.shape
    d_ff = wg.shape[1]
    tm = TOKEN_TILE
    assert m % tm == 0
    widths = (HEAD_WIDTH, HEAD_WIDTH, HEAD_WIDTH, 3 * HEAD_WIDTH, HEAD_WIDTH, LANES)
    assert sum(widths) == win.shape[1]
    tile = lambda w: pl.BlockSpec((tm, w), lambda i: (i, 0))
    return pl.pallas_call(
        functools.partial(_pre_kernel, alpha=alpha),
        grid=(m // tm,),
        in_specs=[tile(d), _resident(wg.shape), _resident(wu.shape), _resident(wd.shape),
                  _resident(g.shape), _resident(b.shape), _resident(win.shape)],
        out_specs=[tile(d)] + [tile(w) for w in widths],
        out_shape=[jax.ShapeDtypeStruct((m, d), F32)] + [jax.ShapeDtypeStruct((m, w), F32) for w in widths],
        scratch_shapes=[pltpu.VMEM((tm, d_ff), BF16)],
        compiler_params=pltpu.CompilerParams(dimension_semantics=("parallel",), vmem_limit_bytes=VMEM_LIMIT),
        name="pre_ffn_proj",
    )(x2d, wg, wu, wd, g, b, win)


def _post_kernel(attn_ref, gdn_ref, x1_ref, wo_ref, g2_ref, b2_ref, wg_ref, wu_ref, wd_ref, g3_ref, b3_ref,
                 y_ref, h_scr, *, alpha):
    mix = jnp.dot(attn_ref[...].astype(BF16), wo_ref[0:HEAD_WIDTH, :], preferred_element_type=F32)
    mix = mix + jnp.dot(gdn_ref[...].astype(BF16), wo_ref[HEAD_WIDTH:2 * HEAD_WIDTH, :],
                        preferred_element_type=F32)
    x2 = _layernorm(alpha * x1_ref[...] + mix, g2_ref[...], b2_ref[...])
    ff = _swiglu(x2.astype(BF16), wg_ref, wu_ref, wd_ref, h_scr)
    y_ref[...] = _layernorm(alpha * x2 + 0.5 * ff, g3_ref[...], b3_ref[...])


def _post_call(attn, gdn, x1, wo, g2, b2, wg, wu, wd, g3, b3, alpha):
    m, d = x1.shape
    tm = TOKEN_TILE
    assert m % tm == 0
    tile = lambda w: pl.BlockSpec((tm, w), lambda i: (i, 0))
    consts = (wo, g2, b2, wg, wu, wd, g3, b3)
    return pl.pallas_call(
        functools.partial(_post_kernel, alpha=alpha),
        grid=(m // tm,),
        in_specs=[tile(HEAD_WIDTH), tile(HEAD_WIDTH), tile(d)] + [_resident(c.shape) for c in consts],
        out_specs=tile(d),
        out_shape=jax.ShapeDtypeStruct((m, d), F32),
        scratch_shapes=[pltpu.VMEM((tm, wg.shape[1]), BF16)],
        compiler_params=pltpu.CompilerParams(dimension_semantics=("parallel",), vmem_limit_bytes=VMEM_LIMIT),
        name="post_out_ffn",
    )(attn, gdn, x1, *consts)


def _t5_bucket_np(dist):
    n = np.maximum(dist, 0)
    nf = np.maximum(n, 1).astype(np.float32)
    large = T5_MAX_EXACT + (np.log(nf / np.float32(T5_MAX_EXACT)) / np.float32(math.log(T5_MAX_DIST / T5_MAX_EXACT))
                            * np.float32(T5_BUCKETS - T5_MAX_EXACT)).astype(np.int32)
    large = np.minimum(large, T5_BUCKETS - 1)
    return np.where(n < T5_MAX_EXACT, n, large).astype(np.int32)


def _t5_table_kernel(rb_ref, bk_ref, out_ref):
    for h in range(N_HEADS):
        bk = bk_ref[0, h % bk_ref.shape[1]]
        acc = jnp.full(bk.shape, NEG_BIG, F32)
        for b in range(T5_BUCKETS):
            acc = jnp.where(bk == b, rb_ref[b, h], acc)
        out_ref[0, h] = acc


def _t5_table_call(rel_bias, buckets):
    p, hb, r, c = buckets.shape
    assert hb in (1, N_HEADS)
    return pl.pallas_call(
        _t5_table_kernel,
        grid=(p,),
        in_specs=[pl.BlockSpec(memory_space=pltpu.SMEM),
                  pl.BlockSpec((1, hb, r, c), lambda i: (i, 0, 0, 0))],
        out_specs=pl.BlockSpec((1, N_HEADS, r, c), lambda i: (i, 0, 0, 0)),
        out_shape=jax.ShapeDtypeStruct((p, N_HEADS, r, c), F32),
        compiler_params=pltpu.CompilerParams(dimension_semantics=("parallel",)),
        name="t5_bias_table",
    )(rel_bias, buckets)


def _prompt_buckets():
    nb = KEYS_PER_BLOCK
    qi = np.arange(nb)[:, None]
    ki = np.arange(2 * nb)[None, :]
    dist = qi + nb - ki
    valid = (dist >= 0) & (dist <= nb)
    out = []
    for window, dil in DILATED_PATTERNS:
        assert window // dil == nb
        out.append(np.where(valid, _t5_bucket_np(dist * dil), -1))
    return np.stack(out).astype(np.int32)[:, None]


def _nt_dot(a, b):
    return lax.dot_general(a, b, (((1,), (1,)), ((), ())), preferred_element_type=F32)


def _attn_prompt_kernel(q_ref, k_ref, v_ref, tab_ref, o_ref, og_scr, *, seq):
    nb = KEYS_PER_BLOCK
    hd = HEAD_DIM
    n_pat = len(DILATED_PATTERNS)
    heads = LANES // hd
    for g, (window, dil) in enumerate(DILATED_PATTERNS):
        n_blocks = seq // (dil * nb)

        def blocks(starts, with_prev, g=g, dil=dil):
            row_sl, qs, ks, vs, kps, vps = [], [], [], [], [], []
            for start in starts:
                rows = pl.ds(start, nb, stride=dil) if dil > 1 else pl.ds(start, nb)
                row_sl.append(rows)
                qs.append(q_ref[0, rows, :] * hd ** -0.5)
                ks.append(k_ref[0, rows, :])
                vs.append(v_ref[0, rows, :])
                if with_prev:
                    rows_p = (pl.ds(start - dil * nb, nb, stride=dil) if dil > 1 else pl.ds(start - nb, nb))
                    kps.append(k_ref[0, rows_p, :])
                    vps.append(v_ref[0, rows_p, :])
            chains = [(u, hh) for u in range(len(starts)) for hh in range(heads)]
            ln = lambda hh: slice(hd * hh, hd * (hh + 1))
            s_c, s_p, m, p_c, p_p, den, acc = {}, {}, {}, {}, {}, {}, {}
            for c in chains:
                u, hh = c
                qb = qs[u][:, ln(hh)].astype(BF16)
                s_c[c] = _nt_dot(qb, ks[u][:, ln(hh)].astype(BF16)) + tab_ref[g, hh, :, nb:2 * nb]
                if with_prev:
                    s_p[c] = _nt_dot(qb, kps[u][:, ln(hh)].astype(BF16)) + tab_ref[g, hh, :, 0:nb]
            for c in chains:
                m[c] = jnp.max(s_c[c], axis=-1, keepdims=True)
                if with_prev:
                    m[c] = jnp.maximum(m[c], jnp.max(s_p[c], axis=-1, keepdims=True))
            for c in chains:
                p_c[c] = jnp.exp(s_c[c] - m[c])
                den[c] = jnp.sum(p_c[c], axis=-1, keepdims=True)
                if with_prev:
                    p_p[c] = jnp.exp(s_p[c] - m[c])
                    den[c] = den[c] + jnp.sum(p_p[c], axis=-1, keepdims=True)
            for c in chains:
                u, hh = c
                acc[c] = jnp.dot(p_c[c].astype(BF16), vs[u][:, ln(hh)].astype(BF16), preferred_element_type=F32)
                if with_prev:
                    acc[c] = acc[c] + jnp.dot(p_p[c].astype(BF16), vps[u][:, ln(hh)].astype(BF16),
                                              preferred_element_type=F32)
            for c in chains:
                u, hh = c
                lse = m[c] + jnp.log(den[c])
                og_scr[g * heads + hh, row_sl[u], :] = jnp.concatenate(
                    [acc[c] / den[c], jnp.broadcast_to(lse, (nb, hd))], axis=-1)

        def unroll_of(count):
            return next(u for u in (4, 3, 2, 1) if count % u == 0)

        u_first = unroll_of(dil)

        def first_body(i, carry, blocks=blocks, u_first=u_first):
            blocks([i * u_first + j for j in range(u_first)], False)
            return carry

        lax.fori_loop(0, dil // u_first, first_body, 0)
        if n_blocks > 1:
            n_rest = dil * (n_blocks - 1)
            u_rest = unroll_of(n_rest)

            def rest_body(i, carry, blocks=blocks, dil=dil, n_blocks=n_blocks, u_rest=u_rest):
                starts = []
                for j in range(u_rest):
                    idx = i * u_rest + j
                    starts.append(idx // (n_blocks - 1) + dil * nb * (idx % (n_blocks - 1) + 1))
                blocks(starts, True)
                return carry

            lax.fori_loop(0, n_rest // u_rest, rest_body, 0)

    rb = 2 * nb

    def merge_body(i, carry):
        rows = pl.ds(pl.multiple_of(i * rb, rb), rb)
        res = []
        for hh in range(heads):
            tiles = [og_scr[g * heads + hh, rows, :] for g in range(n_pat)]
            lses = [t[:, hd:2 * hd] for t in tiles]
            top = functools.reduce(jnp.maximum, lses)
            ws = [jnp.exp(l - top) for l in lses]
            num = functools.reduce(lambda a, b: a + b, [w * t[:, 0:hd] for w, t in zip(ws, tiles)])
            res.append(num / functools.reduce(lambda a, b: a + b, ws))
        o_ref[0, rows, :] = jnp.concatenate(res, axis=-1)
        return carry

    lax.fori_loop(0, seq // rb, merge_body, 0)


def _attn_prompt_call(aq, ak, av, tab):
    bsz, seq, width = aq.shape
    assert width == HEAD_WIDTH and seq % (DILATED_PATTERNS[-1][1] * KEYS_PER_BLOCK) == 0
    n_pat = len(DILATED_PATTERNS)
    heads_per_step = LANES // HEAD_DIM
    qkv_spec = pl.BlockSpec((1, seq, LANES), lambda h, b: (b, 0, h))
    return pl.pallas_call(
        functools.partial(_attn_prompt_kernel, seq=seq),
        grid=(N_HEADS // heads_per_step, bsz),
        in_specs=[qkv_spec, qkv_spec, qkv_spec,
                  pl.BlockSpec((n_pat, heads_per_step, KEYS_PER_BLOCK, 2 * KEYS_PER_BLOCK),
                               lambda h, b: (0, h, 0, 0))],
        out_specs=pl.BlockSpec((1, seq, LANES), lambda h, b: (b, 0, h)),
        out_shape=jax.ShapeDtypeStruct((bsz, seq, width), F32),
        scratch_shapes=[pltpu.VMEM((n_pat * heads_per_step, seq, LANES), F32)],
        compiler_params=pltpu.CompilerParams(dimension_semantics=("parallel", "parallel"),
                                             vmem_limit_bytes=VMEM_LIMIT),
        name="attn_prompt",
    )(aq, ak, av, tab)


def _sample_layout(w_buf, t_new):
    assert w_buf % LANES == 0 and t_new <= SAMPLE_T_PAD
    row_of_col = np.full((w_buf + LANES,), -1, np.int64)
    row_of_col[:w_buf + t_new] = np.arange(w_buf + t_new)
    buckets = np.full((len(DILATED_PATTERNS), 1, SAMPLE_T_PAD, w_buf + LANES), -1, np.int32)
    lows = []
    for g, (window, dil) in enumerate(DILATED_PATTERNS):
        lows.append(max(0, w_buf - window) // LANES * LANES)
        for t in range(t_new):
            dist = w_buf + t - row_of_col
            ok = (row_of_col >= 0) & (dist >= 0) & (dist <= window) & (dist % dil == 0)
            assert int(ok.sum()) == min(window, w_buf + t) // dil + 1 and not ok[:lows[g]].any()
            buckets[g, 0, t] = np.where(ok, _t5_bucket_np(dist), -1)
    return tuple(lows), buckets


def _attn_sample_kernel(q_ref, kn_ref, vn_ref, kt_ref, vt_ref, tab_ref, o_ref, *, lows):
    tp = SAMPLE_T_PAD
    hd = HEAD_DIM
    w_buf = kt_ref.shape[3]
    n_pat = len(DILATED_PATTERNS)
    add = lambda x, y: x + y
    rowmax = lambda x: jnp.max(x, axis=-1, keepdims=True)
    rowsum = lambda x: jnp.sum(x, axis=-1, keepdims=True)
    heads = range(N_HEADS)
    q8 = q_ref[0] * hd ** -0.5
    s_buf = {h: jnp.dot(q8[:, hd * h:hd * (h + 1)], kt_ref[0, h], preferred_element_type=F32) for h in heads}
    s_new = {h: _nt_dot(q8[:, hd * h:hd * (h + 1)], kn_ref[0, h]) for h in heads}
    p_buf, p_new, den, lse = {}, {}, {}, {}
    for h in heads:
        for g in range(n_pat):
            lo = lows[g]
            l_buf = s_buf[h][:, lo:] + tab_ref[g, h, :, lo:w_buf]
            l_new = s_new[h] + tab_ref[g, h, :, w_buf:w_buf + tp]
            m = jnp.maximum(rowmax(l_buf), rowmax(l_new))
            p_buf[h, g] = jnp.exp(l_buf - m)
            p_new[h, g] = jnp.exp(l_new - m)
            den[h, g] = rowsum(p_buf[h, g]) + rowsum(p_new[h, g])
            lse[h, g] = m + jnp.log(den[h, g])
    edges = sorted(set(lows)) + [w_buf]
    outs = {}
    for h in heads:
        top = functools.reduce(jnp.maximum, [lse[h, g] for g in range(n_pat)])
        ws = [jnp.exp(lse[h, g] - top) for g in range(n_pat)]
        w_sum = functools.reduce(add, ws)
        coef = [ws[g] / (w_sum * den[h, g]) for g in range(n_pat)]
        acc = jnp.dot(functools.reduce(add, [coef[g] * p_new[h, g] for g in range(n_pat)]), vn_ref[0, h],
                      preferred_element_type=F32)
        for e0, e1 in zip(edges[:-1], edges[1:]):
            mix = functools.reduce(add, [coef[g] * p_buf[h, g][:, e0 - lows[g]:e1 - lows[g]]
                                         for g in range(n_pat) if lows[g] <= e0])
            acc = acc + _nt_dot(mix, vt_ref[0, h, :, e0:e1])
        outs[h] = acc
    o_ref[0] = jnp.concatenate([outs[h] for h in heads], axis=1)


def _attn_sample_call(aq8, k_new, v_new, cache_kt, cache_vt, tab, lows):
    bsz, n_heads, hd, w_buf = cache_kt.shape
    q_spec = pl.BlockSpec((1, SAMPLE_T_PAD, n_heads * hd), lambda b: (b, 0, 0))
    new_spec = pl.BlockSpec((1, n_heads, SAMPLE_T_PAD, hd), lambda b: (b, 0, 0, 0))
    buf_spec = pl.BlockSpec((1, n_heads, hd, w_buf), lambda b: (b, 0, 0, 0))
    return pl.pallas_call(
        functools.partial(_attn_sample_kernel, lows=lows),
        grid=(bsz,),
        in_specs=[q_spec, new_spec, new_spec, buf_spec, buf_spec, _resident(tab.shape)],
        out_specs=q_spec,
        out_shape=jax.ShapeDtypeStruct((bsz, SAMPLE_T_PAD, n_heads * hd), F32),
        compiler_params=pltpu.CompilerParams(dimension_semantics=("parallel",), vmem_limit_bytes=VMEM_LIMIT),
        name="attn_sample",
    )(aq8, k_new, v_new, cache_kt, cache_vt, tab)


def _aligned(index, size):
    return index * size if isinstance(index, int) else pl.multiple_of(index * size, size)


def _split3(x):
    h1 = x.astype(BF16)
    r1 = x - h1.astype(F32)
    h2 = r1.astype(BF16)
    h3 = (r1 - h2.astype(F32)).astype(BF16)
    return h1, h2, h3


def _head_sums(x, ones_ref):
    h1, h2, _ = _split3(x)
    return (jnp.dot(h1, ones_ref[...], preferred_element_type=F32)
            + jnp.dot(h2, ones_ref[...], preferred_element_type=F32))


def _gdn_kernel(x_ref, z_ref, ba_ref, cb_ref, s0_ref, cw_ref, hp_ref, nw_ref, ones_ref,
                o_ref, sfin_ref,
                ext_scr, q_scr, k_scr, v_scr, oc_scr, bt_scr, g_scr, s_scr,
                uw_scr, in_scr, qd_scr, kd_scr, el_scr, *, chunk, n_valid, group):
    n_bb, tb = x_ref.shape[0], x_ref.shape[1]
    n_chunks = tb // chunk
    hw = HEAD_WIDTH
    hd = HEAD_DIM
    t = pl.program_id(1)
    hist = SUBLANES

    @pl.when(t == 0)
    def _():
        ext_scr[:, 0:hist, :] = cb_ref[...]
        s_scr[...] = s0_ref[...]

    for bb in range(n_bb):
        ext_scr[bb, hist:hist + tb, :] = x_ref[bb]
        conv = jnp.zeros((tb, 3 * hw), F32)
        for j in range(CONV_WIDTH):
            off = hist - (CONV_WIDTH - 1) + j
            conv = conv + ext_scr[bb, off:off + tb, :] * cw_ref[j:j + 1, :]
        ext_scr[bb, 0:hist, :] = ext_scr[bb, tb:tb + hist, :]
        act = jax.nn.silu(conv)
        q = act[:, 0:hw]
        k = act[:, hw:2 * hw]
        q_scr[bb] = q * lax.rsqrt(_head_sums(q * q, ones_ref) + RMS_EPS) * hd ** -0.5
        k_scr[bb] = k * lax.rsqrt(_head_sums(k * k, ones_ref) + RMS_EPS)
        v_scr[bb] = act[:, 2 * hw:3 * hw]

        ba = ba_ref[bb]
        row = t * tb + lax.broadcasted_iota(jnp.int32, ba.shape, 0)
        live = row < n_valid
        bt_scr[bb] = jnp.where(live, jax.nn.sigmoid(ba), 0.0)
        g_scr[bb] = jnp.where(live, -jnp.exp(hp_ref[0:1, :]) * jax.nn.softplus(ba + hp_ref[1:2, :]), 0.0)

    ri = lax.broadcasted_iota(jnp.int32, (chunk, chunk), 0)
    ci = lax.broadcasted_iota(jnp.int32, (chunk, chunk), 1)
    tril = ri >= ci
    strict = ri > ci
    eye_c = (ri == ci).astype(F32)
    tril_b = tril.astype(BF16)
    eye_l = (lax.broadcasted_iota(jnp.int32, (LANES, LANES), 0)
             == lax.broadcasted_iota(jnp.int32, (LANES, LANES), 1)).astype(BF16)
    n_doublings = int(math.log2(chunk)) - 1
    assert 2 ** (n_doublings + 1) == chunk

    dot = functools.partial(jnp.dot, preferred_element_type=F32)

    def state_free_part(gi, carry):
        probs, shared = [], {}
        for bb in range(n_bb):
            for cg in range(group):
                c_idx = gi * group + cg
                rows = pl.ds(_aligned(c_idx, chunk), chunk)
                gc = functools.reduce(lambda a, b: a + b,
                                      [dot(tril_b, p) for p in _split3(g_scr[bb, rows, :])])
                gct = functools.reduce(lambda a, b: a + b, [_nt_dot(eye_l, p) for p in _split3(gc)])
                g_last = gc[chunk - 1:chunk, :]
                el_rows = pl.ds(_aligned(c_idx, SUBLANES), SUBLANES)
                el_scr[bb, el_rows, :] = jnp.broadcast_to(jnp.exp(g_last), (SUBLANES, LANES))
                shared[bb, cg] = (rows, gc, gct, jnp.exp(gc), jnp.exp(g_last - gc), bt_scr[bb, rows, :])
                probs += [(bb, cg, h) for h in range(N_HEADS)]
        kk, qk, rhs, decay, a, t_inv, pw = {}, {}, {}, {}, {}, {}, {}
        for p in probs:
            bb, cg, h = p
            rows, gc, gct, e_gc, e_rem, bt = shared[bb, cg]
            hl = pl.ds(hd * h, hd)
            gl = N_HEADS + h
            qh = q_scr[bb, rows, hl]
            kh = k_scr[bb, rows, hl]
            beta = bt[:, h:h + 1]
            kb = kh * beta
            kk[p] = _nt_dot(kb, kh)
            qk[p] = _nt_dot(qh, kh)
            rhs[p] = jnp.concatenate([v_scr[bb, rows, hl] * beta, kb * e_gc[:, gl:gl + 1]], axis=1)
            qd_scr[bb, rows, hl] = qh * e_gc[:, gl:gl + 1]
            kd_scr[bb, rows, hl] = kh * e_rem[:, gl:gl + 1]
            diff = gc[:, gl:gl + 1] - gct[gl:gl + 1, :]
            decay[p] = jnp.where(tril, jnp.exp(jnp.where(tril, diff, 0.0)), 0.0)
        for p in probs:
            bb, cg, h = p
            rows = shared[bb, cg][0]
            a[p] = jnp.where(strict, kk[p] * decay[p], 0.0)
            in_scr[bb, rows, LANES * h:LANES * h + chunk] = jnp.where(tril, qk[p] * decay[p], 0.0)
            t_inv[p] = eye_c - a[p]
        for p in probs:
            pw[p] = dot(a[p], a[p])
        for _ in range(n_doublings - 1):
            nxt_pw, nxt_t = {}, {}
            for p in probs:
                nxt_pw[p] = dot(pw[p], pw[p])
                nxt_t[p] = dot(t_inv[p], eye_c + pw[p])
            pw, t_inv = nxt_pw, nxt_t
        for p in probs:
            t_inv[p] = dot(t_inv[p], eye_c + pw[p])
        for p in probs:
            bb, cg, h = p
            rows = shared[bb, cg][0]
            uw_scr[bb, rows, LANES * h:LANES * (h + 1)] = dot(t_inv[p], rhs[p])
        return carry

    def state_part(ci_, carry):
        rows = pl.ds(_aligned(ci_, chunk), chunk)
        el_rows = pl.ds(_aligned(ci_, SUBLANES), SUBLANES)
        probs = [(bb, h) for bb in range(n_bb) for h in range(N_HEADS)]
        s_old, uw, both, v_new, o_in, s_add = {}, {}, {}, {}, {}, {}
        for p in probs:
            bb, h = p
            s_old[p] = s_scr[bb, h]
            uw[p] = uw_scr[bb, rows, LANES * h:LANES * (h + 1)]
            lhs = jnp.concatenate([uw[p][:, hd:2 * hd], qd_scr[bb, rows, pl.ds(hd * h, hd)]], axis=0)
            both[p] = dot(lhs, s_old[p])
        for p in probs:
            bb, h = p
            v_new[p] = uw[p][:, 0:hd] - both[p][0:chunk]
            o_in[p] = dot(in_scr[bb, rows, LANES * h:LANES * h + chunk], v_new[p])
            s_add[p] = lax.dot_general(kd_scr[bb, rows, pl.ds(hd * h, hd)], v_new[p],
                                       (((0,), (0,)), ((), ())), preferred_element_type=F32)
        for p in probs:
            bb, h = p
            e_last = el_scr[bb, el_rows, :][0:1, N_HEADS + h:N_HEADS + h + 1]
            oc_scr[bb, rows, pl.ds(hd * h, hd)] = both[p][chunk:2 * chunk] + o_in[p]
            s_scr[bb, h] = s_old[p] * e_last + s_add[p]
        return carry

    assert n_chunks % group == 0
    if n_chunks == group:
        state_free_part(0, 0)
    else:
        lax.fori_loop(0, n_chunks // group, state_free_part, 0)
    if n_chunks == 1:
        state_part(0, 0)
    else:
        lax.fori_loop(0, n_chunks, state_part, 0)

    for bb in range(n_bb):
        o = oc_scr[bb]
        o = o * lax.rsqrt(_head_sums(o * o, ones_ref) * (1.0 / hd) + RMS_EPS) * nw_ref[...]
        o_ref[bb] = o * jax.nn.silu(z_ref[bb])

    @pl.when(t == pl.num_programs(1) - 1)
    def _():
        sfin_ref[...] = s_scr[...]


def _gdn_call(gq, z, ba, conv_buf8, s0, conv_w8, head_params, norm_w_row, n_valid, token_block, chunk,
              batch_block, group):
    bsz, seq, w3 = gq.shape
    hw = HEAD_WIDTH
    assert w3 == 3 * hw and seq % token_block == 0 and token_block % chunk == 0 and bsz % batch_block == 0
    tb = token_block
    nbb = batch_block
    ones_bd = np.kron(np.eye(N_HEADS, dtype=np.float32), np.ones((HEAD_DIM, HEAD_DIM), np.float32))
    blk = lambda w: pl.BlockSpec((nbb, tb, w), lambda b, t: (b, t, 0))
    per_b3 = lambda s: pl.BlockSpec((nbb,) + s, lambda b, t: (b,) + (0,) * len(s))
    return pl.pallas_call(
        functools.partial(_gdn_kernel, chunk=chunk, n_valid=n_valid, group=group),
        grid=(bsz // nbb, seq // tb),
        in_specs=[blk(w3), blk(hw), blk(LANES), per_b3((SUBLANES, w3)), per_b3((N_HEADS, HEAD_DIM, HEAD_DIM)),
                  _resident(conv_w8.shape), _resident(head_params.shape), _resident(norm_w_row.shape),
                  _resident(ones_bd.shape)],
        out_specs=[blk(hw), per_b3((N_HEADS, HEAD_DIM, HEAD_DIM))],
        out_shape=[jax.ShapeDtypeStruct((bsz, seq, hw), F32),
                   jax.ShapeDtypeStruct((bsz, N_HEADS, HEAD_DIM, HEAD_DIM), F32)],
        scratch_shapes=[pltpu.VMEM((nbb, tb + 2 * SUBLANES, w3), F32)]
                       + [pltpu.VMEM((nbb, tb, hw), F32)] * 4
                       + [pltpu.VMEM((nbb, tb, LANES), F32)] * 2
                       + [pltpu.VMEM((nbb, N_HEADS, HEAD_DIM, HEAD_DIM), F32)]
                       + [pltpu.VMEM((nbb, tb, N_HEADS * LANES), F32)] * 2
                       + [pltpu.VMEM((nbb, tb, hw), F32)] * 2
                       + [pltpu.VMEM((nbb, (tb // chunk) * SUBLANES, LANES), F32)],
        compiler_params=pltpu.CompilerParams(dimension_semantics=("parallel", "arbitrary"),
                                             vmem_limit_bytes=VMEM_LIMIT),
        name="gdn",
    )(gq, z, ba, conv_buf8, s0, conv_w8, head_params, norm_w_row, jnp.asarray(ones_bd, BF16))


def _pad_rows(x, rows, front=False):
    extra = rows - x.shape[1]
    cfg = [(0, 0)] * x.ndim
    cfg[1] = (extra, 0) if front else (0, extra)
    return jnp.pad(x, cfg)


def kernel(x_prompt, x_sample, cache_attn_k, cache_attn_v, state_gdn, state_conv, rel_bias, ln1_g, ln1_b,
           ffn1_w_gate, ffn1_w_up, ffn1_w_down, w_in, w_out, gdn_conv_w, gdn_a_log, gdn_dt_bias, gdn_norm_w,
           ln2_g, ln2_b, ffn2_w_gate, ffn2_w_up, ffn2_w_down, ln3_g, ln3_b):
    depth = w_in.shape[0]
    alpha = (2.0 * depth) ** 0.25
    bsz, seq, d_model = x_prompt.shape
    dbsz, dseq, _ = x_sample.shape
    w_buf = cache_attn_k.shape[2]
    hw = HEAD_WIDTH
    in_cols = w_in.shape[2]
    assert in_cols == 7 * hw + 2 * N_HEADS and CONV_WIDTH - 1 <= dseq <= SAMPLE_T_PAD and seq >= CONV_WIDTH - 1

    sample_lows, sample_buckets = _sample_layout(w_buf, dseq)
    tab_p = _t5_table_call(rel_bias, jnp.asarray(_prompt_buckets()))
    tab_s = _t5_table_call(rel_bias, jnp.asarray(sample_buckets))

    yp = x_prompt.reshape(bsz * seq, d_model)
    ys = x_sample.reshape(dbsz * dseq, d_model)
    collected = [[] for _ in range(8)]
    row = lambda v: v.reshape(1, -1)
    for layer in range(depth):
        wg1, wu1, wd1 = (w[layer].astype(BF16) for w in (ffn1_w_gate, ffn1_w_up, ffn1_w_down))
        wg2, wu2, wd2 = (w[layer].astype(BF16) for w in (ffn2_w_gate, ffn2_w_up, ffn2_w_down))
        win = jnp.pad(w_in[layer], ((0, 0), (0, 7 * hw + LANES - in_cols))).astype(BF16)
        wo = w_out[layer].astype(BF16)
        conv_w8 = jnp.pad(gdn_conv_w[layer], ((0, SUBLANES - CONV_WIDTH), (0, 0)))
        head_params = jnp.zeros((SUBLANES, LANES), F32)
        head_params = head_params.at[0, N_HEADS:2 * N_HEADS].set(gdn_a_log[layer])
        head_params = head_params.at[1, N_HEADS:2 * N_HEADS].set(gdn_dt_bias[layer])
        norm_w_row = jnp.tile(gdn_norm_w[layer], N_HEADS).reshape(1, hw)
        pre = functools.partial(_pre_call, wg=wg1, wu=wu1, wd=wd1, g=row(ln1_g[layer]), b=row(ln1_b[layer]),
                                win=win, alpha=alpha)
        post = functools.partial(_post_call, wo=wo, g2=row(ln2_g[layer]), b2=row(ln2_b[layer]), wg=wg2, wu=wu2,
                                 wd=wd2, g3=row(ln3_g[layer]), b3=row(ln3_b[layer]), alpha=alpha)

        x1, aq, ak, av, gq, z, ba = pre(yp)
        shp = lambda a: a.reshape(bsz, seq, a.shape[1])
        attn = _attn_prompt_call(shp(aq), shp(ak), shp(av), tab_p)
        gdn, s_p = _gdn_call(shp(gq), shp(z), shp(ba),
                             jnp.zeros((bsz, SUBLANES, 3 * hw), F32),
                             jnp.zeros((bsz, N_HEADS, HEAD_DIM, HEAD_DIM), F32),
                             conv_w8, head_params, norm_w_row, n_valid=seq,
                             token_block=min(GDN_TOKEN_BLOCK, seq), chunk=min(GDN_CHUNK, seq),
                             batch_block=1, group=GDN_CHUNK_GROUP)
        yp = post(attn.reshape(bsz * seq, hw), gdn.reshape(bsz * seq, hw), x1)
        wp = min(w_buf, seq)
        heads5 = lambda a: a.reshape(bsz, seq, N_HEADS, HEAD_DIM)[:, seq - wp:]
        st_p = (heads5(ak), heads5(av), s_p, shp(gq)[:, seq - (CONV_WIDTH - 1):])

        x1s, aqs, aks, avs, gqs, zs, bas = pre(ys)
        shs = lambda a: _pad_rows(a.reshape(dbsz, dseq, a.shape[1]), SAMPLE_T_PAD)
        heads5s = lambda a: a.reshape(dbsz, dseq, N_HEADS, HEAD_DIM)
        new_rows = lambda a: jnp.pad(heads5s(a).transpose(0, 2, 1, 3),
                                     ((0, 0), (0, 0), (0, SAMPLE_T_PAD - dseq), (0, 0)))
        by_head_t = lambda c: c.transpose(0, 2, 3, 1)
        attn_s = _attn_sample_call(shs(aqs), new_rows(aks), new_rows(avs), by_head_t(cache_attn_k[layer]),
                                   by_head_t(cache_attn_v[layer]), tab_s, sample_lows)
        gdn_s, s_s = _gdn_call(shs(gqs), shs(zs), shs(bas),
                               _pad_rows(state_conv[layer], SUBLANES, front=True), state_gdn[layer],
                               conv_w8, head_params, norm_w_row, n_valid=dseq,
                               token_block=SAMPLE_T_PAD, chunk=SAMPLE_T_PAD,
                               batch_block=GDN_SAMPLE_BATCH_BLOCK, group=1)
        unpad = lambda a: a[:, :dseq].reshape(dbsz * dseq, hw)
        ys = post(unpad(attn_s), unpad(gdn_s), x1s)
        st_s = (heads5s(aks), heads5s(avs), s_s,
                gqs.reshape(dbsz, dseq, 3 * hw)[:, dseq - (CONV_WIDTH - 1):])
        for lst, st in zip(collected, st_p + st_s):
            lst.append(st)
    outs = [jnp.stack(t, axis=0) for t in collected]
    return (yp.reshape(bsz, seq, d_model), ys.reshape(dbsz, dseq, d_model)) + tuple(outs)
```

```python
import functools
import math

import numpy as np
import jax
import jax.numpy as jnp
from jax import lax
from jax.experimental import pallas as pl
from jax.experimental.pallas import tpu as pltpu

F32 = jnp.float32
BF16 = jnp.bfloat16

HEAD_DIM = 64
N_HEADS = 8
HEAD_WIDTH = N_HEADS * HEAD_DIM
DILATED_PATTERNS = ((128, 1), (512, 4), (2048, 16))
KEYS_PER_BLOCK = 128
T5_BUCKETS = 32
T5_MAX_EXACT = 16
T5_MAX_DIST = 2048
CONV_WIDTH = 4
GDN_CHUNK = 64
LN_EPS = 1e-5
RMS_EPS = 1e-6
NEG_BIG = -1e30

LANES = 128
SUBLANES = 8
TOKEN_TILE = 256
GDN_TOKEN_BLOCK = 256
ATTN_SKEW_GROUPS = 2
GDN_CHUNK_GROUP = 4
GDN_SAMPLE_BATCH_BLOCK = 8
SAMPLE_T_PAD = 8
VMEM_LIMIT = 56 * 1024 * 1024


def _resident(shape):
    nd = len(shape)
    return pl.BlockSpec(shape, lambda *_: (0,) * nd, pipeline_mode=pl.Buffered(1))


def _layernorm(y, g, b):
    mu = jnp.mean(y, axis=-1, keepdims=True)
    yc = y - mu
    var = jnp.mean(yc * yc, axis=-1, keepdims=True)
    return yc * lax.rsqrt(var + LN_EPS) * g + b


def _swiglu(xb, wg_ref, wu_ref, wd_ref, h_scr):
    d_ff = wg_ref.shape[1]
    step = 2 * LANES
    assert d_ff % step == 0
    for c in range(d_ff // step):
        sl = slice(c * step, (c + 1) * step)
        gate = jnp.dot(xb, wg_ref[:, sl], preferred_element_type=F32)
        up = jnp.dot(xb, wu_ref[:, sl], preferred_element_type=F32)
        h_scr[:, sl] = (jax.nn.silu(gate) * up).astype(BF16)
    return jnp.dot(h_scr[...], wd_ref[...], preferred_element_type=F32)


def _pre_kernel(x_ref, wg_ref, wu_ref, wd_ref, g_ref, b_ref, win_ref,
                x1_ref, aq_ref, ak_ref, av_ref, gq_ref, z_ref, ba_ref, *rest, alpha):
    h_scr = rest[-1]
    x = x_ref[...]
    ff = _swiglu(x.astype(BF16), wg_ref, wu_ref, wd_ref, h_scr)
    x1 = _layernorm(alpha * x + 0.5 * ff, g_ref[...], b_ref[...])
    x1_ref[...] = x1
    xb = x1.astype(BF16)
    col = 0
    for ref in (aq_ref, ak_ref, av_ref, gq_ref, z_ref, ba_ref):
        width = ref.shape[1]
        ref[...] = jnp.dot(xb, win_ref[:, col:col + width], preferred_element_type=F32)
        col += width
    for src, dst in zip((ak_ref, av_ref), rest[:-1]):
        dst[0] = src[...].T


def _pre_call(x2d, wg, wu, wd, g, b, win, alpha, kv_t_seq=None):
    m, d = x2d.shape
    d_ff = wg.shape[1]
    tm = TOKEN_TILE
    assert m % tm == 0
    widths = (HEAD_WIDTH, HEAD_WIDTH, HEAD_WIDTH, 3 * HEAD_WIDTH, HEAD_WIDTH, LANES)
    assert sum(widths) == win.shape[1]
    tile = lambda w: pl.BlockSpec((tm, w), lambda i: (i, 0))
    out_specs = [tile(d)] + [tile(w) for w in widths]
    out_shape = [jax.ShapeDtypeStruct((m, d), F32)] + [jax.ShapeDtypeStruct((m, w), F32) for w in widths]
    if kv_t_seq is not None:
        assert kv_t_seq % tm == 0 and m % kv_t_seq == 0
        per_seq = kv_t_seq // tm
        out_specs += [pl.BlockSpec((1, HEAD_WIDTH, tm), lambda i: (i // per_seq, 0, i % per_seq))] * 2
        out_shape += [jax.ShapeDtypeStruct((m // kv_t_seq, HEAD_WIDTH, kv_t_seq), F32)] * 2
    return pl.pallas_call(
        functools.partial(_pre_kernel, alpha=alpha),
        grid=(m // tm,),
        in_specs=[tile(d), _resident(wg.shape), _resident(wu.shape), _resident(wd.shape),
                  _resident(g.shape), _resident(b.shape), _resident(win.shape)],
        out_specs=out_specs,
        out_shape=out_shape,
        scratch_shapes=[pltpu.VMEM((tm, d_ff), BF16)],
        compiler_params=pltpu.CompilerParams(dimension_semantics=("parallel",), vmem_limit_bytes=VMEM_LIMIT),
        name="pre_ffn_proj",
    )(x2d, wg, wu, wd, g, b, win)


def _post_kernel(attn_ref, gdn_ref, x1_ref, wo_ref, g2_ref, b2_ref, wg_ref, wu_ref, wd_ref, g3_ref, b3_ref,
                 y_ref, h_scr, *, alpha):
    mix = jnp.dot(attn_ref[...].astype(BF16), wo_ref[0:HEAD_WIDTH, :], preferred_element_type=F32)
    mix = mix + jnp.dot(gdn_ref[...].astype(BF16), wo_ref[HEAD_WIDTH:2 * HEAD_WIDTH, :],
                        preferred_element_type=F32)
    x2 = _layernorm(alpha * x1_ref[...] + mix, g2_ref[...], b2_ref[...])
    ff = _swiglu(x2.astype(BF16), wg_ref, wu_ref, wd_ref, h_scr)
    y_ref[...] = _layernorm(alpha * x2 + 0.5 * ff, g3_ref[...], b3_ref[...])


def _post_call(attn, gdn, x1, wo, g2, b2, wg, wu, wd, g3, b3, alpha):
    m, d = x1.shape
    tm = TOKEN_TILE
    assert m % tm == 0
    tile = lambda w: pl.BlockSpec((tm, w), lambda i: (i, 0))
    consts = (wo, g2, b2, wg, wu, wd, g3, b3)
    return pl.pallas_call(
        functools.partial(_post_kernel, alpha=alpha),
        grid=(m // tm,),
        in_specs=[tile(HEAD_WIDTH), tile(HEAD_WIDTH), tile(d)] + [_resident(c.shape) for c in consts],
        out_specs=tile(d),
        out_shape=jax.ShapeDtypeStruct((m, d), F32),
        scratch_shapes=[pltpu.VMEM((tm, wg.shape[1]), BF16)],
        compiler_params=pltpu.CompilerParams(dimension_semantics=("parallel",), vmem_limit_bytes=VMEM_LIMIT),
        name="post_out_ffn",
    )(attn, gdn, x1, *consts)


def _t5_bucket_np(dist):
    n = np.maximum(dist, 0)
    nf = np.maximum(n, 1).astype(np.float32)
    large = T5_MAX_EXACT + (np.log(nf / np.float32(T5_MAX_EXACT)) / np.float32(math.log(T5_MAX_DIST / T5_MAX_EXACT))
                            * np.float32(T5_BUCKETS - T5_MAX_EXACT)).astype(np.int32)
    large = np.minimum(large, T5_BUCKETS - 1)
    return np.where(n < T5_MAX_EXACT, n, large).astype(np.int32)


def _t5_table_kernel(rb_ref, bk_ref, out_ref):
    for h in range(N_HEADS):
        bk = bk_ref[0, h % bk_ref.shape[1]]
        acc = jnp.full(bk.shape, NEG_BIG, F32)
        for b in range(T5_BUCKETS):
            acc = jnp.where(bk == b, rb_ref[b, h], acc)
        out_ref[0, h] = acc


def _t5_table_call(rel_bias, buckets):
    p, hb, r, c = buckets.shape
    assert hb in (1, N_HEADS)
    return pl.pallas_call(
        _t5_table_kernel,
        grid=(p,),
        in_specs=[pl.BlockSpec(memory_space=pltpu.SMEM),
                  pl.BlockSpec((1, hb, r, c), lambda i: (i, 0, 0, 0))],
        out_specs=pl.BlockSpec((1, N_HEADS, r, c), lambda i: (i, 0, 0, 0)),
        out_shape=jax.ShapeDtypeStruct((p, N_HEADS, r, c), F32),
        compiler_params=pltpu.CompilerParams(dimension_semantics=("parallel",)),
        name="t5_bias_table",
    )(rel_bias, buckets)


def _prompt_buckets():
    nb = KEYS_PER_BLOCK
    qi = np.arange(nb)[:, None]
    ki = np.arange(2 * nb)[None, :]
    dist = qi + nb - ki
    valid = (dist >= 0) & (dist <= nb)
    out = []
    for window, dil in DILATED_PATTERNS:
        assert window // dil == nb
        out.append(np.where(valid, _t5_bucket_np(dist * dil), -1))
    return np.stack(out).astype(np.int32)[:, None]


def _nt_dot(a, b):
    return lax.dot_general(a, b, (((1,), (1,)), ((), ())), preferred_element_type=F32)


def _emit_skewed(stages, items, n_groups):
    n_groups = max(1, min(n_groups, len(items)))
    groups = [items[i::n_groups] for i in range(n_groups)]
    for step in range(len(stages) + n_groups - 1):
        for gi, group in enumerate(groups):
            s = step - gi
            if 0 <= s < len(stages):
                for item in group:
                    stages[s](item)


def _attn_prompt_kernel(q_ref, k_ref, v_ref, tab_ref, o_ref, og_scr, lg_scr, *, seq):
    nb = KEYS_PER_BLOCK
    hd = HEAD_DIM
    n_pat = len(DILATED_PATTERNS)
    heads = LANES // hd
    for g, (window, dil) in enumerate(DILATED_PATTERNS):
        n_blocks = seq // (dil * nb)

        def blocks(starts, with_prev, g=g, dil=dil):
            lane_head = lax.broadcasted_iota(jnp.int32, (nb, LANES), 1) // hd
            row_sl, qs, ks, vs, kps, vps = [], [], [], [], [], []
            for start in starts:
                rows = pl.ds(start, nb, stride=dil) if dil > 1 else pl.ds(start, nb)
                row_sl.append(rows)
                qs.append(q_ref[0, rows, :] * hd ** -0.5)
                ks.append(k_ref[0, rows, :].astype(BF16))
                vs.append(v_ref[0, rows, :].astype(BF16))
                if with_prev:
                    rows_p = (pl.ds(start - dil * nb, nb, stride=dil) if dil > 1 else pl.ds(start - nb, nb))
                    kps.append(k_ref[0, rows_p, :].astype(BF16))
                    vps.append(v_ref[0, rows_p, :].astype(BF16))
            chains = [(u, hh) for u in range(len(starts)) for hh in range(heads)]
            s_c, s_p, m, p_c, p_p, den, acc = {}, {}, {}, {}, {}, {}, {}

            def scores(c):
                u, hh = c
                qb = jnp.where(lane_head == hh, qs[u], 0.0).astype(BF16)
                s_c[c] = _nt_dot(qb, ks[u]) + tab_ref[g, hh, :, nb:2 * nb]
                if with_prev:
                    s_p[c] = _nt_dot(qb, kps[u]) + tab_ref[g, hh, :, 0:nb]

            def row_max(c):
                both = jnp.maximum(s_c[c], s_p[c]) if with_prev else s_c[c]
                m[c] = jnp.max(both, axis=-1, keepdims=True)

            def probs(c):
                p_c[c] = jnp.exp(s_c[c] - m[c])
                both = p_c[c]
                if with_prev:
                    p_p[c] = jnp.exp(s_p[c] - m[c])
                    both = both + p_p[c]
                den[c] = jnp.sum(both, axis=-1, keepdims=True)

            def values(c):
                u, hh = c
                acc[c] = jnp.dot(p_c[c].astype(BF16), vs[u], preferred_element_type=F32)
                if with_prev:
                    acc[c] = acc[c] + jnp.dot(p_p[c].astype(BF16), vps[u], preferred_element_type=F32)

            def finish(c):
                acc[c] = acc[c] / den[c]
                den[c] = m[c] + jnp.log(den[c])

            _emit_skewed((scores, row_max, probs, values, finish), chains, ATTN_SKEW_GROUPS)
            for u in range(len(starts)):
                out, lse = acc[u, 0], den[u, 0]
                for hh in range(1, heads):
                    out = jnp.where(lane_head == hh, acc[u, hh], out)
                    lse = jnp.where(lane_head == hh, den[u, hh], lse)
                og_scr[g, row_sl[u], :] = out
                lg_scr[g, row_sl[u], :] = jnp.broadcast_to(lse, (nb, LANES))

        def unroll_of(count):
            return next(u for u in (4, 3, 2, 1) if count % u == 0)

        u_first = unroll_of(dil)

        def first_body(i, carry, blocks=blocks, u_first=u_first):
            blocks([i * u_first + j for j in range(u_first)], False)
            return carry

        lax.fori_loop(0, dil // u_first, first_body, 0)
        if n_blocks > 1:
            n_rest = dil * (n_blocks - 1)
            u_rest = unroll_of(n_rest)

            def rest_body(i, carry, blocks=blocks, dil=dil, n_blocks=n_blocks, u_rest=u_rest):
                starts = []
                for j in range(u_rest):
                    idx = i * u_rest + j
                    starts.append(idx // (n_blocks - 1) + dil * nb * (idx % (n_blocks - 1) + 1))
                blocks(starts, True)
                return carry

            lax.fori_loop(0, n_rest // u_rest, rest_body, 0)

    rb = 2 * nb

    def merge_body(i, carry):
        rows = pl.ds(pl.multiple_of(i * rb, rb), rb)
        lses = [lg_scr[g, rows, :] for g in range(n_pat)]
        top = functools.reduce(jnp.maximum, lses)
        ws = [jnp.exp(l - top) for l in lses]
        num = functools.reduce(lambda a, b: a + b, [w * og_scr[g, rows, :] for g, w in enumerate(ws)])
        o_ref[0, rows, :] = num / functools.reduce(lambda a, b: a + b, ws)
        return carry

    lax.fori_loop(0, seq // rb, merge_body, 0)


def _attn_prompt_call(aq, ak, av, tab):
    bsz, seq, width = aq.shape
    assert width == HEAD_WIDTH and seq % (DILATED_PATTERNS[-1][1] * KEYS_PER_BLOCK) == 0
    n_pat = len(DILATED_PATTERNS)
    heads_per_step = LANES // HEAD_DIM
    qkv_spec = pl.BlockSpec((1, seq, LANES), lambda h, b: (b, 0, h))
    return pl.pallas_call(
        functools.partial(_attn_prompt_kernel, seq=seq),
        grid=(N_HEADS // heads_per_step, bsz),
        in_specs=[qkv_spec, qkv_spec, qkv_spec,
                  pl.BlockSpec((n_pat, heads_per_step, KEYS_PER_BLOCK, 2 * KEYS_PER_BLOCK),
                               lambda h, b: (0, h, 0, 0))],
        out_specs=pl.BlockSpec((1, seq, LANES), lambda h, b: (b, 0, h)),
        out_shape=jax.ShapeDtypeStruct((bsz, seq, width), F32),
        scratch_shapes=[pltpu.VMEM((n_pat, seq, LANES), F32)] * 2,
        compiler_params=pltpu.CompilerParams(dimension_semantics=("parallel", "parallel"),
                                             vmem_limit_bytes=VMEM_LIMIT),
        name="attn_prompt",
    )(aq, ak, av, tab)


def _sample_layout(w_buf, t_new):
    assert w_buf % LANES == 0 and t_new <= SAMPLE_T_PAD
    row_of_col = np.full((w_buf + LANES,), -1, np.int64)
    row_of_col[:w_buf + t_new] = np.arange(w_buf + t_new)
    buckets = np.full((len(DILATED_PATTERNS), 1, SAMPLE_T_PAD, w_buf + LANES), -1, np.int32)
    lows = []
    for g, (window, dil) in enumerate(DILATED_PATTERNS):
        lows.append(max(0, w_buf - window) // LANES * LANES)
        for t in range(t_new):
            dist = w_buf + t - row_of_col
            ok = (row_of_col >= 0) & (dist >= 0) & (dist <= window) & (dist % dil == 0)
            assert int(ok.sum()) == min(window, w_buf + t) // dil + 1 and not ok[:lows[g]].any()
            buckets[g, 0, t] = np.where(ok, _t5_bucket_np(dist), -1)
    return tuple(lows), buckets


def _attn_sample_kernel(q_ref, kn_ref, vn_ref, kt_ref, vt_ref, tab_ref, o_ref, *, lows):
    tp = SAMPLE_T_PAD
    hd = HEAD_DIM
    w_buf = kt_ref.shape[3]
    n_pat = len(DILATED_PATTERNS)
    add = lambda x, y: x + y
    rowmax = lambda x: jnp.max(x, axis=-1, keepdims=True)
    rowsum = lambda x: jnp.sum(x, axis=-1, keepdims=True)
    heads = range(N_HEADS)
    q8 = q_ref[0] * hd ** -0.5
    s_buf = {h: jnp.dot(q8[:, hd * h:hd * (h + 1)], kt_ref[0, h], preferred_element_type=F32) for h in heads}
    s_new = {h: _nt_dot(q8[:, hd * h:hd * (h + 1)], kn_ref[0, h]) for h in heads}
    p_buf, p_new, den, lse = {}, {}, {}, {}
    for h in heads:
        for g in range(n_pat):
            lo = lows[g]
            l_buf = s_buf[h][:, lo:] + tab_ref[g, h, :, lo:w_buf]
            l_new = s_new[h] + tab_ref[g, h, :, w_buf:w_buf + tp]
            m = jnp.maximum(rowmax(l_buf), rowmax(l_new))
            p_buf[h, g] = jnp.exp(l_buf - m)
            p_new[h, g] = jnp.exp(l_new - m)
            den[h, g] = rowsum(p_buf[h, g]) + rowsum(p_new[h, g])
            lse[h, g] = m + jnp.log(den[h, g])
    edges = sorted(set(lows)) + [w_buf]
    outs = {}
    for h in heads:
        top = functools.reduce(jnp.maximum, [lse[h, g] for g in range(n_pat)])
        ws = [jnp.exp(lse[h, g] - top) for g in range(n_pat)]
        w_sum = functools.reduce(add, ws)
        coef = [ws[g] / (w_sum * den[h, g]) for g in range(n_pat)]
        acc = jnp.dot(functools.reduce(add, [coef[g] * p_new[h, g] for g in range(n_pat)]), vn_ref[0, h],
                      preferred_element_type=F32)
        for e0, e1 in zip(edges[:-1], edges[1:]):
            mix = functools.reduce(add, [coef[g] * p_buf[h, g][:, e0 - lows[g]:e1 - lows[g]]
                                         for g in range(n_pat) if lows[g] <= e0])
            acc = acc + _nt_dot(mix, vt_ref[0, h, :, e0:e1])
        outs[h] = acc
    o_ref[0] = jnp.concatenate([outs[h] for h in heads], axis=1)


def _attn_sample_call(aq8, k_new, v_new, cache_kt, cache_vt, tab, lows):
    bsz, n_heads, hd, w_buf = cache_kt.shape
    q_spec = pl.BlockSpec((1, SAMPLE_T_PAD, n_heads * hd), lambda b: (b, 0, 0))
    new_spec = pl.BlockSpec((1, n_heads, SAMPLE_T_PAD, hd), lambda b: (b, 0, 0, 0))
    buf_spec = pl.BlockSpec((1, n_heads, hd, w_buf), lambda b: (b, 0, 0, 0))
    return pl.pallas_call(
        functools.partial(_attn_sample_kernel, lows=lows),
        grid=(bsz,),
        in_specs=[q_spec, new_spec, new_spec, buf_spec, buf_spec, _resident(tab.shape)],
        out_specs=q_spec,
        out_shape=jax.ShapeDtypeStruct((bsz, SAMPLE_T_PAD, n_heads * hd), F32),
        compiler_params=pltpu.CompilerParams(dimension_semantics=("parallel",), vmem_limit_bytes=VMEM_LIMIT),
        name="attn_sample",
    )(aq8, k_new, v_new, cache_kt, cache_vt, tab)


def _aligned(index, size):
    return index * size if isinstance(index, int) else pl.multiple_of(index * size, size)


def _split3(x):
    h1 = x.astype(BF16)
    r1 = x - h1.astype(F32)
    h2 = r1.astype(BF16)
    h3 = (r1 - h2.astype(F32)).astype(BF16)
    return h1, h2, h3


def _head_sums(x, ones_ref):
    h1, h2, _ = _split3(x)
    return (jnp.dot(h1, ones_ref[...], preferred_element_type=F32)
            + jnp.dot(h2, ones_ref[...], preferred_element_type=F32))


def _gdn_kernel(x_ref, z_ref, ba_ref, cb_ref, s0_ref, cw_ref, hp_ref, nw_ref, ones_ref,
                o_ref, sfin_ref,
                ext_scr, q_scr, k_scr, v_scr, oc_scr, bt_scr, g_scr, s_scr,
                uw_scr, in_scr, qd_scr, kd_scr, el_scr, *, chunk, n_valid, group):
    n_bb, tb = x_ref.shape[0], x_ref.shape[1]
    n_chunks = tb // chunk
    hw = HEAD_WIDTH
    hd = HEAD_DIM
    t = pl.program_id(1)
    hist = SUBLANES

    @pl.when(t == 0)
    def _():
        ext_scr[:, 0:hist, :] = cb_ref[...]
        s_scr[...] = s0_ref[...]

    for bb in range(n_bb):
        ext_scr[bb, hist:hist + tb, :] = x_ref[bb]
        conv = jnp.zeros((tb, 3 * hw), F32)
        for j in range(CONV_WIDTH):
            off = hist - (CONV_WIDTH - 1) + j
            conv = conv + ext_scr[bb, off:off + tb, :] * cw_ref[j:j + 1, :]
        ext_scr[bb, 0:hist, :] = ext_scr[bb, tb:tb + hist, :]
        act = jax.nn.silu(conv)
        q = act[:, 0:hw]
        k = act[:, hw:2 * hw]
        q_scr[bb] = q * lax.rsqrt(_head_sums(q * q, ones_ref) + RMS_EPS) * hd ** -0.5
        k_scr[bb] = k * lax.rsqrt(_head_sums(k * k, ones_ref) + RMS_EPS)
        v_scr[bb] = act[:, 2 * hw:3 * hw]

        ba = ba_ref[bb]
        row = t * tb + lax.broadcasted_iota(jnp.int32, ba.shape, 0)
        live = row < n_valid
        bt_scr[bb] = jnp.where(live, jax.nn.sigmoid(ba), 0.0)
        g_scr[bb] = jnp.where(live, -jnp.exp(hp_ref[0:1, :]) * jax.nn.softplus(ba + hp_ref[1:2, :]), 0.0)

    ri = lax.broadcasted_iota(jnp.int32, (chunk, chunk), 0)
    ci = lax.broadcasted_iota(jnp.int32, (chunk, chunk), 1)
    tril = ri >= ci
    strict = ri > ci
    eye_c = (ri == ci).astype(F32)
    tril_b = tril.astype(BF16)
    eye_l = (lax.broadcasted_iota(jnp.int32, (LANES, LANES), 0)
             == lax.broadcasted_iota(jnp.int32, (LANES, LANES), 1)).astype(BF16)
    n_doublings = int(math.log2(chunk)) - 1
    assert 2 ** (n_doublings + 1) == chunk

    dot = functools.partial(jnp.dot, preferred_element_type=F32)

    def state_free_part(gi, carry):
        probs, shared = [], {}
        for bb in range(n_bb):
            for cg in range(group):
                c_idx = gi * group + cg
                rows = pl.ds(_aligned(c_idx, chunk), chunk)
                gc = functools.reduce(lambda a, b: a + b,
                                      [dot(tril_b, p) for p in _split3(g_scr[bb, rows, :])])
                gct = gc.T
                g_last = gc[chunk - 1:chunk, :]
                el_rows = pl.ds(_aligned(c_idx, SUBLANES), SUBLANES)
                el_scr[bb, el_rows, :] = jnp.broadcast_to(jnp.exp(g_last), (SUBLANES, LANES))
                shared[bb, cg] = (rows, gc, gct, jnp.exp(gc), jnp.exp(g_last - gc), bt_scr[bb, rows, :])
                probs += [(bb, cg, h) for h in range(N_HEADS)]
        kk, qk, rhs, decay, a, t_inv, pw = {}, {}, {}, {}, {}, {}, {}
        for p in probs:
            bb, cg, h = p
            rows, gc, gct, e_gc, e_rem, bt = shared[bb, cg]
            hl = pl.ds(hd * h, hd)
            gl = N_HEADS + h
            qh = q_scr[bb, rows, hl]
            kh = k_scr[bb, rows, hl]
            beta = bt[:, h:h + 1]
            kb = kh * beta
            kk[p] = _nt_dot(kb, kh)
            qk[p] = _nt_dot(qh, kh)
            rhs[p] = jnp.concatenate([v_scr[bb, rows, hl] * beta, kb * e_gc[:, gl:gl + 1]], axis=1)
            qd_scr[bb, rows, hl] = qh * e_gc[:, gl:gl + 1]
            kd_scr[bb, rows, hl] = kh * e_rem[:, gl:gl + 1]
            diff = gc[:, gl:gl + 1] - gct[gl:gl + 1, :]
            decay[p] = jnp.where(tril, jnp.exp(jnp.where(tril, diff, 0.0)), 0.0)
        for p in probs:
            bb, cg, h = p
            rows = shared[bb, cg][0]
            a[p] = jnp.where(strict, kk[p] * decay[p], 0.0)
            in_scr[bb, rows, LANES * h:LANES * h + chunk] = jnp.where(tril, qk[p] * decay[p], 0.0)
            t_inv[p] = eye_c - a[p]
        for p in probs:
            pw[p] = dot(a[p], a[p])
        for _ in range(n_doublings - 1):
            nxt_pw, nxt_t = {}, {}
            for p in probs:
                nxt_pw[p] = dot(pw[p], pw[p])
                nxt_t[p] = dot(t_inv[p], eye_c + pw[p])
            pw, t_inv = nxt_pw, nxt_t
        for p in probs:
            t_inv[p] = dot(t_inv[p], eye_c + pw[p])
        for p in probs:
            bb, cg, h = p
            rows = shared[bb, cg][0]
            uw_scr[bb, rows, LANES * h:LANES * (h + 1)] = dot(t_inv[p], rhs[p])
        return carry

    def state_part(ci_, carry):
        rows = pl.ds(_aligned(ci_, chunk), chunk)
        el_rows = pl.ds(_aligned(ci_, SUBLANES), SUBLANES)
        probs = [(bb, h) for bb in range(n_bb) for h in range(N_HEADS)]
        s_old, uw, both, v_new, o_in, s_add = {}, {}, {}, {}, {}, {}
        for p in probs:
            bb, h = p
            s_old[p] = s_scr[bb, h]
            uw[p] = uw_scr[bb, rows, LANES * h:LANES * (h + 1)]
            lhs = jnp.concatenate([uw[p][:, hd:2 * hd], qd_scr[bb, rows, pl.ds(hd * h, hd)]], axis=0)
            both[p] = dot(lhs, s_old[p])
        for p in probs:
            bb, h = p
            v_new[p] = uw[p][:, 0:hd] - both[p][0:chunk]
            o_in[p] = dot(in_scr[bb, rows, LANES * h:LANES * h + chunk], v_new[p])
            s_add[p] = lax.dot_general(kd_scr[bb, rows, pl.ds(hd * h, hd)], v_new[p],
                                       (((0,), (0,)), ((), ())), preferred_element_type=F32)
        for p in probs:
            bb, h = p
            e_last = el_scr[bb, el_rows, :][0:1, N_HEADS + h:N_HEADS + h + 1]
            oc_scr[bb, rows, pl.ds(hd * h, hd)] = both[p][chunk:2 * chunk] + o_in[p]
            s_scr[bb, h] = s_old[p] * e_last + s_add[p]
        return carry

    assert n_chunks % group == 0
    if n_chunks == group:
        state_free_part(0, 0)
    else:
        lax.fori_loop(0, n_chunks // group, state_free_part, 0)
    if n_chunks == 1:
        state_part(0, 0)
    else:
        lax.fori_loop(0, n_chunks, state_part, 0)

    for bb in range(n_bb):
        o = oc_scr[bb]
        o = o * lax.rsqrt(_head_sums(o * o, ones_ref) * (1.0 / hd) + RMS_EPS) * nw_ref[...]
        o_ref[bb] = o * jax.nn.silu(z_ref[bb])

    @pl.when(t == pl.num_programs(1) - 1)
    def _():
        sfin_ref[...] = s_scr[...]


def _gdn_call(gq, z, ba, conv_buf8, s0, conv_w8, head_params, norm_w_row, n_valid, token_block, chunk,
              batch_block, group):
    bsz, seq, w3 = gq.shape
    hw = HEAD_WIDTH
    assert w3 == 3 * hw and seq % token_block == 0 and token_block % chunk == 0 and bsz % batch_block == 0
    tb = token_block
    nbb = batch_block
    ones_bd = np.kron(np.eye(N_HEADS, dtype=np.float32), np.ones((HEAD_DIM, HEAD_DIM), np.float32))
    blk = lambda w: pl.BlockSpec((nbb, tb, w), lambda b, t: (b, t, 0))
    per_b3 = lambda s: pl.BlockSpec((nbb,) + s, lambda b, t: (b,) + (0,) * len(s))
    return pl.pallas_call(
        functools.partial(_gdn_kernel, chunk=chunk, n_valid=n_valid, group=group),
        grid=(bsz // nbb, seq // tb),
        in_specs=[blk(w3), blk(hw), blk(LANES), per_b3((SUBLANES, w3)), per_b3((N_HEADS, HEAD_DIM, HEAD_DIM)),
                  _resident(conv_w8.shape), _resident(head_params.shape), _resident(norm_w_row.shape),
                  _resident(ones_bd.shape)],
        out_specs=[blk(hw), per_b3((N_HEADS, HEAD_DIM, HEAD_DIM))],
        out_shape=[jax.ShapeDtypeStruct((bsz, seq, hw), F32),
                   jax.ShapeDtypeStruct((bsz, N_HEADS, HEAD_DIM, HEAD_DIM), F32)],
        scratch_shapes=[pltpu.VMEM((nbb, tb + 2 * SUBLANES, w3), F32)]
                       + [pltpu.VMEM((nbb, tb, hw), F32)] * 4
                       + [pltpu.VMEM((nbb, tb, LANES), F32)] * 2
                       + [pltpu.VMEM((nbb, N_HEADS, HEAD_DIM, HEAD_DIM), F32)]
                       + [pltpu.VMEM((nbb, tb, N_HEADS * LANES), F32)] * 2
                       + [pltpu.VMEM((nbb, tb, hw), F32)] * 2
                       + [pltpu.VMEM((nbb, (tb // chunk) * SUBLANES, LANES), F32)],
        compiler_params=pltpu.CompilerParams(dimension_semantics=("parallel", "arbitrary"),
                                             vmem_limit_bytes=VMEM_LIMIT),
        name="gdn",
    )(gq, z, ba, conv_buf8, s0, conv_w8, head_params, norm_w_row, jnp.asarray(ones_bd, BF16))


def _pad_rows(x, rows, front=False):
    extra = rows - x.shape[1]
    cfg = [(0, 0)] * x.ndim
    cfg[1] = (extra, 0) if front else (0, extra)
    return jnp.pad(x, cfg)


def kernel(x_prompt, x_sample, cache_attn_k, cache_attn_v, state_gdn, state_conv, rel_bias, ln1_g, ln1_b,
           ffn1_w_gate, ffn1_w_up, ffn1_w_down, w_in, w_out, gdn_conv_w, gdn_a_log, gdn_dt_bias, gdn_norm_w,
           ln2_g, ln2_b, ffn2_w_gate, ffn2_w_up, ffn2_w_down, ln3_g, ln3_b):
    depth = w_in.shape[0]
    alpha = (2.0 * depth) ** 0.25
    bsz, seq, d_model = x_prompt.shape
    dbsz, dseq, _ = x_sample.shape
    w_buf = cache_attn_k.shape[2]
    hw = HEAD_WIDTH
    in_cols = w_in.shape[2]
    assert in_cols == 7 * hw + 2 * N_HEADS and CONV_WIDTH - 1 <= dseq <= SAMPLE_T_PAD and seq >= CONV_WIDTH - 1

    sample_lows, sample_buckets = _sample_layout(w_buf, dseq)
    tab_p = _t5_table_call(rel_bias, jnp.asarray(_prompt_buckets()))
    tab_s = _t5_table_call(rel_bias, jnp.asarray(sample_buckets))

    yp = x_prompt.reshape(bsz * seq, d_model)
    ys = x_sample.reshape(dbsz * dseq, d_model)
    collected = [[] for _ in range(8)]
    row = lambda v: v.reshape(1, -1)
    for layer in range(depth):
        wg1, wu1, wd1 = (w[layer].astype(BF16) for w in (ffn1_w_gate, ffn1_w_up, ffn1_w_down))
        wg2, wu2, wd2 = (w[layer].astype(BF16) for w in (ffn2_w_gate, ffn2_w_up, ffn2_w_down))
        win = jnp.pad(w_in[layer], ((0, 0), (0, 7 * hw + LANES - in_cols))).astype(BF16)
        wo = w_out[layer].astype(BF16)
        conv_w8 = jnp.pad(gdn_conv_w[layer], ((0, SUBLANES - CONV_WIDTH), (0, 0)))
        head_params = jnp.zeros((SUBLANES, LANES), F32)
        head_params = head_params.at[0, N_HEADS:2 * N_HEADS].set(gdn_a_log[layer])
        head_params = head_params.at[1, N_HEADS:2 * N_HEADS].set(gdn_dt_bias[layer])
        norm_w_row = jnp.tile(gdn_norm_w[layer], N_HEADS).reshape(1, hw)
        pre = functools.partial(_pre_call, wg=wg1, wu=wu1, wd=wd1, g=row(ln1_g[layer]), b=row(ln1_b[layer]),
                                win=win, alpha=alpha)
        post = functools.partial(_post_call, wo=wo, g2=row(ln2_g[layer]), b2=row(ln2_b[layer]), wg=wg2, wu=wu2,
                                 wd=wd2, g3=row(ln3_g[layer]), b3=row(ln3_b[layer]), alpha=alpha)

        x1, aq, ak, av, gq, z, ba, ak_t, av_t = pre(yp, kv_t_seq=seq)
        shp = lambda a: a.reshape(bsz, seq, a.shape[1])
        attn = _attn_prompt_call(shp(aq), shp(ak), shp(av), tab_p)
        gdn, s_p = _gdn_call(shp(gq), shp(z), shp(ba),
                             jnp.zeros((bsz, SUBLANES, 3 * hw), F32),
                             jnp.zeros((bsz, N_HEADS, HEAD_DIM, HEAD_DIM), F32),
                             conv_w8, head_params, norm_w_row, n_valid=seq,
                             token_block=min(GDN_TOKEN_BLOCK, seq), chunk=min(GDN_CHUNK, seq),
                             batch_block=1, group=GDN_CHUNK_GROUP)
        yp = post(attn.reshape(bsz * seq, hw), gdn.reshape(bsz * seq, hw), x1)
        wp = min(w_buf, seq)
        heads5 = lambda a_t: a_t.reshape(bsz, N_HEADS, HEAD_DIM, seq).transpose(0, 3, 1, 2)[:, seq - wp:]
        st_p = (heads5(ak_t), heads5(av_t), s_p, shp(gq)[:, seq - (CONV_WIDTH - 1):])

        x1s, aqs, aks, avs, gqs, zs, bas = pre(ys)
        shs = lambda a: _pad_rows(a.reshape(dbsz, dseq, a.shape[1]), SAMPLE_T_PAD)
        heads5s = lambda a: a.reshape(dbsz, dseq, N_HEADS, HEAD_DIM)
        new_rows = lambda a: jnp.pad(heads5s(a).transpose(0, 2, 1, 3),
                                     ((0, 0), (0, 0), (0, SAMPLE_T_PAD - dseq), (0, 0)))
        by_head_t = lambda c: c.transpose(0, 2, 3, 1)
        attn_s = _attn_sample_call(shs(aqs), new_rows(aks), new_rows(avs), by_head_t(cache_attn_k[layer]),
                                   by_head_t(cache_attn_v[layer]), tab_s, sample_lows)
        gdn_s, s_s = _gdn_call(shs(gqs), shs(zs), shs(bas),
                               _pad_rows(state_conv[layer], SUBLANES, front=True), state_gdn[layer],
                               conv_w8, head_params, norm_w_row, n_valid=dseq,
                               token_block=SAMPLE_T_PAD, chunk=SAMPLE_T_PAD,
                               batch_block=GDN_SAMPLE_BATCH_BLOCK, group=1)
        unpad = lambda a: a[:, :dseq].reshape(dbsz * dseq, hw)
        ys = post(unpad(attn_s), unpad(gdn_s), x1s)
        st_s = (heads5s(aks), heads5s(avs), s_s,
                gqs.reshape(dbsz, dseq, 3 * hw)[:, dseq - (CONV_WIDTH - 1):])
        for lst, st in zip(collected, st_p + st_s):
            lst.append(st)
    outs = [jnp.stack(t, axis=0) for t in collected]
    return (yp.reshape(bsz, seq, d_model), ys.reshape(dbsz, dseq, d_model)) + tuple(outs)
```

```python
import functools
import math

import numpy as np
import jax
import jax.numpy as jnp
from jax import lax
from jax.experimental import pallas as pl
from jax.experimental.pallas import tpu as pltpu

F32 = jnp.float32
BF16 = jnp.bfloat16

HEAD_DIM = 64
N_HEADS = 8
HEAD_WIDTH = N_HEADS * HEAD_DIM
DILATED_PATTERNS = ((128, 1), (512, 4), (2048, 16))
KEYS_PER_BLOCK = 128
T5_BUCKETS = 32
T5_MAX_EXACT = 16
T5_MAX_DIST = 2048
CONV_WIDTH = 4
GDN_CHUNK = 64
LN_EPS = 1e-5
RMS_EPS = 1e-6
NEG_BIG = -1e30

LANES = 128
SUBLANES = 8
TOKEN_TILE = 256
GDN_TOKEN_BLOCK = 256
ATTN_SKEW_GROUPS = 2
GDN_CHUNK_GROUP = 4
GDN_SAMPLE_BATCH_BLOCK = 8
SAMPLE_T_PAD = 8
VMEM_LIMIT = 56 * 1024 * 1024


def _resident(shape):
    nd = len(shape)
    return pl.BlockSpec(shape, lambda *_: (0,) * nd, pipeline_mode=pl.Buffered(1))


def _layernorm(y, g, b):
    mu = jnp.mean(y, axis=-1, keepdims=True)
    yc = y - mu
    var = jnp.mean(yc * yc, axis=-1, keepdims=True)
    return yc * lax.rsqrt(var + LN_EPS) * g + b


def _swiglu(xb, wg_ref, wu_ref, wd_ref, h_scr):
    d_ff = wg_ref.shape[1]
    step = 2 * LANES
    assert d_ff % step == 0
    for c in range(d_ff // step):
        sl = slice(c * step, (c + 1) * step)
        gate = jnp.dot(xb, wg_ref[:, sl], preferred_element_type=F32)
        up = jnp.dot(xb, wu_ref[:, sl], preferred_element_type=F32)
        h_scr[:, sl] = (jax.nn.silu(gate) * up).astype(BF16)
    return jnp.dot(h_scr[...], wd_ref[...], preferred_element_type=F32)


PROJ_WIDTHS = (HEAD_WIDTH, HEAD_WIDTH, HEAD_WIDTH, 3 * HEAD_WIDTH, HEAD_WIDTH, LANES)
PROJ_EDGES = tuple(int(e) for e in np.cumsum((0,) + PROJ_WIDTHS))


def _first_half_step(x_ref, wg_ref, wu_ref, wd_ref, g_ref, b_ref, h_scr, alpha):
    x = x_ref[...]
    ff = _swiglu(x.astype(BF16), wg_ref, wu_ref, wd_ref, h_scr)
    return _layernorm(alpha * x + 0.5 * ff, g_ref[...], b_ref[...])


def _pre_kernel(x_ref, wg_ref, wu_ref, wd_ref, g_ref, b_ref, win_ref,
                x1_ref, aq_ref, ak_ref, av_ref, gq_ref, z_ref, ba_ref, h_scr, *, alpha):
    x1 = _first_half_step(x_ref, wg_ref, wu_ref, wd_ref, g_ref, b_ref, h_scr, alpha)
    x1_ref[...] = x1
    xb = x1.astype(BF16)
    for i, ref in enumerate((aq_ref, ak_ref, av_ref, gq_ref, z_ref, ba_ref)):
        ref[...] = jnp.dot(xb, win_ref[:, PROJ_EDGES[i]:PROJ_EDGES[i + 1]], preferred_element_type=F32)


def _gdn_gates(ba, hp_ref):
    beta = jax.nn.sigmoid(ba)
    g = -jnp.exp(hp_ref[0:1, :]) * jax.nn.softplus(ba + hp_ref[1:2, :])
    return beta, g


def _causal_conv_silu(ext_scr, x, cw_ref):
    tb = x.shape[0]
    hist = SUBLANES
    ext_scr[hist:hist + tb, :] = x
    conv = jnp.zeros(x.shape, F32)
    for j in range(CONV_WIDTH):
        off = hist - (CONV_WIDTH - 1) + j
        conv = conv + ext_scr[off:off + tb, :] * cw_ref[j:j + 1, :]
    ext_scr[0:hist, :] = ext_scr[tb:tb + hist, :]
    return jax.nn.silu(conv)


def _pre_prompt_kernel(x_ref, wg_ref, wu_ref, wd_ref, g_ref, b_ref, win_ref, cw_ref, hp_ref,
                       x1_ref, aq_ref, ak_ref, av_ref, akt_ref, avt_ref, qkv_ref, z_ref, bt_ref, gg_ref, tail_ref,
                       h_scr, ext_scr, *, alpha, tiles_per_seq):
    x1 = _first_half_step(x_ref, wg_ref, wu_ref, wd_ref, g_ref, b_ref, h_scr, alpha)
    x1_ref[...] = x1
    xb = x1.astype(BF16)
    proj = lambda i: jnp.dot(xb, win_ref[:, PROJ_EDGES[i]:PROJ_EDGES[i + 1]], preferred_element_type=F32)

    @pl.when(pl.program_id(0) % tiles_per_seq == 0)
    def _():
        ext_scr[0:SUBLANES, :] = jnp.zeros((SUBLANES, ext_scr.shape[1]), F32)

    tm = x1.shape[0]
    hist = SUBLANES
    ext_scr[hist:hist + tm, :] = proj(3)
    for c in range(3):
        cols = slice(HEAD_WIDTH * c, HEAD_WIDTH * (c + 1))
        conv = jnp.zeros((tm, HEAD_WIDTH), F32)
        for j in range(CONV_WIDTH):
            off = hist - (CONV_WIDTH - 1) + j
            conv = conv + ext_scr[off:off + tm, cols] * cw_ref[j:j + 1, cols]
        qkv_ref[:, cols] = jax.nn.silu(conv)
        if c == 0:
            aq_ref[...] = proj(0)
        else:
            ref, ref_t = ((ak_ref, akt_ref), (av_ref, avt_ref))[c - 1]
            kv = proj(c)
            ref[...] = kv
            ref_t[0] = kv.T
    tail_ref[0] = ext_scr[tm:tm + hist, :]
    ext_scr[0:hist, :] = ext_scr[tm:tm + hist, :]
    bt_ref[...], gg_ref[...] = _gdn_gates(proj(5), hp_ref)
    z_ref[...] = proj(4)


def _pre_call(x2d, wg, wu, wd, g, b, win, alpha):
    m, d = x2d.shape
    d_ff = wg.shape[1]
    tm = TOKEN_TILE
    assert m % tm == 0 and sum(PROJ_WIDTHS) == win.shape[1]
    tile = lambda w: pl.BlockSpec((tm, w), lambda i: (i, 0))
    return pl.pallas_call(
        functools.partial(_pre_kernel, alpha=alpha),
        grid=(m // tm,),
        in_specs=[tile(d), _resident(wg.shape), _resident(wu.shape), _resident(wd.shape),
                  _resident(g.shape), _resident(b.shape), _resident(win.shape)],
        out_specs=[tile(d)] + [tile(w) for w in PROJ_WIDTHS],
        out_shape=[jax.ShapeDtypeStruct((m, d), F32)] + [jax.ShapeDtypeStruct((m, w), F32) for w in PROJ_WIDTHS],
        scratch_shapes=[pltpu.VMEM((tm, d_ff), BF16)],
        compiler_params=pltpu.CompilerParams(dimension_semantics=("parallel",), vmem_limit_bytes=VMEM_LIMIT),
        name="pre_ffn_proj",
    )(x2d, wg, wu, wd, g, b, win)


def _pre_prompt_call(x2d, wg, wu, wd, g, b, win, conv_w8, head_params, alpha, seq):
    m, d = x2d.shape
    d_ff = wg.shape[1]
    tm = TOKEN_TILE
    hw = HEAD_WIDTH
    assert seq % tm == 0 and m % seq == 0 and sum(PROJ_WIDTHS) == win.shape[1]
    per_seq = seq // tm
    tile = lambda w: pl.BlockSpec((tm, w), lambda i: (i, 0))
    t_spec = pl.BlockSpec((1, hw, tm), lambda i: (i // per_seq, 0, i % per_seq))
    t_shape = jax.ShapeDtypeStruct((m // seq, hw, seq), F32)
    rows = lambda w: jax.ShapeDtypeStruct((m, w), F32)
    consts = (wg, wu, wd, g, b, win, conv_w8, head_params)
    return pl.pallas_call(
        functools.partial(_pre_prompt_kernel, alpha=alpha, tiles_per_seq=per_seq),
        grid=(m // tm,),
        in_specs=[tile(d)] + [_resident(c.shape) for c in consts],
        out_specs=[tile(d), tile(hw), tile(hw), tile(hw), t_spec, t_spec, tile(3 * hw), tile(hw), tile(LANES),
                   tile(LANES), pl.BlockSpec((1, SUBLANES, 3 * hw), lambda i: (i // per_seq, 0, 0))],
        out_shape=[rows(d), rows(hw), rows(hw), rows(hw), t_shape, t_shape, rows(3 * hw), rows(hw), rows(LANES),
                   rows(LANES), jax.ShapeDtypeStruct((m // seq, SUBLANES, 3 * hw), F32)],
        scratch_shapes=[pltpu.VMEM((tm, d_ff), BF16), pltpu.VMEM((tm + 2 * SUBLANES, 3 * hw), F32)],
        compiler_params=pltpu.CompilerParams(dimension_semantics=("arbitrary",), vmem_limit_bytes=VMEM_LIMIT),
        name="pre_ffn_proj_gdnprep",
    )(x2d, *consts)


def _post_kernel(attn_ref, gdn_ref, x1_ref, wo_ref, g2_ref, b2_ref, wg_ref, wu_ref, wd_ref, g3_ref, b3_ref,
                 y_ref, h_scr, *, alpha):
    mix = jnp.dot(attn_ref[...].astype(BF16), wo_ref[0:HEAD_WIDTH, :], preferred_element_type=F32)
    mix = mix + jnp.dot(gdn_ref[...].astype(BF16), wo_ref[HEAD_WIDTH:2 * HEAD_WIDTH, :],
                        preferred_element_type=F32)
    x2 = _layernorm(alpha * x1_ref[...] + mix, g2_ref[...], b2_ref[...])
    ff = _swiglu(x2.astype(BF16), wg_ref, wu_ref, wd_ref, h_scr)
    y_ref[...] = _layernorm(alpha * x2 + 0.5 * ff, g3_ref[...], b3_ref[...])


def _post_call(attn, gdn, x1, wo, g2, b2, wg, wu, wd, g3, b3, alpha):
    m, d = x1.shape
    tm = TOKEN_TILE
    assert m % tm == 0
    tile = lambda w: pl.BlockSpec((tm, w), lambda i: (i, 0))
    consts = (wo, g2, b2, wg, wu, wd, g3, b3)
    return pl.pallas_call(
        functools.partial(_post_kernel, alpha=alpha),
        grid=(m // tm,),
        in_specs=[tile(HEAD_WIDTH), tile(HEAD_WIDTH), tile(d)] + [_resident(c.shape) for c in consts],
        out_specs=tile(d),
        out_shape=jax.ShapeDtypeStruct((m, d), F32),
        scratch_shapes=[pltpu.VMEM((tm, wg.shape[1]), BF16)],
        compiler_params=pltpu.CompilerParams(dimension_semantics=("parallel",), vmem_limit_bytes=VMEM_LIMIT),
        name="post_out_ffn",
    )(attn, gdn, x1, *consts)


def _t5_bucket_np(dist):
    n = np.maximum(dist, 0)
    nf = np.maximum(n, 1).astype(np.float32)
    large = T5_MAX_EXACT + (np.log(nf / np.float32(T5_MAX_EXACT)) / np.float32(math.log(T5_MAX_DIST / T5_MAX_EXACT))
                            * np.float32(T5_BUCKETS - T5_MAX_EXACT)).astype(np.int32)
    large = np.minimum(large, T5_BUCKETS - 1)
    return np.where(n < T5_MAX_EXACT, n, large).astype(np.int32)


def _t5_table_kernel(rb_ref, bk_ref, out_ref):
    for h in range(N_HEADS):
        bk = bk_ref[0, h % bk_ref.shape[1]]
        acc = jnp.full(bk.shape, NEG_BIG, F32)
        for b in range(T5_BUCKETS):
            acc = jnp.where(bk == b, rb_ref[b, h], acc)
        out_ref[0, h] = acc


def _t5_table_call(rel_bias, buckets):
    p, hb, r, c = buckets.shape
    assert hb in (1, N_HEADS)
    return pl.pallas_call(
        _t5_table_kernel,
        grid=(p,),
        in_specs=[pl.BlockSpec(memory_space=pltpu.SMEM),
                  pl.BlockSpec((1, hb, r, c), lambda i: (i, 0, 0, 0))],
        out_specs=pl.BlockSpec((1, N_HEADS, r, c), lambda i: (i, 0, 0, 0)),
        out_shape=jax.ShapeDtypeStruct((p, N_HEADS, r, c), F32),
        compiler_params=pltpu.CompilerParams(dimension_semantics=("parallel",)),
        name="t5_bias_table",
    )(rel_bias, buckets)


def _prompt_buckets():
    nb = KEYS_PER_BLOCK
    qi = np.arange(nb)[:, None]
    ki = np.arange(2 * nb)[None, :]
    dist = qi + nb - ki
    valid = (dist >= 0) & (dist <= nb)
    out = []
    for window, dil in DILATED_PATTERNS:
        assert window // dil == nb
        out.append(np.where(valid, _t5_bucket_np(dist * dil), -1))
    return np.stack(out).astype(np.int32)[:, None]


def _nt_dot(a, b):
    return lax.dot_general(a, b, (((1,), (1,)), ((), ())), preferred_element_type=F32)


def _emit_skewed(stages, items, n_groups):
    n_groups = max(1, min(n_groups, len(items)))
    groups = [items[i::n_groups] for i in range(n_groups)]
    for step in range(len(stages) + n_groups - 1):
        for gi, group in enumerate(groups):
            s = step - gi
            if 0 <= s < len(stages):
                for item in group:
                    stages[s](item)


def _attn_prompt_kernel(q_ref, k_ref, v_ref, tab_ref, o_ref, og_scr, lg_scr, *, seq):
    nb = KEYS_PER_BLOCK
    hd = HEAD_DIM
    n_pat = len(DILATED_PATTERNS)
    heads = LANES // hd
    for g, (window, dil) in enumerate(DILATED_PATTERNS):
        n_blocks = seq // (dil * nb)

        def blocks(starts, with_prev, g=g, dil=dil):
            lane_head = lax.broadcasted_iota(jnp.int32, (nb, LANES), 1) // hd
            row_sl, qs, ks, vs, kps, vps = [], [], [], [], [], []
            for start in starts:
                rows = pl.ds(start, nb, stride=dil) if dil > 1 else pl.ds(start, nb)
                row_sl.append(rows)
                qs.append(q_ref[0, rows, :] * hd ** -0.5)
                ks.append(k_ref[0, rows, :].astype(BF16))
                vs.append(v_ref[0, rows, :].astype(BF16))
                if with_prev:
                    rows_p = (pl.ds(start - dil * nb, nb, stride=dil) if dil > 1 else pl.ds(start - nb, nb))
                    kps.append(k_ref[0, rows_p, :].astype(BF16))
                    vps.append(v_ref[0, rows_p, :].astype(BF16))
            chains = [(u, hh) for u in range(len(starts)) for hh in range(heads)]
            s_c, s_p, m, p_c, p_p, den, acc = {}, {}, {}, {}, {}, {}, {}

            def scores(c):
                u, hh = c
                qb = jnp.where(lane_head == hh, qs[u], 0.0).astype(BF16)
                s_c[c] = _nt_dot(qb, ks[u]) + tab_ref[g, hh, :, nb:2 * nb]
                if with_prev:
                    s_p[c] = _nt_dot(qb, kps[u]) + tab_ref[g, hh, :, 0:nb]

            def row_max(c):
                both = jnp.maximum(s_c[c], s_p[c]) if with_prev else s_c[c]
                m[c] = jnp.max(both, axis=-1, keepdims=True)

            def probs(c):
                p_c[c] = jnp.exp(s_c[c] - m[c])
                both = p_c[c]
                if with_prev:
                    p_p[c] = jnp.exp(s_p[c] - m[c])
                    both = both + p_p[c]
                den[c] = jnp.sum(both, axis=-1, keepdims=True)

            def values(c):
                u, hh = c
                acc[c] = jnp.dot(p_c[c].astype(BF16), vs[u], preferred_element_type=F32)
                if with_prev:
                    acc[c] = acc[c] + jnp.dot(p_p[c].astype(BF16), vps[u], preferred_element_type=F32)

            def finish(c):
                acc[c] = acc[c] / den[c]
                den[c] = m[c] + jnp.log(den[c])

            _emit_skewed((scores, row_max, probs, values, finish), chains, ATTN_SKEW_GROUPS)
            for u in range(len(starts)):
                out, lse = acc[u, 0], den[u, 0]
                for hh in range(1, heads):
                    out = jnp.where(lane_head == hh, acc[u, hh], out)
                    lse = jnp.where(lane_head == hh, den[u, hh], lse)
                og_scr[g, row_sl[u], :] = out
                lg_scr[g, row_sl[u], :] = jnp.broadcast_to(lse, (nb, LANES))

        def unroll_of(count):
            return next(u for u in (4, 3, 2, 1) if count % u == 0)

        u_first = unroll_of(dil)

        def first_body(i, carry, blocks=blocks, u_first=u_first):
            blocks([i * u_first + j for j in range(u_first)], False)
            return carry

        lax.fori_loop(0, dil // u_first, first_body, 0)
        if n_blocks > 1:
            n_rest = dil * (n_blocks - 1)
            u_rest = unroll_of(n_rest)

            def rest_body(i, carry, blocks=blocks, dil=dil, n_blocks=n_blocks, u_rest=u_rest):
                starts = []
                for j in range(u_rest):
                    idx = i * u_rest + j
                    starts.append(idx // (n_blocks - 1) + dil * nb * (idx % (n_blocks - 1) + 1))
                blocks(starts, True)
                return carry

            lax.fori_loop(0, n_rest // u_rest, rest_body, 0)

    rb = 2 * nb

    def merge_body(i, carry):
        rows = pl.ds(pl.multiple_of(i * rb, rb), rb)
        lses = [lg_scr[g, rows, :] for g in range(n_pat)]
        top = functools.reduce(jnp.maximum, lses)
        ws = [jnp.exp(l - top) for l in lses]
        num = functools.reduce(lambda a, b: a + b, [w * og_scr[g, rows, :] for g, w in enumerate(ws)])
        o_ref[0, rows, :] = num / functools.reduce(lambda a, b: a + b, ws)
        return carry

    lax.fori_loop(0, seq // rb, merge_body, 0)


def _attn_prompt_call(aq, ak, av, tab):
    bsz, seq, width = aq.shape
    assert width == HEAD_WIDTH and seq % (DILATED_PATTERNS[-1][1] * KEYS_PER_BLOCK) == 0
    n_pat = len(DILATED_PATTERNS)
    heads_per_step = LANES // HEAD_DIM
    qkv_spec = pl.BlockSpec((1, seq, LANES), lambda h, b: (b, 0, h))
    return pl.pallas_call(
        functools.partial(_attn_prompt_kernel, seq=seq),
        grid=(N_HEADS // heads_per_step, bsz),
        in_specs=[qkv_spec, qkv_spec, qkv_spec,
                  pl.BlockSpec((n_pat, heads_per_step, KEYS_PER_BLOCK, 2 * KEYS_PER_BLOCK),
                               lambda h, b: (0, h, 0, 0))],
        out_specs=pl.BlockSpec((1, seq, LANES), lambda h, b: (b, 0, h)),
        out_shape=jax.ShapeDtypeStruct((bsz, seq, width), F32),
        scratch_shapes=[pltpu.VMEM((n_pat, seq, LANES), F32)] * 2,
        compiler_params=pltpu.CompilerParams(dimension_semantics=("parallel", "parallel"),
                                             vmem_limit_bytes=VMEM_LIMIT),
        name="attn_prompt",
    )(aq, ak, av, tab)


def _sample_layout(w_buf, t_new):
    assert w_buf % LANES == 0 and t_new <= SAMPLE_T_PAD
    row_of_col = np.full((w_buf + LANES,), -1, np.int64)
    row_of_col[:w_buf + t_new] = np.arange(w_buf + t_new)
    buckets = np.full((len(DILATED_PATTERNS), 1, SAMPLE_T_PAD, w_buf + LANES), -1, np.int32)
    lows = []
    for g, (window, dil) in enumerate(DILATED_PATTERNS):
        lows.append(max(0, w_buf - window) // LANES * LANES)
        for t in range(t_new):
            dist = w_buf + t - row_of_col
            ok = (row_of_col >= 0) & (dist >= 0) & (dist <= window) & (dist % dil == 0)
            assert int(ok.sum()) == min(window, w_buf + t) // dil + 1 and not ok[:lows[g]].any()
            buckets[g, 0, t] = np.where(ok, _t5_bucket_np(dist), -1)
    return tuple(lows), buckets


def _attn_sample_kernel(q_ref, kn_ref, vn_ref, kt_ref, vt_ref, tab_ref, o_ref, *, lows):
    tp = SAMPLE_T_PAD
    hd = HEAD_DIM
    w_buf = kt_ref.shape[3]
    n_pat = len(DILATED_PATTERNS)
    add = lambda x, y: x + y
    rowmax = lambda x: jnp.max(x, axis=-1, keepdims=True)
    rowsum = lambda x: jnp.sum(x, axis=-1, keepdims=True)
    heads = range(N_HEADS)
    q8 = q_ref[0] * hd ** -0.5
    s_buf = {h: jnp.dot(q8[:, hd * h:hd * (h + 1)], kt_ref[0, h], preferred_element_type=F32) for h in heads}
    s_new = {h: _nt_dot(q8[:, hd * h:hd * (h + 1)], kn_ref[0, h]) for h in heads}
    p_buf, p_new, den, lse = {}, {}, {}, {}
    for h in heads:
        for g in range(n_pat):
            lo = lows[g]
            l_buf = s_buf[h][:, lo:] + tab_ref[g, h, :, lo:w_buf]
            l_new = s_new[h] + tab_ref[g, h, :, w_buf:w_buf + tp]
            m = jnp.maximum(rowmax(l_buf), rowmax(l_new))
            p_buf[h, g] = jnp.exp(l_buf - m)
            p_new[h, g] = jnp.exp(l_new - m)
            den[h, g] = rowsum(p_buf[h, g]) + rowsum(p_new[h, g])
            lse[h, g] = m + jnp.log(den[h, g])
    edges = sorted(set(lows)) + [w_buf]
    outs = {}
    for h in heads:
        top = functools.reduce(jnp.maximum, [lse[h, g] for g in range(n_pat)])
        ws = [jnp.exp(lse[h, g] - top) for g in range(n_pat)]
        w_sum = functools.reduce(add, ws)
        coef = [ws[g] / (w_sum * den[h, g]) for g in range(n_pat)]
        acc = jnp.dot(functools.reduce(add, [coef[g] * p_new[h, g] for g in range(n_pat)]), vn_ref[0, h],
                      preferred_element_type=F32)
        for e0, e1 in zip(edges[:-1], edges[1:]):
            mix = functools.reduce(add, [coef[g] * p_buf[h, g][:, e0 - lows[g]:e1 - lows[g]]
                                         for g in range(n_pat) if lows[g] <= e0])
            acc = acc + _nt_dot(mix, vt_ref[0, h, :, e0:e1])
        outs[h] = acc
    o_ref[0] = jnp.concatenate([outs[h] for h in heads], axis=1)


def _attn_sample_call(aq8, k_new, v_new, cache_kt, cache_vt, tab, lows):
    bsz, n_heads, hd, w_buf = cache_kt.shape
    q_spec = pl.BlockSpec((1, SAMPLE_T_PAD, n_heads * hd), lambda b: (b, 0, 0))
    new_spec = pl.BlockSpec((1, n_heads, SAMPLE_T_PAD, hd), lambda b: (b, 0, 0, 0))
    buf_spec = pl.BlockSpec((1, n_heads, hd, w_buf), lambda b: (b, 0, 0, 0))
    return pl.pallas_call(
        functools.partial(_attn_sample_kernel, lows=lows),
        grid=(bsz,),
        in_specs=[q_spec, new_spec, new_spec, buf_spec, buf_spec, _resident(tab.shape)],
        out_specs=q_spec,
        out_shape=jax.ShapeDtypeStruct((bsz, SAMPLE_T_PAD, n_heads * hd), F32),
        compiler_params=pltpu.CompilerParams(dimension_semantics=("parallel",), vmem_limit_bytes=VMEM_LIMIT),
        name="attn_sample",
    )(aq8, k_new, v_new, cache_kt, cache_vt, tab)


def _aligned(index, size):
    return index * size if isinstance(index, int) else pl.multiple_of(index * size, size)


def _split3(x):
    h1 = x.astype(BF16)
    r1 = x - h1.astype(F32)
    h2 = r1.astype(BF16)
    h3 = (r1 - h2.astype(F32)).astype(BF16)
    return h1, h2, h3


def _head_sums(x, ones_ref):
    h1, h2, _ = _split3(x)
    return (jnp.dot(h1, ones_ref[...], preferred_element_type=F32)
            + jnp.dot(h2, ones_ref[...], preferred_element_type=F32))


def _store_normed_qk(act, q_dst, k_dst, ones_ref):
    hw = HEAD_WIDTH
    q = act[:, 0:hw]
    k = act[:, hw:2 * hw]
    q_dst[...] = q * lax.rsqrt(_head_sums(q * q, ones_ref) + RMS_EPS) * HEAD_DIM ** -0.5
    k_dst[...] = k * lax.rsqrt(_head_sums(k * k, ones_ref) + RMS_EPS)


def _gdn_kernel(x_ref, z_ref, ba_ref, cb_ref, s0_ref, cw_ref, hp_ref, nw_ref, ones_ref,
                o_ref, sfin_ref,
                ext_scr, q_scr, k_scr, v_scr, bt_scr, g_scr, *core_scr, chunk, n_valid, group):
    n_bb, tb = x_ref.shape[0], x_ref.shape[1]
    t = pl.program_id(1)

    @pl.when(t == 0)
    def _():
        ext_scr[:, 0:SUBLANES, :] = cb_ref[...]

    for bb in range(n_bb):
        act = _causal_conv_silu(ext_scr.at[bb], x_ref[bb], cw_ref)
        _store_normed_qk(act, q_scr.at[bb], k_scr.at[bb], ones_ref)
        v_scr[bb] = act[:, 2 * HEAD_WIDTH:3 * HEAD_WIDTH]
        ba = ba_ref[bb]
        live = t * tb + lax.broadcasted_iota(jnp.int32, ba.shape, 0) < n_valid
        beta, g = _gdn_gates(ba, hp_ref)
        bt_scr[bb] = jnp.where(live, beta, 0.0)
        g_scr[bb] = jnp.where(live, g, 0.0)
    _gdn_core(s0_ref, z_ref, nw_ref, ones_ref, o_ref, sfin_ref, q_scr, k_scr, v_scr, 0, bt_scr, g_scr,
              *core_scr, chunk=chunk, group=group)


def _gdn_prepared_kernel(qkv_ref, z_ref, bt_ref, g_ref, s0_ref, nw_ref, ones_ref, o_ref, sfin_ref,
                         q_scr, k_scr, *core_scr, chunk, group):
    for bb in range(qkv_ref.shape[0]):
        _store_normed_qk(qkv_ref[bb], q_scr.at[bb], k_scr.at[bb], ones_ref)
    _gdn_core_pairs(s0_ref, z_ref, nw_ref, ones_ref, o_ref, sfin_ref, q_scr, k_scr, qkv_ref, 2 * HEAD_WIDTH,
                    bt_ref, g_ref, *core_scr, group=group)


def _gdn_core_pairs(s0_ref, z_ref, nw_ref, ones_ref, o_ref, sfin_ref, q_scr, k_scr, v_scr, v_lane0, bt_scr, g_scr,
                    oc_scr, s_scr, u_scr, w_scr, in_scr, qd_scr, kd_scr, el_scr, *, group):
    chunk = HEAD_DIM
    n_bb, tb = q_scr.shape[0], q_scr.shape[1]
    n_chunks = tb // chunk
    n_pairs = N_HEADS // 2
    hd = HEAD_DIM
    t = pl.program_id(1)
    dot = functools.partial(jnp.dot, preferred_element_type=F32)

    lane = lax.broadcasted_iota(jnp.int32, (chunk, LANES), 1)
    row = lax.broadcasted_iota(jnp.int32, (chunk, LANES), 0)
    second = lane >= hd
    col = lane - jnp.where(second, hd, 0)
    tril = row >= col
    strict = row > col
    eye2 = (row == col).astype(F32)
    tril_b = (lax.broadcasted_iota(jnp.int32, (chunk, chunk), 0)
              >= lax.broadcasted_iota(jnp.int32, (chunk, chunk), 1)).astype(BF16)
    n_doublings = int(math.log2(chunk)) - 1

    def blockdiag(x):
        return jnp.concatenate([jnp.where(second, 0.0, x), jnp.where(second, x, 0.0)], axis=0)

    def per_head(cols, j):
        return jnp.where(second, cols[:, 2 * j + 1:2 * j + 2], cols[:, 2 * j:2 * j + 1])

    @pl.when(t == 0)
    def _():
        for bb in range(n_bb):
            for j in range(n_pairs):
                s_scr[bb, j] = jnp.concatenate([s0_ref[bb, 2 * j], s0_ref[bb, 2 * j + 1]], axis=1)

    def state_free_part(gi, carry):
        probs, shared = [], {}
        for bb in range(n_bb):
            for cg in range(group):
                c_idx = gi * group + cg
                rows = pl.ds(_aligned(c_idx, chunk), chunk)
                gc = functools.reduce(lambda a, b: a + b,
                                      [dot(tril_b, p) for p in _split3(g_scr[bb, rows, :])])
                shared[bb, cg] = (rows, c_idx, gc[:, N_HEADS:2 * N_HEADS], gc.T[N_HEADS:2 * N_HEADS, :],
                                  bt_scr[bb, rows, :])
                probs += [(bb, cg, j) for j in range(n_pairs)]
        kk, qk, decay, a, t_inv, pw, vb, kbg = {}, {}, {}, {}, {}, {}, {}, {}
        for p in probs:
            bb, cg, j = p
            rows, c_idx, gc, gct, bt = shared[bb, cg]
            pl_ = pl.ds(LANES * j, LANES)
            q2 = q_scr[bb, rows, pl_]
            k2 = k_scr[bb, rows, pl_]
            beta2 = per_head(bt, j)
            gcc = per_head(gc, j)
            gcr = jnp.concatenate([gct[2 * j:2 * j + 1, :], gct[2 * j + 1:2 * j + 2, :]], axis=1)
            g_last = gcc[chunk - 1:chunk, :]
            e_gc = jnp.exp(gcc)
            kb2 = k2 * beta2
            k_bd = blockdiag(k2)
            kk[p] = _nt_dot(kb2, k_bd)
            qk[p] = _nt_dot(q2, k_bd)
            vb[p] = v_scr[bb, rows, pl.ds(v_lane0 + LANES * j, LANES)] * beta2
            kbg[p] = kb2 * e_gc
            qd_scr[bb, rows, pl_] = q2 * e_gc
            kd_scr[bb, rows, pl_] = k2 * jnp.exp(g_last - gcc)
            el_scr[bb, pl.ds(_aligned(c_idx, SUBLANES), SUBLANES), pl_] = jnp.broadcast_to(
                jnp.exp(g_last), (SUBLANES, LANES))
            decay[p] = jnp.where(tril, jnp.exp(jnp.where(tril, gcc - gcr, 0.0)), 0.0)
        for p in probs:
            bb, cg, j = p
            rows = shared[bb, cg][0]
            a[p] = jnp.where(strict, kk[p] * decay[p], 0.0)
            in_scr[bb, rows, pl.ds(LANES * j, LANES)] = jnp.where(tril, qk[p] * decay[p], 0.0)
            t_inv[p] = eye2 - a[p]
        for p in probs:
            pw[p] = dot(a[p], blockdiag(a[p]))
        for _ in range(n_doublings - 1):
            nxt_pw, nxt_t = {}, {}
            for p in probs:
                nxt_pw[p] = dot(pw[p], blockdiag(pw[p]))
                nxt_t[p] = dot(t_inv[p], blockdiag(eye2 + pw[p]))
            pw, t_inv = nxt_pw, nxt_t
        for p in probs:
            t_inv[p] = dot(t_inv[p], blockdiag(eye2 + pw[p]))
        for p in probs:
            bb, cg, j = p
            rows = shared[bb, cg][0]
            uw = dot(t_inv[p], jnp.concatenate([blockdiag(vb[p]), blockdiag(kbg[p])], axis=1))
            u_scr[bb, rows, pl.ds(LANES * j, LANES)] = uw[:, 0:LANES]
            w_scr[bb, rows, pl.ds(LANES * j, LANES)] = uw[:, LANES:2 * LANES]
        return carry

    def state_part(ci_, carry):
        rows = pl.ds(_aligned(ci_, chunk), chunk)
        el_rows = pl.ds(_aligned(ci_, SUBLANES), SUBLANES)
        probs = [(bb, j) for bb in range(n_bb) for j in range(n_pairs)]
        s_old, both, v_new, o_in, s_add = {}, {}, {}, {}, {}
        for p in probs:
            bb, j = p
            pl_ = pl.ds(LANES * j, LANES)
            s_old[p] = s_scr[bb, j]
            lhs = jnp.concatenate([w_scr[bb, rows, pl_], qd_scr[bb, rows, pl_]], axis=0)
            both[p] = dot(lhs, blockdiag(s_old[p]))
        for p in probs:
            bb, j = p
            pl_ = pl.ds(LANES * j, LANES)
            v_new[p] = u_scr[bb, rows, pl_] - both[p][0:chunk]
            o_in[p] = dot(in_scr[bb, rows, pl_], blockdiag(v_new[p]))
            cross = lax.dot_general(kd_scr[bb, rows, pl_], v_new[p], (((0,), (0,)), ((), ())),
                                    preferred_element_type=F32)
            s_add[p] = jnp.where(second, cross[hd:2 * hd], cross[0:hd])
        for p in probs:
            bb, j = p
            pl_ = pl.ds(LANES * j, LANES)
            oc_scr[bb, rows, pl_] = both[p][chunk:2 * chunk] + o_in[p]
            s_scr[bb, j] = s_old[p] * el_scr[bb, el_rows, pl_][0:1, :] + s_add[p]
        return carry

    assert n_chunks % group == 0
    if n_chunks == group:
        state_free_part(0, 0)
    else:
        lax.fori_loop(0, n_chunks // group, state_free_part, 0)
    if n_chunks == 1:
        state_part(0, 0)
    else:
        lax.fori_loop(0, n_chunks, state_part, 0)

    for bb in range(n_bb):
        o = oc_scr[bb]
        o = o * lax.rsqrt(_head_sums(o * o, ones_ref) * (1.0 / hd) + RMS_EPS) * nw_ref[...]
        o_ref[bb] = o * jax.nn.silu(z_ref[bb])

    @pl.when(t == pl.num_programs(1) - 1)
    def _():
        for bb in range(n_bb):
            for j in range(n_pairs):
                sfin_ref[bb, 2 * j] = s_scr[bb, j][:, 0:hd]
                sfin_ref[bb, 2 * j + 1] = s_scr[bb, j][:, hd:2 * hd]


def _gdn_core(s0_ref, z_ref, nw_ref, ones_ref, o_ref, sfin_ref, q_scr, k_scr, v_scr, v_lane0, bt_scr, g_scr,
              oc_scr, s_scr, uw_scr, in_scr, qd_scr, kd_scr, el_scr, *, chunk, group):
    n_bb, tb = q_scr.shape[0], q_scr.shape[1]
    n_chunks = tb // chunk
    hd = HEAD_DIM
    t = pl.program_id(1)

    @pl.when(t == 0)
    def _():
        s_scr[...] = s0_ref[...]

    ri = lax.broadcasted_iota(jnp.int32, (chunk, chunk), 0)
    ci = lax.broadcasted_iota(jnp.int32, (chunk, chunk), 1)
    tril = ri >= ci
    strict = ri > ci
    eye_c = (ri == ci).astype(F32)
    tril_b = tril.astype(BF16)
    n_doublings = int(math.log2(chunk)) - 1
    assert 2 ** (n_doublings + 1) == chunk

    dot = functools.partial(jnp.dot, preferred_element_type=F32)

    def state_free_part(gi, carry):
        probs, shared = [], {}
        for bb in range(n_bb):
            for cg in range(group):
                c_idx = gi * group + cg
                rows = pl.ds(_aligned(c_idx, chunk), chunk)
                gc = functools.reduce(lambda a, b: a + b,
                                      [dot(tril_b, p) for p in _split3(g_scr[bb, rows, :])])
                gct = gc.T
                g_last = gc[chunk - 1:chunk, :]
                el_rows = pl.ds(_aligned(c_idx, SUBLANES), SUBLANES)
                el_scr[bb, el_rows, :] = jnp.broadcast_to(jnp.exp(g_last), (SUBLANES, LANES))
                shared[bb, cg] = (rows, gc, gct, jnp.exp(gc), jnp.exp(g_last - gc), bt_scr[bb, rows, :])
                probs += [(bb, cg, h) for h in range(N_HEADS)]
        kk, qk, rhs, decay, a, t_inv, pw = {}, {}, {}, {}, {}, {}, {}
        for p in probs:
            bb, cg, h = p
            rows, gc, gct, e_gc, e_rem, bt = shared[bb, cg]
            hl = pl.ds(hd * h, hd)
            gl = N_HEADS + h
            qh = q_scr[bb, rows, hl]
            kh = k_scr[bb, rows, hl]
            beta = bt[:, h:h + 1]
            kb = kh * beta
            kk[p] = _nt_dot(kb, kh)
            qk[p] = _nt_dot(qh, kh)
            v_h = v_scr[bb, rows, pl.ds(v_lane0 + hd * h, hd)]
            rhs[p] = jnp.concatenate([v_h * beta, kb * e_gc[:, gl:gl + 1]], axis=1)
            qd_scr[bb, rows, hl] = qh * e_gc[:, gl:gl + 1]
            kd_scr[bb, rows, hl] = kh * e_rem[:, gl:gl + 1]
            diff = gc[:, gl:gl + 1] - gct[gl:gl + 1, :]
            decay[p] = jnp.where(tril, jnp.exp(jnp.where(tril, diff, 0.0)), 0.0)
        for p in probs:
            bb, cg, h = p
            rows = shared[bb, cg][0]
            a[p] = jnp.where(strict, kk[p] * decay[p], 0.0)
            in_scr[bb, rows, LANES * h:LANES * h + chunk] = jnp.where(tril, qk[p] * decay[p], 0.0)
            t_inv[p] = eye_c - a[p]
        for p in probs:
            pw[p] = dot(a[p], a[p])
        for _ in range(n_doublings - 1):
            nxt_pw, nxt_t = {}, {}
            for p in probs:
                nxt_pw[p] = dot(pw[p], pw[p])
                nxt_t[p] = dot(t_inv[p], eye_c + pw[p])
            pw, t_inv = nxt_pw, nxt_t
        for p in probs:
            t_inv[p] = dot(t_inv[p], eye_c + pw[p])
        for p in probs:
            bb, cg, h = p
            rows = shared[bb, cg][0]
            uw_scr[bb, rows, LANES * h:LANES * (h + 1)] = dot(t_inv[p], rhs[p])
        return carry

    def state_part(ci_, carry):
        rows = pl.ds(_aligned(ci_, chunk), chunk)
        el_rows = pl.ds(_aligned(ci_, SUBLANES), SUBLANES)
        probs = [(bb, h) for bb in range(n_bb) for h in range(N_HEADS)]
        s_old, uw, both, v_new, o_in, s_add = {}, {}, {}, {}, {}, {}
        for p in probs:
            bb, h = p
            s_old[p] = s_scr[bb, h]
            uw[p] = uw_scr[bb, rows, LANES * h:LANES * (h + 1)]
            lhs = jnp.concatenate([uw[p][:, hd:2 * hd], qd_scr[bb, rows, pl.ds(hd * h, hd)]], axis=0)
            both[p] = dot(lhs, s_old[p])
        for p in probs:
            bb, h = p
            v_new[p] = uw[p][:, 0:hd] - both[p][0:chunk]
            o_in[p] = dot(in_scr[bb, rows, LANES * h:LANES * h + chunk], v_new[p])
            s_add[p] = lax.dot_general(kd_scr[bb, rows, pl.ds(hd * h, hd)], v_new[p],
                                       (((0,), (0,)), ((), ())), preferred_element_type=F32)
        for p in probs:
            bb, h = p
            e_last = el_scr[bb, el_rows, :][0:1, N_HEADS + h:N_HEADS + h + 1]
            oc_scr[bb, rows, pl.ds(hd * h, hd)] = both[p][chunk:2 * chunk] + o_in[p]
            s_scr[bb, h] = s_old[p] * e_last + s_add[p]
        return carry

    assert n_chunks % group == 0
    if n_chunks == group:
        state_free_part(0, 0)
    else:
        lax.fori_loop(0, n_chunks // group, state_free_part, 0)
    if n_chunks == 1:
        state_part(0, 0)
    else:
        lax.fori_loop(0, n_chunks, state_part, 0)

    for bb in range(n_bb):
        o = oc_scr[bb]
        o = o * lax.rsqrt(_head_sums(o * o, ones_ref) * (1.0 / hd) + RMS_EPS) * nw_ref[...]
        o_ref[bb] = o * jax.nn.silu(z_ref[bb])

    @pl.when(t == pl.num_programs(1) - 1)
    def _():
        sfin_ref[...] = s_scr[...]


def _gdn_core_scratch(nbb, tb, chunk):
    hw = HEAD_WIDTH
    return ([pltpu.VMEM((nbb, tb, hw), F32)]
            + [pltpu.VMEM((nbb, N_HEADS, HEAD_DIM, HEAD_DIM), F32)]
            + [pltpu.VMEM((nbb, tb, N_HEADS * LANES), F32)] * 2
            + [pltpu.VMEM((nbb, tb, hw), F32)] * 2
            + [pltpu.VMEM((nbb, (tb // chunk) * SUBLANES, LANES), F32)])


def _head_ones():
    ones_bd = np.kron(np.eye(N_HEADS, dtype=np.float32), np.ones((HEAD_DIM, HEAD_DIM), np.float32))
    return jnp.asarray(ones_bd, BF16)


def _gdn_call(gq, z, ba, conv_buf8, s0, conv_w8, head_params, norm_w_row, n_valid, token_block, chunk,
              batch_block, group):
    bsz, seq, w3 = gq.shape
    hw = HEAD_WIDTH
    assert w3 == 3 * hw and seq % token_block == 0 and token_block % chunk == 0 and bsz % batch_block == 0
    tb = token_block
    nbb = batch_block
    ones_bd = _head_ones()
    blk = lambda w: pl.BlockSpec((nbb, tb, w), lambda b, t: (b, t, 0))
    per_b3 = lambda s: pl.BlockSpec((nbb,) + s, lambda b, t: (b,) + (0,) * len(s))
    return pl.pallas_call(
        functools.partial(_gdn_kernel, chunk=chunk, n_valid=n_valid, group=group),
        grid=(bsz // nbb, seq // tb),
        in_specs=[blk(w3), blk(hw), blk(LANES), per_b3((SUBLANES, w3)), per_b3((N_HEADS, HEAD_DIM, HEAD_DIM)),
                  _resident(conv_w8.shape), _resident(head_params.shape), _resident(norm_w_row.shape),
                  _resident(ones_bd.shape)],
        out_specs=[blk(hw), per_b3((N_HEADS, HEAD_DIM, HEAD_DIM))],
        out_shape=[jax.ShapeDtypeStruct((bsz, seq, hw), F32),
                   jax.ShapeDtypeStruct((bsz, N_HEADS, HEAD_DIM, HEAD_DIM), F32)],
        scratch_shapes=[pltpu.VMEM((nbb, tb + 2 * SUBLANES, w3), F32)]
                       + [pltpu.VMEM((nbb, tb, hw), F32)] * 3
                       + [pltpu.VMEM((nbb, tb, LANES), F32)] * 2
                       + _gdn_core_scratch(nbb, tb, chunk),
        compiler_params=pltpu.CompilerParams(dimension_semantics=("parallel", "arbitrary"),
                                             vmem_limit_bytes=VMEM_LIMIT),
        name="gdn",
    )(gq, z, ba, conv_buf8, s0, conv_w8, head_params, norm_w_row, ones_bd)


def _gdn_prepared_call(qkv, z, beta, g, s0, norm_w_row, token_block, chunk, group):
    bsz, seq, w3 = qkv.shape
    hw = HEAD_WIDTH
    assert w3 == 3 * hw and seq % token_block == 0 and token_block % chunk == 0
    tb = token_block
    ones_bd = _head_ones()
    blk = lambda w: pl.BlockSpec((1, tb, w), lambda b, t: (b, t, 0))
    per_b3 = lambda s: pl.BlockSpec((1,) + s, lambda b, t: (b,) + (0,) * len(s))
    return pl.pallas_call(
        functools.partial(_gdn_prepared_kernel, chunk=chunk, group=group),
        grid=(bsz, seq // tb),
        in_specs=[blk(w3), blk(hw), blk(LANES), blk(LANES), per_b3((N_HEADS, HEAD_DIM, HEAD_DIM)),
                  _resident(norm_w_row.shape), _resident(ones_bd.shape)],
        out_specs=[blk(hw), per_b3((N_HEADS, HEAD_DIM, HEAD_DIM))],
        out_shape=[jax.ShapeDtypeStruct((bsz, seq, hw), F32),
                   jax.ShapeDtypeStruct((bsz, N_HEADS, HEAD_DIM, HEAD_DIM), F32)],
        scratch_shapes=[pltpu.VMEM((1, tb, hw), F32)] * 3
                       + [pltpu.VMEM((1, N_HEADS // 2, HEAD_DIM, LANES), F32)]
                       + [pltpu.VMEM((1, tb, hw), F32)] * 5
                       + [pltpu.VMEM((1, (tb // chunk) * SUBLANES, hw), F32)],
        compiler_params=pltpu.CompilerParams(dimension_semantics=("parallel", "arbitrary"),
                                             vmem_limit_bytes=VMEM_LIMIT),
        name="gdn_prepared",
    )(qkv, z, beta, g, s0, norm_w_row, ones_bd)


def _pad_rows(x, rows, front=False):
    extra = rows - x.shape[1]
    cfg = [(0, 0)] * x.ndim
    cfg[1] = (extra, 0) if front else (0, extra)
    return jnp.pad(x, cfg)


def kernel(x_prompt, x_sample, cache_attn_k, cache_attn_v, state_gdn, state_conv, rel_bias, ln1_g, ln1_b,
           ffn1_w_gate, ffn1_w_up, ffn1_w_down, w_in, w_out, gdn_conv_w, gdn_a_log, gdn_dt_bias, gdn_norm_w,
           ln2_g, ln2_b, ffn2_w_gate, ffn2_w_up, ffn2_w_down, ln3_g, ln3_b):
    depth = w_in.shape[0]
    alpha = (2.0 * depth) ** 0.25
    bsz, seq, d_model = x_prompt.shape
    dbsz, dseq, _ = x_sample.shape
    w_buf = cache_attn_k.shape[2]
    hw = HEAD_WIDTH
    in_cols = w_in.shape[2]
    assert in_cols == 7 * hw + 2 * N_HEADS and CONV_WIDTH - 1 <= dseq <= SAMPLE_T_PAD
    assert seq % GDN_TOKEN_BLOCK == 0 and GDN_TOKEN_BLOCK % (GDN_CHUNK * GDN_CHUNK_GROUP) == 0

    sample_lows, sample_buckets = _sample_layout(w_buf, dseq)
    tab_p = _t5_table_call(rel_bias, jnp.asarray(_prompt_buckets()))
    tab_s = _t5_table_call(rel_bias, jnp.asarray(sample_buckets))

    yp = x_prompt.reshape(bsz * seq, d_model)
    ys = x_sample.reshape(dbsz * dseq, d_model)
    collected = [[] for _ in range(8)]
    row = lambda v: v.reshape(1, -1)
    for layer in range(depth):
        wg1, wu1, wd1 = (w[layer].astype(BF16) for w in (ffn1_w_gate, ffn1_w_up, ffn1_w_down))
        wg2, wu2, wd2 = (w[layer].astype(BF16) for w in (ffn2_w_gate, ffn2_w_up, ffn2_w_down))
        win = jnp.pad(w_in[layer], ((0, 0), (0, 7 * hw + LANES - in_cols))).astype(BF16)
        wo = w_out[layer].astype(BF16)
        conv_w8 = jnp.pad(gdn_conv_w[layer], ((0, SUBLANES - CONV_WIDTH), (0, 0)))
        head_params = jnp.zeros((SUBLANES, LANES), F32)
        head_params = head_params.at[0, N_HEADS:2 * N_HEADS].set(gdn_a_log[layer])
        head_params = head_params.at[1, N_HEADS:2 * N_HEADS].set(gdn_dt_bias[layer])
        norm_w_row = jnp.tile(gdn_norm_w[layer], N_HEADS).reshape(1, hw)
        pre = functools.partial(_pre_call, wg=wg1, wu=wu1, wd=wd1, g=row(ln1_g[layer]), b=row(ln1_b[layer]),
                                win=win, alpha=alpha)
        post = functools.partial(_post_call, wo=wo, g2=row(ln2_g[layer]), b2=row(ln2_b[layer]), wg=wg2, wu=wu2,
                                 wd=wd2, g3=row(ln3_g[layer]), b3=row(ln3_b[layer]), alpha=alpha)

        x1, aq, ak, av, ak_t, av_t, gdn_qkv, z, beta, gate, gq_tail = _pre_prompt_call(
            yp, wg1, wu1, wd1, row(ln1_g[layer]), row(ln1_b[layer]), win, conv_w8, head_params, alpha, seq)
        shp = lambda a: a.reshape(bsz, seq, a.shape[1])
        attn = _attn_prompt_call(shp(aq), shp(ak), shp(av), tab_p)
        gdn, s_p = _gdn_prepared_call(shp(gdn_qkv), shp(z), shp(beta), shp(gate),
                                      jnp.zeros((bsz, N_HEADS, HEAD_DIM, HEAD_DIM), F32), norm_w_row,
                                      token_block=GDN_TOKEN_BLOCK, chunk=GDN_CHUNK, group=GDN_CHUNK_GROUP)
        yp = post(attn.reshape(bsz * seq, hw), gdn.reshape(bsz * seq, hw), x1)
        wp = min(w_buf, seq)
        heads5 = lambda a_t: a_t.reshape(bsz, N_HEADS, HEAD_DIM, seq).transpose(0, 3, 1, 2)[:, seq - wp:]
        st_p = (heads5(ak_t), heads5(av_t), s_p, gq_tail[:, SUBLANES - (CONV_WIDTH - 1):])

        x1s, aqs, aks, avs, gqs, zs, bas = pre(ys)
        shs = lambda a: _pad_rows(a.reshape(dbsz, dseq, a.shape[1]), SAMPLE_T_PAD)
        heads5s = lambda a: a.reshape(dbsz, dseq, N_HEADS, HEAD_DIM)
        new_rows = lambda a: jnp.pad(heads5s(a).transpose(0, 2, 1, 3),
                                     ((0, 0), (0, 0), (0, SAMPLE_T_PAD - dseq), (0, 0)))
        by_head_t = lambda c: c.transpose(0, 2, 3, 1)
        attn_s = _attn_sample_call(shs(aqs), new_rows(aks), new_rows(avs), by_head_t(cache_attn_k[layer]),
                                   by_head_t(cache_attn_v[layer]), tab_s, sample_lows)
        gdn_s, s_s = _gdn_call(shs(gqs), shs(zs), shs(bas),
                               _pad_rows(state_conv[layer], SUBLANES, front=True), state_gdn[layer],
                               conv_w8, head_params, norm_w_row, n_valid=dseq,
                               token_block=SAMPLE_T_PAD, chunk=SAMPLE_T_PAD,
                               batch_block=GDN_SAMPLE_BATCH_BLOCK, group=1)
        unpad = lambda a: a[:, :dseq].reshape(dbsz * dseq, hw)
        ys = post(unpad(attn_s), unpad(gdn_s), x1s)
        st_s = (heads5s(aks), heads5s(avs), s_s,
                gqs.reshape(dbsz, dseq, 3 * hw)[:, dseq - (CONV_WIDTH - 1):])
        for lst, st in zip(collected, st_p + st_s):
            lst.append(st)
    outs = [jnp.stack(t, axis=0) for t in collected]
    return (yp.reshape(bsz, seq, d_model), ys.reshape(dbsz, dseq, d_model)) + tuple(outs)
```

```python
import functools
import math

import numpy as np
import jax
import jax.numpy as jnp
from jax import lax
from jax.experimental import pallas as pl
from jax.experimental.pallas import tpu as pltpu

F32 = jnp.float32
BF16 = jnp.bfloat16

HEAD_DIM = 64
N_HEADS = 8
HEAD_WIDTH = N_HEADS * HEAD_DIM
DILATED_PATTERNS = ((128, 1), (512, 4), (2048, 16))
KEYS_PER_BLOCK = 128
T5_BUCKETS = 32
T5_MAX_EXACT = 16
T5_MAX_DIST = 2048
CONV_WIDTH = 4
GDN_CHUNK = 64
LN_EPS = 1e-5
RMS_EPS = 1e-6
NEG_BIG = -1e30

LANES = 128
SUBLANES = 8
TOKEN_TILE = 256
GDN_TOKEN_BLOCK = 256
ATTN_SKEW_GROUPS = 1
GDN_CHUNK_GROUP = 4
GDN_SAMPLE_BATCH_BLOCK = 8
SAMPLE_T_PAD = 8
VMEM_LIMIT = 56 * 1024 * 1024


def _resident(shape):
    nd = len(shape)
    return pl.BlockSpec(shape, lambda *_: (0,) * nd, pipeline_mode=pl.Buffered(1))


def _layernorm(y, g, b):
    mu = jnp.mean(y, axis=-1, keepdims=True)
    yc = y - mu
    var = jnp.mean(yc * yc, axis=-1, keepdims=True)
    return yc * lax.rsqrt(var + LN_EPS) * g + b


def _swiglu(xb, wg_ref, wu_ref, wd_ref, h_scr):
    d_ff = wg_ref.shape[1]
    step = 2 * LANES
    assert d_ff % step == 0
    for c in range(d_ff // step):
        sl = slice(c * step, (c + 1) * step)
        gate = jnp.dot(xb, wg_ref[:, sl], preferred_element_type=F32)
        up = jnp.dot(xb, wu_ref[:, sl], preferred_element_type=F32)
        h_scr[:, sl] = (jax.nn.silu(gate) * up).astype(BF16)
    return jnp.dot(h_scr[...], wd_ref[...], preferred_element_type=F32)


PROJ_WIDTHS = (HEAD_WIDTH, HEAD_WIDTH, HEAD_WIDTH, 3 * HEAD_WIDTH, HEAD_WIDTH, LANES)
PROJ_EDGES = tuple(int(e) for e in np.cumsum((0,) + PROJ_WIDTHS))


def _first_half_step(x_ref, wg_ref, wu_ref, wd_ref, g_ref, b_ref, h_scr, alpha):
    x = x_ref[...]
    ff = _swiglu(x.astype(BF16), wg_ref, wu_ref, wd_ref, h_scr)
    return _layernorm(alpha * x + 0.5 * ff, g_ref[...], b_ref[...])


def _pre_kernel(x_ref, wg_ref, wu_ref, wd_ref, g_ref, b_ref, win_ref,
                x1_ref, aq_ref, ak_ref, av_ref, gq_ref, z_ref, ba_ref, h_scr, *, alpha):
    x1 = _first_half_step(x_ref, wg_ref, wu_ref, wd_ref, g_ref, b_ref, h_scr, alpha)
    x1_ref[...] = x1
    xb = x1.astype(BF16)
    for i, ref in enumerate((aq_ref, ak_ref, av_ref, gq_ref, z_ref, ba_ref)):
        ref[...] = jnp.dot(xb, win_ref[:, PROJ_EDGES[i]:PROJ_EDGES[i + 1]], preferred_element_type=F32)


def _gdn_gates(ba, hp_ref):
    beta = jax.nn.sigmoid(ba)
    g = -jnp.exp(hp_ref[0:1, :]) * jax.nn.softplus(ba + hp_ref[1:2, :])
    return beta, g


def _causal_conv_silu(ext_scr, x, cw_ref):
    tb = x.shape[0]
    hist = SUBLANES
    ext_scr[hist:hist + tb, :] = x
    ext = ext_scr[0:hist + tb, :]
    conv = jnp.zeros(x.shape, F32)
    for j in range(CONV_WIDTH):
        back = CONV_WIDTH - 1 - j
        rows = pltpu.roll(ext, back, axis=0)[hist:hist + tb, :] if back else x
        conv = conv + rows * cw_ref[j:j + 1, :]
    ext_scr[0:hist, :] = ext_scr[tb:tb + hist, :]
    return jax.nn.silu(conv)


def _pre_prompt_kernel(x_ref, wg_ref, wu_ref, wd_ref, g_ref, b_ref, win_ref, cw_ref, hp_ref, ones_ref,
                       x1_ref, aq_ref, ak_ref, av_ref, akt_ref, avt_ref, qkv_ref, z_ref, bt_ref, gg_ref, tail_ref,
                       h_scr, ext_scr, *, alpha, tiles_per_seq):
    x1 = _first_half_step(x_ref, wg_ref, wu_ref, wd_ref, g_ref, b_ref, h_scr, alpha)
    x1_ref[...] = x1
    xb = x1.astype(BF16)
    proj = lambda i: jnp.dot(xb, win_ref[:, PROJ_EDGES[i]:PROJ_EDGES[i + 1]], preferred_element_type=F32)

    @pl.when(pl.program_id(0) % tiles_per_seq == 0)
    def _():
        ext_scr[0:SUBLANES, :] = jnp.zeros((SUBLANES, ext_scr.shape[1]), F32)

    raw = proj(3)
    tail_ref[0] = raw[raw.shape[0] - SUBLANES:, :]
    act = _causal_conv_silu(ext_scr, raw, cw_ref)
    hw = HEAD_WIDTH
    head_sq = lambda x: jnp.dot((x * x).astype(BF16), ones_ref[...], preferred_element_type=F32)
    q = act[:, 0:hw]
    k = act[:, hw:2 * hw]
    qkv_ref[:, 0:hw] = q * lax.rsqrt(head_sq(q) + RMS_EPS) * HEAD_DIM ** -0.5
    qkv_ref[:, hw:2 * hw] = k * lax.rsqrt(head_sq(k) + RMS_EPS)
    qkv_ref[:, 2 * hw:3 * hw] = act[:, 2 * hw:3 * hw]
    bt_ref[...], gg_ref[...] = _gdn_gates(proj(5), hp_ref)
    aq_ref[...] = proj(0)
    for i, ref, ref_t in ((1, ak_ref, akt_ref), (2, av_ref, avt_ref)):
        kv = proj(i)
        ref[...] = kv
        ref_t[0] = kv.T
    z_ref[...] = proj(4)


def _pre_call(x2d, wg, wu, wd, g, b, win, alpha):
    m, d = x2d.shape
    d_ff = wg.shape[1]
    tm = TOKEN_TILE
    assert m % tm == 0 and sum(PROJ_WIDTHS) == win.shape[1]
    tile = lambda w: pl.BlockSpec((tm, w), lambda i: (i, 0))
    return pl.pallas_call(
        functools.partial(_pre_kernel, alpha=alpha),
        grid=(m // tm,),
        in_specs=[tile(d), _resident(wg.shape), _resident(wu.shape), _resident(wd.shape),
                  _resident(g.shape), _resident(b.shape), _resident(win.shape)],
        out_specs=[tile(d)] + [tile(w) for w in PROJ_WIDTHS],
        out_shape=[jax.ShapeDtypeStruct((m, d), F32)] + [jax.ShapeDtypeStruct((m, w), F32) for w in PROJ_WIDTHS],
        scratch_shapes=[pltpu.VMEM((tm, d_ff), BF16)],
        compiler_params=pltpu.CompilerParams(dimension_semantics=("parallel",), vmem_limit_bytes=VMEM_LIMIT),
        name="pre_ffn_proj",
    )(x2d, wg, wu, wd, g, b, win)


def _pre_prompt_call(x2d, wg, wu, wd, g, b, win, conv_w8, head_params, alpha, seq):
    m, d = x2d.shape
    d_ff = wg.shape[1]
    tm = TOKEN_TILE
    hw = HEAD_WIDTH
    assert seq % tm == 0 and m % seq == 0 and sum(PROJ_WIDTHS) == win.shape[1]
    per_seq = seq // tm
    tile = lambda w: pl.BlockSpec((tm, w), lambda i: (i, 0))
    t_spec = pl.BlockSpec((1, hw, tm), lambda i: (i // per_seq, 0, i % per_seq))
    t_shape = jax.ShapeDtypeStruct((m // seq, hw, seq), F32)
    rows = lambda w: jax.ShapeDtypeStruct((m, w), F32)
    consts = (wg, wu, wd, g, b, win, conv_w8, head_params, _head_ones())
    return pl.pallas_call(
        functools.partial(_pre_prompt_kernel, alpha=alpha, tiles_per_seq=per_seq),
        grid=(m // tm,),
        in_specs=[tile(d)] + [_resident(c.shape) for c in consts],
        out_specs=[tile(d), tile(hw), tile(hw), tile(hw), t_spec, t_spec, tile(3 * hw), tile(hw), tile(LANES),
                   tile(LANES), pl.BlockSpec((1, SUBLANES, 3 * hw), lambda i: (i // per_seq, 0, 0))],
        out_shape=[rows(d), rows(hw), rows(hw), rows(hw), t_shape, t_shape, rows(3 * hw), rows(hw), rows(LANES),
                   rows(LANES), jax.ShapeDtypeStruct((m // seq, SUBLANES, 3 * hw), F32)],
        scratch_shapes=[pltpu.VMEM((tm, d_ff), BF16), pltpu.VMEM((tm + 2 * SUBLANES, 3 * hw), F32)],
        compiler_params=pltpu.CompilerParams(dimension_semantics=("arbitrary",), vmem_limit_bytes=VMEM_LIMIT),
        name="pre_ffn_proj_gdnprep",
    )(x2d, *consts)


def _post_kernel(attn_ref, gdn_ref, x1_ref, wo_ref, g2_ref, b2_ref, wg_ref, wu_ref, wd_ref, g3_ref, b3_ref,
                 y_ref, h_scr, *, alpha):
    mix = jnp.dot(attn_ref[...].astype(BF16), wo_ref[0:HEAD_WIDTH, :], preferred_element_type=F32)
    mix = mix + jnp.dot(gdn_ref[...].astype(BF16), wo_ref[HEAD_WIDTH:2 * HEAD_WIDTH, :],
                        preferred_element_type=F32)
    x2 = _layernorm(alpha * x1_ref[...] + mix, g2_ref[...], b2_ref[...])
    ff = _swiglu(x2.astype(BF16), wg_ref, wu_ref, wd_ref, h_scr)
    y_ref[...] = _layernorm(alpha * x2 + 0.5 * ff, g3_ref[...], b3_ref[...])


def _post_call(attn, gdn, x1, wo, g2, b2, wg, wu, wd, g3, b3, alpha):
    m, d = x1.shape
    tm = TOKEN_TILE
    assert m % tm == 0
    tile = lambda w: pl.BlockSpec((tm, w), lambda i: (i, 0))
    consts = (wo, g2, b2, wg, wu, wd, g3, b3)
    return pl.pallas_call(
        functools.partial(_post_kernel, alpha=alpha),
        grid=(m // tm,),
        in_specs=[tile(HEAD_WIDTH), tile(HEAD_WIDTH), tile(d)] + [_resident(c.shape) for c in consts],
        out_specs=tile(d),
        out_shape=jax.ShapeDtypeStruct((m, d), F32),
        scratch_shapes=[pltpu.VMEM((tm, wg.shape[1]), BF16)],
        compiler_params=pltpu.CompilerParams(dimension_semantics=("parallel",), vmem_limit_bytes=VMEM_LIMIT),
        name="post_out_ffn",
    )(attn, gdn, x1, *consts)


def _t5_bucket_np(dist):
    n = np.maximum(dist, 0)
    nf = np.maximum(n, 1).astype(np.float32)
    large = T5_MAX_EXACT + (np.log(nf / np.float32(T5_MAX_EXACT)) / np.float32(math.log(T5_MAX_DIST / T5_MAX_EXACT))
                            * np.float32(T5_BUCKETS - T5_MAX_EXACT)).astype(np.int32)
    large = np.minimum(large, T5_BUCKETS - 1)
    return np.where(n < T5_MAX_EXACT, n, large).astype(np.int32)


def _t5_table_kernel(rb_ref, bk_ref, out_ref):
    for h in range(N_HEADS):
        bk = bk_ref[0, h % bk_ref.shape[1]]
        acc = jnp.full(bk.shape, NEG_BIG, F32)
        for b in range(T5_BUCKETS):
            acc = jnp.where(bk == b, rb_ref[b, h], acc)
        out_ref[0, h] = acc


def _t5_table_call(rel_bias, buckets):
    p, hb, r, c = buckets.shape
    assert hb in (1, N_HEADS)
    return pl.pallas_call(
        _t5_table_kernel,
        grid=(p,),
        in_specs=[pl.BlockSpec(memory_space=pltpu.SMEM),
                  pl.BlockSpec((1, hb, r, c), lambda i: (i, 0, 0, 0))],
        out_specs=pl.BlockSpec((1, N_HEADS, r, c), lambda i: (i, 0, 0, 0)),
        out_shape=jax.ShapeDtypeStruct((p, N_HEADS, r, c), F32),
        compiler_params=pltpu.CompilerParams(dimension_semantics=("parallel",)),
        name="t5_bias_table",
    )(rel_bias, buckets)


def _prompt_buckets():
    nb = KEYS_PER_BLOCK
    qi = np.arange(nb)[:, None]
    ki = np.arange(2 * nb)[None, :]
    dist = qi + nb - ki
    valid = (dist >= 0) & (dist <= nb)
    out = []
    for window, dil in DILATED_PATTERNS:
        assert window // dil == nb
        out.append(np.where(valid, _t5_bucket_np(dist * dil), -1))
    return np.stack(out).astype(np.int32)[:, None]


def _nt_dot(a, b):
    return lax.dot_general(a, b, (((1,), (1,)), ((), ())), preferred_element_type=F32)


def _emit_skewed(stages, items, n_groups):
    n_groups = max(1, min(n_groups, len(items)))
    groups = [items[i::n_groups] for i in range(n_groups)]
    for step in range(len(stages) + n_groups - 1):
        for gi, group in enumerate(groups):
            s = step - gi
            if 0 <= s < len(stages):
                for item in group:
                    stages[s](item)


def _attn_prompt_kernel(q_ref, k_ref, v_ref, tab_ref, o_ref, og_scr, lg_scr, *, seq):
    nb = KEYS_PER_BLOCK
    hd = HEAD_DIM
    n_pat = len(DILATED_PATTERNS)
    heads = LANES // hd
    for g, (window, dil) in enumerate(DILATED_PATTERNS):
        n_blocks = seq // (dil * nb)

        def blocks(starts, with_prev, g=g, dil=dil):
            lane_head = lax.broadcasted_iota(jnp.int32, (nb, LANES), 1) // hd
            n_keys = 2 * nb if with_prev else nb
            row_sl, qs, ks, vs = [], [], [], []
            for start in starts:
                first_key = start - dil * (n_keys - nb)
                row_sl.append(pl.ds(start, nb, stride=dil) if dil > 1 else pl.ds(start, nb))
                key_rows = pl.ds(first_key, n_keys, stride=dil) if dil > 1 else pl.ds(first_key, n_keys)
                qs.append(q_ref[0, row_sl[-1], :] * hd ** -0.5)
                ks.append(k_ref[0, key_rows, :].astype(BF16))
                vs.append(v_ref[0, key_rows, :].astype(BF16))
            chains = [(u, hh) for u in range(len(starts)) for hh in range(heads)]
            s, m, p, den, acc = {}, {}, {}, {}, {}

            def scores(c):
                u, hh = c
                qb = jnp.where(lane_head == hh, qs[u], 0.0).astype(BF16)
                s[c] = _nt_dot(qb, ks[u]) + tab_ref[g, hh, :, 2 * nb - n_keys:2 * nb]

            def row_max(c):
                m[c] = jnp.max(s[c], axis=-1, keepdims=True)

            def probs(c):
                p[c] = jnp.exp(s[c] - m[c])
                den[c] = jnp.sum(p[c], axis=-1, keepdims=True)

            def values(c):
                u, hh = c
                acc[c] = jnp.dot(p[c].astype(BF16), vs[u], preferred_element_type=F32)

            def finish(c):
                acc[c] = acc[c] / den[c]
                den[c] = m[c] + jnp.log(den[c])

            _emit_skewed((scores, row_max, probs, values, finish), chains, ATTN_SKEW_GROUPS)
            for u in range(len(starts)):
                out, lse = acc[u, 0], den[u, 0]
                for hh in range(1, heads):
                    out = jnp.where(lane_head == hh, acc[u, hh], out)
                    lse = jnp.where(lane_head == hh, den[u, hh], lse)
                og_scr[g, row_sl[u], :] = out
                lg_scr[g, row_sl[u], :] = jnp.broadcast_to(lse, (nb, LANES))

        def unroll_of(count):
            return next(u for u in (4, 3, 2, 1) if count % u == 0)

        u_first = unroll_of(dil)

        def first_body(i, carry, blocks=blocks, u_first=u_first):
            blocks([i * u_first + j for j in range(u_first)], False)
            return carry

        lax.fori_loop(0, dil // u_first, first_body, 0)
        if n_blocks > 1:
            n_rest = dil * (n_blocks - 1)
            u_rest = unroll_of(n_rest)

            def rest_body(i, carry, blocks=blocks, dil=dil, n_blocks=n_blocks, u_rest=u_rest):
                starts = []
                for j in range(u_rest):
                    idx = i * u_rest + j
                    starts.append(idx // (n_blocks - 1) + dil * nb * (idx % (n_blocks - 1) + 1))
                blocks(starts, True)
                return carry

            lax.fori_loop(0, n_rest // u_rest, rest_body, 0)

    rb = 2 * nb

    def merge_body(i, carry):
        rows = pl.ds(pl.multiple_of(i * rb, rb), rb)
        lses = [lg_scr[g, rows, :] for g in range(n_pat)]
        top = functools.reduce(jnp.maximum, lses)
        ws = [jnp.exp(l - top) for l in lses]
        num = functools.reduce(lambda a, b: a + b, [w * og_scr[g, rows, :] for g, w in enumerate(ws)])
        o_ref[0, rows, :] = num / functools.reduce(lambda a, b: a + b, ws)
        return carry

    lax.fori_loop(0, seq // rb, merge_body, 0)


def _attn_prompt_call(aq, ak, av, tab):
    bsz, seq, width = aq.shape
    assert width == HEAD_WIDTH and seq % (DILATED_PATTERNS[-1][1] * KEYS_PER_BLOCK) == 0
    n_pat = len(DILATED_PATTERNS)
    heads_per_step = LANES // HEAD_DIM
    qkv_spec = pl.BlockSpec((1, seq, LANES), lambda h, b: (b, 0, h))
    return pl.pallas_call(
        functools.partial(_attn_prompt_kernel, seq=seq),
        grid=(N_HEADS // heads_per_step, bsz),
        in_specs=[qkv_spec, qkv_spec, qkv_spec,
                  pl.BlockSpec((n_pat, heads_per_step, KEYS_PER_BLOCK, 2 * KEYS_PER_BLOCK),
                               lambda h, b: (0, h, 0, 0))],
        out_specs=pl.BlockSpec((1, seq, LANES), lambda h, b: (b, 0, h)),
        out_shape=jax.ShapeDtypeStruct((bsz, seq, width), F32),
        scratch_shapes=[pltpu.VMEM((n_pat, seq, LANES), F32)] * 2,
        compiler_params=pltpu.CompilerParams(dimension_semantics=("parallel", "parallel"),
                                             vmem_limit_bytes=VMEM_LIMIT),
        name="attn_prompt",
    )(aq, ak, av, tab)


def _sample_layout(w_buf, t_new):
    assert w_buf % LANES == 0 and t_new <= SAMPLE_T_PAD
    row_of_col = np.full((w_buf + LANES,), -1, np.int64)
    row_of_col[:w_buf + t_new] = np.arange(w_buf + t_new)
    buckets = np.full((len(DILATED_PATTERNS), 1, SAMPLE_T_PAD, w_buf + LANES), -1, np.int32)
    lows = []
    for g, (window, dil) in enumerate(DILATED_PATTERNS):
        lows.append(max(0, w_buf - window) // LANES * LANES)
        for t in range(t_new):
            dist = w_buf + t - row_of_col
            ok = (row_of_col >= 0) & (dist >= 0) & (dist <= window) & (dist % dil == 0)
            assert int(ok.sum()) == min(window, w_buf + t) // dil + 1 and not ok[:lows[g]].any()
            buckets[g, 0, t] = np.where(ok, _t5_bucket_np(dist), -1)
    return tuple(lows), buckets


def _attn_sample_kernel(q_ref, kn_ref, vn_ref, kt_ref, vt_ref, tab_ref, o_ref, *, lows):
    tp = SAMPLE_T_PAD
    hd = HEAD_DIM
    w_buf = kt_ref.shape[3]
    n_pat = len(DILATED_PATTERNS)
    add = lambda x, y: x + y
    rowmax = lambda x: jnp.max(x, axis=-1, keepdims=True)
    rowsum = lambda x: jnp.sum(x, axis=-1, keepdims=True)
    heads = range(N_HEADS)
    q8 = q_ref[0] * hd ** -0.5
    s_buf = {h: jnp.dot(q8[:, hd * h:hd * (h + 1)], kt_ref[0, h], preferred_element_type=F32) for h in heads}
    s_new = {h: _nt_dot(q8[:, hd * h:hd * (h + 1)], kn_ref[0, h]) for h in heads}
    p_buf, p_new, den, lse = {}, {}, {}, {}
    for h in heads:
        for g in range(n_pat):
            lo = lows[g]
            l_buf = s_buf[h][:, lo:] + tab_ref[g, h, :, lo:w_buf]
            l_new = s_new[h] + tab_ref[g, h, :, w_buf:w_buf + tp]
            m = jnp.maximum(rowmax(l_buf), rowmax(l_new))
            p_buf[h, g] = jnp.exp(l_buf - m)
            p_new[h, g] = jnp.exp(l_new - m)
            den[h, g] = rowsum(p_buf[h, g]) + rowsum(p_new[h, g])
            lse[h, g] = m + jnp.log(den[h, g])
    edges = sorted(set(lows)) + [w_buf]
    outs = {}
    for h in heads:
        top = functools.reduce(jnp.maximum, [lse[h, g] for g in range(n_pat)])
        ws = [jnp.exp(lse[h, g] - top) for g in range(n_pat)]
        w_sum = functools.reduce(add, ws)
        coef = [ws[g] / (w_sum * den[h, g]) for g in range(n_pat)]
        acc = jnp.dot(functools.reduce(add, [coef[g] * p_new[h, g] for g in range(n_pat)]), vn_ref[0, h],
                      preferred_element_type=F32)
        for e0, e1 in zip(edges[:-1], edges[1:]):
            mix = functools.reduce(add, [coef[g] * p_buf[h, g][:, e0 - lows[g]:e1 - lows[g]]
                                         for g in range(n_pat) if lows[g] <= e0])
            acc = acc + _nt_dot(mix, vt_ref[0, h, :, e0:e1])
        outs[h] = acc
    o_ref[0] = jnp.concatenate([outs[h] for h in heads], axis=1)


def _attn_sample_call(aq8, k_new, v_new, cache_kt, cache_vt, tab, lows):
    bsz, n_heads, hd, w_buf = cache_kt.shape
    q_spec = pl.BlockSpec((1, SAMPLE_T_PAD, n_heads * hd), lambda b: (b, 0, 0))
    new_spec = pl.BlockSpec((1, n_heads, SAMPLE_T_PAD, hd), lambda b: (b, 0, 0, 0))
    buf_spec = pl.BlockSpec((1, n_heads, hd, w_buf), lambda b: (b, 0, 0, 0))
    return pl.pallas_call(
        functools.partial(_attn_sample_kernel, lows=lows),
        grid=(bsz,),
        in_specs=[q_spec, new_spec, new_spec, buf_spec, buf_spec, _resident(tab.shape)],
        out_specs=q_spec,
        out_shape=jax.ShapeDtypeStruct((bsz, SAMPLE_T_PAD, n_heads * hd), F32),
        compiler_params=pltpu.CompilerParams(dimension_semantics=("parallel",), vmem_limit_bytes=VMEM_LIMIT),
        name="attn_sample",
    )(aq8, k_new, v_new, cache_kt, cache_vt, tab)


def _aligned(index, size):
    return index * size if isinstance(index, int) else pl.multiple_of(index * size, size)


def _split3(x):
    h1 = x.astype(BF16)
    r1 = x - h1.astype(F32)
    h2 = r1.astype(BF16)
    h3 = (r1 - h2.astype(F32)).astype(BF16)
    return h1, h2, h3


def _head_sums(x, ones_ref):
    h1, h2, _ = _split3(x)
    return (jnp.dot(h1, ones_ref[...], preferred_element_type=F32)
            + jnp.dot(h2, ones_ref[...], preferred_element_type=F32))


def _store_normed_qk(act, q_dst, k_dst, ones_ref):
    hw = HEAD_WIDTH
    q = act[:, 0:hw]
    k = act[:, hw:2 * hw]
    q_dst[...] = q * lax.rsqrt(_head_sums(q * q, ones_ref) + RMS_EPS) * HEAD_DIM ** -0.5
    k_dst[...] = k * lax.rsqrt(_head_sums(k * k, ones_ref) + RMS_EPS)


def _gdn_kernel(x_ref, z_ref, ba_ref, cb_ref, s0_ref, cw_ref, hp_ref, nw_ref, ones_ref,
                o_ref, sfin_ref,
                ext_scr, q_scr, k_scr, v_scr, bt_scr, g_scr, *core_scr, chunk, n_valid, group):
    n_bb, tb = x_ref.shape[0], x_ref.shape[1]
    t = pl.program_id(1)

    @pl.when(t == 0)
    def _():
        ext_scr[:, 0:SUBLANES, :] = cb_ref[...]

    for bb in range(n_bb):
        act = _causal_conv_silu(ext_scr.at[bb], x_ref[bb], cw_ref)
        _store_normed_qk(act, q_scr.at[bb], k_scr.at[bb], ones_ref)
        v_scr[bb] = act[:, 2 * HEAD_WIDTH:3 * HEAD_WIDTH]
        ba = ba_ref[bb]
        live = t * tb + lax.broadcasted_iota(jnp.int32, ba.shape, 0) < n_valid
        beta, g = _gdn_gates(ba, hp_ref)
        bt_scr[bb] = jnp.where(live, beta, 0.0)
        g_scr[bb] = jnp.where(live, g, 0.0)
    _gdn_core(s0_ref, z_ref, nw_ref, ones_ref, o_ref, sfin_ref, q_scr, k_scr, v_scr, 0, bt_scr, g_scr,
              *core_scr, chunk=chunk, group=group)


def _gdn_pairs_kernel(qkv_ref, z_ref, bt_scr, g_scr, s0_ref, nw_ref, ones_ref, o_ref, sfin_ref,
                      oc_scr, s_scr, u_scr, w_scr, in_scr, qd_scr, kd_scr, el_scr, *, group):
    chunk = HEAD_DIM
    k_lane0, v_lane0 = HEAD_WIDTH, 2 * HEAD_WIDTH
    n_bb, tb = bt_scr.shape[0], bt_scr.shape[1]
    n_chunks = tb // chunk
    n_pairs = N_HEADS // 2
    hd = HEAD_DIM
    t = pl.program_id(1)
    dot = functools.partial(jnp.dot, preferred_element_type=F32)

    lane = lax.broadcasted_iota(jnp.int32, (chunk, LANES), 1)
    row = lax.broadcasted_iota(jnp.int32, (chunk, LANES), 0)
    second = lane >= hd
    col = lane - jnp.where(second, hd, 0)
    tril = row >= col
    strict = row > col
    eye2 = (row == col).astype(F32)
    tril_b = (lax.broadcasted_iota(jnp.int32, (chunk, chunk), 0)
              >= lax.broadcasted_iota(jnp.int32, (chunk, chunk), 1)).astype(BF16)
    n_doublings = int(math.log2(chunk)) - 1

    def blockdiag(x):
        return jnp.concatenate([jnp.where(second, 0.0, x), jnp.where(second, x, 0.0)], axis=0)

    def per_head(cols, j):
        return jnp.where(second, cols[:, 2 * j + 1:2 * j + 2], cols[:, 2 * j:2 * j + 1])

    @pl.when(t == 0)
    def _():
        for bb in range(n_bb):
            for j in range(n_pairs):
                s_scr[bb, j] = jnp.concatenate([s0_ref[bb, 2 * j], s0_ref[bb, 2 * j + 1]], axis=1)

    def state_free_part(gi, carry):
        probs, shared = [], {}
        for bb in range(n_bb):
            for cg in range(group):
                c_idx = gi * group + cg
                rows = pl.ds(_aligned(c_idx, chunk), chunk)
                gc = functools.reduce(lambda a, b: a + b,
                                      [dot(tril_b, p) for p in _split3(g_scr[bb, rows, :])])
                shared[bb, cg] = (rows, c_idx, gc[:, N_HEADS:2 * N_HEADS], gc.T[N_HEADS:2 * N_HEADS, :],
                                  bt_scr[bb, rows, :])
                probs += [(bb, cg, j) for j in range(n_pairs)]
        kk, qk, decay, a, t_inv, pw, vb, kbg = {}, {}, {}, {}, {}, {}, {}, {}
        for p in probs:
            bb, cg, j = p
            rows, c_idx, gc, gct, bt = shared[bb, cg]
            pl_ = pl.ds(LANES * j, LANES)
            q2 = qkv_ref[bb, rows, pl_]
            k2 = qkv_ref[bb, rows, pl.ds(k_lane0 + LANES * j, LANES)]
            beta2 = per_head(bt, j)
            gcc = per_head(gc, j)
            gcr = jnp.concatenate([gct[2 * j:2 * j + 1, :], gct[2 * j + 1:2 * j + 2, :]], axis=1)
            g_last = gcc[chunk - 1:chunk, :]
            e_gc = jnp.exp(gcc)
            kb2 = k2 * beta2
            k_bd = blockdiag(k2)
            kk[p] = _nt_dot(kb2, k_bd)
            qk[p] = _nt_dot(q2, k_bd)
            vb[p] = qkv_ref[bb, rows, pl.ds(v_lane0 + LANES * j, LANES)] * beta2
            kbg[p] = kb2 * e_gc
            qd_scr[bb, rows, pl_] = q2 * e_gc
            kd_scr[bb, rows, pl_] = k2 * jnp.exp(g_last - gcc)
            el_scr[bb, pl.ds(_aligned(c_idx, SUBLANES), SUBLANES), pl_] = jnp.broadcast_to(
                jnp.exp(g_last), (SUBLANES, LANES))
            decay[p] = jnp.where(tril, jnp.exp(jnp.where(tril, gcc - gcr, 0.0)), 0.0)
        for p in probs:
            bb, cg, j = p
            rows = shared[bb, cg][0]
            a[p] = jnp.where(strict, kk[p] * decay[p], 0.0)
            in_scr[bb, rows, pl.ds(LANES * j, LANES)] = jnp.where(tril, qk[p] * decay[p], 0.0)
            t_inv[p] = eye2 - a[p]
        for p in probs:
            pw[p] = dot(a[p], blockdiag(a[p]))
        for _ in range(n_doublings - 1):
            nxt_pw, nxt_t = {}, {}
            for p in probs:
                nxt_pw[p] = dot(pw[p], blockdiag(pw[p]))
                nxt_t[p] = dot(t_inv[p], blockdiag(eye2 + pw[p]))
            pw, t_inv = nxt_pw, nxt_t
        for p in probs:
            t_inv[p] = dot(t_inv[p], blockdiag(eye2 + pw[p]))
        for p in probs:
            bb, cg, j = p
            rows = shared[bb, cg][0]
            uw = dot(t_inv[p], jnp.concatenate([blockdiag(vb[p]), blockdiag(kbg[p])], axis=1))
            u_scr[bb, rows, pl.ds(LANES * j, LANES)] = uw[:, 0:LANES]
            w_scr[bb, rows, pl.ds(LANES * j, LANES)] = uw[:, LANES:2 * LANES]
        return carry

    def state_part(ci_, carry):
        rows = pl.ds(_aligned(ci_, chunk), chunk)
        el_rows = pl.ds(_aligned(ci_, SUBLANES), SUBLANES)
        probs = [(bb, j) for bb in range(n_bb) for j in range(n_pairs)]
        s_old, both, v_new, o_in, s_add = {}, {}, {}, {}, {}
        for p in probs:
            bb, j = p
            pl_ = pl.ds(LANES * j, LANES)
            s_old[p] = s_scr[bb, j]
            lhs = jnp.concatenate([w_scr[bb, rows, pl_], qd_scr[bb, rows, pl_]], axis=0)
            both[p] = dot(lhs, blockdiag(s_old[p]))
        for p in probs:
            bb, j = p
            pl_ = pl.ds(LANES * j, LANES)
            v_new[p] = u_scr[bb, rows, pl_] - both[p][0:chunk]
            o_in[p] = dot(in_scr[bb, rows, pl_], blockdiag(v_new[p]))
            cross = lax.dot_general(kd_scr[bb, rows, pl_], v_new[p], (((0,), (0,)), ((), ())),
                                    preferred_element_type=F32)
            s_add[p] = jnp.where(second, cross[hd:2 * hd], cross[0:hd])
        for p in probs:
            bb, j = p
            pl_ = pl.ds(LANES * j, LANES)
            oc_scr[bb, rows, pl_] = both[p][chunk:2 * chunk] + o_in[p]
            s_scr[bb, j] = s_old[p] * el_scr[bb, el_rows, pl_][0:1, :] + s_add[p]
        return carry

    assert n_chunks % group == 0
    if n_chunks == group:
        state_free_part(0, 0)
    else:
        lax.fori_loop(0, n_chunks // group, state_free_part, 0)
    if n_chunks == 1:
        state_part(0, 0)
    else:
        lax.fori_loop(0, n_chunks, state_part, 0)

    for bb in range(n_bb):
        o = oc_scr[bb]
        o = o * lax.rsqrt(_head_sums(o * o, ones_ref) * (1.0 / hd) + RMS_EPS) * nw_ref[...]
        o_ref[bb] = o * jax.nn.silu(z_ref[bb])

    @pl.when(t == pl.num_programs(1) - 1)
    def _():
        for bb in range(n_bb):
            for j in range(n_pairs):
                sfin_ref[bb, 2 * j] = s_scr[bb, j][:, 0:hd]
                sfin_ref[bb, 2 * j + 1] = s_scr[bb, j][:, hd:2 * hd]


def _gdn_core(s0_ref, z_ref, nw_ref, ones_ref, o_ref, sfin_ref, q_scr, k_scr, v_scr, v_lane0, bt_scr, g_scr,
              oc_scr, s_scr, uw_scr, in_scr, qd_scr, kd_scr, el_scr, *, chunk, group):
    n_bb, tb = q_scr.shape[0], q_scr.shape[1]
    n_chunks = tb // chunk
    hd = HEAD_DIM
    t = pl.program_id(1)

    @pl.when(t == 0)
    def _():
        s_scr[...] = s0_ref[...]

    ri = lax.broadcasted_iota(jnp.int32, (chunk, chunk), 0)
    ci = lax.broadcasted_iota(jnp.int32, (chunk, chunk), 1)
    tril = ri >= ci
    strict = ri > ci
    eye_c = (ri == ci).astype(F32)
    tril_b = tril.astype(BF16)
    n_doublings = int(math.log2(chunk)) - 1
    assert 2 ** (n_doublings + 1) == chunk

    dot = functools.partial(jnp.dot, preferred_element_type=F32)

    def state_free_part(gi, carry):
        probs, shared = [], {}
        for bb in range(n_bb):
            for cg in range(group):
                c_idx = gi * group + cg
                rows = pl.ds(_aligned(c_idx, chunk), chunk)
                gc = functools.reduce(lambda a, b: a + b,
                                      [dot(tril_b, p) for p in _split3(g_scr[bb, rows, :])])
                gct = gc.T
                g_last = gc[chunk - 1:chunk, :]
                el_rows = pl.ds(_aligned(c_idx, SUBLANES), SUBLANES)
                el_scr[bb, el_rows, :] = jnp.broadcast_to(jnp.exp(g_last), (SUBLANES, LANES))
                shared[bb, cg] = (rows, gc, gct, jnp.exp(gc), jnp.exp(g_last - gc), bt_scr[bb, rows, :])
                probs += [(bb, cg, h) for h in range(N_HEADS)]
        kk, qk, rhs, decay, a, t_inv, pw = {}, {}, {}, {}, {}, {}, {}
        for p in probs:
            bb, cg, h = p
            rows, gc, gct, e_gc, e_rem, bt = shared[bb, cg]
            hl = pl.ds(hd * h, hd)
            gl = N_HEADS + h
            qh = q_scr[bb, rows, hl]
            kh = k_scr[bb, rows, hl]
            beta = bt[:, h:h + 1]
            kb = kh * beta
            kk[p] = _nt_dot(kb, kh)
            qk[p] = _nt_dot(qh, kh)
            v_h = v_scr[bb, rows, pl.ds(v_lane0 + hd * h, hd)]
            rhs[p] = jnp.concatenate([v_h * beta, kb * e_gc[:, gl:gl + 1]], axis=1)
            qd_scr[bb, rows, hl] = qh * e_gc[:, gl:gl + 1]
            kd_scr[bb, rows, hl] = kh * e_rem[:, gl:gl + 1]
            diff = gc[:, gl:gl + 1] - gct[gl:gl + 1, :]
            decay[p] = jnp.where(tril, jnp.exp(jnp.where(tril, diff, 0.0)), 0.0)
        for p in probs:
            bb, cg, h = p
            rows = shared[bb, cg][0]
            a[p] = jnp.where(strict, kk[p] * decay[p], 0.0)
            in_scr[bb, rows, LANES * h:LANES * h + chunk] = jnp.where(tril, qk[p] * decay[p], 0.0)
            t_inv[p] = eye_c - a[p]
        for p in probs:
            pw[p] = dot(a[p], a[p])
        for _ in range(n_doublings - 1):
            nxt_pw, nxt_t = {}, {}
            for p in probs:
                nxt_pw[p] = dot(pw[p], pw[p])
                nxt_t[p] = dot(t_inv[p], eye_c + pw[p])
            pw, t_inv = nxt_pw, nxt_t
        for p in probs:
            t_inv[p] = dot(t_inv[p], eye_c + pw[p])
        for p in probs:
            bb, cg, h = p
            rows = shared[bb, cg][0]
            uw_scr[bb, rows, LANES * h:LANES * (h + 1)] = dot(t_inv[p], rhs[p])
        return carry

    def state_part(ci_, carry):
        rows = pl.ds(_aligned(ci_, chunk), chunk)
        el_rows = pl.ds(_aligned(ci_, SUBLANES), SUBLANES)
        probs = [(bb, h) for bb in range(n_bb) for h in range(N_HEADS)]
        s_old, uw, both, v_new, o_in, s_add = {}, {}, {}, {}, {}, {}
        for p in probs:
            bb, h = p
            s_old[p] = s_scr[bb, h]
            uw[p] = uw_scr[bb, rows, LANES * h:LANES * (h + 1)]
            lhs = jnp.concatenate([uw[p][:, hd:2 * hd], qd_scr[bb, rows, pl.ds(hd * h, hd)]], axis=0)
            both[p] = dot(lhs, s_old[p])
        for p in probs:
            bb, h = p
            v_new[p] = uw[p][:, 0:hd] - both[p][0:chunk]
            o_in[p] = dot(in_scr[bb, rows, LANES * h:LANES * h + chunk], v_new[p])
            s_add[p] = lax.dot_general(kd_scr[bb, rows, pl.ds(hd * h, hd)], v_new[p],
                                       (((0,), (0,)), ((), ())), preferred_element_type=F32)
        for p in probs:
            bb, h = p
            e_last = el_scr[bb, el_rows, :][0:1, N_HEADS + h:N_HEADS + h + 1]
            oc_scr[bb, rows, pl.ds(hd * h, hd)] = both[p][chunk:2 * chunk] + o_in[p]
            s_scr[bb, h] = s_old[p] * e_last + s_add[p]
        return carry

    assert n_chunks % group == 0
    if n_chunks == group:
        state_free_part(0, 0)
    else:
        lax.fori_loop(0, n_chunks // group, state_free_part, 0)
    if n_chunks == 1:
        state_part(0, 0)
    else:
        lax.fori_loop(0, n_chunks, state_part, 0)

    for bb in range(n_bb):
        o = oc_scr[bb]
        o = o * lax.rsqrt(_head_sums(o * o, ones_ref) * (1.0 / hd) + RMS_EPS) * nw_ref[...]
        o_ref[bb] = o * jax.nn.silu(z_ref[bb])

    @pl.when(t == pl.num_programs(1) - 1)
    def _():
        sfin_ref[...] = s_scr[...]


def _gdn_core_scratch(nbb, tb, chunk):
    hw = HEAD_WIDTH
    return ([pltpu.VMEM((nbb, tb, hw), F32)]
            + [pltpu.VMEM((nbb, N_HEADS, HEAD_DIM, HEAD_DIM), F32)]
            + [pltpu.VMEM((nbb, tb, N_HEADS * LANES), F32)] * 2
            + [pltpu.VMEM((nbb, tb, hw), F32)] * 2
            + [pltpu.VMEM((nbb, (tb // chunk) * SUBLANES, LANES), F32)])


def _head_ones():
    ones_bd = np.kron(np.eye(N_HEADS, dtype=np.float32), np.ones((HEAD_DIM, HEAD_DIM), np.float32))
    return jnp.asarray(ones_bd, BF16)


def _gdn_call(gq, z, ba, conv_buf8, s0, conv_w8, head_params, norm_w_row, n_valid, token_block, chunk,
              batch_block, group):
    bsz, seq, w3 = gq.shape
    hw = HEAD_WIDTH
    assert w3 == 3 * hw and seq % token_block == 0 and token_block % chunk == 0 and bsz % batch_block == 0
    tb = token_block
    nbb = batch_block
    ones_bd = _head_ones()
    blk = lambda w: pl.BlockSpec((nbb, tb, w), lambda b, t: (b, t, 0))
    per_b3 = lambda s: pl.BlockSpec((nbb,) + s, lambda b, t: (b,) + (0,) * len(s))
    return pl.pallas_call(
        functools.partial(_gdn_kernel, chunk=chunk, n_valid=n_valid, group=group),
        grid=(bsz // nbb, seq // tb),
        in_specs=[blk(w3), blk(hw), blk(LANES), per_b3((SUBLANES, w3)), per_b3((N_HEADS, HEAD_DIM, HEAD_DIM)),
                  _resident(conv_w8.shape), _resident(head_params.shape), _resident(norm_w_row.shape),
                  _resident(ones_bd.shape)],
        out_specs=[blk(hw), per_b3((N_HEADS, HEAD_DIM, HEAD_DIM))],
        out_shape=[jax.ShapeDtypeStruct((bsz, seq, hw), F32),
                   jax.ShapeDtypeStruct((bsz, N_HEADS, HEAD_DIM, HEAD_DIM), F32)],
        scratch_shapes=[pltpu.VMEM((nbb, tb + 2 * SUBLANES, w3), F32)]
                       + [pltpu.VMEM((nbb, tb, hw), F32)] * 3
                       + [pltpu.VMEM((nbb, tb, LANES), F32)] * 2
                       + _gdn_core_scratch(nbb, tb, chunk),
        compiler_params=pltpu.CompilerParams(dimension_semantics=("parallel", "arbitrary"),
                                             vmem_limit_bytes=VMEM_LIMIT),
        name="gdn",
    )(gq, z, ba, conv_buf8, s0, conv_w8, head_params, norm_w_row, ones_bd)


def _gdn_pairs_call(qkv, z, beta, g, s0, norm_w_row, token_block, group):
    bsz, seq, w3 = qkv.shape
    hw = HEAD_WIDTH
    chunk = HEAD_DIM
    assert w3 == 3 * hw and seq % token_block == 0 and token_block % (chunk * group) == 0
    tb = token_block
    ones_bd = _head_ones()
    blk = lambda w: pl.BlockSpec((1, tb, w), lambda b, t: (b, t, 0))
    per_b3 = lambda s: pl.BlockSpec((1,) + s, lambda b, t: (b,) + (0,) * len(s))
    return pl.pallas_call(
        functools.partial(_gdn_pairs_kernel, group=group),
        grid=(bsz, seq // tb),
        in_specs=[blk(w3), blk(hw), blk(LANES), blk(LANES), per_b3((N_HEADS, HEAD_DIM, HEAD_DIM)),
                  _resident(norm_w_row.shape), _resident(ones_bd.shape)],
        out_specs=[blk(hw), per_b3((N_HEADS, HEAD_DIM, HEAD_DIM))],
        out_shape=[jax.ShapeDtypeStruct((bsz, seq, hw), F32),
                   jax.ShapeDtypeStruct((bsz, N_HEADS, HEAD_DIM, HEAD_DIM), F32)],
        scratch_shapes=[pltpu.VMEM((1, tb, hw), F32)]
                       + [pltpu.VMEM((1, N_HEADS // 2, HEAD_DIM, LANES), F32)]
                       + [pltpu.VMEM((1, tb, hw), F32)] * 5
                       + [pltpu.VMEM((1, (tb // chunk) * SUBLANES, hw), F32)],
        compiler_params=pltpu.CompilerParams(dimension_semantics=("parallel", "arbitrary"),
                                             vmem_limit_bytes=VMEM_LIMIT),
        name="gdn_pairs",
    )(qkv, z, beta, g, s0, norm_w_row, ones_bd)


def _pad_rows(x, rows, front=False):
    extra = rows - x.shape[1]
    cfg = [(0, 0)] * x.ndim
    cfg[1] = (extra, 0) if front else (0, extra)
    return jnp.pad(x, cfg)


def kernel(x_prompt, x_sample, cache_attn_k, cache_attn_v, state_gdn, state_conv, rel_bias, ln1_g, ln1_b,
           ffn1_w_gate, ffn1_w_up, ffn1_w_down, w_in, w_out, gdn_conv_w, gdn_a_log, gdn_dt_bias, gdn_norm_w,
           ln2_g, ln2_b, ffn2_w_gate, ffn2_w_up, ffn2_w_down, ln3_g, ln3_b):
    depth = w_in.shape[0]
    alpha = (2.0 * depth) ** 0.25
    bsz, seq, d_model = x_prompt.shape
    dbsz, dseq, _ = x_sample.shape
    w_buf = cache_attn_k.shape[2]
    hw = HEAD_WIDTH
    in_cols = w_in.shape[2]
    assert in_cols == 7 * hw + 2 * N_HEADS and CONV_WIDTH - 1 <= dseq <= SAMPLE_T_PAD
    assert seq % GDN_TOKEN_BLOCK == 0 and GDN_TOKEN_BLOCK % (GDN_CHUNK * GDN_CHUNK_GROUP) == 0

    sample_lows, sample_buckets = _sample_layout(w_buf, dseq)
    tab_p = _t5_table_call(rel_bias, jnp.asarray(_prompt_buckets()))
    tab_s = _t5_table_call(rel_bias, jnp.asarray(sample_buckets))

    yp = x_prompt.reshape(bsz * seq, d_model)
    ys = x_sample.reshape(dbsz * dseq, d_model)
    collected = [[] for _ in range(8)]
    row = lambda v: v.reshape(1, -1)
    for layer in range(depth):
        wg1, wu1, wd1 = (w[layer].astype(BF16) for w in (ffn1_w_gate, ffn1_w_up, ffn1_w_down))
        wg2, wu2, wd2 = (w[layer].astype(BF16) for w in (ffn2_w_gate, ffn2_w_up, ffn2_w_down))
        win = jnp.pad(w_in[layer], ((0, 0), (0, 7 * hw + LANES - in_cols))).astype(BF16)
        wo = w_out[layer].astype(BF16)
        conv_w8 = jnp.pad(gdn_conv_w[layer], ((0, SUBLANES - CONV_WIDTH), (0, 0)))
        head_params = jnp.zeros((SUBLANES, LANES), F32)
        head_params = head_params.at[0, N_HEADS:2 * N_HEADS].set(gdn_a_log[layer])
        head_params = head_params.at[1, N_HEADS:2 * N_HEADS].set(gdn_dt_bias[layer])
        norm_w_row = jnp.tile(gdn_norm_w[layer], N_HEADS).reshape(1, hw)
        pre = functools.partial(_pre_call, wg=wg1, wu=wu1, wd=wd1, g=row(ln1_g[layer]), b=row(ln1_b[layer]),
                                win=win, alpha=alpha)
        post = functools.partial(_post_call, wo=wo, g2=row(ln2_g[layer]), b2=row(ln2_b[layer]), wg=wg2, wu=wu2,
                                 wd=wd2, g3=row(ln3_g[layer]), b3=row(ln3_b[layer]), alpha=alpha)

        x1, aq, ak, av, ak_t, av_t, gdn_qkv, z, beta, gate, gq_tail = _pre_prompt_call(
            yp, wg1, wu1, wd1, row(ln1_g[layer]), row(ln1_b[layer]), win, conv_w8, head_params, alpha, seq)
        shp = lambda a: a.reshape(bsz, seq, a.shape[1])
        attn = _attn_prompt_call(shp(aq), shp(ak), shp(av), tab_p)
        gdn, s_p = _gdn_pairs_call(shp(gdn_qkv), shp(z), shp(beta), shp(gate),
                                   jnp.zeros((bsz, N_HEADS, HEAD_DIM, HEAD_DIM), F32), norm_w_row,
                                   token_block=GDN_TOKEN_BLOCK, group=GDN_CHUNK_GROUP)
        yp = post(attn.reshape(bsz * seq, hw), gdn.reshape(bsz * seq, hw), x1)
        wp = min(w_buf, seq)
        heads5 = lambda a_t: a_t.reshape(bsz, N_HEADS, HEAD_DIM, seq).transpose(0, 3, 1, 2)[:, seq - wp:]
        st_p = (heads5(ak_t), heads5(av_t), s_p, gq_tail[:, SUBLANES - (CONV_WIDTH - 1):])

        x1s, aqs, aks, avs, gqs, zs, bas = pre(ys)
        shs = lambda a: _pad_rows(a.reshape(dbsz, dseq, a.shape[1]), SAMPLE_T_PAD)
        heads5s = lambda a: a.reshape(dbsz, dseq, N_HEADS, HEAD_DIM)
        new_rows = lambda a: jnp.pad(heads5s(a).transpose(0, 2, 1, 3),
                                     ((0, 0), (0, 0), (0, SAMPLE_T_PAD - dseq), (0, 0)))
        by_head_t = lambda c: c.transpose(0, 2, 3, 1)
        attn_s = _attn_sample_call(shs(aqs), new_rows(aks), new_rows(avs), by_head_t(cache_attn_k[layer]),
                                   by_head_t(cache_attn_v[layer]), tab_s, sample_lows)
        gdn_s, s_s = _gdn_call(shs(gqs), shs(zs), shs(bas),
                               _pad_rows(state_conv[layer], SUBLANES, front=True), state_gdn[layer],
                               conv_w8, head_params, norm_w_row, n_valid=dseq,
                               token_block=SAMPLE_T_PAD, chunk=SAMPLE_T_PAD,
                               batch_block=GDN_SAMPLE_BATCH_BLOCK, group=1)
        unpad = lambda a: a[:, :dseq].reshape(dbsz * dseq, hw)
        ys = post(unpad(attn_s), unpad(gdn_s), x1s)
        st_s = (heads5s(aks), heads5s(avs), s_s,
                gqs.reshape(dbsz, dseq, 3 * hw)[:, dseq - (CONV_WIDTH - 1):])
        for lst, st in zip(collected, st_p + st_s):
            lst.append(st)
    outs = [jnp.stack(t, axis=0) for t in collected]
    return (yp.reshape(bsz, seq, d_model), ys.reshape(dbsz, dseq, d_model)) + tuple(outs)
```

```python
import functools
import math

import numpy as np
import jax
import jax.numpy as jnp
from jax import lax
from jax.experimental import pallas as pl
from jax.experimental.pallas import tpu as pltpu

F32 = jnp.float32
BF16 = jnp.bfloat16

HEAD_DIM = 64
N_HEADS = 8
HEAD_WIDTH = N_HEADS * HEAD_DIM
DILATED_PATTERNS = ((128, 1), (512, 4), (2048, 16))
KEYS_PER_BLOCK = 128
T5_BUCKETS = 32
T5_MAX_EXACT = 16
T5_MAX_DIST = 2048
CONV_WIDTH = 4
GDN_CHUNK = 64
LN_EPS = 1e-5
RMS_EPS = 1e-6
NEG_BIG = -1e30

LANES = 128
SUBLANES = 8
TOKEN_TILE = 256
POST_TOKEN_TILE = 512
GDN_TOKEN_BLOCK = 256
ATTN_SKEW_GROUPS = 1
GDN_CHUNK_GROUP = 4
GDN_SAMPLE_BATCH_BLOCK = 8
SAMPLE_T_PAD = 8
VMEM_LIMIT = 56 * 1024 * 1024


def _resident(shape):
    nd = len(shape)
    return pl.BlockSpec(shape, lambda *_: (0,) * nd, pipeline_mode=pl.Buffered(1))


def _layernorm(y, g, b):
    mu = jnp.mean(y, axis=-1, keepdims=True)
    yc = y - mu
    var = jnp.mean(yc * yc, axis=-1, keepdims=True)
    return yc * lax.rsqrt(var + LN_EPS) * g + b


def _swiglu(xb, wg_ref, wu_ref, wd_ref, h_scr):
    d_ff = wg_ref.shape[1]
    step = 2 * LANES
    assert d_ff % step == 0
    for c in range(d_ff // step):
        sl = slice(c * step, (c + 1) * step)
        gate = jnp.dot(xb, wg_ref[:, sl], preferred_element_type=F32)
        up = jnp.dot(xb, wu_ref[:, sl], preferred_element_type=F32)
        h_scr[:, sl] = (jax.nn.silu(gate) * up).astype(BF16)
    return jnp.dot(h_scr[...], wd_ref[...], preferred_element_type=F32)


PROJ_WIDTHS = (HEAD_WIDTH, HEAD_WIDTH, HEAD_WIDTH, 3 * HEAD_WIDTH, HEAD_WIDTH, LANES)
PROJ_EDGES = tuple(int(e) for e in np.cumsum((0,) + PROJ_WIDTHS))


def _first_half_step(x_ref, wg_ref, wu_ref, wd_ref, g_ref, b_ref, h_scr, alpha):
    x = x_ref[...]
    ff = _swiglu(x.astype(BF16), wg_ref, wu_ref, wd_ref, h_scr)
    return _layernorm(alpha * x + 0.5 * ff, g_ref[...], b_ref[...])


def _pre_kernel(x_ref, wg_ref, wu_ref, wd_ref, g_ref, b_ref, win_ref,
                x1_ref, aq_ref, ak_ref, av_ref, gq_ref, z_ref, ba_ref, h_scr, *, alpha):
    x1 = _first_half_step(x_ref, wg_ref, wu_ref, wd_ref, g_ref, b_ref, h_scr, alpha)
    x1_ref[...] = x1
    xb = x1.astype(BF16)
    for i, ref in enumerate((aq_ref, ak_ref, av_ref, gq_ref, z_ref, ba_ref)):
        ref[...] = jnp.dot(xb, win_ref[:, PROJ_EDGES[i]:PROJ_EDGES[i + 1]], preferred_element_type=F32)


def _gdn_gates(ba, hp_ref):
    beta = jax.nn.sigmoid(ba)
    g = -jnp.exp(hp_ref[0:1, :]) * jax.nn.softplus(ba + hp_ref[1:2, :])
    return beta, g


def _causal_conv_silu(ext_scr, x, cw_ref):
    tb = x.shape[0]
    hist = SUBLANES
    ext_scr[hist:hist + tb, :] = x
    ext = ext_scr[0:hist + tb, :]
    conv = jnp.zeros(x.shape, F32)
    for j in range(CONV_WIDTH):
        back = CONV_WIDTH - 1 - j
        rows = pltpu.roll(ext, back, axis=0)[hist:hist + tb, :] if back else x
        conv = conv + rows * cw_ref[j:j + 1, :]
    ext_scr[0:hist, :] = ext_scr[tb:tb + hist, :]
    return jax.nn.silu(conv)


def _pre_prompt_kernel(x_ref, wg_ref, wu_ref, wd_ref, g_ref, b_ref, win_ref, cw_ref, hp_ref, ones_ref,
                       x1_ref, aq_ref, ak_ref, av_ref, akt_ref, avt_ref, qkv_ref, z_ref, bt_ref, gg_ref, tail_ref,
                       h_scr, ext_scr, *, alpha, tiles_per_seq):
    x1 = _first_half_step(x_ref, wg_ref, wu_ref, wd_ref, g_ref, b_ref, h_scr, alpha)
    x1_ref[...] = x1
    xb = x1.astype(BF16)
    proj = lambda i: jnp.dot(xb, win_ref[:, PROJ_EDGES[i]:PROJ_EDGES[i + 1]], preferred_element_type=F32)

    @pl.when(pl.program_id(0) % tiles_per_seq == 0)
    def _():
        ext_scr[0:SUBLANES, :] = jnp.zeros((SUBLANES, ext_scr.shape[1]), F32)

    raw = proj(3)
    tail_ref[0] = raw[raw.shape[0] - SUBLANES:, :]
    act = _causal_conv_silu(ext_scr, raw, cw_ref)
    hw = HEAD_WIDTH
    head_sq = lambda x: jnp.dot((x * x).astype(BF16), ones_ref[...], preferred_element_type=F32)
    q = act[:, 0:hw]
    k = act[:, hw:2 * hw]
    qkv_ref[:, 0:hw] = q * lax.rsqrt(head_sq(q) + RMS_EPS) * HEAD_DIM ** -0.5
    qkv_ref[:, hw:2 * hw] = k * lax.rsqrt(head_sq(k) + RMS_EPS)
    qkv_ref[:, 2 * hw:3 * hw] = act[:, 2 * hw:3 * hw]
    bt_ref[...], gg_ref[...] = _gdn_gates(proj(5), hp_ref)
    aq_ref[...] = proj(0)
    for i, ref, ref_t in ((1, ak_ref, akt_ref), (2, av_ref, avt_ref)):
        kv = proj(i)
        ref[...] = kv
        ref_t[0] = kv.T
    z_ref[...] = proj(4)


def _pre_call(x2d, wg, wu, wd, g, b, win, alpha):
    m, d = x2d.shape
    d_ff = wg.shape[1]
    tm = TOKEN_TILE
    assert m % tm == 0 and sum(PROJ_WIDTHS) == win.shape[1]
    tile = lambda w: pl.BlockSpec((tm, w), lambda i: (i, 0))
    return pl.pallas_call(
        functools.partial(_pre_kernel, alpha=alpha),
        grid=(m // tm,),
        in_specs=[tile(d), _resident(wg.shape), _resident(wu.shape), _resident(wd.shape),
                  _resident(g.shape), _resident(b.shape), _resident(win.shape)],
        out_specs=[tile(d)] + [tile(w) for w in PROJ_WIDTHS],
        out_shape=[jax.ShapeDtypeStruct((m, d), F32)] + [jax.ShapeDtypeStruct((m, w), F32) for w in PROJ_WIDTHS],
        scratch_shapes=[pltpu.VMEM((tm, d_ff), BF16)],
        compiler_params=pltpu.CompilerParams(dimension_semantics=("parallel",), vmem_limit_bytes=VMEM_LIMIT),
        name="pre_ffn_proj",
    )(x2d, wg, wu, wd, g, b, win)


def _pre_prompt_call(x2d, wg, wu, wd, g, b, win, conv_w8, head_params, alpha, seq):
    m, d = x2d.shape
    d_ff = wg.shape[1]
    tm = TOKEN_TILE
    hw = HEAD_WIDTH
    assert seq % tm == 0 and m % seq == 0 and sum(PROJ_WIDTHS) == win.shape[1]
    per_seq = seq // tm
    tile = lambda w: pl.BlockSpec((tm, w), lambda i: (i, 0))
    t_spec = pl.BlockSpec((1, hw, tm), lambda i: (i // per_seq, 0, i % per_seq))
    t_shape = jax.ShapeDtypeStruct((m // seq, hw, seq), F32)
    rows = lambda w: jax.ShapeDtypeStruct((m, w), F32)
    consts = (wg, wu, wd, g, b, win, conv_w8, head_params, _head_ones())
    return pl.pallas_call(
        functools.partial(_pre_prompt_kernel, alpha=alpha, tiles_per_seq=per_seq),
        grid=(m // tm,),
        in_specs=[tile(d)] + [_resident(c.shape) for c in consts],
        out_specs=[tile(d), tile(hw), tile(hw), tile(hw), t_spec, t_spec, tile(3 * hw), tile(hw), tile(LANES),
                   tile(LANES), pl.BlockSpec((1, SUBLANES, 3 * hw), lambda i: (i // per_seq, 0, 0))],
        out_shape=[rows(d), rows(hw), rows(hw), rows(hw), t_shape, t_shape, rows(3 * hw), rows(hw), rows(LANES),
                   rows(LANES), jax.ShapeDtypeStruct((m // seq, SUBLANES, 3 * hw), F32)],
        scratch_shapes=[pltpu.VMEM((tm, d_ff), BF16), pltpu.VMEM((tm + 2 * SUBLANES, 3 * hw), F32)],
        compiler_params=pltpu.CompilerParams(dimension_semantics=("arbitrary",), vmem_limit_bytes=VMEM_LIMIT),
        name="pre_ffn_proj_gdnprep",
    )(x2d, *consts)


def _post_kernel(attn_ref, gdn_ref, x1_ref, wo_ref, g2_ref, b2_ref, wg_ref, wu_ref, wd_ref, g3_ref, b3_ref,
                 y_ref, h_scr, *, alpha):
    mix = jnp.dot(attn_ref[...].astype(BF16), wo_ref[0:HEAD_WIDTH, :], preferred_element_type=F32)
    mix = mix + jnp.dot(gdn_ref[...].astype(BF16), wo_ref[HEAD_WIDTH:2 * HEAD_WIDTH, :],
                        preferred_element_type=F32)
    x2 = _layernorm(alpha * x1_ref[...] + mix, g2_ref[...], b2_ref[...])
    ff = _swiglu(x2.astype(BF16), wg_ref, wu_ref, wd_ref, h_scr)
    y_ref[...] = _layernorm(alpha * x2 + 0.5 * ff, g3_ref[...], b3_ref[...])


def _post_call(attn, gdn, x1, wo, g2, b2, wg, wu, wd, g3, b3, alpha):
    m, d = x1.shape
    tm = POST_TOKEN_TILE
    assert m % tm == 0
    tile = lambda w: pl.BlockSpec((tm, w), lambda i: (i, 0))
    consts = (wo, g2, b2, wg, wu, wd, g3, b3)
    return pl.pallas_call(
        functools.partial(_post_kernel, alpha=alpha),
        grid=(m // tm,),
        in_specs=[tile(HEAD_WIDTH), tile(HEAD_WIDTH), tile(d)] + [_resident(c.shape) for c in consts],
        out_specs=tile(d),
        out_shape=jax.ShapeDtypeStruct((m, d), F32),
        scratch_shapes=[pltpu.VMEM((tm, wg.shape[1]), BF16)],
        compiler_params=pltpu.CompilerParams(dimension_semantics=("parallel",), vmem_limit_bytes=VMEM_LIMIT),
        name="post_out_ffn",
    )(attn, gdn, x1, *consts)


def _t5_bucket_np(dist):
    n = np.maximum(dist, 0)
    nf = np.maximum(n, 1).astype(np.float32)
    large = T5_MAX_EXACT + (np.log(nf / np.float32(T5_MAX_EXACT)) / np.float32(math.log(T5_MAX_DIST / T5_MAX_EXACT))
                            * np.float32(T5_BUCKETS - T5_MAX_EXACT)).astype(np.int32)
    large = np.minimum(large, T5_BUCKETS - 1)
    return np.where(n < T5_MAX_EXACT, n, large).astype(np.int32)


def _t5_table_kernel(rb_ref, bk_ref, out_ref):
    for h in range(N_HEADS):
        bk = bk_ref[0, h % bk_ref.shape[1]]
        acc = jnp.full(bk.shape, NEG_BIG, F32)
        for b in range(T5_BUCKETS):
            acc = jnp.where(bk == b, rb_ref[b, h], acc)
        out_ref[0, h] = acc


def _t5_table_call(rel_bias, buckets):
    p, hb, r, c = buckets.shape
    assert hb in (1, N_HEADS)
    return pl.pallas_call(
        _t5_table_kernel,
        grid=(p,),
        in_specs=[pl.BlockSpec(memory_space=pltpu.SMEM),
                  pl.BlockSpec((1, hb, r, c), lambda i: (i, 0, 0, 0))],
        out_specs=pl.BlockSpec((1, N_HEADS, r, c), lambda i: (i, 0, 0, 0)),
        out_shape=jax.ShapeDtypeStruct((p, N_HEADS, r, c), F32),
        compiler_params=pltpu.CompilerParams(dimension_semantics=("parallel",)),
        name="t5_bias_table",
    )(rel_bias, buckets)


def _prompt_buckets():
    nb = KEYS_PER_BLOCK
    qi = np.arange(nb)[:, None]
    ki = np.arange(2 * nb)[None, :]
    dist = qi + nb - ki
    valid = (dist >= 0) & (dist <= nb)
    out = []
    for window, dil in DILATED_PATTERNS:
        assert window // dil == nb
        out.append(np.where(valid, _t5_bucket_np(dist * dil), -1))
    return np.stack(out).astype(np.int32)[:, None]


def _nt_dot(a, b):
    return lax.dot_general(a, b, (((1,), (1,)), ((), ())), preferred_element_type=F32)


def _emit_skewed(stages, items, n_groups):
    n_groups = max(1, min(n_groups, len(items)))
    groups = [items[i::n_groups] for i in range(n_groups)]
    for step in range(len(stages) + n_groups - 1):
        for gi, group in enumerate(groups):
            s = step - gi
            if 0 <= s < len(stages):
                for item in group:
                    stages[s](item)


def _attn_prompt_kernel(q_ref, k_ref, v_ref, tab_ref, o_ref, og_scr, lg_scr, *, seq):
    nb = KEYS_PER_BLOCK
    hd = HEAD_DIM
    n_pat = len(DILATED_PATTERNS)
    heads = LANES // hd
    for g, (window, dil) in enumerate(DILATED_PATTERNS):
        n_blocks = seq // (dil * nb)

        def blocks(starts, with_prev, g=g, dil=dil):
            lane_head = lax.broadcasted_iota(jnp.int32, (nb, LANES), 1) // hd
            n_keys = 2 * nb if with_prev else nb
            row_sl, qs, ks, vs = [], [], [], []
            for start in starts:
                first_key = start - dil * (n_keys - nb)
                row_sl.append(pl.ds(start, nb, stride=dil) if dil > 1 else pl.ds(start, nb))
                key_rows = pl.ds(first_key, n_keys, stride=dil) if dil > 1 else pl.ds(first_key, n_keys)
                qs.append(q_ref[0, row_sl[-1], :] * hd ** -0.5)
                ks.append(k_ref[0, key_rows, :].astype(BF16))
                vs.append(v_ref[0, key_rows, :].astype(BF16))
            chains = [(u, hh) for u in range(len(starts)) for hh in range(heads)]
            s, m, p, den, acc = {}, {}, {}, {}, {}

            def scores(c):
                u, hh = c
                qb = jnp.where(lane_head == hh, qs[u], 0.0).astype(BF16)
                s[c] = _nt_dot(qb, ks[u]) + tab_ref[g, hh, :, 2 * nb - n_keys:2 * nb]

            def row_max(c):
                m[c] = jnp.max(s[c], axis=-1, keepdims=True)

            def probs(c):
                p[c] = jnp.exp(s[c] - m[c])
                den[c] = jnp.sum(p[c], axis=-1, keepdims=True)

            def values(c):
                u, hh = c
                acc[c] = jnp.dot(p[c].astype(BF16), vs[u], preferred_element_type=F32)

            def finish(c):
                acc[c] = acc[c] / den[c]
                den[c] = m[c] + jnp.log(den[c])

            _emit_skewed((scores, row_max, probs, values, finish), chains, ATTN_SKEW_GROUPS)
            for u in range(len(starts)):
                out, lse = acc[u, 0], den[u, 0]
                for hh in range(1, heads):
                    out = jnp.where(lane_head == hh, acc[u, hh], out)
                    lse = jnp.where(lane_head == hh, den[u, hh], lse)
                og_scr[g, row_sl[u], :] = out
                lg_scr[g, row_sl[u], :] = jnp.broadcast_to(lse, (nb, LANES))

        def unroll_of(count):
            return next(u for u in (4, 3, 2, 1) if count % u == 0)

        u_first = unroll_of(dil)

        def first_body(i, carry, blocks=blocks, u_first=u_first):
            blocks([i * u_first + j for j in range(u_first)], False)
            return carry

        lax.fori_loop(0, dil // u_first, first_body, 0)
        if n_blocks > 1:
            n_rest = dil * (n_blocks - 1)
            u_rest = unroll_of(n_rest)

            def rest_body(i, carry, blocks=blocks, dil=dil, n_blocks=n_blocks, u_rest=u_rest):
                starts = []
                for j in range(u_rest):
                    idx = i * u_rest + j
                    starts.append(idx // (n_blocks - 1) + dil * nb * (idx % (n_blocks - 1) + 1))
                blocks(starts, True)
                return carry

            lax.fori_loop(0, n_rest // u_rest, rest_body, 0)

    rb = 2 * nb

    def merge_body(i, carry):
        rows = pl.ds(pl.multiple_of(i * rb, rb), rb)
        lses = [lg_scr[g, rows, :] for g in range(n_pat)]
        top = functools.reduce(jnp.maximum, lses)
        ws = [jnp.exp(l - top) for l in lses]
        num = functools.reduce(lambda a, b: a + b, [w * og_scr[g, rows, :] for g, w in enumerate(ws)])
        o_ref[0, rows, :] = num / functools.reduce(lambda a, b: a + b, ws)
        return carry

    lax.fori_loop(0, seq // rb, merge_body, 0)


def _attn_prompt_call(aq, ak, av, tab):
    bsz, seq, width = aq.shape
    assert width == HEAD_WIDTH and seq % (DILATED_PATTERNS[-1][1] * KEYS_PER_BLOCK) == 0
    n_pat = len(DILATED_PATTERNS)
    heads_per_step = LANES // HEAD_DIM
    qkv_spec = pl.BlockSpec((1, seq, LANES), lambda h, b: (b, 0, h))
    return pl.pallas_call(
        functools.partial(_attn_prompt_kernel, seq=seq),
        grid=(N_HEADS // heads_per_step, bsz),
        in_specs=[qkv_spec, qkv_spec, qkv_spec,
                  pl.BlockSpec((n_pat, heads_per_step, KEYS_PER_BLOCK, 2 * KEYS_PER_BLOCK),
                               lambda h, b: (0, h, 0, 0))],
        out_specs=pl.BlockSpec((1, seq, LANES), lambda h, b: (b, 0, h)),
        out_shape=jax.ShapeDtypeStruct((bsz, seq, width), F32),
        scratch_shapes=[pltpu.VMEM((n_pat, seq, LANES), F32)] * 2,
        compiler_params=pltpu.CompilerParams(dimension_semantics=("parallel", "parallel"),
                                             vmem_limit_bytes=VMEM_LIMIT),
        name="attn_prompt",
    )(aq, ak, av, tab)


def _sample_layout(w_buf, t_new):
    assert w_buf % LANES == 0 and t_new <= SAMPLE_T_PAD
    row_of_col = np.full((w_buf + LANES,), -1, np.int64)
    row_of_col[:w_buf + t_new] = np.arange(w_buf + t_new)
    buckets = np.full((len(DILATED_PATTERNS), 1, SAMPLE_T_PAD, w_buf + LANES), -1, np.int32)
    lows = []
    for g, (window, dil) in enumerate(DILATED_PATTERNS):
        lows.append(max(0, w_buf - window) // LANES * LANES)
        for t in range(t_new):
            dist = w_buf + t - row_of_col
            ok = (row_of_col >= 0) & (dist >= 0) & (dist <= window) & (dist % dil == 0)
            assert int(ok.sum()) == min(window, w_buf + t) // dil + 1 and not ok[:lows[g]].any()
            buckets[g, 0, t] = np.where(ok, _t5_bucket_np(dist), -1)
    return tuple(lows), buckets


def _attn_sample_kernel(q_ref, kn_ref, vn_ref, kt_ref, vt_ref, tab_ref, o_ref, *, lows):
    tp = SAMPLE_T_PAD
    hd = HEAD_DIM
    w_buf = kt_ref.shape[3]
    n_pat = len(DILATED_PATTERNS)
    add = lambda x, y: x + y
    rowmax = lambda x: jnp.max(x, axis=-1, keepdims=True)
    rowsum = lambda x: jnp.sum(x, axis=-1, keepdims=True)
    heads = range(N_HEADS)
    q8 = q_ref[0] * hd ** -0.5
    s_buf = {h: jnp.dot(q8[:, hd * h:hd * (h + 1)], kt_ref[0, h], preferred_element_type=F32) for h in heads}
    s_new = {h: _nt_dot(q8[:, hd * h:hd * (h + 1)], kn_ref[0, h]) for h in heads}
    p_buf, p_new, den, lse = {}, {}, {}, {}
    for h in heads:
        for g in range(n_pat):
            lo = lows[g]
            l_buf = s_buf[h][:, lo:] + tab_ref[g, h, :, lo:w_buf]
            l_new = s_new[h] + tab_ref[g, h, :, w_buf:w_buf + tp]
            m = jnp.maximum(rowmax(l_buf), rowmax(l_new))
            p_buf[h, g] = jnp.exp(l_buf - m)
            p_new[h, g] = jnp.exp(l_new - m)
            den[h, g] = rowsum(p_buf[h, g]) + rowsum(p_new[h, g])
            lse[h, g] = m + jnp.log(den[h, g])
    edges = sorted(set(lows)) + [w_buf]
    outs = {}
    for h in heads:
        top = functools.reduce(jnp.maximum, [lse[h, g] for g in range(n_pat)])
        ws = [jnp.exp(lse[h, g] - top) for g in range(n_pat)]
        w_sum = functools.reduce(add, ws)
        coef = [ws[g] / (w_sum * den[h, g]) for g in range(n_pat)]
        acc = jnp.dot(functools.reduce(add, [coef[g] * p_new[h, g] for g in range(n_pat)]), vn_ref[0, h],
                      preferred_element_type=F32)
        for e0, e1 in zip(edges[:-1], edges[1:]):
            mix = functools.reduce(add, [coef[g] * p_buf[h, g][:, e0 - lows[g]:e1 - lows[g]]
                                         for g in range(n_pat) if lows[g] <= e0])
            acc = acc + _nt_dot(mix, vt_ref[0, h, :, e0:e1])
        outs[h] = acc
    o_ref[0] = jnp.concatenate([outs[h] for h in heads], axis=1)


def _attn_sample_call(aq8, k_new, v_new, cache_kt, cache_vt, tab, lows):
    bsz, n_heads, hd, w_buf = cache_kt.shape
    q_spec = pl.BlockSpec((1, SAMPLE_T_PAD, n_heads * hd), lambda b: (b, 0, 0))
    new_spec = pl.BlockSpec((1, n_heads, SAMPLE_T_PAD, hd), lambda b: (b, 0, 0, 0))
    buf_spec = pl.BlockSpec((1, n_heads, hd, w_buf), lambda b: (b, 0, 0, 0))
    return pl.pallas_call(
        functools.partial(_attn_sample_kernel, lows=lows),
        grid=(bsz,),
        in_specs=[q_spec, new_spec, new_spec, buf_spec, buf_spec, _resident(tab.shape)],
        out_specs=q_spec,
        out_shape=jax.ShapeDtypeStruct((bsz, SAMPLE_T_PAD, n_heads * hd), F32),
        compiler_params=pltpu.CompilerParams(dimension_semantics=("parallel",), vmem_limit_bytes=VMEM_LIMIT),
        name="attn_sample",
    )(aq8, k_new, v_new, cache_kt, cache_vt, tab)


def _aligned(index, size):
    return index * size if isinstance(index, int) else pl.multiple_of(index * size, size)


def _split3(x):
    h1 = x.astype(BF16)
    r1 = x - h1.astype(F32)
    h2 = r1.astype(BF16)
    h3 = (r1 - h2.astype(F32)).astype(BF16)
    return h1, h2, h3


def _head_sums(x, ones_ref):
    h1, h2, _ = _split3(x)
    return (jnp.dot(h1, ones_ref[...], preferred_element_type=F32)
            + jnp.dot(h2, ones_ref[...], preferred_element_type=F32))


def _store_normed_qk(act, q_dst, k_dst, ones_ref):
    hw = HEAD_WIDTH
    q = act[:, 0:hw]
    k = act[:, hw:2 * hw]
    q_dst[...] = q * lax.rsqrt(_head_sums(q * q, ones_ref) + RMS_EPS) * HEAD_DIM ** -0.5
    k_dst[...] = k * lax.rsqrt(_head_sums(k * k, ones_ref) + RMS_EPS)


def _gdn_kernel(x_ref, z_ref, ba_ref, cb_ref, s0_ref, cw_ref, hp_ref, nw_ref, ones_ref,
                o_ref, sfin_ref,
                ext_scr, q_scr, k_scr, v_scr, bt_scr, g_scr, *core_scr, chunk, n_valid, group):
    n_bb, tb = x_ref.shape[0], x_ref.shape[1]
    t = pl.program_id(1)

    @pl.when(t == 0)
    def _():
        ext_scr[:, 0:SUBLANES, :] = cb_ref[...]

    for bb in range(n_bb):
        act = _causal_conv_silu(ext_scr.at[bb], x_ref[bb], cw_ref)
        _store_normed_qk(act, q_scr.at[bb], k_scr.at[bb], ones_ref)
        v_scr[bb] = act[:, 2 * HEAD_WIDTH:3 * HEAD_WIDTH]
        ba = ba_ref[bb]
        live = t * tb + lax.broadcasted_iota(jnp.int32, ba.shape, 0) < n_valid
        beta, g = _gdn_gates(ba, hp_ref)
        bt_scr[bb] = jnp.where(live, beta, 0.0)
        g_scr[bb] = jnp.where(live, g, 0.0)
    _gdn_core(s0_ref, z_ref, nw_ref, ones_ref, o_ref, sfin_ref, q_scr, k_scr, v_scr, 0, bt_scr, g_scr,
              *core_scr, chunk=chunk, group=group)


def _gdn_pairs_kernel(qkv_ref, z_ref, bt_scr, g_scr, s0_ref, nw_ref, ones_ref, o_ref, sfin_ref,
                      oc_scr, s_scr, u_scr, w_scr, in_scr, qd_scr, kd_scr, el_scr, *, group):
    chunk = HEAD_DIM
    k_lane0, v_lane0 = HEAD_WIDTH, 2 * HEAD_WIDTH
    n_bb, tb = bt_scr.shape[0], bt_scr.shape[1]
    n_chunks = tb // chunk
    n_pairs = N_HEADS // 2
    hd = HEAD_DIM
    t = pl.program_id(1)
    dot = functools.partial(jnp.dot, preferred_element_type=F32)

    lane = lax.broadcasted_iota(jnp.int32, (chunk, LANES), 1)
    row = lax.broadcasted_iota(jnp.int32, (chunk, LANES), 0)
    second = lane >= hd
    col = lane - jnp.where(second, hd, 0)
    tril = row >= col
    strict = row > col
    eye2 = (row == col).astype(F32)
    tril_b = (lax.broadcasted_iota(jnp.int32, (chunk, chunk), 0)
              >= lax.broadcasted_iota(jnp.int32, (chunk, chunk), 1)).astype(BF16)
    n_doublings = int(math.log2(chunk)) - 1

    def blockdiag(x):
        return jnp.concatenate([jnp.where(second, 0.0, x), jnp.where(second, x, 0.0)], axis=0)

    def per_head(cols, j):
        return jnp.where(second, cols[:, 2 * j + 1:2 * j + 2], cols[:, 2 * j:2 * j + 1])

    @pl.when(t == 0)
    def _():
        for bb in range(n_bb):
            for j in range(n_pairs):
                s_scr[bb, j] = jnp.concatenate([s0_ref[bb, 2 * j], s0_ref[bb, 2 * j + 1]], axis=1)

    def state_free_part(gi, carry):
        probs, shared = [], {}
        for bb in range(n_bb):
            for cg in range(group):
                c_idx = gi * group + cg
                rows = pl.ds(_aligned(c_idx, chunk), chunk)
                gc = functools.reduce(lambda a, b: a + b,
                                      [dot(tril_b, p) for p in _split3(g_scr[bb, rows, :])])
                shared[bb, cg] = (rows, c_idx, gc[:, N_HEADS:2 * N_HEADS], gc.T[N_HEADS:2 * N_HEADS, :],
                                  bt_scr[bb, rows, :])
                probs += [(bb, cg, j) for j in range(n_pairs)]
        kk, qk, decay, a, t_inv, pw, vb, kbg = {}, {}, {}, {}, {}, {}, {}, {}
        for p in probs:
            bb, cg, j = p
            rows, c_idx, gc, gct, bt = shared[bb, cg]
            pl_ = pl.ds(LANES * j, LANES)
            q2 = qkv_ref[bb, rows, pl_]
            k2 = qkv_ref[bb, rows, pl.ds(k_lane0 + LANES * j, LANES)]
            beta2 = per_head(bt, j)
            gcc = per_head(gc, j)
            gcr = jnp.concatenate([gct[2 * j:2 * j + 1, :], gct[2 * j + 1:2 * j + 2, :]], axis=1)
            g_last = gcc[chunk - 1:chunk, :]
            e_gc = jnp.exp(gcc)
            kb2 = k2 * beta2
            k_bd = blockdiag(k2)
            kk[p] = _nt_dot(kb2, k_bd)
            qk[p] = _nt_dot(q2, k_bd)
            vb[p] = qkv_ref[bb, rows, pl.ds(v_lane0 + LANES * j, LANES)] * beta2
            kbg[p] = kb2 * e_gc
            qd_scr[bb, rows, pl_] = q2 * e_gc
            kd_scr[bb, rows, pl_] = k2 * jnp.exp(g_last - gcc)
            el_scr[bb, pl.ds(_aligned(c_idx, SUBLANES), SUBLANES), pl_] = jnp.broadcast_to(
                jnp.exp(g_last), (SUBLANES, LANES))
            decay[p] = jnp.where(tril, jnp.exp(jnp.where(tril, gcc - gcr, 0.0)), 0.0)
        for p in probs:
            bb, cg, j = p
            rows = shared[bb, cg][0]
            a[p] = jnp.where(strict, kk[p] * decay[p], 0.0)
            in_scr[bb, rows, pl.ds(LANES * j, LANES)] = jnp.where(tril, qk[p] * decay[p], 0.0)
            t_inv[p] = eye2 - a[p]
        for p in probs:
            pw[p] = dot(a[p], blockdiag(a[p]))
        for _ in range(n_doublings - 1):
            nxt_pw, nxt_t = {}, {}
            for p in probs:
                nxt_pw[p] = dot(pw[p], blockdiag(pw[p]))
                nxt_t[p] = dot(t_inv[p], blockdiag(eye2 + pw[p]))
            pw, t_inv = nxt_pw, nxt_t
        for p in probs:
            t_inv[p] = dot(t_inv[p], blockdiag(eye2 + pw[p]))
        for p in probs:
            bb, cg, j = p
            rows = shared[bb, cg][0]
            uw = dot(t_inv[p], jnp.concatenate([blockdiag(vb[p]), blockdiag(kbg[p])], axis=1))
            u_scr[bb, rows, pl.ds(LANES * j, LANES)] = uw[:, 0:LANES]
            w_scr[bb, rows, pl.ds(LANES * j, LANES)] = uw[:, LANES:2 * LANES]
        return carry

    def state_part(ci_, carry):
        rows = pl.ds(_aligned(ci_, chunk), chunk)
        el_rows = pl.ds(_aligned(ci_, SUBLANES), SUBLANES)
        probs = [(bb, j) for bb in range(n_bb) for j in range(n_pairs)]
        s_old, both, v_new, o_in, s_add = {}, {}, {}, {}, {}
        for p in probs:
            bb, j = p
            pl_ = pl.ds(LANES * j, LANES)
            s_old[p] = s_scr[bb, j]
            lhs = jnp.concatenate([w_scr[bb, rows, pl_], qd_scr[bb, rows, pl_]], axis=0)
            both[p] = dot(lhs, blockdiag(s_old[p]))
        for p in probs:
            bb, j = p
            pl_ = pl.ds(LANES * j, LANES)
            v_new[p] = u_scr[bb, rows, pl_] - both[p][0:chunk]
            o_in[p] = dot(in_scr[bb, rows, pl_], blockdiag(v_new[p]))
            cross = lax.dot_general(kd_scr[bb, rows, pl_], v_new[p], (((0,), (0,)), ((), ())),
                                    preferred_element_type=F32)
            s_add[p] = jnp.where(second, cross[hd:2 * hd], cross[0:hd])
        for p in probs:
            bb, j = p
            pl_ = pl.ds(LANES * j, LANES)
            oc_scr[bb, rows, pl_] = both[p][chunk:2 * chunk] + o_in[p]
            s_scr[bb, j] = s_old[p] * el_scr[bb, el_rows, pl_][0:1, :] + s_add[p]
        return carry

    assert n_chunks % group == 0
    if n_chunks == group:
        state_free_part(0, 0)
    else:
        lax.fori_loop(0, n_chunks // group, state_free_part, 0)
    if n_chunks == 1:
        state_part(0, 0)
    else:
        lax.fori_loop(0, n_chunks, state_part, 0)

    for bb in range(n_bb):
        o = oc_scr[bb]
        o = o * lax.rsqrt(_head_sums(o * o, ones_ref) * (1.0 / hd) + RMS_EPS) * nw_ref[...]
        o_ref[bb] = o * jax.nn.silu(z_ref[bb])

    @pl.when(t == pl.num_programs(1) - 1)
    def _():
        for bb in range(n_bb):
            for j in range(n_pairs):
                sfin_ref[bb, 2 * j] = s_scr[bb, j][:, 0:hd]
                sfin_ref[bb, 2 * j + 1] = s_scr[bb, j][:, hd:2 * hd]


def _gdn_core(s0_ref, z_ref, nw_ref, ones_ref, o_ref, sfin_ref, q_scr, k_scr, v_scr, v_lane0, bt_scr, g_scr,
              oc_scr, s_scr, uw_scr, in_scr, qd_scr, kd_scr, el_scr, *, chunk, group):
    n_bb, tb = q_scr.shape[0], q_scr.shape[1]
    n_chunks = tb // chunk
    hd = HEAD_DIM
    t = pl.program_id(1)

    @pl.when(t == 0)
    def _():
        s_scr[...] = s0_ref[...]

    ri = lax.broadcasted_iota(jnp.int32, (chunk, chunk), 0)
    ci = lax.broadcasted_iota(jnp.int32, (chunk, chunk), 1)
    tril = ri >= ci
    strict = ri > ci
    eye_c = (ri == ci).astype(F32)
    tril_b = tril.astype(BF16)
    n_doublings = int(math.log2(chunk)) - 1
    assert 2 ** (n_doublings + 1) == chunk

    dot = functools.partial(jnp.dot, preferred_element_type=F32)

    def state_free_part(gi, carry):
        probs, shared = [], {}
        for bb in range(n_bb):
            for cg in range(group):
                c_idx = gi * group + cg
                rows = pl.ds(_aligned(c_idx, chunk), chunk)
                gc = functools.reduce(lambda a, b: a + b,
                                      [dot(tril_b, p) for p in _split3(g_scr[bb, rows, :])])
                gct = gc.T
                g_last = gc[chunk - 1:chunk, :]
                el_rows = pl.ds(_aligned(c_idx, SUBLANES), SUBLANES)
                el_scr[bb, el_rows, :] = jnp.broadcast_to(jnp.exp(g_last), (SUBLANES, LANES))
                shared[bb, cg] = (rows, gc, gct, jnp.exp(gc), jnp.exp(g_last - gc), bt_scr[bb, rows, :])
                probs += [(bb, cg, h) for h in range(N_HEADS)]
        kk, qk, rhs, decay, a, t_inv, pw = {}, {}, {}, {}, {}, {}, {}
        for p in probs:
            bb, cg, h = p
            rows, gc, gct, e_gc, e_rem, bt = shared[bb, cg]
            hl = pl.ds(hd * h, hd)
            gl = N_HEADS + h
            qh = q_scr[bb, rows, hl]
            kh = k_scr[bb, rows, hl]
            beta = bt[:, h:h + 1]
            kb = kh * beta
            kk[p] = _nt_dot(kb, kh)
            qk[p] = _nt_dot(qh, kh)
            v_h = v_scr[bb, rows, pl.ds(v_lane0 + hd * h, hd)]
            rhs[p] = jnp.concatenate([v_h * beta, kb * e_gc[:, gl:gl + 1]], axis=1)
            qd_scr[bb, rows, hl] = qh * e_gc[:, gl:gl + 1]
            kd_scr[bb, rows, hl] = kh * e_rem[:, gl:gl + 1]
            diff = gc[:, gl:gl + 1] - gct[gl:gl + 1, :]
            decay[p] = jnp.where(tril, jnp.exp(jnp.where(tril, diff, 0.0)), 0.0)
        for p in probs:
            bb, cg, h = p
            rows = shared[bb, cg][0]
            a[p] = jnp.where(strict, kk[p] * decay[p], 0.0)
            in_scr[bb, rows, LANES * h:LANES * h + chunk] = jnp.where(tril, qk[p] * decay[p], 0.0)
            t_inv[p] = eye_c - a[p]
        for p in probs:
            pw[p] = dot(a[p], a[p])
        for _ in range(n_doublings - 1):
            nxt_pw, nxt_t = {}, {}
            for p in probs:
                nxt_pw[p] = dot(pw[p], pw[p])
                nxt_t[p] = dot(t_inv[p], eye_c + pw[p])
            pw, t_inv = nxt_pw, nxt_t
        for p in probs:
            t_inv[p] = dot(t_inv[p], eye_c + pw[p])
        for p in probs:
            bb, cg, h = p
            rows = shared[bb, cg][0]
            uw_scr[bb, rows, LANES * h:LANES * (h + 1)] = dot(t_inv[p], rhs[p])
        return carry

    def state_part(ci_, carry):
        rows = pl.ds(_aligned(ci_, chunk), chunk)
        el_rows = pl.ds(_aligned(ci_, SUBLANES), SUBLANES)
        probs = [(bb, h) for bb in range(n_bb) for h in range(N_HEADS)]
        s_old, uw, both, v_new, o_in, s_add = {}, {}, {}, {}, {}, {}
        for p in probs:
            bb, h = p
            s_old[p] = s_scr[bb, h]
            uw[p] = uw_scr[bb, rows, LANES * h:LANES * (h + 1)]
            lhs = jnp.concatenate([uw[p][:, hd:2 * hd], qd_scr[bb, rows, pl.ds(hd * h, hd)]], axis=0)
            both[p] = dot(lhs, s_old[p])
        for p in probs:
            bb, h = p
            v_new[p] = uw[p][:, 0:hd] - both[p][0:chunk]
            o_in[p] = dot(in_scr[bb, rows, LANES * h:LANES * h + chunk], v_new[p])
            s_add[p] = lax.dot_general(kd_scr[bb, rows, pl.ds(hd * h, hd)], v_new[p],
                                       (((0,), (0,)), ((), ())), preferred_element_type=F32)
        for p in probs:
            bb, h = p
            e_last = el_scr[bb, el_rows, :][0:1, N_HEADS + h:N_HEADS + h + 1]
            oc_scr[bb, rows, pl.ds(hd * h, hd)] = both[p][chunk:2 * chunk] + o_in[p]
            s_scr[bb, h] = s_old[p] * e_last + s_add[p]
        return carry

    assert n_chunks % group == 0
    if n_chunks == group:
        state_free_part(0, 0)
    else:
        lax.fori_loop(0, n_chunks // group, state_free_part, 0)
    if n_chunks == 1:
        state_part(0, 0)
    else:
        lax.fori_loop(0, n_chunks, state_part, 0)

    for bb in range(n_bb):
        o = oc_scr[bb]
        o = o * lax.rsqrt(_head_sums(o * o, ones_ref) * (1.0 / hd) + RMS_EPS) * nw_ref[...]
        o_ref[bb] = o * jax.nn.silu(z_ref[bb])

    @pl.when(t == pl.num_programs(1) - 1)
    def _():
        sfin_ref[...] = s_scr[...]


def _gdn_core_scratch(nbb, tb, chunk):
    hw = HEAD_WIDTH
    return ([pltpu.VMEM((nbb, tb, hw), F32)]
            + [pltpu.VMEM((nbb, N_HEADS, HEAD_DIM, HEAD_DIM), F32)]
            + [pltpu.VMEM((nbb, tb, N_HEADS * LANES), F32)] * 2
            + [pltpu.VMEM((nbb, tb, hw), F32)] * 2
            + [pltpu.VMEM((nbb, (tb // chunk) * SUBLANES, LANES), F32)])


def _head_ones():
    ones_bd = np.kron(np.eye(N_HEADS, dtype=np.float32), np.ones((HEAD_DIM, HEAD_DIM), np.float32))
    return jnp.asarray(ones_bd, BF16)


def _gdn_call(gq, z, ba, conv_buf8, s0, conv_w8, head_params, norm_w_row, n_valid, token_block, chunk,
              batch_block, group):
    bsz, seq, w3 = gq.shape
    hw = HEAD_WIDTH
    assert w3 == 3 * hw and seq % token_block == 0 and token_block % chunk == 0 and bsz % batch_block == 0
    tb = token_block
    nbb = batch_block
    ones_bd = _head_ones()
    blk = lambda w: pl.BlockSpec((nbb, tb, w), lambda b, t: (b, t, 0))
    per_b3 = lambda s: pl.BlockSpec((nbb,) + s, lambda b, t: (b,) + (0,) * len(s))
    return pl.pallas_call(
        functools.partial(_gdn_kernel, chunk=chunk, n_valid=n_valid, group=group),
        grid=(bsz // nbb, seq // tb),
        in_specs=[blk(w3), blk(hw), blk(LANES), per_b3((SUBLANES, w3)), per_b3((N_HEADS, HEAD_DIM, HEAD_DIM)),
                  _resident(conv_w8.shape), _resident(head_params.shape), _resident(norm_w_row.shape),
                  _resident(ones_bd.shape)],
        out_specs=[blk(hw), per_b3((N_HEADS, HEAD_DIM, HEAD_DIM))],
        out_shape=[jax.ShapeDtypeStruct((bsz, seq, hw), F32),
                   jax.ShapeDtypeStruct((bsz, N_HEADS, HEAD_DIM, HEAD_DIM), F32)],
        scratch_shapes=[pltpu.VMEM((nbb, tb + 2 * SUBLANES, w3), F32)]
                       + [pltpu.VMEM((nbb, tb, hw), F32)] * 3
                       + [pltpu.VMEM((nbb, tb, LANES), F32)] * 2
                       + _gdn_core_scratch(nbb, tb, chunk),
        compiler_params=pltpu.CompilerParams(dimension_semantics=("parallel", "arbitrary"),
                                             vmem_limit_bytes=VMEM_LIMIT),
        name="gdn",
    )(gq, z, ba, conv_buf8, s0, conv_w8, head_params, norm_w_row, ones_bd)


def _gdn_pairs_call(qkv, z, beta, g, s0, norm_w_row, token_block, group):
    bsz, seq, w3 = qkv.shape
    hw = HEAD_WIDTH
    chunk = HEAD_DIM
    assert w3 == 3 * hw and seq % token_block == 0 and token_block % (chunk * group) == 0
    tb = token_block
    ones_bd = _head_ones()
    blk = lambda w: pl.BlockSpec((1, tb, w), lambda b, t: (b, t, 0))
    per_b3 = lambda s: pl.BlockSpec((1,) + s, lambda b, t: (b,) + (0,) * len(s))
    return pl.pallas_call(
        functools.partial(_gdn_pairs_kernel, group=group),
        grid=(bsz, seq // tb),
        in_specs=[blk(w3), blk(hw), blk(LANES), blk(LANES), per_b3((N_HEADS, HEAD_DIM, HEAD_DIM)),
                  _resident(norm_w_row.shape), _resident(ones_bd.shape)],
        out_specs=[blk(hw), per_b3((N_HEADS, HEAD_DIM, HEAD_DIM))],
        out_shape=[jax.ShapeDtypeStruct((bsz, seq, hw), F32),
                   jax.ShapeDtypeStruct((bsz, N_HEADS, HEAD_DIM, HEAD_DIM), F32)],
        scratch_shapes=[pltpu.VMEM((1, tb, hw), F32)]
                       + [pltpu.VMEM((1, N_HEADS // 2, HEAD_DIM, LANES), F32)]
                       + [pltpu.VMEM((1, tb, hw), F32)] * 5
                       + [pltpu.VMEM((1, (tb // chunk) * SUBLANES, hw), F32)],
        compiler_params=pltpu.CompilerParams(dimension_semantics=("parallel", "arbitrary"),
                                             vmem_limit_bytes=VMEM_LIMIT),
        name="gdn_pairs",
    )(qkv, z, beta, g, s0, norm_w_row, ones_bd)


def _gdn_step_kernel(xq_ref, xk_ref, xv_ref, cq_ref, ck_ref, cv_ref, wq_ref, wk_ref, wv_ref, z_ref, ba_ref, hp_ref,
                     nw_ref, s0_ref, o_ref, s_ref, k_scr, q_scr):
    n_tok, hd, _ = xq_ref.shape
    h = pl.program_id(0)
    col_sum = lambda x: jnp.sum(x, axis=0, keepdims=True)

    def conv_act(x_ref, c_ref, w_ref, t):
        planes = [c_ref[i] for i in range(CONV_WIDTH - 1)] + [x_ref[i] for i in range(t + 1)]
        taps = planes[t:t + CONV_WIDTH]
        acc = jnp.zeros(taps[0].shape, F32)
        for j in range(CONV_WIDTH):
            acc = acc + taps[j] * w_ref[j]
        return jax.nn.silu(acc)

    s_ref[...] = s0_ref[...]
    neg_rate = -jnp.exp(hp_ref[0, pl.ds(h, 1), :])
    dt_bias = hp_ref[1, pl.ds(h, 1), :]
    for t in range(n_tok):
        q = conv_act(xq_ref, cq_ref, wq_ref, t)
        k = conv_act(xk_ref, ck_ref, wk_ref, t)
        v = conv_act(xv_ref, cv_ref, wv_ref, t)
        q_scr[...] = q * lax.rsqrt(col_sum(q * q) + RMS_EPS) * hd ** -0.5
        k_scr[...] = k * lax.rsqrt(col_sum(k * k) + RMS_EPS)
        beta = jax.nn.sigmoid(ba_ref[t, pl.ds(h, 1), :])
        decay = jnp.exp(neg_rate * jax.nn.softplus(ba_ref[t, pl.ds(N_HEADS + h, 1), :] + dt_bias))

        def k_dot_state(d, acc):
            return acc + k_scr[pl.ds(d, 1), :] * s_ref[0, d]

        ks = lax.fori_loop(0, hd, k_dot_state, jnp.zeros((hd, xq_ref.shape[2]), F32), unroll=8)
        delta = beta * (v - decay * ks)

        def update(d, acc):
            s_new = decay * s_ref[0, d] + k_scr[pl.ds(d, 1), :] * delta
            s_ref[0, d] = s_new
            return acc + q_scr[pl.ds(d, 1), :] * s_new

        o = lax.fori_loop(0, hd, update, jnp.zeros((hd, xq_ref.shape[2]), F32), unroll=8)
        o = o * lax.rsqrt(col_sum(o * o) * (1.0 / hd) + RMS_EPS) * nw_ref[...]
        o_ref[t] = o * jax.nn.silu(z_ref[t])


def _gdn_step_call(x_t, conv_t, conv_w_t, z_t, ba_t, hp_t, nw_t, s0_t):
    n_tok, w3, bsz = x_t.shape
    hd = HEAD_DIM
    part = lambda rows, p: pl.BlockSpec((rows, hd, bsz), lambda h, p=p: (0, p * N_HEADS + h, 0))
    whole = lambda a: pl.BlockSpec(a.shape, lambda h: (0,) * a.ndim)
    state_spec = pl.BlockSpec((1, hd, hd, bsz), lambda h: (h, 0, 0, 0))
    return pl.pallas_call(
        _gdn_step_kernel,
        grid=(N_HEADS,),
        in_specs=[part(n_tok, 0), part(n_tok, 1), part(n_tok, 2),
                  part(CONV_WIDTH - 1, 0), part(CONV_WIDTH - 1, 1), part(CONV_WIDTH - 1, 2),
                  part(CONV_WIDTH, 0), part(CONV_WIDTH, 1), part(CONV_WIDTH, 2),
                  part(n_tok, 0), whole(ba_t), whole(hp_t), whole(nw_t), state_spec],
        out_specs=[part(n_tok, 0), state_spec],
        out_shape=[jax.ShapeDtypeStruct((n_tok, N_HEADS * hd, bsz), F32),
                   jax.ShapeDtypeStruct(s0_t.shape, F32)],
        scratch_shapes=[pltpu.VMEM((hd, bsz), F32)] * 2,
        compiler_params=pltpu.CompilerParams(dimension_semantics=("parallel",), vmem_limit_bytes=VMEM_LIMIT),
        name="gdn_step",
    )(x_t, x_t, x_t, conv_t, conv_t, conv_t, conv_w_t, conv_w_t, conv_w_t, z_t, ba_t, hp_t, nw_t, s0_t)


def _pad_rows(x, rows, front=False):
    extra = rows - x.shape[1]
    cfg = [(0, 0)] * x.ndim
    cfg[1] = (extra, 0) if front else (0, extra)
    return jnp.pad(x, cfg)


def kernel(x_prompt, x_sample, cache_attn_k, cache_attn_v, state_gdn, state_conv, rel_bias, ln1_g, ln1_b,
           ffn1_w_gate, ffn1_w_up, ffn1_w_down, w_in, w_out, gdn_conv_w, gdn_a_log, gdn_dt_bias, gdn_norm_w,
           ln2_g, ln2_b, ffn2_w_gate, ffn2_w_up, ffn2_w_down, ln3_g, ln3_b):
    depth = w_in.shape[0]
    alpha = (2.0 * depth) ** 0.25
    bsz, seq, d_model = x_prompt.shape
    dbsz, dseq, _ = x_sample.shape
    w_buf = cache_attn_k.shape[2]
    hw = HEAD_WIDTH
    in_cols = w_in.shape[2]
    assert in_cols == 7 * hw + 2 * N_HEADS and CONV_WIDTH - 1 <= dseq <= SAMPLE_T_PAD
    assert seq % GDN_TOKEN_BLOCK == 0 and GDN_TOKEN_BLOCK % (GDN_CHUNK * GDN_CHUNK_GROUP) == 0

    sample_lows, sample_buckets = _sample_layout(w_buf, dseq)
    tab_p = _t5_table_call(rel_bias, jnp.asarray(_prompt_buckets()))
    tab_s = _t5_table_call(rel_bias, jnp.asarray(sample_buckets))

    yp = x_prompt.reshape(bsz * seq, d_model)
    ys = x_sample.reshape(dbsz * dseq, d_model)
    collected = [[] for _ in range(8)]
    row = lambda v: v.reshape(1, -1)
    for layer in range(depth):
        wg1, wu1, wd1 = (w[layer].astype(BF16) for w in (ffn1_w_gate, ffn1_w_up, ffn1_w_down))
        wg2, wu2, wd2 = (w[layer].astype(BF16) for w in (ffn2_w_gate, ffn2_w_up, ffn2_w_down))
        win = jnp.pad(w_in[layer], ((0, 0), (0, 7 * hw + LANES - in_cols))).astype(BF16)
        wo = w_out[layer].astype(BF16)
        conv_w8 = jnp.pad(gdn_conv_w[layer], ((0, SUBLANES - CONV_WIDTH), (0, 0)))
        head_params = jnp.zeros((SUBLANES, LANES), F32)
        head_params = head_params.at[0, N_HEADS:2 * N_HEADS].set(gdn_a_log[layer])
        head_params = head_params.at[1, N_HEADS:2 * N_HEADS].set(gdn_dt_bias[layer])
        norm_w_row = jnp.tile(gdn_norm_w[layer], N_HEADS).reshape(1, hw)
        pre = functools.partial(_pre_call, wg=wg1, wu=wu1, wd=wd1, g=row(ln1_g[layer]), b=row(ln1_b[layer]),
                                win=win, alpha=alpha)
        post = functools.partial(_post_call, wo=wo, g2=row(ln2_g[layer]), b2=row(ln2_b[layer]), wg=wg2, wu=wu2,
                                 wd=wd2, g3=row(ln3_g[layer]), b3=row(ln3_b[layer]), alpha=alpha)

        x1, aq, ak, av, ak_t, av_t, gdn_qkv, z, beta, gate, gq_tail = _pre_prompt_call(
            yp, wg1, wu1, wd1, row(ln1_g[layer]), row(ln1_b[layer]), win, conv_w8, head_params, alpha, seq)
        shp = lambda a: a.reshape(bsz, seq, a.shape[1])
        attn = _attn_prompt_call(shp(aq), shp(ak), shp(av), tab_p)
        gdn, s_p = _gdn_pairs_call(shp(gdn_qkv), shp(z), shp(beta), shp(gate),
                                   jnp.zeros((bsz, N_HEADS, HEAD_DIM, HEAD_DIM), F32), norm_w_row,
                                   token_block=GDN_TOKEN_BLOCK, group=GDN_CHUNK_GROUP)
        yp = post(attn.reshape(bsz * seq, hw), gdn.reshape(bsz * seq, hw), x1)
        wp = min(w_buf, seq)
        heads5 = lambda a_t: a_t.reshape(bsz, N_HEADS, HEAD_DIM, seq).transpose(0, 3, 1, 2)[:, seq - wp:]
        st_p = (heads5(ak_t), heads5(av_t), s_p, gq_tail[:, SUBLANES - (CONV_WIDTH - 1):])

        x1s, aqs, aks, avs, gqs, zs, bas = pre(ys)
        shs = lambda a: _pad_rows(a.reshape(dbsz, dseq, a.shape[1]), SAMPLE_T_PAD)
        heads5s = lambda a: a.reshape(dbsz, dseq, N_HEADS, HEAD_DIM)
        new_rows = lambda a: jnp.pad(heads5s(a).transpose(0, 2, 1, 3),
                                     ((0, 0), (0, 0), (0, SAMPLE_T_PAD - dseq), (0, 0)))
        by_head_t = lambda c: c.transpose(0, 2, 3, 1)
        attn_s = _attn_sample_call(shs(aqs), new_rows(aks), new_rows(avs), by_head_t(cache_attn_k[layer]),
                                   by_head_t(cache_attn_v[layer]), tab_s, sample_lows)
        to_lanes = lambda a: a.reshape(dbsz, dseq * a.shape[1]).T.reshape(dseq, a.shape[1], dbsz)
        on_lanes = lambda a: jnp.broadcast_to(a[..., None], a.shape + (dbsz,))
        gdn_t, s_t = _gdn_step_call(
            to_lanes(gqs), state_conv[layer].transpose(1, 2, 0), on_lanes(gdn_conv_w[layer]), to_lanes(zs),
            to_lanes(bas[:, :2 * N_HEADS]), on_lanes(jnp.stack([gdn_a_log[layer], gdn_dt_bias[layer]])),
            on_lanes(gdn_norm_w[layer]), state_gdn[layer].transpose(1, 2, 3, 0))
        gdn_s = gdn_t.reshape(dseq * hw, dbsz).T.reshape(dbsz * dseq, hw)
        s_s = s_t.transpose(3, 0, 1, 2)
        ys = post(attn_s[:, :dseq].reshape(dbsz * dseq, hw), gdn_s, x1s)
        st_s = (heads5s(aks), heads5s(avs), s_s,
                gqs.reshape(dbsz, dseq, 3 * hw)[:, dseq - (CONV_WIDTH - 1):])
        for lst, st in zip(collected, st_p + st_s):
            lst.append(st)
    outs = [jnp.stack(t, axis=0) for t in collected]
    return (yp.reshape(bsz, seq, d_model), ys.reshape(dbsz, dseq, d_model)) + tuple(outs)
```

```python
import functools
import math

import numpy as np
import jax
import jax.numpy as jnp
from jax import lax
from jax.experimental import pallas as pl
from jax.experimental.pallas import tpu as pltpu

F32 = jnp.float32
BF16 = jnp.bfloat16

HEAD_DIM = 64
N_HEADS = 8
HEAD_WIDTH = N_HEADS * HEAD_DIM
DILATED_PATTERNS = ((128, 1), (512, 4), (2048, 16))
KEYS_PER_BLOCK = 128
T5_BUCKETS = 32
T5_MAX_EXACT = 16
T5_MAX_DIST = 2048
CONV_WIDTH = 4
GDN_CHUNK = 64
LN_EPS = 1e-5
RMS_EPS = 1e-6
NEG_BIG = -1e30

LANES = 128
SUBLANES = 8
TOKEN_TILE = 256
POST_TOKEN_TILE = 512
GDN_TOKEN_BLOCK = 256
ATTN_SKEW_GROUPS = 1
GDN_CHUNK_GROUP = 4
GDN_BATCH_BLOCK = 4
GDN_SAMPLE_BATCH_BLOCK = 8
SAMPLE_T_PAD = 8
VMEM_LIMIT = 56 * 1024 * 1024


def _resident(shape):
    nd = len(shape)
    return pl.BlockSpec(shape, lambda *_: (0,) * nd, pipeline_mode=pl.Buffered(1))


def _layernorm(y, g, b):
    mu = jnp.mean(y, axis=-1, keepdims=True)
    yc = y - mu
    var = jnp.mean(yc * yc, axis=-1, keepdims=True)
    return yc * lax.rsqrt(var + LN_EPS) * g + b


def _swiglu(xb, wg_ref, wu_ref, wd_ref, h_scr):
    d_ff = wg_ref.shape[1]
    step = 2 * LANES
    assert d_ff % step == 0
    for c in range(d_ff // step):
        sl = slice(c * step, (c + 1) * step)
        gate = jnp.dot(xb, wg_ref[:, sl], preferred_element_type=F32)
        up = jnp.dot(xb, wu_ref[:, sl], preferred_element_type=F32)
        h_scr[:, sl] = (jax.nn.silu(gate) * up).astype(BF16)
    return jnp.dot(h_scr[...], wd_ref[...], preferred_element_type=F32)


PROJ_WIDTHS = (HEAD_WIDTH, HEAD_WIDTH, HEAD_WIDTH, 3 * HEAD_WIDTH, HEAD_WIDTH, LANES)
PROJ_EDGES = tuple(int(e) for e in np.cumsum((0,) + PROJ_WIDTHS))


def _first_half_step(x_ref, wg_ref, wu_ref, wd_ref, g_ref, b_ref, h_scr, alpha):
    x = x_ref[...]
    ff = _swiglu(x.astype(BF16), wg_ref, wu_ref, wd_ref, h_scr)
    return _layernorm(alpha * x + 0.5 * ff, g_ref[...], b_ref[...])


def _pre_kernel(x_ref, wg_ref, wu_ref, wd_ref, g_ref, b_ref, win_ref,
                x1_ref, aq_ref, ak_ref, av_ref, gq_ref, z_ref, ba_ref, h_scr, *, alpha):
    x1 = _first_half_step(x_ref, wg_ref, wu_ref, wd_ref, g_ref, b_ref, h_scr, alpha)
    x1_ref[...] = x1
    xb = x1.astype(BF16)
    for i, ref in enumerate((aq_ref, ak_ref, av_ref, gq_ref, z_ref, ba_ref)):
        ref[...] = jnp.dot(xb, win_ref[:, PROJ_EDGES[i]:PROJ_EDGES[i + 1]], preferred_element_type=F32)


def _gdn_gates(ba, hp_ref):
    beta = jax.nn.sigmoid(ba)
    g = -jnp.exp(hp_ref[0:1, :]) * jax.nn.softplus(ba + hp_ref[1:2, :])
    return beta, g


def _causal_conv_silu(ext_scr, x, cw_ref):
    tb = x.shape[0]
    hist = SUBLANES
    ext_scr[hist:hist + tb, :] = x
    ext = ext_scr[0:hist + tb, :]
    conv = jnp.zeros(x.shape, F32)
    for j in range(CONV_WIDTH):
        back = CONV_WIDTH - 1 - j
        rows = pltpu.roll(ext, back, axis=0)[hist:hist + tb, :] if back else x
        conv = conv + rows * cw_ref[j:j + 1, :]
    ext_scr[0:hist, :] = ext_scr[tb:tb + hist, :]
    return jax.nn.silu(conv)


def _pre_prompt_kernel(x_ref, wg_ref, wu_ref, wd_ref, g_ref, b_ref, win_ref, cw_ref, hp_ref, ones_ref,
                       x1_ref, aq_ref, ak_ref, av_ref, akt_ref, avt_ref, qkv_ref, z_ref, bt_ref, gg_ref, tail_ref,
                       h_scr, ext_scr, *, alpha, tiles_per_seq):
    x1 = _first_half_step(x_ref, wg_ref, wu_ref, wd_ref, g_ref, b_ref, h_scr, alpha)
    x1_ref[...] = x1
    xb = x1.astype(BF16)
    proj = lambda i: jnp.dot(xb, win_ref[:, PROJ_EDGES[i]:PROJ_EDGES[i + 1]], preferred_element_type=F32)

    @pl.when(pl.program_id(0) % tiles_per_seq == 0)
    def _():
        ext_scr[0:SUBLANES, :] = jnp.zeros((SUBLANES, ext_scr.shape[1]), F32)

    raw = proj(3)
    tail_ref[0] = raw[raw.shape[0] - SUBLANES:, :]
    act = _causal_conv_silu(ext_scr, raw, cw_ref)
    hw = HEAD_WIDTH
    head_sq = lambda x: jnp.dot((x * x).astype(BF16), ones_ref[...], preferred_element_type=F32)
    q = act[:, 0:hw]
    k = act[:, hw:2 * hw]
    qkv_ref[:, 0:hw] = q * lax.rsqrt(head_sq(q) + RMS_EPS) * HEAD_DIM ** -0.5
    qkv_ref[:, hw:2 * hw] = k * lax.rsqrt(head_sq(k) + RMS_EPS)
    qkv_ref[:, 2 * hw:3 * hw] = act[:, 2 * hw:3 * hw]
    bt_ref[...], gg_ref[...] = _gdn_gates(proj(5), hp_ref)
    aq_ref[...] = proj(0)
    for i, ref, ref_t in ((1, ak_ref, akt_ref), (2, av_ref, avt_ref)):
        kv = proj(i)
        ref[...] = kv
        ref_t[0] = kv.T
    z_ref[...] = proj(4)


def _pre_call(x2d, wg, wu, wd, g, b, win, alpha):
    m, d = x2d.shape
    d_ff = wg.shape[1]
    tm = TOKEN_TILE
    assert m % tm == 0 and sum(PROJ_WIDTHS) == win.shape[1]
    tile = lambda w: pl.BlockSpec((tm, w), lambda i: (i, 0))
    return pl.pallas_call(
        functools.partial(_pre_kernel, alpha=alpha),
        grid=(m // tm,),
        in_specs=[tile(d), _resident(wg.shape), _resident(wu.shape), _resident(wd.shape),
                  _resident(g.shape), _resident(b.shape), _resident(win.shape)],
        out_specs=[tile(d)] + [tile(w) for w in PROJ_WIDTHS],
        out_shape=[jax.ShapeDtypeStruct((m, d), F32)] + [jax.ShapeDtypeStruct((m, w), F32) for w in PROJ_WIDTHS],
        scratch_shapes=[pltpu.VMEM((tm, d_ff), BF16)],
        compiler_params=pltpu.CompilerParams(dimension_semantics=("parallel",), vmem_limit_bytes=VMEM_LIMIT),
        name="pre_ffn_proj",
    )(x2d, wg, wu, wd, g, b, win)


def _pre_prompt_call(x2d, wg, wu, wd, g, b, win, conv_w8, head_params, alpha, seq):
    m, d = x2d.shape
    d_ff = wg.shape[1]
    tm = TOKEN_TILE
    hw = HEAD_WIDTH
    assert seq % tm == 0 and m % seq == 0 and sum(PROJ_WIDTHS) == win.shape[1]
    per_seq = seq // tm
    tile = lambda w: pl.BlockSpec((tm, w), lambda i: (i, 0))
    t_spec = pl.BlockSpec((1, hw, tm), lambda i: (i // per_seq, 0, i % per_seq))
    t_shape = jax.ShapeDtypeStruct((m // seq, hw, seq), F32)
    rows = lambda w: jax.ShapeDtypeStruct((m, w), F32)
    consts = (wg, wu, wd, g, b, win, conv_w8, head_params, _head_ones())
    return pl.pallas_call(
        functools.partial(_pre_prompt_kernel, alpha=alpha, tiles_per_seq=per_seq),
        grid=(m // tm,),
        in_specs=[tile(d)] + [_resident(c.shape) for c in consts],
        out_specs=[tile(d), tile(hw), tile(hw), tile(hw), t_spec, t_spec, tile(3 * hw), tile(hw), tile(LANES),
                   tile(LANES), pl.BlockSpec((1, SUBLANES, 3 * hw), lambda i: (i // per_seq, 0, 0))],
        out_shape=[rows(d), rows(hw), rows(hw), rows(hw), t_shape, t_shape, rows(3 * hw), rows(hw), rows(LANES),
                   rows(LANES), jax.ShapeDtypeStruct((m // seq, SUBLANES, 3 * hw), F32)],
        scratch_shapes=[pltpu.VMEM((tm, d_ff), BF16), pltpu.VMEM((tm + 2 * SUBLANES, 3 * hw), F32)],
        compiler_params=pltpu.CompilerParams(dimension_semantics=("arbitrary",), vmem_limit_bytes=VMEM_LIMIT),
        name="pre_ffn_proj_gdnprep",
    )(x2d, *consts)


def _post_kernel(attn_ref, gdn_ref, x1_ref, wo_ref, g2_ref, b2_ref, wg_ref, wu_ref, wd_ref, g3_ref, b3_ref,
                 y_ref, h_scr, *, alpha):
    mix = jnp.dot(attn_ref[...].astype(BF16), wo_ref[0:HEAD_WIDTH, :], preferred_element_type=F32)
    mix = mix + jnp.dot(gdn_ref[...].astype(BF16), wo_ref[HEAD_WIDTH:2 * HEAD_WIDTH, :],
                        preferred_element_type=F32)
    x2 = _layernorm(alpha * x1_ref[...] + mix, g2_ref[...], b2_ref[...])
    ff = _swiglu(x2.astype(BF16), wg_ref, wu_ref, wd_ref, h_scr)
    y_ref[...] = _layernorm(alpha * x2 + 0.5 * ff, g3_ref[...], b3_ref[...])


def _post_call(attn, gdn, x1, wo, g2, b2, wg, wu, wd, g3, b3, alpha):
    m, d = x1.shape
    tm = POST_TOKEN_TILE
    assert m % tm == 0
    tile = lambda w: pl.BlockSpec((tm, w), lambda i: (i, 0))
    consts = (wo, g2, b2, wg, wu, wd, g3, b3)
    return pl.pallas_call(
        functools.partial(_post_kernel, alpha=alpha),
        grid=(m // tm,),
        in_specs=[tile(HEAD_WIDTH), tile(HEAD_WIDTH), tile(d)] + [_resident(c.shape) for c in consts],
        out_specs=tile(d),
        out_shape=jax.ShapeDtypeStruct((m, d), F32),
        scratch_shapes=[pltpu.VMEM((tm, wg.shape[1]), BF16)],
        compiler_params=pltpu.CompilerParams(dimension_semantics=("parallel",), vmem_limit_bytes=VMEM_LIMIT),
        name="post_out_ffn",
    )(attn, gdn, x1, *consts)


def _t5_bucket_np(dist):
    n = np.maximum(dist, 0)
    nf = np.maximum(n, 1).astype(np.float32)
    large = T5_MAX_EXACT + (np.log(nf / np.float32(T5_MAX_EXACT)) / np.float32(math.log(T5_MAX_DIST / T5_MAX_EXACT))
                            * np.float32(T5_BUCKETS - T5_MAX_EXACT)).astype(np.int32)
    large = np.minimum(large, T5_BUCKETS - 1)
    return np.where(n < T5_MAX_EXACT, n, large).astype(np.int32)


def _t5_table_kernel(rb_ref, bk_ref, out_ref):
    for h in range(N_HEADS):
        bk = bk_ref[0, h % bk_ref.shape[1]]
        acc = jnp.full(bk.shape, NEG_BIG, F32)
        for b in range(T5_BUCKETS):
            acc = jnp.where(bk == b, rb_ref[b, h], acc)
        out_ref[0, h] = acc


def _t5_table_call(rel_bias, buckets):
    p, hb, r, c = buckets.shape
    assert hb in (1, N_HEADS)
    return pl.pallas_call(
        _t5_table_kernel,
        grid=(p,),
        in_specs=[pl.BlockSpec(memory_space=pltpu.SMEM),
                  pl.BlockSpec((1, hb, r, c), lambda i: (i, 0, 0, 0))],
        out_specs=pl.BlockSpec((1, N_HEADS, r, c), lambda i: (i, 0, 0, 0)),
        out_shape=jax.ShapeDtypeStruct((p, N_HEADS, r, c), F32),
        compiler_params=pltpu.CompilerParams(dimension_semantics=("parallel",)),
        name="t5_bias_table",
    )(rel_bias, buckets)


def _prompt_buckets():
    nb = KEYS_PER_BLOCK
    qi = np.arange(nb)[:, None]
    ki = np.arange(2 * nb)[None, :]
    dist = qi + nb - ki
    valid = (dist >= 0) & (dist <= nb)
    out = []
    for window, dil in DILATED_PATTERNS:
        assert window // dil == nb
        out.append(np.where(valid, _t5_bucket_np(dist * dil), -1))
    return np.stack(out).astype(np.int32)[:, None]


def _nt_dot(a, b):
    return lax.dot_general(a, b, (((1,), (1,)), ((), ())), preferred_element_type=F32)


def _emit_skewed(stages, items, n_groups):
    n_groups = max(1, min(n_groups, len(items)))
    groups = [items[i::n_groups] for i in range(n_groups)]
    for step in range(len(stages) + n_groups - 1):
        for gi, group in enumerate(groups):
            s = step - gi
            if 0 <= s < len(stages):
                for item in group:
                    stages[s](item)


def _attn_prompt_kernel(q_ref, k_ref, v_ref, tab_ref, o_ref, og_scr, lg_scr, *, seq):
    nb = KEYS_PER_BLOCK
    hd = HEAD_DIM
    n_pat = len(DILATED_PATTERNS)
    heads = LANES // hd
    for g, (window, dil) in enumerate(DILATED_PATTERNS):
        n_blocks = seq // (dil * nb)

        def blocks(starts, with_prev, g=g, dil=dil):
            lane_head = lax.broadcasted_iota(jnp.int32, (nb, LANES), 1) // hd
            n_keys = 2 * nb if with_prev else nb
            row_sl, qs, ks, vs = [], [], [], []
            for start in starts:
                first_key = start - dil * (n_keys - nb)
                row_sl.append(pl.ds(start, nb, stride=dil) if dil > 1 else pl.ds(start, nb))
                key_rows = pl.ds(first_key, n_keys, stride=dil) if dil > 1 else pl.ds(first_key, n_keys)
                qs.append(q_ref[0, row_sl[-1], :] * hd ** -0.5)
                ks.append(k_ref[0, key_rows, :].astype(BF16))
                vs.append(v_ref[0, key_rows, :].astype(BF16))
            chains = [(u, hh) for u in range(len(starts)) for hh in range(heads)]
            s, m, p, den, acc = {}, {}, {}, {}, {}

            def scores(c):
                u, hh = c
                qb = jnp.where(lane_head == hh, qs[u], 0.0).astype(BF16)
                s[c] = _nt_dot(qb, ks[u]) + tab_ref[g, hh, :, 2 * nb - n_keys:2 * nb]

            def row_max(c):
                m[c] = jnp.max(s[c], axis=-1, keepdims=True)

            def probs(c):
                p[c] = jnp.exp(s[c] - m[c])
                den[c] = jnp.sum(p[c], axis=-1, keepdims=True)

            def values(c):
                u, hh = c
                acc[c] = jnp.dot(p[c].astype(BF16), vs[u], preferred_element_type=F32)

            def finish(c):
                acc[c] = acc[c] / den[c]
                den[c] = m[c] + jnp.log(den[c])

            _emit_skewed((scores, row_max, probs, values, finish), chains, ATTN_SKEW_GROUPS)
            for u in range(len(starts)):
                out, lse = acc[u, 0], den[u, 0]
                for hh in range(1, heads):
                    out = jnp.where(lane_head == hh, acc[u, hh], out)
                    lse = jnp.where(lane_head == hh, den[u, hh], lse)
                og_scr[g, row_sl[u], :] = out
                lg_scr[g, row_sl[u], :] = jnp.broadcast_to(lse, (nb, LANES))

        def unroll_of(count):
            return next(u for u in (8, 6, 5, 4, 3, 2, 1) if count % u == 0)

        u_first = unroll_of(dil)

        def first_body(i, carry, blocks=blocks, u_first=u_first):
            blocks([i * u_first + j for j in range(u_first)], False)
            return carry

        lax.fori_loop(0, dil // u_first, first_body, 0)
        if n_blocks > 1:
            n_rest = dil * (n_blocks - 1)
            u_rest = unroll_of(n_rest)

            def rest_body(i, carry, blocks=blocks, dil=dil, n_blocks=n_blocks, u_rest=u_rest):
                starts = []
                for j in range(u_rest):
                    idx = i * u_rest + j
                    starts.append(idx // (n_blocks - 1) + dil * nb * (idx % (n_blocks - 1) + 1))
                blocks(starts, True)
                return carry

            lax.fori_loop(0, n_rest // u_rest, rest_body, 0)

    rb = 2 * nb

    def merge_body(i, carry):
        rows = pl.ds(pl.multiple_of(i * rb, rb), rb)
        lses = [lg_scr[g, rows, :] for g in range(n_pat)]
        top = functools.reduce(jnp.maximum, lses)
        ws = [jnp.exp(l - top) for l in lses]
        num = functools.reduce(lambda a, b: a + b, [w * og_scr[g, rows, :] for g, w in enumerate(ws)])
        o_ref[0, rows, :] = num / functools.reduce(lambda a, b: a + b, ws)
        return carry

    lax.fori_loop(0, seq // rb, merge_body, 0)


def _attn_prompt_call(aq, ak, av, tab):
    bsz, seq, width = aq.shape
    assert width == HEAD_WIDTH and seq % (DILATED_PATTERNS[-1][1] * KEYS_PER_BLOCK) == 0
    n_pat = len(DILATED_PATTERNS)
    heads_per_step = LANES // HEAD_DIM
    qkv_spec = pl.BlockSpec((1, seq, LANES), lambda h, b: (b, 0, h))
    return pl.pallas_call(
        functools.partial(_attn_prompt_kernel, seq=seq),
        grid=(N_HEADS // heads_per_step, bsz),
        in_specs=[qkv_spec, qkv_spec, qkv_spec,
                  pl.BlockSpec((n_pat, heads_per_step, KEYS_PER_BLOCK, 2 * KEYS_PER_BLOCK),
                               lambda h, b: (0, h, 0, 0))],
        out_specs=pl.BlockSpec((1, seq, LANES), lambda h, b: (b, 0, h)),
        out_shape=jax.ShapeDtypeStruct((bsz, seq, width), F32),
        scratch_shapes=[pltpu.VMEM((n_pat, seq, LANES), F32)] * 2,
        compiler_params=pltpu.CompilerParams(dimension_semantics=("parallel", "parallel"),
                                             vmem_limit_bytes=VMEM_LIMIT),
        name="attn_prompt",
    )(aq, ak, av, tab)


def _sample_layout(w_buf, t_new):
    assert w_buf % LANES == 0 and t_new <= SAMPLE_T_PAD
    row_of_col = np.full((w_buf + LANES,), -1, np.int64)
    row_of_col[:w_buf + t_new] = np.arange(w_buf + t_new)
    buckets = np.full((len(DILATED_PATTERNS), 1, SAMPLE_T_PAD, w_buf + LANES), -1, np.int32)
    lows = []
    for g, (window, dil) in enumerate(DILATED_PATTERNS):
        lows.append(max(0, w_buf - window) // LANES * LANES)
        for t in range(t_new):
            dist = w_buf + t - row_of_col
            ok = (row_of_col >= 0) & (dist >= 0) & (dist <= window) & (dist % dil == 0)
            assert int(ok.sum()) == min(window, w_buf + t) // dil + 1 and not ok[:lows[g]].any()
            buckets[g, 0, t] = np.where(ok, _t5_bucket_np(dist), -1)
    return tuple(lows), buckets


def _attn_sample_kernel(q_ref, kn_ref, vn_ref, kt_ref, vt_ref, tab_ref, o_ref, *, lows):
    tp = SAMPLE_T_PAD
    hd = HEAD_DIM
    w_buf = kt_ref.shape[3]
    n_pat = len(DILATED_PATTERNS)
    add = lambda x, y: x + y
    rowmax = lambda x: jnp.max(x, axis=-1, keepdims=True)
    rowsum = lambda x: jnp.sum(x, axis=-1, keepdims=True)
    heads = range(N_HEADS)
    q8 = q_ref[0] * hd ** -0.5
    s_buf = {h: jnp.dot(q8[:, hd * h:hd * (h + 1)], kt_ref[0, h], preferred_element_type=F32) for h in heads}
    s_new = {h: _nt_dot(q8[:, hd * h:hd * (h + 1)], kn_ref[0, h]) for h in heads}
    p_buf, p_new, den, lse = {}, {}, {}, {}
    for h in heads:
        for g in range(n_pat):
            lo = lows[g]
            l_buf = s_buf[h][:, lo:] + tab_ref[g, h, :, lo:w_buf]
            l_new = s_new[h] + tab_ref[g, h, :, w_buf:w_buf + tp]
            m = jnp.maximum(rowmax(l_buf), rowmax(l_new))
            p_buf[h, g] = jnp.exp(l_buf - m)
            p_new[h, g] = jnp.exp(l_new - m)
            den[h, g] = rowsum(p_buf[h, g]) + rowsum(p_new[h, g])
            lse[h, g] = m + jnp.log(den[h, g])
    edges = sorted(set(lows)) + [w_buf]
    outs = {}
    for h in heads:
        top = functools.reduce(jnp.maximum, [lse[h, g] for g in range(n_pat)])
        ws = [jnp.exp(lse[h, g] - top) for g in range(n_pat)]
        w_sum = functools.reduce(add, ws)
        coef = [ws[g] / (w_sum * den[h, g]) for g in range(n_pat)]
        acc = jnp.dot(functools.reduce(add, [coef[g] * p_new[h, g] for g in range(n_pat)]), vn_ref[0, h],
                      preferred_element_type=F32)
        for e0, e1 in zip(edges[:-1], edges[1:]):
            mix = functools.reduce(add, [coef[g] * p_buf[h, g][:, e0 - lows[g]:e1 - lows[g]]
                                         for g in range(n_pat) if lows[g] <= e0])
            acc = acc + _nt_dot(mix, vt_ref[0, h, :, e0:e1])
        outs[h] = acc
    o_ref[0] = jnp.concatenate([outs[h] for h in heads], axis=1)


def _attn_sample_call(aq8, k_new, v_new, cache_kt, cache_vt, tab, lows):
    bsz, n_heads, hd, w_buf = cache_kt.shape
    q_spec = pl.BlockSpec((1, SAMPLE_T_PAD, n_heads * hd), lambda b: (b, 0, 0))
    new_spec = pl.BlockSpec((1, n_heads, SAMPLE_T_PAD, hd), lambda b: (b, 0, 0, 0))
    buf_spec = pl.BlockSpec((1, n_heads, hd, w_buf), lambda b: (b, 0, 0, 0))
    return pl.pallas_call(
        functools.partial(_attn_sample_kernel, lows=lows),
        grid=(bsz,),
        in_specs=[q_spec, new_spec, new_spec, buf_spec, buf_spec, _resident(tab.shape)],
        out_specs=q_spec,
        out_shape=jax.ShapeDtypeStruct((bsz, SAMPLE_T_PAD, n_heads * hd), F32),
        compiler_params=pltpu.CompilerParams(dimension_semantics=("parallel",), vmem_limit_bytes=VMEM_LIMIT),
        name="attn_sample",
    )(aq8, k_new, v_new, cache_kt, cache_vt, tab)


def _aligned(index, size):
    return index * size if isinstance(index, int) else pl.multiple_of(index * size, size)


def _split3(x):
    h1 = x.astype(BF16)
    r1 = x - h1.astype(F32)
    h2 = r1.astype(BF16)
    h3 = (r1 - h2.astype(F32)).astype(BF16)
    return h1, h2, h3


def _head_sums(x, ones_ref):
    h1, h2, _ = _split3(x)
    return (jnp.dot(h1, ones_ref[...], preferred_element_type=F32)
            + jnp.dot(h2, ones_ref[...], preferred_element_type=F32))


def _store_normed_qk(act, q_dst, k_dst, ones_ref):
    hw = HEAD_WIDTH
    q = act[:, 0:hw]
    k = act[:, hw:2 * hw]
    q_dst[...] = q * lax.rsqrt(_head_sums(q * q, ones_ref) + RMS_EPS) * HEAD_DIM ** -0.5
    k_dst[...] = k * lax.rsqrt(_head_sums(k * k, ones_ref) + RMS_EPS)


def _gdn_kernel(x_ref, z_ref, ba_ref, cb_ref, s0_ref, cw_ref, hp_ref, nw_ref, ones_ref,
                o_ref, sfin_ref,
                ext_scr, q_scr, k_scr, v_scr, bt_scr, g_scr, *core_scr, chunk, n_valid, group):
    n_bb, tb = x_ref.shape[0], x_ref.shape[1]
    t = pl.program_id(1)

    @pl.when(t == 0)
    def _():
        ext_scr[:, 0:SUBLANES, :] = cb_ref[...]

    for bb in range(n_bb):
        act = _causal_conv_silu(ext_scr.at[bb], x_ref[bb], cw_ref)
        _store_normed_qk(act, q_scr.at[bb], k_scr.at[bb], ones_ref)
        v_scr[bb] = act[:, 2 * HEAD_WIDTH:3 * HEAD_WIDTH]
        ba = ba_ref[bb]
        live = t * tb + lax.broadcasted_iota(jnp.int32, ba.shape, 0) < n_valid
        beta, g = _gdn_gates(ba, hp_ref)
        bt_scr[bb] = jnp.where(live, beta, 0.0)
        g_scr[bb] = jnp.where(live, g, 0.0)
    _gdn_core(s0_ref, z_ref, nw_ref, ones_ref, o_ref, sfin_ref, q_scr, k_scr, v_scr, 0, bt_scr, g_scr,
              *core_scr, chunk=chunk, group=group)


def _gdn_pairs_kernel(qkv_ref, z_ref, bt_scr, g_scr, s0_ref, nw_ref, ones_ref, o_ref, sfin_ref,
                      oc_scr, s_scr, u_scr, w_scr, in_scr, qd_scr, kd_scr, el_scr, *, group):
    chunk = HEAD_DIM
    k_lane0, v_lane0 = HEAD_WIDTH, 2 * HEAD_WIDTH
    n_bb, tb = bt_scr.shape[0], bt_scr.shape[1]
    n_chunks = tb // chunk
    n_pairs = N_HEADS // 2
    hd = HEAD_DIM
    t = pl.program_id(1)
    dot = functools.partial(jnp.dot, preferred_element_type=F32)

    lane = lax.broadcasted_iota(jnp.int32, (chunk, LANES), 1)
    row = lax.broadcasted_iota(jnp.int32, (chunk, LANES), 0)
    second = lane >= hd
    col = lane - jnp.where(second, hd, 0)
    tril = row >= col
    strict = row > col
    eye2 = (row == col).astype(F32)
    tril_b = (lax.broadcasted_iota(jnp.int32, (chunk, chunk), 0)
              >= lax.broadcasted_iota(jnp.int32, (chunk, chunk), 1)).astype(BF16)
    n_doublings = int(math.log2(chunk)) - 1

    def blockdiag(x):
        return jnp.concatenate([jnp.where(second, 0.0, x), jnp.where(second, x, 0.0)], axis=0)

    def per_head(cols, j):
        return jnp.where(second, cols[:, 2 * j + 1:2 * j + 2], cols[:, 2 * j:2 * j + 1])

    @pl.when(t == 0)
    def _():
        for bb in range(n_bb):
            for j in range(n_pairs):
                s_scr[bb, j] = jnp.concatenate([s0_ref[bb, 2 * j], s0_ref[bb, 2 * j + 1]], axis=1)

    def state_free_part(gi, carry):
        probs, shared = [], {}
        for bb in range(n_bb):
            for cg in range(group):
                c_idx = gi * group + cg
                rows = pl.ds(_aligned(c_idx, chunk), chunk)
                gc = functools.reduce(lambda a, b: a + b,
                                      [dot(tril_b, p) for p in _split3(g_scr[bb, rows, :])])
                shared[bb, cg] = (rows, c_idx, gc[:, N_HEADS:2 * N_HEADS], gc.T[N_HEADS:2 * N_HEADS, :],
                                  bt_scr[bb, rows, :])
                probs += [(bb, cg, j) for j in range(n_pairs)]
        kk, qk, decay, a, t_inv, pw, vb, kbg = {}, {}, {}, {}, {}, {}, {}, {}
        for p in probs:
            bb, cg, j = p
            rows, c_idx, gc, gct, bt = shared[bb, cg]
            pl_ = pl.ds(LANES * j, LANES)
            q2 = qkv_ref[bb, rows, pl_]
            k2 = qkv_ref[bb, rows, pl.ds(k_lane0 + LANES * j, LANES)]
            beta2 = per_head(bt, j)
            gcc = per_head(gc, j)
            gcr = jnp.concatenate([gct[2 * j:2 * j + 1, :], gct[2 * j + 1:2 * j + 2, :]], axis=1)
            g_last = gcc[chunk - 1:chunk, :]
            e_gc = jnp.exp(gcc)
            kb2 = k2 * beta2
            k_bd = blockdiag(k2)
            both = _nt_dot(jnp.concatenate([kb2, q2], axis=0), k_bd)
            kk[p] = both[0:chunk]
            qk[p] = both[chunk:2 * chunk]
            vb[p] = qkv_ref[bb, rows, pl.ds(v_lane0 + LANES * j, LANES)] * beta2
            kbg[p] = kb2 * e_gc
            qd_scr[bb, rows, pl_] = q2 * e_gc
            kd_scr[bb, rows, pl_] = k2 * jnp.exp(g_last - gcc)
            el_scr[bb, pl.ds(_aligned(c_idx, SUBLANES), SUBLANES), pl_] = jnp.broadcast_to(
                jnp.exp(g_last), (SUBLANES, LANES))
            decay[p] = jnp.where(tril, jnp.exp(jnp.where(tril, gcc - gcr, 0.0)), 0.0)
        for p in probs:
            bb, cg, j = p
            rows = shared[bb, cg][0]
            a[p] = jnp.where(strict, kk[p] * decay[p], 0.0)
            in_scr[bb, rows, pl.ds(LANES * j, LANES)] = jnp.where(tril, qk[p] * decay[p], 0.0)
            t_inv[p] = eye2 - a[p]
        for p in probs:
            pw[p] = dot(a[p], blockdiag(a[p]))
        for _ in range(n_doublings - 1):
            nxt_pw, nxt_t = {}, {}
            for p in probs:
                nxt_pw[p] = dot(pw[p], blockdiag(pw[p]))
                nxt_t[p] = dot(t_inv[p], blockdiag(eye2 + pw[p]))
            pw, t_inv = nxt_pw, nxt_t
        for p in probs:
            t_inv[p] = dot(t_inv[p], blockdiag(eye2 + pw[p]))
        for p in probs:
            bb, cg, j = p
            rows = shared[bb, cg][0]
            uw = dot(t_inv[p], jnp.concatenate([blockdiag(vb[p]), blockdiag(kbg[p])], axis=1))
            u_scr[bb, rows, pl.ds(LANES * j, LANES)] = uw[:, 0:LANES]
            w_scr[bb, rows, pl.ds(LANES * j, LANES)] = uw[:, LANES:2 * LANES]
        return carry

    def state_part(ci_, carry):
        rows = pl.ds(_aligned(ci_, chunk), chunk)
        el_rows = pl.ds(_aligned(ci_, SUBLANES), SUBLANES)
        probs = [(bb, j) for bb in range(n_bb) for j in range(n_pairs)]
        s_old, both, v_new, o_in, s_add = {}, {}, {}, {}, {}
        for p in probs:
            bb, j = p
            pl_ = pl.ds(LANES * j, LANES)
            s_old[p] = s_scr[bb, j]
            lhs = jnp.concatenate([w_scr[bb, rows, pl_], qd_scr[bb, rows, pl_]], axis=0)
            both[p] = dot(lhs, blockdiag(s_old[p]))
        for p in probs:
            bb, j = p
            pl_ = pl.ds(LANES * j, LANES)
            v_new[p] = u_scr[bb, rows, pl_] - both[p][0:chunk]
            o_in[p] = dot(in_scr[bb, rows, pl_], blockdiag(v_new[p]))
            cross = lax.dot_general(kd_scr[bb, rows, pl_], v_new[p], (((0,), (0,)), ((), ())),
                                    preferred_element_type=F32)
            s_add[p] = jnp.where(second, cross[hd:2 * hd], cross[0:hd])
        for p in probs:
            bb, j = p
            pl_ = pl.ds(LANES * j, LANES)
            oc_scr[bb, rows, pl_] = both[p][chunk:2 * chunk] + o_in[p]
            s_scr[bb, j] = s_old[p] * el_scr[bb, el_rows, pl_][0:1, :] + s_add[p]
        return carry

    assert n_chunks % group == 0
    if n_chunks == group:
        state_free_part(0, 0)
    else:
        lax.fori_loop(0, n_chunks // group, state_free_part, 0)
    if n_chunks == 1:
        state_part(0, 0)
    else:
        lax.fori_loop(0, n_chunks, state_part, 0)

    for bb in range(n_bb):
        o = oc_scr[bb]
        o = o * lax.rsqrt(_head_sums(o * o, ones_ref) * (1.0 / hd) + RMS_EPS) * nw_ref[...]
        o_ref[bb] = o * jax.nn.silu(z_ref[bb])

    @pl.when(t == pl.num_programs(1) - 1)
    def _():
        for bb in range(n_bb):
            for j in range(n_pairs):
                sfin_ref[bb, 2 * j] = s_scr[bb, j][:, 0:hd]
                sfin_ref[bb, 2 * j + 1] = s_scr[bb, j][:, hd:2 * hd]


def _gdn_core(s0_ref, z_ref, nw_ref, ones_ref, o_ref, sfin_ref, q_scr, k_scr, v_scr, v_lane0, bt_scr, g_scr,
              oc_scr, s_scr, uw_scr, in_scr, qd_scr, kd_scr, el_scr, *, chunk, group):
    n_bb, tb = q_scr.shape[0], q_scr.shape[1]
    n_chunks = tb // chunk
    hd = HEAD_DIM
    t = pl.program_id(1)

    @pl.when(t == 0)
    def _():
        s_scr[...] = s0_ref[...]

    ri = lax.broadcasted_iota(jnp.int32, (chunk, chunk), 0)
    ci = lax.broadcasted_iota(jnp.int32, (chunk, chunk), 1)
    tril = ri >= ci
    strict = ri > ci
    eye_c = (ri == ci).astype(F32)
    tril_b = tril.astype(BF16)
    n_doublings = int(math.log2(chunk)) - 1
    assert 2 ** (n_doublings + 1) == chunk

    dot = functools.partial(jnp.dot, preferred_element_type=F32)

    def state_free_part(gi, carry):
        probs, shared = [], {}
        for bb in range(n_bb):
            for cg in range(group):
                c_idx = gi * group + cg
                rows = pl.ds(_aligned(c_idx, chunk), chunk)
                gc = functools.reduce(lambda a, b: a + b,
                                      [dot(tril_b, p) for p in _split3(g_scr[bb, rows, :])])
                gct = gc.T
                g_last = gc[chunk - 1:chunk, :]
                el_rows = pl.ds(_aligned(c_idx, SUBLANES), SUBLANES)
                el_scr[bb, el_rows, :] = jnp.broadcast_to(jnp.exp(g_last), (SUBLANES, LANES))
                shared[bb, cg] = (rows, gc, gct, jnp.exp(gc), jnp.exp(g_last - gc), bt_scr[bb, rows, :])
                probs += [(bb, cg, h) for h in range(N_HEADS)]
        kk, qk, rhs, decay, a, t_inv, pw = {}, {}, {}, {}, {}, {}, {}
        for p in probs:
            bb, cg, h = p
            rows, gc, gct, e_gc, e_rem, bt = shared[bb, cg]
            hl = pl.ds(hd * h, hd)
            gl = N_HEADS + h
            qh = q_scr[bb, rows, hl]
            kh = k_scr[bb, rows, hl]
            beta = bt[:, h:h + 1]
            kb = kh * beta
            kk[p] = _nt_dot(kb, kh)
            qk[p] = _nt_dot(qh, kh)
            v_h = v_scr[bb, rows, pl.ds(v_lane0 + hd * h, hd)]
            rhs[p] = jnp.concatenate([v_h * beta, kb * e_gc[:, gl:gl + 1]], axis=1)
            qd_scr[bb, rows, hl] = qh * e_gc[:, gl:gl + 1]
            kd_scr[bb, rows, hl] = kh * e_rem[:, gl:gl + 1]
            diff = gc[:, gl:gl + 1] - gct[gl:gl + 1, :]
            decay[p] = jnp.where(tril, jnp.exp(jnp.where(tril, diff, 0.0)), 0.0)
        for p in probs:
            bb, cg, h = p
            rows = shared[bb, cg][0]
            a[p] = jnp.where(strict, kk[p] * decay[p], 0.0)
            in_scr[bb, rows, LANES * h:LANES * h + chunk] = jnp.where(tril, qk[p] * decay[p], 0.0)
            t_inv[p] = eye_c - a[p]
        for p in probs:
            pw[p] = dot(a[p], a[p])
        for _ in range(n_doublings - 1):
            nxt_pw, nxt_t = {}, {}
            for p in probs:
                nxt_pw[p] = dot(pw[p], pw[p])
                nxt_t[p] = dot(t_inv[p], eye_c + pw[p])
            pw, t_inv = nxt_pw, nxt_t
        for p in probs:
            t_inv[p] = dot(t_inv[p], eye_c + pw[p])
        for p in probs:
            bb, cg, h = p
            rows = shared[bb, cg][0]
            uw_scr[bb, rows, LANES * h:LANES * (h + 1)] = dot(t_inv[p], rhs[p])
        return carry

    def state_part(ci_, carry):
        rows = pl.ds(_aligned(ci_, chunk), chunk)
        el_rows = pl.ds(_aligned(ci_, SUBLANES), SUBLANES)
        probs = [(bb, h) for bb in range(n_bb) for h in range(N_HEADS)]
        s_old, uw, both, v_new, o_in, s_add = {}, {}, {}, {}, {}, {}
        for p in probs:
            bb, h = p
            s_old[p] = s_scr[bb, h]
            uw[p] = uw_scr[bb, rows, LANES * h:LANES * (h + 1)]
            lhs = jnp.concatenate([uw[p][:, hd:2 * hd], qd_scr[bb, rows, pl.ds(hd * h, hd)]], axis=0)
            both[p] = dot(lhs, s_old[p])
        for p in probs:
            bb, h = p
            v_new[p] = uw[p][:, 0:hd] - both[p][0:chunk]
            o_in[p] = dot(in_scr[bb, rows, LANES * h:LANES * h + chunk], v_new[p])
            s_add[p] = lax.dot_general(kd_scr[bb, rows, pl.ds(hd * h, hd)], v_new[p],
                                       (((0,), (0,)), ((), ())), preferred_element_type=F32)
        for p in probs:
            bb, h = p
            e_last = el_scr[bb, el_rows, :][0:1, N_HEADS + h:N_HEADS + h + 1]
            oc_scr[bb, rows, pl.ds(hd * h, hd)] = both[p][chunk:2 * chunk] + o_in[p]
            s_scr[bb, h] = s_old[p] * e_last + s_add[p]
        return carry

    assert n_chunks % group == 0
    if n_chunks == group:
        state_free_part(0, 0)
    else:
        lax.fori_loop(0, n_chunks // group, state_free_part, 0)
    if n_chunks == 1:
        state_part(0, 0)
    else:
        lax.fori_loop(0, n_chunks, state_part, 0)

    for bb in range(n_bb):
        o = oc_scr[bb]
        o = o * lax.rsqrt(_head_sums(o * o, ones_ref) * (1.0 / hd) + RMS_EPS) * nw_ref[...]
        o_ref[bb] = o * jax.nn.silu(z_ref[bb])

    @pl.when(t == pl.num_programs(1) - 1)
    def _():
        sfin_ref[...] = s_scr[...]


def _gdn_core_scratch(nbb, tb, chunk):
    hw = HEAD_WIDTH
    return ([pltpu.VMEM((nbb, tb, hw), F32)]
            + [pltpu.VMEM((nbb, N_HEADS, HEAD_DIM, HEAD_DIM), F32)]
            + [pltpu.VMEM((nbb, tb, N_HEADS * LANES), F32)] * 2
            + [pltpu.VMEM((nbb, tb, hw), F32)] * 2
            + [pltpu.VMEM((nbb, (tb // chunk) * SUBLANES, LANES), F32)])


def _head_ones():
    ones_bd = np.kron(np.eye(N_HEADS, dtype=np.float32), np.ones((HEAD_DIM, HEAD_DIM), np.float32))
    return jnp.asarray(ones_bd, BF16)


def _gdn_call(gq, z, ba, conv_buf8, s0, conv_w8, head_params, norm_w_row, n_valid, token_block, chunk,
              batch_block, group):
    bsz, seq, w3 = gq.shape
    hw = HEAD_WIDTH
    assert w3 == 3 * hw and seq % token_block == 0 and token_block % chunk == 0 and bsz % batch_block == 0
    tb = token_block
    nbb = batch_block
    ones_bd = _head_ones()
    blk = lambda w: pl.BlockSpec((nbb, tb, w), lambda b, t: (b, t, 0))
    per_b3 = lambda s: pl.BlockSpec((nbb,) + s, lambda b, t: (b,) + (0,) * len(s))
    return pl.pallas_call(
        functools.partial(_gdn_kernel, chunk=chunk, n_valid=n_valid, group=group),
        grid=(bsz // nbb, seq // tb),
        in_specs=[blk(w3), blk(hw), blk(LANES), per_b3((SUBLANES, w3)), per_b3((N_HEADS, HEAD_DIM, HEAD_DIM)),
                  _resident(conv_w8.shape), _resident(head_params.shape), _resident(norm_w_row.shape),
                  _resident(ones_bd.shape)],
        out_specs=[blk(hw), per_b3((N_HEADS, HEAD_DIM, HEAD_DIM))],
        out_shape=[jax.ShapeDtypeStruct((bsz, seq, hw), F32),
                   jax.ShapeDtypeStruct((bsz, N_HEADS, HEAD_DIM, HEAD_DIM), F32)],
        scratch_shapes=[pltpu.VMEM((nbb, tb + 2 * SUBLANES, w3), F32)]
                       + [pltpu.VMEM((nbb, tb, hw), F32)] * 3
                       + [pltpu.VMEM((nbb, tb, LANES), F32)] * 2
                       + _gdn_core_scratch(nbb, tb, chunk),
        compiler_params=pltpu.CompilerParams(dimension_semantics=("parallel", "arbitrary"),
                                             vmem_limit_bytes=VMEM_LIMIT),
        name="gdn",
    )(gq, z, ba, conv_buf8, s0, conv_w8, head_params, norm_w_row, ones_bd)


def _gdn_pairs_call(qkv, z, beta, g, s0, norm_w_row, token_block, group):
    bsz, seq, w3 = qkv.shape
    hw = HEAD_WIDTH
    chunk = HEAD_DIM
    nbb = GDN_BATCH_BLOCK
    assert w3 == 3 * hw and seq % token_block == 0 and token_block % (chunk * group) == 0 and bsz % nbb == 0
    tb = token_block
    ones_bd = _head_ones()
    blk = lambda w: pl.BlockSpec((nbb, tb, w), lambda b, t: (b, t, 0))
    per_b3 = lambda s: pl.BlockSpec((nbb,) + s, lambda b, t: (b,) + (0,) * len(s))
    return pl.pallas_call(
        functools.partial(_gdn_pairs_kernel, group=group),
        grid=(bsz // nbb, seq // tb),
        in_specs=[blk(w3), blk(hw), blk(LANES), blk(LANES), per_b3((N_HEADS, HEAD_DIM, HEAD_DIM)),
                  _resident(norm_w_row.shape), _resident(ones_bd.shape)],
        out_specs=[blk(hw), per_b3((N_HEADS, HEAD_DIM, HEAD_DIM))],
        out_shape=[jax.ShapeDtypeStruct((bsz, seq, hw), F32),
                   jax.ShapeDtypeStruct((bsz, N_HEADS, HEAD_DIM, HEAD_DIM), F32)],
        scratch_shapes=[pltpu.VMEM((nbb, tb, hw), F32)]
                       + [pltpu.VMEM((nbb, N_HEADS // 2, HEAD_DIM, LANES), F32)]
                       + [pltpu.VMEM((nbb, tb, hw), F32)] * 5
                       + [pltpu.VMEM((nbb, (tb // chunk) * SUBLANES, hw), F32)],
        compiler_params=pltpu.CompilerParams(dimension_semantics=("parallel", "arbitrary"),
                                             vmem_limit_bytes=VMEM_LIMIT),
        name="gdn_pairs",
    )(qkv, z, beta, g, s0, norm_w_row, ones_bd)


def _gdn_step_kernel(xq_ref, xk_ref, xv_ref, cq_ref, ck_ref, cv_ref, wq_ref, wk_ref, wv_ref, z_ref, ba_ref, hp_ref,
                     nw_ref, s0_ref, o_ref, s_ref, k_scr, q_scr):
    n_tok, hd, _ = xq_ref.shape
    h = pl.program_id(0)
    col_sum = lambda x: jnp.sum(x, axis=0, keepdims=True)

    def conv_act(x_ref, c_ref, w_ref, t):
        planes = [c_ref[i] for i in range(CONV_WIDTH - 1)] + [x_ref[i] for i in range(t + 1)]
        taps = planes[t:t + CONV_WIDTH]
        acc = jnp.zeros(taps[0].shape, F32)
        for j in range(CONV_WIDTH):
            acc = acc + taps[j] * w_ref[j]
        return jax.nn.silu(acc)

    s_ref[...] = s0_ref[...]
    neg_rate = -jnp.exp(hp_ref[0, pl.ds(h, 1), :])
    dt_bias = hp_ref[1, pl.ds(h, 1), :]
    for t in range(n_tok):
        q = conv_act(xq_ref, cq_ref, wq_ref, t)
        k = conv_act(xk_ref, ck_ref, wk_ref, t)
        v = conv_act(xv_ref, cv_ref, wv_ref, t)
        q_scr[...] = q * lax.rsqrt(col_sum(q * q) + RMS_EPS) * hd ** -0.5
        k_scr[...] = k * lax.rsqrt(col_sum(k * k) + RMS_EPS)
        beta = jax.nn.sigmoid(ba_ref[t, pl.ds(h, 1), :])
        decay = jnp.exp(neg_rate * jax.nn.softplus(ba_ref[t, pl.ds(N_HEADS + h, 1), :] + dt_bias))

        def k_dot_state(d, acc):
            return acc + k_scr[pl.ds(d, 1), :] * s_ref[0, d]

        ks = lax.fori_loop(0, hd, k_dot_state, jnp.zeros((hd, xq_ref.shape[2]), F32), unroll=8)
        delta = beta * (v - decay * ks)

        def update(d, acc):
            s_new = decay * s_ref[0, d] + k_scr[pl.ds(d, 1), :] * delta
            s_ref[0, d] = s_new
            return acc + q_scr[pl.ds(d, 1), :] * s_new

        o = lax.fori_loop(0, hd, update, jnp.zeros((hd, xq_ref.shape[2]), F32), unroll=8)
        o = o * lax.rsqrt(col_sum(o * o) * (1.0 / hd) + RMS_EPS) * nw_ref[...]
        o_ref[t] = o * jax.nn.silu(z_ref[t])


def _gdn_step_call(x_t, conv_t, conv_w_t, z_t, ba_t, hp_t, nw_t, s0_t):
    n_tok, w3, bsz = x_t.shape
    hd = HEAD_DIM
    part = lambda rows, p: pl.BlockSpec((rows, hd, bsz), lambda h, p=p: (0, p * N_HEADS + h, 0))
    whole = lambda a: pl.BlockSpec(a.shape, lambda h: (0,) * a.ndim)
    state_spec = pl.BlockSpec((1, hd, hd, bsz), lambda h: (h, 0, 0, 0))
    return pl.pallas_call(
        _gdn_step_kernel,
        grid=(N_HEADS,),
        in_specs=[part(n_tok, 0), part(n_tok, 1), part(n_tok, 2),
                  part(CONV_WIDTH - 1, 0), part(CONV_WIDTH - 1, 1), part(CONV_WIDTH - 1, 2),
                  part(CONV_WIDTH, 0), part(CONV_WIDTH, 1), part(CONV_WIDTH, 2),
                  part(n_tok, 0), whole(ba_t), whole(hp_t), whole(nw_t), state_spec],
        out_specs=[part(n_tok, 0), state_spec],
        out_shape=[jax.ShapeDtypeStruct((n_tok, N_HEADS * hd, bsz), F32),
                   jax.ShapeDtypeStruct(s0_t.shape, F32)],
        scratch_shapes=[pltpu.VMEM((hd, bsz), F32)] * 2,
        compiler_params=pltpu.CompilerParams(dimension_semantics=("parallel",), vmem_limit_bytes=VMEM_LIMIT),
        name="gdn_step",
    )(x_t, x_t, x_t, conv_t, conv_t, conv_t, conv_w_t, conv_w_t, conv_w_t, z_t, ba_t, hp_t, nw_t, s0_t)


def _pad_rows(x, rows, front=False):
    extra = rows - x.shape[1]
    cfg = [(0, 0)] * x.ndim
    cfg[1] = (extra, 0) if front else (0, extra)
    return jnp.pad(x, cfg)


def kernel(x_prompt, x_sample, cache_attn_k, cache_attn_v, state_gdn, state_conv, rel_bias, ln1_g, ln1_b,
           ffn1_w_gate, ffn1_w_up, ffn1_w_down, w_in, w_out, gdn_conv_w, gdn_a_log, gdn_dt_bias, gdn_norm_w,
           ln2_g, ln2_b, ffn2_w_gate, ffn2_w_up, ffn2_w_down, ln3_g, ln3_b):
    depth = w_in.shape[0]
    alpha = (2.0 * depth) ** 0.25
    bsz, seq, d_model = x_prompt.shape
    dbsz, dseq, _ = x_sample.shape
    w_buf = cache_attn_k.shape[2]
    hw = HEAD_WIDTH
    in_cols = w_in.shape[2]
    assert in_cols == 7 * hw + 2 * N_HEADS and CONV_WIDTH - 1 <= dseq <= SAMPLE_T_PAD
    assert seq % GDN_TOKEN_BLOCK == 0 and GDN_TOKEN_BLOCK % (GDN_CHUNK * GDN_CHUNK_GROUP) == 0

    sample_lows, sample_buckets = _sample_layout(w_buf, dseq)
    tab_p = _t5_table_call(rel_bias, jnp.asarray(_prompt_buckets()))
    tab_s = _t5_table_call(rel_bias, jnp.asarray(sample_buckets))

    yp = x_prompt.reshape(bsz * seq, d_model)
    ys = x_sample.reshape(dbsz * dseq, d_model)
    collected = [[] for _ in range(8)]
    row = lambda v: v.reshape(1, -1)
    for layer in range(depth):
        wg1, wu1, wd1 = (w[layer].astype(BF16) for w in (ffn1_w_gate, ffn1_w_up, ffn1_w_down))
        wg2, wu2, wd2 = (w[layer].astype(BF16) for w in (ffn2_w_gate, ffn2_w_up, ffn2_w_down))
        win = jnp.pad(w_in[layer], ((0, 0), (0, 7 * hw + LANES - in_cols))).astype(BF16)
        wo = w_out[layer].astype(BF16)
        conv_w8 = jnp.pad(gdn_conv_w[layer], ((0, SUBLANES - CONV_WIDTH), (0, 0)))
        head_params = jnp.zeros((SUBLANES, LANES), F32)
        head_params = head_params.at[0, N_HEADS:2 * N_HEADS].set(gdn_a_log[layer])
        head_params = head_params.at[1, N_HEADS:2 * N_HEADS].set(gdn_dt_bias[layer])
        norm_w_row = jnp.tile(gdn_norm_w[layer], N_HEADS).reshape(1, hw)
        pre = functools.partial(_pre_call, wg=wg1, wu=wu1, wd=wd1, g=row(ln1_g[layer]), b=row(ln1_b[layer]),
                                win=win, alpha=alpha)
        post = functools.partial(_post_call, wo=wo, g2=row(ln2_g[layer]), b2=row(ln2_b[layer]), wg=wg2, wu=wu2,
                                 wd=wd2, g3=row(ln3_g[layer]), b3=row(ln3_b[layer]), alpha=alpha)

        x1, aq, ak, av, ak_t, av_t, gdn_qkv, z, beta, gate, gq_tail = _pre_prompt_call(
            yp, wg1, wu1, wd1, row(ln1_g[layer]), row(ln1_b[layer]), win, conv_w8, head_params, alpha, seq)
        shp = lambda a: a.reshape(bsz, seq, a.shape[1])
        attn = _attn_prompt_call(shp(aq), shp(ak), shp(av), tab_p)
        gdn, s_p = _gdn_pairs_call(shp(gdn_qkv), shp(z), shp(beta), shp(gate),
                                   jnp.zeros((bsz, N_HEADS, HEAD_DIM, HEAD_DIM), F32), norm_w_row,
                                   token_block=GDN_TOKEN_BLOCK, group=GDN_CHUNK_GROUP)
        yp = post(attn.reshape(bsz * seq, hw), gdn.reshape(bsz * seq, hw), x1)
        wp = min(w_buf, seq)
        heads5 = lambda a_t: a_t.reshape(bsz, N_HEADS, HEAD_DIM, seq).transpose(0, 3, 1, 2)[:, seq - wp:]
        st_p = (heads5(ak_t), heads5(av_t), s_p, gq_tail[:, SUBLANES - (CONV_WIDTH - 1):])

        x1s, aqs, aks, avs, gqs, zs, bas = pre(ys)
        shs = lambda a: _pad_rows(a.reshape(dbsz, dseq, a.shape[1]), SAMPLE_T_PAD)
        heads5s = lambda a: a.reshape(dbsz, dseq, N_HEADS, HEAD_DIM)
        new_rows = lambda a: jnp.pad(heads5s(a).transpose(0, 2, 1, 3),
                                     ((0, 0), (0, 0), (0, SAMPLE_T_PAD - dseq), (0, 0)))
        by_head_t = lambda c: c.transpose(0, 2, 3, 1)
        attn_s = _attn_sample_call(shs(aqs), new_rows(aks), new_rows(avs), by_head_t(cache_attn_k[layer]),
                                   by_head_t(cache_attn_v[layer]), tab_s, sample_lows)
        to_lanes = lambda a: a.reshape(dbsz, dseq * a.shape[1]).T.reshape(dseq, a.shape[1], dbsz)
        on_lanes = lambda a: jnp.broadcast_to(a[..., None], a.shape + (dbsz,))
        gdn_t, s_t = _gdn_step_call(
            to_lanes(gqs), state_conv[layer].transpose(1, 2, 0), on_lanes(gdn_conv_w[layer]), to_lanes(zs),
            to_lanes(bas[:, :2 * N_HEADS]), on_lanes(jnp.stack([gdn_a_log[layer], gdn_dt_bias[layer]])),
            on_lanes(gdn_norm_w[layer]), state_gdn[layer].transpose(1, 2, 3, 0))
        gdn_s = gdn_t.reshape(dseq * hw, dbsz).T.reshape(dbsz * dseq, hw)
        s_s = s_t.transpose(3, 0, 1, 2)
        ys = post(attn_s[:, :dseq].reshape(dbsz * dseq, hw), gdn_s, x1s)
        st_s = (heads5s(aks), heads5s(avs), s_s,
                gqs.reshape(dbsz, dseq, 3 * hw)[:, dseq - (CONV_WIDTH - 1):])
        for lst, st in zip(collected, st_p + st_s):
            lst.append(st)
    outs = [jnp.stack(t, axis=0) for t in collected]
    return (yp.reshape(bsz, seq, d_model), ys.reshape(dbsz, dseq, d_model)) + tuple(outs)
```

```python
import functools
import math

import numpy as np
import jax
import jax.numpy as jnp
from jax import lax
from jax.experimental import pallas as pl
from jax.experimental.pallas import tpu as pltpu

F32 = jnp.float32
BF16 = jnp.bfloat16

HEAD_DIM = 64
N_HEADS = 8
HEAD_WIDTH = N_HEADS * HEAD_DIM
DILATED_PATTERNS = ((128, 1), (512, 4), (2048, 16))
KEYS_PER_BLOCK = 128
T5_BUCKETS = 32
T5_MAX_EXACT = 16
T5_MAX_DIST = 2048
CONV_WIDTH = 4
GDN_CHUNK = 64
LN_EPS = 1e-5
RMS_EPS = 1e-6
NEG_BIG = -1e30

LANES = 128
SUBLANES = 8
TOKEN_TILE = 256
POST_TOKEN_TILE = 512
GDN_TOKEN_BLOCK = 256
ATTN_SKEW_GROUPS = 1
GDN_CHUNK_GROUP = 4
GDN_BATCH_BLOCK = 4
GDN_SAMPLE_BATCH_BLOCK = 8
SAMPLE_T_PAD = 8
VMEM_LIMIT = 56 * 1024 * 1024


def _resident(shape):
    nd = len(shape)
    return pl.BlockSpec(shape, lambda *_: (0,) * nd, pipeline_mode=pl.Buffered(1))


def _layernorm(y, g, b):
    mu = jnp.mean(y, axis=-1, keepdims=True)
    yc = y - mu
    var = jnp.mean(yc * yc, axis=-1, keepdims=True)
    return yc * lax.rsqrt(var + LN_EPS) * g + b


def _swiglu(xb, wg_ref, wu_ref, wd_ref, h_scr):
    d_ff = wg_ref.shape[1]
    step = 2 * LANES
    assert d_ff % step == 0
    for c in range(d_ff // step):
        sl = slice(c * step, (c + 1) * step)
        gate = jnp.dot(xb, wg_ref[:, sl], preferred_element_type=F32)
        up = jnp.dot(xb, wu_ref[:, sl], preferred_element_type=F32)
        h_scr[:, sl] = (jax.nn.silu(gate) * up).astype(BF16)
    return jnp.dot(h_scr[...], wd_ref[...], preferred_element_type=F32)


PROJ_WIDTHS = (HEAD_WIDTH, HEAD_WIDTH, HEAD_WIDTH, 3 * HEAD_WIDTH, HEAD_WIDTH, LANES)
PROJ_EDGES = tuple(int(e) for e in np.cumsum((0,) + PROJ_WIDTHS))


def _first_half_step(x_ref, wg_ref, wu_ref, wd_ref, g_ref, b_ref, h_scr, alpha):
    x = x_ref[...]
    ff = _swiglu(x.astype(BF16), wg_ref, wu_ref, wd_ref, h_scr)
    return _layernorm(alpha * x + 0.5 * ff, g_ref[...], b_ref[...])


def _pre_kernel(x_ref, wg_ref, wu_ref, wd_ref, g_ref, b_ref, win_ref,
                x1_ref, aq_ref, ak_ref, av_ref, gq_ref, z_ref, ba_ref, h_scr, *, alpha):
    x1 = _first_half_step(x_ref, wg_ref, wu_ref, wd_ref, g_ref, b_ref, h_scr, alpha)
    x1_ref[...] = x1
    xb = x1.astype(BF16)
    for i, ref in enumerate((aq_ref, ak_ref, av_ref, gq_ref, z_ref, ba_ref)):
        ref[...] = jnp.dot(xb, win_ref[:, PROJ_EDGES[i]:PROJ_EDGES[i + 1]], preferred_element_type=F32)


def _gdn_gates(ba, hp_ref):
    beta = jax.nn.sigmoid(ba)
    g = -jnp.exp(hp_ref[0:1, :]) * jax.nn.softplus(ba + hp_ref[1:2, :])
    return beta, g


def _causal_conv_silu(ext_scr, x, cw_ref):
    tb = x.shape[0]
    hist = SUBLANES
    ext_scr[hist:hist + tb, :] = x
    ext = ext_scr[0:hist + tb, :]
    conv = jnp.zeros(x.shape, F32)
    for j in range(CONV_WIDTH):
        back = CONV_WIDTH - 1 - j
        rows = pltpu.roll(ext, back, axis=0)[hist:hist + tb, :] if back else x
        conv = conv + rows * cw_ref[j:j + 1, :]
    ext_scr[0:hist, :] = ext_scr[tb:tb + hist, :]
    return jax.nn.silu(conv)


def _pre_prompt_kernel(x_ref, wg_ref, wu_ref, wd_ref, g_ref, b_ref, win_ref, cw_ref, hp_ref, ones_ref,
                       x1_ref, aq_ref, ak_ref, av_ref, akt_ref, avt_ref, qkv_ref, z_ref, bt_ref, gg_ref, tail_ref,
                       h_scr, ext_scr, *, alpha, tiles_per_seq):
    x1 = _first_half_step(x_ref, wg_ref, wu_ref, wd_ref, g_ref, b_ref, h_scr, alpha)
    x1_ref[...] = x1
    xb = x1.astype(BF16)
    proj = lambda i: jnp.dot(xb, win_ref[:, PROJ_EDGES[i]:PROJ_EDGES[i + 1]], preferred_element_type=F32)

    @pl.when(pl.program_id(0) % tiles_per_seq == 0)
    def _():
        ext_scr[0:SUBLANES, :] = jnp.zeros((SUBLANES, ext_scr.shape[1]), F32)

    raw = proj(3)
    tail_ref[0] = raw[raw.shape[0] - SUBLANES:, :]
    act = _causal_conv_silu(ext_scr, raw, cw_ref)
    hw = HEAD_WIDTH
    head_sq = lambda x: jnp.dot((x * x).astype(BF16), ones_ref[...], preferred_element_type=F32)
    q = act[:, 0:hw]
    k = act[:, hw:2 * hw]
    qkv_ref[:, 0:hw] = q * lax.rsqrt(head_sq(q) + RMS_EPS) * HEAD_DIM ** -0.5
    qkv_ref[:, hw:2 * hw] = k * lax.rsqrt(head_sq(k) + RMS_EPS)
    qkv_ref[:, 2 * hw:3 * hw] = act[:, 2 * hw:3 * hw]
    bt_ref[...], gg_ref[...] = _gdn_gates(proj(5), hp_ref)
    aq_ref[...] = proj(0)
    for i, ref, ref_t in ((1, ak_ref, akt_ref), (2, av_ref, avt_ref)):
        kv = proj(i)
        ref[...] = kv
        ref_t[0] = kv.T
    z_ref[...] = proj(4)


def _pre_call(x2d, wg, wu, wd, g, b, win, alpha):
    m, d = x2d.shape
    d_ff = wg.shape[1]
    tm = TOKEN_TILE
    assert m % tm == 0 and sum(PROJ_WIDTHS) == win.shape[1]
    tile = lambda w: pl.BlockSpec((tm, w), lambda i: (i, 0))
    return pl.pallas_call(
        functools.partial(_pre_kernel, alpha=alpha),
        grid=(m // tm,),
        in_specs=[tile(d), _resident(wg.shape), _resident(wu.shape), _resident(wd.shape),
                  _resident(g.shape), _resident(b.shape), _resident(win.shape)],
        out_specs=[tile(d)] + [tile(w) for w in PROJ_WIDTHS],
        out_shape=[jax.ShapeDtypeStruct((m, d), F32)] + [jax.ShapeDtypeStruct((m, w), F32) for w in PROJ_WIDTHS],
        scratch_shapes=[pltpu.VMEM((tm, d_ff), BF16)],
        compiler_params=pltpu.CompilerParams(dimension_semantics=("parallel",), vmem_limit_bytes=VMEM_LIMIT),
        name="pre_ffn_proj",
    )(x2d, wg, wu, wd, g, b, win)


def _pre_prompt_call(x2d, wg, wu, wd, g, b, win, conv_w8, head_params, alpha, seq):
    m, d = x2d.shape
    d_ff = wg.shape[1]
    tm = TOKEN_TILE
    hw = HEAD_WIDTH
    assert seq % tm == 0 and m % seq == 0 and sum(PROJ_WIDTHS) == win.shape[1]
    per_seq = seq // tm
    tile = lambda w: pl.BlockSpec((tm, w), lambda i: (i, 0))
    t_spec = pl.BlockSpec((1, hw, tm), lambda i: (i // per_seq, 0, i % per_seq))
    t_shape = jax.ShapeDtypeStruct((m // seq, hw, seq), F32)
    rows = lambda w: jax.ShapeDtypeStruct((m, w), F32)
    consts = (wg, wu, wd, g, b, win, conv_w8, head_params, _head_ones())
    return pl.pallas_call(
        functools.partial(_pre_prompt_kernel, alpha=alpha, tiles_per_seq=per_seq),
        grid=(m // tm,),
        in_specs=[tile(d)] + [_resident(c.shape) for c in consts],
        out_specs=[tile(d), tile(hw), tile(hw), tile(hw), t_spec, t_spec, tile(3 * hw), tile(hw), tile(LANES),
                   tile(LANES), pl.BlockSpec((1, SUBLANES, 3 * hw), lambda i: (i // per_seq, 0, 0))],
        out_shape=[rows(d), rows(hw), rows(hw), rows(hw), t_shape, t_shape, rows(3 * hw), rows(hw), rows(LANES),
                   rows(LANES), jax.ShapeDtypeStruct((m // seq, SUBLANES, 3 * hw), F32)],
        scratch_shapes=[pltpu.VMEM((tm, d_ff), BF16), pltpu.VMEM((tm + 2 * SUBLANES, 3 * hw), F32)],
        compiler_params=pltpu.CompilerParams(dimension_semantics=("arbitrary",), vmem_limit_bytes=VMEM_LIMIT),
        name="pre_ffn_proj_gdnprep",
    )(x2d, *consts)


def _post_kernel(attn_ref, gdn_ref, x1_ref, wo_ref, g2_ref, b2_ref, wg_ref, wu_ref, wd_ref, g3_ref, b3_ref,
                 y_ref, h_scr, *, alpha):
    tm = y_ref.shape[0]
    n_split = 2 if tm >= 2 * TOKEN_TILE else 1
    halves = [pl.ds(i * (tm // n_split), tm // n_split) for i in range(n_split)]
    mix, x2, ff = {}, {}, {}
    for i, rows in enumerate(halves):
        mix[i] = (jnp.dot(attn_ref[rows, :].astype(BF16), wo_ref[0:HEAD_WIDTH, :], preferred_element_type=F32)
                  + jnp.dot(gdn_ref[rows, :].astype(BF16), wo_ref[HEAD_WIDTH:2 * HEAD_WIDTH, :],
                            preferred_element_type=F32))
    for i, rows in enumerate(halves):
        x2[i] = _layernorm(alpha * x1_ref[rows, :] + mix[i], g2_ref[...], b2_ref[...])
    for i, rows in enumerate(halves):
        ff[i] = _swiglu(x2[i].astype(BF16), wg_ref, wu_ref, wd_ref, h_scr.at[rows])
    for i, rows in enumerate(halves):
        y_ref[rows, :] = _layernorm(alpha * x2[i] + 0.5 * ff[i], g3_ref[...], b3_ref[...])


def _post_call(attn, gdn, x1, wo, g2, b2, wg, wu, wd, g3, b3, alpha):
    m, d = x1.shape
    tm = POST_TOKEN_TILE
    assert m % tm == 0
    tile = lambda w: pl.BlockSpec((tm, w), lambda i: (i, 0))
    consts = (wo, g2, b2, wg, wu, wd, g3, b3)
    return pl.pallas_call(
        functools.partial(_post_kernel, alpha=alpha),
        grid=(m // tm,),
        in_specs=[tile(HEAD_WIDTH), tile(HEAD_WIDTH), tile(d)] + [_resident(c.shape) for c in consts],
        out_specs=tile(d),
        out_shape=jax.ShapeDtypeStruct((m, d), F32),
        scratch_shapes=[pltpu.VMEM((tm, wg.shape[1]), BF16)],
        compiler_params=pltpu.CompilerParams(dimension_semantics=("parallel",), vmem_limit_bytes=VMEM_LIMIT),
        name="post_out_ffn",
    )(attn, gdn, x1, *consts)


def _t5_bucket_np(dist):
    n = np.maximum(dist, 0)
    nf = np.maximum(n, 1).astype(np.float32)
    large = T5_MAX_EXACT + (np.log(nf / np.float32(T5_MAX_EXACT)) / np.float32(math.log(T5_MAX_DIST / T5_MAX_EXACT))
                            * np.float32(T5_BUCKETS - T5_MAX_EXACT)).astype(np.int32)
    large = np.minimum(large, T5_BUCKETS - 1)
    return np.where(n < T5_MAX_EXACT, n, large).astype(np.int32)


def _t5_table_kernel(rb_ref, bk_ref, out_ref):
    for h in range(N_HEADS):
        bk = bk_ref[0, h % bk_ref.shape[1]]
        acc = jnp.full(bk.shape, NEG_BIG, F32)
        for b in range(T5_BUCKETS):
            acc = jnp.where(bk == b, rb_ref[b, h], acc)
        out_ref[0, h] = acc


def _t5_table_call(rel_bias, buckets):
    p, hb, r, c = buckets.shape
    assert hb in (1, N_HEADS)
    return pl.pallas_call(
        _t5_table_kernel,
        grid=(p,),
        in_specs=[pl.BlockSpec(memory_space=pltpu.SMEM),
                  pl.BlockSpec((1, hb, r, c), lambda i: (i, 0, 0, 0))],
        out_specs=pl.BlockSpec((1, N_HEADS, r, c), lambda i: (i, 0, 0, 0)),
        out_shape=jax.ShapeDtypeStruct((p, N_HEADS, r, c), F32),
        compiler_params=pltpu.CompilerParams(dimension_semantics=("parallel",)),
        name="t5_bias_table",
    )(rel_bias, buckets)


def _prompt_buckets():
    nb = KEYS_PER_BLOCK
    qi = np.arange(nb)[:, None]
    ki = np.arange(2 * nb)[None, :]
    dist = qi + nb - ki
    valid = (dist >= 0) & (dist <= nb)
    out = []
    for window, dil in DILATED_PATTERNS:
        assert window // dil == nb
        out.append(np.where(valid, _t5_bucket_np(dist * dil), -1))
    return np.stack(out).astype(np.int32)[:, None]


def _nt_dot(a, b):
    return lax.dot_general(a, b, (((1,), (1,)), ((), ())), preferred_element_type=F32)


def _emit_skewed(stages, items, n_groups):
    n_groups = max(1, min(n_groups, len(items)))
    groups = [items[i::n_groups] for i in range(n_groups)]
    for step in range(len(stages) + n_groups - 1):
        for gi, group in enumerate(groups):
            s = step - gi
            if 0 <= s < len(stages):
                for item in group:
                    stages[s](item)


def _attn_prompt_kernel(q_ref, k_ref, v_ref, tab_ref, o_ref, og_scr, lg_scr, *, seq):
    nb = KEYS_PER_BLOCK
    hd = HEAD_DIM
    n_pat = len(DILATED_PATTERNS)
    heads = LANES // hd
    for g, (window, dil) in enumerate(DILATED_PATTERNS):
        n_blocks = seq // (dil * nb)

        def blocks(starts, with_prev, g=g, dil=dil):
            lane_head = lax.broadcasted_iota(jnp.int32, (nb, LANES), 1) // hd
            n_keys = 2 * nb if with_prev else nb
            row_sl, qs, ks, vs = [], [], [], []
            for start in starts:
                first_key = start - dil * (n_keys - nb)
                row_sl.append(pl.ds(start, nb, stride=dil) if dil > 1 else pl.ds(start, nb))
                key_rows = pl.ds(first_key, n_keys, stride=dil) if dil > 1 else pl.ds(first_key, n_keys)
                qs.append(q_ref[0, row_sl[-1], :] * hd ** -0.5)
                ks.append(k_ref[0, key_rows, :].astype(BF16))
                vs.append(v_ref[0, key_rows, :].astype(BF16))
            chains = [(u, hh) for u in range(len(starts)) for hh in range(heads)]
            s, m, p, den, acc = {}, {}, {}, {}, {}

            def scores(c):
                u, hh = c
                qb = jnp.where(lane_head == hh, qs[u], 0.0).astype(BF16)
                s[c] = _nt_dot(qb, ks[u]) + tab_ref[g, hh, :, 2 * nb - n_keys:2 * nb]

            def row_max(c):
                m[c] = jnp.max(s[c], axis=-1, keepdims=True)

            def probs(c):
                p[c] = jnp.exp(s[c] - m[c])
                den[c] = jnp.sum(p[c], axis=-1, keepdims=True)

            def values(c):
                u, hh = c
                acc[c] = jnp.dot(p[c].astype(BF16), vs[u], preferred_element_type=F32)

            def finish(c):
                acc[c] = acc[c] / den[c]
                den[c] = m[c] + jnp.log(den[c])

            _emit_skewed((scores, row_max, probs, values, finish), chains, ATTN_SKEW_GROUPS)
            for u in range(len(starts)):
                out, lse = acc[u, 0], den[u, 0]
                for hh in range(1, heads):
                    out = jnp.where(lane_head == hh, acc[u, hh], out)
                    lse = jnp.where(lane_head == hh, den[u, hh], lse)
                og_scr[g, row_sl[u], :] = out
                lg_scr[g, row_sl[u], :] = jnp.broadcast_to(lse, (nb, LANES))

        def unroll_of(count):
            return next(u for u in (8, 6, 5, 4, 3, 2, 1) if count % u == 0)

        u_first = unroll_of(dil)

        def first_body(i, carry, blocks=blocks, u_first=u_first):
            blocks([i * u_first + j for j in range(u_first)], False)
            return carry

        lax.fori_loop(0, dil // u_first, first_body, 0)
        if n_blocks > 1:
            n_rest = dil * (n_blocks - 1)
            u_rest = unroll_of(n_rest)

            def rest_body(i, carry, blocks=blocks, dil=dil, n_blocks=n_blocks, u_rest=u_rest):
                starts = []
                for j in range(u_rest):
                    idx = i * u_rest + j
                    starts.append(idx // (n_blocks - 1) + dil * nb * (idx % (n_blocks - 1) + 1))
                blocks(starts, True)
                return carry

            lax.fori_loop(0, n_rest // u_rest, rest_body, 0)

    rb = 2 * nb

    def merge_body(i, carry):
        rows = pl.ds(pl.multiple_of(i * rb, rb), rb)
        lses = [lg_scr[g, rows, :] for g in range(n_pat)]
        top = functools.reduce(jnp.maximum, lses)
        ws = [jnp.exp(l - top) for l in lses]
        num = functools.reduce(lambda a, b: a + b, [w * og_scr[g, rows, :] for g, w in enumerate(ws)])
        o_ref[0, rows, :] = num / functools.reduce(lambda a, b: a + b, ws)
        return carry

    lax.fori_loop(0, seq // rb, merge_body, 0)


def _attn_prompt_call(aq, ak, av, tab):
    bsz, seq, width = aq.shape
    assert width == HEAD_WIDTH and seq % (DILATED_PATTERNS[-1][1] * KEYS_PER_BLOCK) == 0
    n_pat = len(DILATED_PATTERNS)
    heads_per_step = LANES // HEAD_DIM
    qkv_spec = pl.BlockSpec((1, seq, LANES), lambda h, b: (b, 0, h))
    return pl.pallas_call(
        functools.partial(_attn_prompt_kernel, seq=seq),
        grid=(N_HEADS // heads_per_step, bsz),
        in_specs=[qkv_spec, qkv_spec, qkv_spec,
                  pl.BlockSpec((n_pat, heads_per_step, KEYS_PER_BLOCK, 2 * KEYS_PER_BLOCK),
                               lambda h, b: (0, h, 0, 0))],
        out_specs=pl.BlockSpec((1, seq, LANES), lambda h, b: (b, 0, h)),
        out_shape=jax.ShapeDtypeStruct((bsz, seq, width), F32),
        scratch_shapes=[pltpu.VMEM((n_pat, seq, LANES), F32)] * 2,
        compiler_params=pltpu.CompilerParams(dimension_semantics=("parallel", "parallel"),
                                             vmem_limit_bytes=VMEM_LIMIT),
        name="attn_prompt",
    )(aq, ak, av, tab)


def _sample_layout(w_buf, t_new):
    assert w_buf % LANES == 0 and t_new <= SAMPLE_T_PAD
    row_of_col = np.full((w_buf + LANES,), -1, np.int64)
    row_of_col[:w_buf + t_new] = np.arange(w_buf + t_new)
    buckets = np.full((len(DILATED_PATTERNS), 1, SAMPLE_T_PAD, w_buf + LANES), -1, np.int32)
    lows = []
    for g, (window, dil) in enumerate(DILATED_PATTERNS):
        lows.append(max(0, w_buf - window) // LANES * LANES)
        for t in range(t_new):
            dist = w_buf + t - row_of_col
            ok = (row_of_col >= 0) & (dist >= 0) & (dist <= window) & (dist % dil == 0)
            assert int(ok.sum()) == min(window, w_buf + t) // dil + 1 and not ok[:lows[g]].any()
            buckets[g, 0, t] = np.where(ok, _t5_bucket_np(dist), -1)
    return tuple(lows), buckets


def _attn_sample_kernel(q_ref, kn_ref, vn_ref, kt_ref, vt_ref, tab_ref, o_ref, *, lows):
    tp = SAMPLE_T_PAD
    hd = HEAD_DIM
    w_buf = kt_ref.shape[3]
    n_pat = len(DILATED_PATTERNS)
    add = lambda x, y: x + y
    rowmax = lambda x: jnp.max(x, axis=-1, keepdims=True)
    rowsum = lambda x: jnp.sum(x, axis=-1, keepdims=True)
    heads = range(N_HEADS)
    q8 = q_ref[0] * hd ** -0.5
    s_buf = {h: jnp.dot(q8[:, hd * h:hd * (h + 1)], kt_ref[0, h], preferred_element_type=F32) for h in heads}
    s_new = {h: _nt_dot(q8[:, hd * h:hd * (h + 1)], kn_ref[0, h]) for h in heads}
    p_buf, p_new, den, lse = {}, {}, {}, {}
    for h in heads:
        for g in range(n_pat):
            lo = lows[g]
            l_buf = s_buf[h][:, lo:] + tab_ref[g, h, :, lo:w_buf]
            l_new = s_new[h] + tab_ref[g, h, :, w_buf:w_buf + tp]
            m = jnp.maximum(rowmax(l_buf), rowmax(l_new))
            p_buf[h, g] = jnp.exp(l_buf - m)
            p_new[h, g] = jnp.exp(l_new - m)
            den[h, g] = rowsum(p_buf[h, g]) + rowsum(p_new[h, g])
            lse[h, g] = m + jnp.log(den[h, g])
    edges = sorted(set(lows)) + [w_buf]
    outs = {}
    for h in heads:
        top = functools.reduce(jnp.maximum, [lse[h, g] for g in range(n_pat)])
        ws = [jnp.exp(lse[h, g] - top) for g in range(n_pat)]
        w_sum = functools.reduce(add, ws)
        coef = [ws[g] / (w_sum * den[h, g]) for g in range(n_pat)]
        acc = jnp.dot(functools.reduce(add, [coef[g] * p_new[h, g] for g in range(n_pat)]), vn_ref[0, h],
                      preferred_element_type=F32)
        for e0, e1 in zip(edges[:-1], edges[1:]):
            mix = functools.reduce(add, [coef[g] * p_buf[h, g][:, e0 - lows[g]:e1 - lows[g]]
                                         for g in range(n_pat) if lows[g] <= e0])
            acc = acc + _nt_dot(mix, vt_ref[0, h, :, e0:e1])
        outs[h] = acc
    o_ref[0] = jnp.concatenate([outs[h] for h in heads], axis=1)


N_POST_INPUTS = 11
N_ATTN_SAMPLE_INPUTS = 6


def _post_with_sample_attn_kernel(*refs, alpha, lows):
    post_in = refs[:N_POST_INPUTS]
    attn_in = refs[N_POST_INPUTS:N_POST_INPUTS + N_ATTN_SAMPLE_INPUTS]
    y_ref, o_ref, h_scr = refs[N_POST_INPUTS + N_ATTN_SAMPLE_INPUTS:]
    _post_kernel(*post_in, y_ref, h_scr, alpha=alpha)
    _attn_sample_kernel(*attn_in, o_ref, lows=lows)


def _post_with_sample_attn_call(attn, gdn, x1, wo, g2, b2, wg, wu, wd, g3, b3, alpha,
                                aq8, k_new, v_new, cache_kt, cache_vt, tab, lows):
    m, d = x1.shape
    bsz, n_heads, hd, w_buf = cache_kt.shape
    assert m % bsz == 0
    tm = m // bsz
    assert tm % SUBLANES == 0
    tile = lambda w: pl.BlockSpec((tm, w), lambda i: (i, 0))
    consts = (wo, g2, b2, wg, wu, wd, g3, b3)
    q_spec = pl.BlockSpec((1, SAMPLE_T_PAD, n_heads * hd), lambda b: (b, 0, 0))
    new_spec = pl.BlockSpec((1, n_heads, SAMPLE_T_PAD, hd), lambda b: (b, 0, 0, 0))
    buf_spec = pl.BlockSpec((1, n_heads, hd, w_buf), lambda b: (b, 0, 0, 0))
    assert len(consts) + 3 == N_POST_INPUTS
    return pl.pallas_call(
        functools.partial(_post_with_sample_attn_kernel, alpha=alpha, lows=lows),
        grid=(bsz,),
        in_specs=[tile(HEAD_WIDTH), tile(HEAD_WIDTH), tile(d)] + [_resident(c.shape) for c in consts]
                 + [q_spec, new_spec, new_spec, buf_spec, buf_spec, _resident(tab.shape)],
        out_specs=[tile(d), q_spec],
        out_shape=[jax.ShapeDtypeStruct((m, d), F32),
                   jax.ShapeDtypeStruct((bsz, SAMPLE_T_PAD, n_heads * hd), F32)],
        scratch_shapes=[pltpu.VMEM((tm, wg.shape[1]), BF16)],
        compiler_params=pltpu.CompilerParams(dimension_semantics=("parallel",), vmem_limit_bytes=VMEM_LIMIT),
        name="post_with_sample_attn",
    )(attn, gdn, x1, *consts, aq8, k_new, v_new, cache_kt, cache_vt, tab)


def _aligned(index, size):
    return index * size if isinstance(index, int) else pl.multiple_of(index * size, size)


def _split3(x):
    h1 = x.astype(BF16)
    r1 = x - h1.astype(F32)
    h2 = r1.astype(BF16)
    h3 = (r1 - h2.astype(F32)).astype(BF16)
    return h1, h2, h3


def _head_sums(x, ones_ref):
    h1, h2, _ = _split3(x)
    return (jnp.dot(h1, ones_ref[...], preferred_element_type=F32)
            + jnp.dot(h2, ones_ref[...], preferred_element_type=F32))


def _store_normed_qk(act, q_dst, k_dst, ones_ref):
    hw = HEAD_WIDTH
    q = act[:, 0:hw]
    k = act[:, hw:2 * hw]
    q_dst[...] = q * lax.rsqrt(_head_sums(q * q, ones_ref) + RMS_EPS) * HEAD_DIM ** -0.5
    k_dst[...] = k * lax.rsqrt(_head_sums(k * k, ones_ref) + RMS_EPS)


def _gdn_kernel(x_ref, z_ref, ba_ref, cb_ref, s0_ref, cw_ref, hp_ref, nw_ref, ones_ref,
                o_ref, sfin_ref,
                ext_scr, q_scr, k_scr, v_scr, bt_scr, g_scr, *core_scr, chunk, n_valid, group):
    n_bb, tb = x_ref.shape[0], x_ref.shape[1]
    t = pl.program_id(1)

    @pl.when(t == 0)
    def _():
        ext_scr[:, 0:SUBLANES, :] = cb_ref[...]

    for bb in range(n_bb):
        act = _causal_conv_silu(ext_scr.at[bb], x_ref[bb], cw_ref)
        _store_normed_qk(act, q_scr.at[bb], k_scr.at[bb], ones_ref)
        v_scr[bb] = act[:, 2 * HEAD_WIDTH:3 * HEAD_WIDTH]
        ba = ba_ref[bb]
        live = t * tb + lax.broadcasted_iota(jnp.int32, ba.shape, 0) < n_valid
        beta, g = _gdn_gates(ba, hp_ref)
        bt_scr[bb] = jnp.where(live, beta, 0.0)
        g_scr[bb] = jnp.where(live, g, 0.0)
    _gdn_core(s0_ref, z_ref, nw_ref, ones_ref, o_ref, sfin_ref, q_scr, k_scr, v_scr, 0, bt_scr, g_scr,
              *core_scr, chunk=chunk, group=group)


def _gdn_pairs_kernel(qkv_ref, z_ref, bt_scr, g_scr, s0_ref, nw_ref, ones_ref, o_ref, sfin_ref,
                      oc_scr, s_scr, u_scr, w_scr, in_scr, qd_scr, kd_scr, el_scr, *, group):
    chunk = HEAD_DIM
    k_lane0, v_lane0 = HEAD_WIDTH, 2 * HEAD_WIDTH
    n_bb, tb = bt_scr.shape[0], bt_scr.shape[1]
    n_chunks = tb // chunk
    n_pairs = N_HEADS // 2
    hd = HEAD_DIM
    t = pl.program_id(1)
    dot = functools.partial(jnp.dot, preferred_element_type=F32)

    lane = lax.broadcasted_iota(jnp.int32, (chunk, LANES), 1)
    row = lax.broadcasted_iota(jnp.int32, (chunk, LANES), 0)
    second = lane >= hd
    col = lane - jnp.where(second, hd, 0)
    tril = row >= col
    strict = row > col
    eye2 = (row == col).astype(F32)
    tril_b = (lax.broadcasted_iota(jnp.int32, (chunk, chunk), 0)
              >= lax.broadcasted_iota(jnp.int32, (chunk, chunk), 1)).astype(BF16)
    n_doublings = int(math.log2(chunk)) - 1

    def blockdiag(x):
        return jnp.concatenate([jnp.where(second, 0.0, x), jnp.where(second, x, 0.0)], axis=0)

    def per_head(cols, j):
        return jnp.where(second, cols[:, 2 * j + 1:2 * j + 2], cols[:, 2 * j:2 * j + 1])

    @pl.when(t == 0)
    def _():
        for bb in range(n_bb):
            for j in range(n_pairs):
                s_scr[bb, j] = jnp.concatenate([s0_ref[bb, 2 * j], s0_ref[bb, 2 * j + 1]], axis=1)

    def state_free_part(gi, carry):
        probs, shared = [], {}
        for bb in range(n_bb):
            for cg in range(group):
                c_idx = gi * group + cg
                rows = pl.ds(_aligned(c_idx, chunk), chunk)
                gc = functools.reduce(lambda a, b: a + b,
                                      [dot(tril_b, p) for p in _split3(g_scr[bb, rows, :])])
                shared[bb, cg] = (rows, c_idx, gc[:, N_HEADS:2 * N_HEADS], gc.T[N_HEADS:2 * N_HEADS, :],
                                  bt_scr[bb, rows, :])
                probs += [(bb, cg, j) for j in range(n_pairs)]
        kk, qk, decay, a, t_inv, pw, vb, kbg = {}, {}, {}, {}, {}, {}, {}, {}
        for p in probs:
            bb, cg, j = p
            rows, c_idx, gc, gct, bt = shared[bb, cg]
            pl_ = pl.ds(LANES * j, LANES)
            q2 = qkv_ref[bb, rows, pl_]
            k2 = qkv_ref[bb, rows, pl.ds(k_lane0 + LANES * j, LANES)]
            beta2 = per_head(bt, j)
            gcc = per_head(gc, j)
            gcr = jnp.concatenate([gct[2 * j:2 * j + 1, :], gct[2 * j + 1:2 * j + 2, :]], axis=1)
            g_last = gcc[chunk - 1:chunk, :]
            e_gc = jnp.exp(gcc)
            kb2 = k2 * beta2
            k_bd = blockdiag(k2)
            both = _nt_dot(jnp.concatenate([kb2, q2], axis=0), k_bd)
            kk[p] = both[0:chunk]
            qk[p] = both[chunk:2 * chunk]
            vb[p] = qkv_ref[bb, rows, pl.ds(v_lane0 + LANES * j, LANES)] * beta2
            kbg[p] = kb2 * e_gc
            qd_scr[bb, rows, pl_] = q2 * e_gc
            kd_scr[bb, rows, pl_] = k2 * jnp.exp(g_last - gcc)
            el_scr[bb, pl.ds(_aligned(c_idx, SUBLANES), SUBLANES), pl_] = jnp.broadcast_to(
                jnp.exp(g_last), (SUBLANES, LANES))
            decay[p] = jnp.where(tril, jnp.exp(jnp.where(tril, gcc - gcr, 0.0)), 0.0)
        for p in probs:
            bb, cg, j = p
            rows = shared[bb, cg][0]
            a[p] = jnp.where(strict, kk[p] * decay[p], 0.0)
            in_scr[bb, rows, pl.ds(LANES * j, LANES)] = jnp.where(tril, qk[p] * decay[p], 0.0)
            t_inv[p] = eye2 - a[p]
        for p in probs:
            pw[p] = dot(a[p], blockdiag(a[p]))
        for _ in range(n_doublings - 1):
            nxt_pw, nxt_t = {}, {}
            for p in probs:
                nxt_pw[p] = dot(pw[p], blockdiag(pw[p]))
                nxt_t[p] = dot(t_inv[p], blockdiag(eye2 + pw[p]))
            pw, t_inv = nxt_pw, nxt_t
        for p in probs:
            t_inv[p] = dot(t_inv[p], blockdiag(eye2 + pw[p]))
        for p in probs:
            bb, cg, j = p
            rows = shared[bb, cg][0]
            uw = dot(t_inv[p], jnp.concatenate([blockdiag(vb[p]), blockdiag(kbg[p])], axis=1))
            u_scr[bb, rows, pl.ds(LANES * j, LANES)] = uw[:, 0:LANES]
            w_scr[bb, rows, pl.ds(LANES * j, LANES)] = uw[:, LANES:2 * LANES]
        return carry

    def state_part(ci_, carry):
        rows = pl.ds(_aligned(ci_, chunk), chunk)
        el_rows = pl.ds(_aligned(ci_, SUBLANES), SUBLANES)
        probs = [(bb, j) for bb in range(n_bb) for j in range(n_pairs)]
        s_old, both, v_new, o_in, s_add = {}, {}, {}, {}, {}
        for p in probs:
            bb, j = p
            pl_ = pl.ds(LANES * j, LANES)
            s_old[p] = s_scr[bb, j]
            lhs = jnp.concatenate([w_scr[bb, rows, pl_], qd_scr[bb, rows, pl_]], axis=0)
            both[p] = dot(lhs, blockdiag(s_old[p]))
        for p in probs:
            bb, j = p
            pl_ = pl.ds(LANES * j, LANES)
            v_new[p] = u_scr[bb, rows, pl_] - both[p][0:chunk]
            o_in[p] = dot(in_scr[bb, rows, pl_], blockdiag(v_new[p]))
            cross = lax.dot_general(kd_scr[bb, rows, pl_], v_new[p], (((0,), (0,)), ((), ())),
                                    preferred_element_type=F32)
            s_add[p] = jnp.where(second, cross[hd:2 * hd], cross[0:hd])
        for p in probs:
            bb, j = p
            pl_ = pl.ds(LANES * j, LANES)
            oc_scr[bb, rows, pl_] = both[p][chunk:2 * chunk] + o_in[p]
            s_scr[bb, j] = s_old[p] * el_scr[bb, el_rows, pl_][0:1, :] + s_add[p]
        return carry

    assert n_chunks % group == 0
    if n_chunks == group:
        state_free_part(0, 0)
    else:
        lax.fori_loop(0, n_chunks // group, state_free_part, 0)
    if n_chunks == 1:
        state_part(0, 0)
    else:
        lax.fori_loop(0, n_chunks, state_part, 0)

    for bb in range(n_bb):
        o = oc_scr[bb]
        o = o * lax.rsqrt(_head_sums(o * o, ones_ref) * (1.0 / hd) + RMS_EPS) * nw_ref[...]
        o_ref[bb] = o * jax.nn.silu(z_ref[bb])

    @pl.when(t == pl.num_programs(1) - 1)
    def _():
        for bb in range(n_bb):
            for j in range(n_pairs):
                sfin_ref[bb, 2 * j] = s_scr[bb, j][:, 0:hd]
                sfin_ref[bb, 2 * j + 1] = s_scr[bb, j][:, hd:2 * hd]


def _gdn_core(s0_ref, z_ref, nw_ref, ones_ref, o_ref, sfin_ref, q_scr, k_scr, v_scr, v_lane0, bt_scr, g_scr,
              oc_scr, s_scr, uw_scr, in_scr, qd_scr, kd_scr, el_scr, *, chunk, group):
    n_bb, tb = q_scr.shape[0], q_scr.shape[1]
    n_chunks = tb // chunk
    hd = HEAD_DIM
    t = pl.program_id(1)

    @pl.when(t == 0)
    def _():
        s_scr[...] = s0_ref[...]

    ri = lax.broadcasted_iota(jnp.int32, (chunk, chunk), 0)
    ci = lax.broadcasted_iota(jnp.int32, (chunk, chunk), 1)
    tril = ri >= ci
    strict = ri > ci
    eye_c = (ri == ci).astype(F32)
    tril_b = tril.astype(BF16)
    n_doublings = int(math.log2(chunk)) - 1
    assert 2 ** (n_doublings + 1) == chunk

    dot = functools.partial(jnp.dot, preferred_element_type=F32)

    def state_free_part(gi, carry):
        probs, shared = [], {}
        for bb in range(n_bb):
            for cg in range(group):
                c_idx = gi * group + cg
                rows = pl.ds(_aligned(c_idx, chunk), chunk)
                gc = functools.reduce(lambda a, b: a + b,
                                      [dot(tril_b, p) for p in _split3(g_scr[bb, rows, :])])
                gct = gc.T
                g_last = gc[chunk - 1:chunk, :]
                el_rows = pl.ds(_aligned(c_idx, SUBLANES), SUBLANES)
                el_scr[bb, el_rows, :] = jnp.broadcast_to(jnp.exp(g_last), (SUBLANES, LANES))
                shared[bb, cg] = (rows, gc, gct, jnp.exp(gc), jnp.exp(g_last - gc), bt_scr[bb, rows, :])
                probs += [(bb, cg, h) for h in range(N_HEADS)]
        kk, qk, rhs, decay, a, t_inv, pw = {}, {}, {}, {}, {}, {}, {}
        for p in probs:
            bb, cg, h = p
            rows, gc, gct, e_gc, e_rem, bt = shared[bb, cg]
            hl = pl.ds(hd * h, hd)
            gl = N_HEADS + h
            qh = q_scr[bb, rows, hl]
            kh = k_scr[bb, rows, hl]
            beta = bt[:, h:h + 1]
            kb = kh * beta
            kk[p] = _nt_dot(kb, kh)
            qk[p] = _nt_dot(qh, kh)
            v_h = v_scr[bb, rows, pl.ds(v_lane0 + hd * h, hd)]
            rhs[p] = jnp.concatenate([v_h * beta, kb * e_gc[:, gl:gl + 1]], axis=1)
            qd_scr[bb, rows, hl] = qh * e_gc[:, gl:gl + 1]
            kd_scr[bb, rows, hl] = kh * e_rem[:, gl:gl + 1]
            diff = gc[:, gl:gl + 1] - gct[gl:gl + 1, :]
            decay[p] = jnp.where(tril, jnp.exp(jnp.where(tril, diff, 0.0)), 0.0)
        for p in probs:
            bb, cg, h = p
            rows = shared[bb, cg][0]
            a[p] = jnp.where(strict, kk[p] * decay[p], 0.0)
            in_scr[bb, rows, LANES * h:LANES * h + chunk] = jnp.where(tril, qk[p] * decay[p], 0.0)
            t_inv[p] = eye_c - a[p]
        for p in probs:
            pw[p] = dot(a[p], a[p])
        for _ in range(n_doublings - 1):
            nxt_pw, nxt_t = {}, {}
            for p in probs:
                nxt_pw[p] = dot(pw[p], pw[p])
                nxt_t[p] = dot(t_inv[p], eye_c + pw[p])
            pw, t_inv = nxt_pw, nxt_t
        for p in probs:
            t_inv[p] = dot(t_inv[p], eye_c + pw[p])
        for p in probs:
            bb, cg, h = p
            rows = shared[bb, cg][0]
            uw_scr[bb, rows, LANES * h:LANES * (h + 1)] = dot(t_inv[p], rhs[p])
        return carry

    def state_part(ci_, carry):
        rows = pl.ds(_aligned(ci_, chunk), chunk)
        el_rows = pl.ds(_aligned(ci_, SUBLANES), SUBLANES)
        probs = [(bb, h) for bb in range(n_bb) for h in range(N_HEADS)]
        s_old, uw, both, v_new, o_in, s_add = {}, {}, {}, {}, {}, {}
        for p in probs:
            bb, h = p
            s_old[p] = s_scr[bb, h]
            uw[p] = uw_scr[bb, rows, LANES * h:LANES * (h + 1)]
            lhs = jnp.concatenate([uw[p][:, hd:2 * hd], qd_scr[bb, rows, pl.ds(hd * h, hd)]], axis=0)
            both[p] = dot(lhs, s_old[p])
        for p in probs:
            bb, h = p
            v_new[p] = uw[p][:, 0:hd] - both[p][0:chunk]
            o_in[p] = dot(in_scr[bb, rows, LANES * h:LANES * h + chunk], v_new[p])
            s_add[p] = lax.dot_general(kd_scr[bb, rows, pl.ds(hd * h, hd)], v_new[p],
                                       (((0,), (0,)), ((), ())), preferred_element_type=F32)
        for p in probs:
            bb, h = p
            e_last = el_scr[bb, el_rows, :][0:1, N_HEADS + h:N_HEADS + h + 1]
            oc_scr[bb, rows, pl.ds(hd * h, hd)] = both[p][chunk:2 * chunk] + o_in[p]
            s_scr[bb, h] = s_old[p] * e_last + s_add[p]
        return carry

    assert n_chunks % group == 0
    if n_chunks == group:
        state_free_part(0, 0)
    else:
        lax.fori_loop(0, n_chunks // group, state_free_part, 0)
    if n_chunks == 1:
        state_part(0, 0)
    else:
        lax.fori_loop(0, n_chunks, state_part, 0)

    for bb in range(n_bb):
        o = oc_scr[bb]
        o = o * lax.rsqrt(_head_sums(o * o, ones_ref) * (1.0 / hd) + RMS_EPS) * nw_ref[...]
        o_ref[bb] = o * jax.nn.silu(z_ref[bb])

    @pl.when(t == pl.num_programs(1) - 1)
    def _():
        sfin_ref[...] = s_scr[...]


def _gdn_core_scratch(nbb, tb, chunk):
    hw = HEAD_WIDTH
    return ([pltpu.VMEM((nbb, tb, hw), F32)]
            + [pltpu.VMEM((nbb, N_HEADS, HEAD_DIM, HEAD_DIM), F32)]
            + [pltpu.VMEM((nbb, tb, N_HEADS * LANES), F32)] * 2
            + [pltpu.VMEM((nbb, tb, hw), F32)] * 2
            + [pltpu.VMEM((nbb, (tb // chunk) * SUBLANES, LANES), F32)])


def _head_ones():
    ones_bd = np.kron(np.eye(N_HEADS, dtype=np.float32), np.ones((HEAD_DIM, HEAD_DIM), np.float32))
    return jnp.asarray(ones_bd, BF16)


def _gdn_call(gq, z, ba, conv_buf8, s0, conv_w8, head_params, norm_w_row, n_valid, token_block, chunk,
              batch_block, group):
    bsz, seq, w3 = gq.shape
    hw = HEAD_WIDTH
    assert w3 == 3 * hw and seq % token_block == 0 and token_block % chunk == 0 and bsz % batch_block == 0
    tb = token_block
    nbb = batch_block
    ones_bd = _head_ones()
    blk = lambda w: pl.BlockSpec((nbb, tb, w), lambda b, t: (b, t, 0))
    per_b3 = lambda s: pl.BlockSpec((nbb,) + s, lambda b, t: (b,) + (0,) * len(s))
    return pl.pallas_call(
        functools.partial(_gdn_kernel, chunk=chunk, n_valid=n_valid, group=group),
        grid=(bsz // nbb, seq // tb),
        in_specs=[blk(w3), blk(hw), blk(LANES), per_b3((SUBLANES, w3)), per_b3((N_HEADS, HEAD_DIM, HEAD_DIM)),
                  _resident(conv_w8.shape), _resident(head_params.shape), _resident(norm_w_row.shape),
                  _resident(ones_bd.shape)],
        out_specs=[blk(hw), per_b3((N_HEADS, HEAD_DIM, HEAD_DIM))],
        out_shape=[jax.ShapeDtypeStruct((bsz, seq, hw), F32),
                   jax.ShapeDtypeStruct((bsz, N_HEADS, HEAD_DIM, HEAD_DIM), F32)],
        scratch_shapes=[pltpu.VMEM((nbb, tb + 2 * SUBLANES, w3), F32)]
                       + [pltpu.VMEM((nbb, tb, hw), F32)] * 3
                       + [pltpu.VMEM((nbb, tb, LANES), F32)] * 2
                       + _gdn_core_scratch(nbb, tb, chunk),
        compiler_params=pltpu.CompilerParams(dimension_semantics=("parallel", "arbitrary"),
                                             vmem_limit_bytes=VMEM_LIMIT),
        name="gdn",
    )(gq, z, ba, conv_buf8, s0, conv_w8, head_params, norm_w_row, ones_bd)


def _gdn_pairs_call(qkv, z, beta, g, s0, norm_w_row, token_block, group):
    bsz, seq, w3 = qkv.shape
    hw = HEAD_WIDTH
    chunk = HEAD_DIM
    nbb = GDN_BATCH_BLOCK
    assert w3 == 3 * hw and seq % token_block == 0 and token_block % (chunk * group) == 0 and bsz % nbb == 0
    tb = token_block
    ones_bd = _head_ones()
    blk = lambda w: pl.BlockSpec((nbb, tb, w), lambda b, t: (b, t, 0))
    per_b3 = lambda s: pl.BlockSpec((nbb,) + s, lambda b, t: (b,) + (0,) * len(s))
    return pl.pallas_call(
        functools.partial(_gdn_pairs_kernel, group=group),
        grid=(bsz // nbb, seq // tb),
        in_specs=[blk(w3), blk(hw), blk(LANES), blk(LANES), per_b3((N_HEADS, HEAD_DIM, HEAD_DIM)),
                  _resident(norm_w_row.shape), _resident(ones_bd.shape)],
        out_specs=[blk(hw), per_b3((N_HEADS, HEAD_DIM, HEAD_DIM))],
        out_shape=[jax.ShapeDtypeStruct((bsz, seq, hw), F32),
                   jax.ShapeDtypeStruct((bsz, N_HEADS, HEAD_DIM, HEAD_DIM), F32)],
        scratch_shapes=[pltpu.VMEM((nbb, tb, hw), F32)]
                       + [pltpu.VMEM((nbb, N_HEADS // 2, HEAD_DIM, LANES), F32)]
                       + [pltpu.VMEM((nbb, tb, hw), F32)] * 5
                       + [pltpu.VMEM((nbb, (tb // chunk) * SUBLANES, hw), F32)],
        compiler_params=pltpu.CompilerParams(dimension_semantics=("parallel", "arbitrary"),
                                             vmem_limit_bytes=VMEM_LIMIT),
        name="gdn_pairs",
    )(qkv, z, beta, g, s0, norm_w_row, ones_bd)


def _gdn_step_kernel(xq_ref, xk_ref, xv_ref, cq_ref, ck_ref, cv_ref, wq_ref, wk_ref, wv_ref, z_ref, ba_ref, hp_ref,
                     nw_ref, s0_ref, o_ref, s_ref, k_scr, q_scr):
    n_tok, hd, _ = xq_ref.shape
    h = pl.program_id(0)
    col_sum = lambda x: jnp.sum(x, axis=0, keepdims=True)

    def conv_act(x_ref, c_ref, w_ref, t):
        planes = [c_ref[i] for i in range(CONV_WIDTH - 1)] + [x_ref[i] for i in range(t + 1)]
        taps = planes[t:t + CONV_WIDTH]
        acc = jnp.zeros(taps[0].shape, F32)
        for j in range(CONV_WIDTH):
            acc = acc + taps[j] * w_ref[j]
        return jax.nn.silu(acc)

    s_ref[...] = s0_ref[...]
    neg_rate = -jnp.exp(hp_ref[0, pl.ds(h, 1), :])
    dt_bias = hp_ref[1, pl.ds(h, 1), :]
    for t in range(n_tok):
        q = conv_act(xq_ref, cq_ref, wq_ref, t)
        k = conv_act(xk_ref, ck_ref, wk_ref, t)
        v = conv_act(xv_ref, cv_ref, wv_ref, t)
        q_scr[...] = q * lax.rsqrt(col_sum(q * q) + RMS_EPS) * hd ** -0.5
        k_scr[...] = k * lax.rsqrt(col_sum(k * k) + RMS_EPS)
        beta = jax.nn.sigmoid(ba_ref[t, pl.ds(h, 1), :])
        decay = jnp.exp(neg_rate * jax.nn.softplus(ba_ref[t, pl.ds(N_HEADS + h, 1), :] + dt_bias))

        def k_dot_state(d, acc):
            return acc + k_scr[pl.ds(d, 1), :] * s_ref[0, d]

        ks = lax.fori_loop(0, hd, k_dot_state, jnp.zeros((hd, xq_ref.shape[2]), F32), unroll=8)
        delta = beta * (v - decay * ks)

        def update(d, acc):
            s_new = decay * s_ref[0, d] + k_scr[pl.ds(d, 1), :] * delta
            s_ref[0, d] = s_new
            return acc + q_scr[pl.ds(d, 1), :] * s_new

        o = lax.fori_loop(0, hd, update, jnp.zeros((hd, xq_ref.shape[2]), F32), unroll=8)
        o = o * lax.rsqrt(col_sum(o * o) * (1.0 / hd) + RMS_EPS) * nw_ref[...]
        o_ref[t] = o * jax.nn.silu(z_ref[t])


def _gdn_step_call(x_t, conv_t, conv_w_t, z_t, ba_t, hp_t, nw_t, s0_t):
    n_tok, w3, bsz = x_t.shape
    hd = HEAD_DIM
    part = lambda rows, p: pl.BlockSpec((rows, hd, bsz), lambda h, p=p: (0, p * N_HEADS + h, 0))
    whole = lambda a: pl.BlockSpec(a.shape, lambda h: (0,) * a.ndim)
    state_spec = pl.BlockSpec((1, hd, hd, bsz), lambda h: (h, 0, 0, 0))
    return pl.pallas_call(
        _gdn_step_kernel,
        grid=(N_HEADS,),
        in_specs=[part(n_tok, 0), part(n_tok, 1), part(n_tok, 2),
                  part(CONV_WIDTH - 1, 0), part(CONV_WIDTH - 1, 1), part(CONV_WIDTH - 1, 2),
                  part(CONV_WIDTH, 0), part(CONV_WIDTH, 1), part(CONV_WIDTH, 2),
                  part(n_tok, 0), whole(ba_t), whole(hp_t), whole(nw_t), state_spec],
        out_specs=[part(n_tok, 0), state_spec],
        out_shape=[jax.ShapeDtypeStruct((n_tok, N_HEADS * hd, bsz), F32),
                   jax.ShapeDtypeStruct(s0_t.shape, F32)],
        scratch_shapes=[pltpu.VMEM((hd, bsz), F32)] * 2,
        compiler_params=pltpu.CompilerParams(dimension_semantics=("parallel",), vmem_limit_bytes=VMEM_LIMIT),
        name="gdn_step",
    )(x_t, x_t, x_t, conv_t, conv_t, conv_t, conv_w_t, conv_w_t, conv_w_t, z_t, ba_t, hp_t, nw_t, s0_t)


def _pad_rows(x, rows, front=False):
    extra = rows - x.shape[1]
    cfg = [(0, 0)] * x.ndim
    cfg[1] = (extra, 0) if front else (0, extra)
    return jnp.pad(x, cfg)


def kernel(x_prompt, x_sample, cache_attn_k, cache_attn_v, state_gdn, state_conv, rel_bias, ln1_g, ln1_b,
           ffn1_w_gate, ffn1_w_up, ffn1_w_down, w_in, w_out, gdn_conv_w, gdn_a_log, gdn_dt_bias, gdn_norm_w,
           ln2_g, ln2_b, ffn2_w_gate, ffn2_w_up, ffn2_w_down, ln3_g, ln3_b):
    depth = w_in.shape[0]
    alpha = (2.0 * depth) ** 0.25
    bsz, seq, d_model = x_prompt.shape
    dbsz, dseq, _ = x_sample.shape
    w_buf = cache_attn_k.shape[2]
    hw = HEAD_WIDTH
    in_cols = w_in.shape[2]
    assert in_cols == 7 * hw + 2 * N_HEADS and CONV_WIDTH - 1 <= dseq <= SAMPLE_T_PAD
    assert seq % GDN_TOKEN_BLOCK == 0 and GDN_TOKEN_BLOCK % (GDN_CHUNK * GDN_CHUNK_GROUP) == 0

    sample_lows, sample_buckets = _sample_layout(w_buf, dseq)
    tab_p = _t5_table_call(rel_bias, jnp.asarray(_prompt_buckets()))
    tab_s = _t5_table_call(rel_bias, jnp.asarray(sample_buckets))

    yp = x_prompt.reshape(bsz * seq, d_model)
    ys = x_sample.reshape(dbsz * dseq, d_model)
    collected = [[] for _ in range(8)]
    row = lambda v: v.reshape(1, -1)
    for layer in range(depth):
        wg1, wu1, wd1 = (w[layer].astype(BF16) for w in (ffn1_w_gate, ffn1_w_up, ffn1_w_down))
        wg2, wu2, wd2 = (w[layer].astype(BF16) for w in (ffn2_w_gate, ffn2_w_up, ffn2_w_down))
        win = jnp.pad(w_in[layer], ((0, 0), (0, 7 * hw + LANES - in_cols))).astype(BF16)
        wo = w_out[layer].astype(BF16)
        conv_w8 = jnp.pad(gdn_conv_w[layer], ((0, SUBLANES - CONV_WIDTH), (0, 0)))
        head_params = jnp.zeros((SUBLANES, LANES), F32)
        head_params = head_params.at[0, N_HEADS:2 * N_HEADS].set(gdn_a_log[layer])
        head_params = head_params.at[1, N_HEADS:2 * N_HEADS].set(gdn_dt_bias[layer])
        norm_w_row = jnp.tile(gdn_norm_w[layer], N_HEADS).reshape(1, hw)
        pre = functools.partial(_pre_call, wg=wg1, wu=wu1, wd=wd1, g=row(ln1_g[layer]), b=row(ln1_b[layer]),
                                win=win, alpha=alpha)
        post = functools.partial(_post_call, wo=wo, g2=row(ln2_g[layer]), b2=row(ln2_b[layer]), wg=wg2, wu=wu2,
                                 wd=wd2, g3=row(ln3_g[layer]), b3=row(ln3_b[layer]), alpha=alpha)

        x1, aq, ak, av, ak_t, av_t, gdn_qkv, z, beta, gate, gq_tail = _pre_prompt_call(
            yp, wg1, wu1, wd1, row(ln1_g[layer]), row(ln1_b[layer]), win, conv_w8, head_params, alpha, seq)
        shp = lambda a: a.reshape(bsz, seq, a.shape[1])
        attn = _attn_prompt_call(shp(aq), shp(ak), shp(av), tab_p)
        gdn, s_p = _gdn_pairs_call(shp(gdn_qkv), shp(z), shp(beta), shp(gate),
                                   jnp.zeros((bsz, N_HEADS, HEAD_DIM, HEAD_DIM), F32), norm_w_row,
                                   token_block=GDN_TOKEN_BLOCK, group=GDN_CHUNK_GROUP)
        wp = min(w_buf, seq)
        heads5 = lambda a_t: a_t.reshape(bsz, N_HEADS, HEAD_DIM, seq).transpose(0, 3, 1, 2)[:, seq - wp:]
        st_p = (heads5(ak_t), heads5(av_t), s_p, gq_tail[:, SUBLANES - (CONV_WIDTH - 1):])

        x1s, aqs, aks, avs, gqs, zs, bas = pre(ys)
        shs = lambda a: _pad_rows(a.reshape(dbsz, dseq, a.shape[1]), SAMPLE_T_PAD)
        heads5s = lambda a: a.reshape(dbsz, dseq, N_HEADS, HEAD_DIM)
        new_rows = lambda a: jnp.pad(heads5s(a).transpose(0, 2, 1, 3),
                                     ((0, 0), (0, 0), (0, SAMPLE_T_PAD - dseq), (0, 0)))
        by_head_t = lambda c: c.transpose(0, 2, 3, 1)
        yp, attn_s = _post_with_sample_attn_call(
            attn.reshape(bsz * seq, hw), gdn.reshape(bsz * seq, hw), x1, wo, row(ln2_g[layer]), row(ln2_b[layer]),
            wg2, wu2, wd2, row(ln3_g[layer]), row(ln3_b[layer]), alpha,
            shs(aqs), new_rows(aks), new_rows(avs), by_head_t(cache_attn_k[layer]), by_head_t(cache_attn_v[layer]),
            tab_s, sample_lows)
        to_lanes = lambda a: a.reshape(dbsz, dseq * a.shape[1]).T.reshape(dseq, a.shape[1], dbsz)
        on_lanes = lambda a: jnp.broadcast_to(a[..., None], a.shape + (dbsz,))
        gdn_t, s_t = _gdn_step_call(
            to_lanes(gqs), state_conv[layer].transpose(1, 2, 0), on_lanes(gdn_conv_w[layer]), to_lanes(zs),
            to_lanes(bas[:, :2 * N_HEADS]), on_lanes(jnp.stack([gdn_a_log[layer], gdn_dt_bias[layer]])),
            on_lanes(gdn_norm_w[layer]), state_gdn[layer].transpose(1, 2, 3, 0))
        gdn_s = gdn_t.reshape(dseq * hw, dbsz).T.reshape(dbsz * dseq, hw)
        s_s = s_t.transpose(3, 0, 1, 2)
        ys = post(attn_s[:, :dseq].reshape(dbsz * dseq, hw), gdn_s, x1s)
        st_s = (heads5s(aks), heads5s(avs), s_s,
                gqs.reshape(dbsz, dseq, 3 * hw)[:, dseq - (CONV_WIDTH - 1):])
        for lst, st in zip(collected, st_p + st_s):
            lst.append(st)
    outs = [jnp.stack(t, axis=0) for t in collected]
    return (yp.reshape(bsz, seq, d_model), ys.reshape(dbsz, dseq, d_model)) + tuple(outs)
```

```python
import functools
import math

import numpy as np
import jax
import jax.numpy as jnp
from jax import lax
from jax.experimental import pallas as pl
from jax.experimental.pallas import tpu as pltpu

F32 = jnp.float32
BF16 = jnp.bfloat16

HEAD_DIM = 64
N_HEADS = 8
HEAD_WIDTH = N_HEADS * HEAD_DIM
DILATED_PATTERNS = ((128, 1), (512, 4), (2048, 16))
KEYS_PER_BLOCK = 128
T5_BUCKETS = 32
T5_MAX_EXACT = 16
T5_MAX_DIST = 2048
CONV_WIDTH = 4
GDN_CHUNK = 64
LN_EPS = 1e-5
RMS_EPS = 1e-6
NEG_BIG = -1e30

LANES = 128
SUBLANES = 8
TOKEN_TILE = 256
POST_TOKEN_TILE = 512
GDN_TOKEN_BLOCK = 256
ATTN_SAMPLE_HEAD_GROUPS = 1
ATTN_SKEW_GROUPS = 1
GDN_CHUNK_GROUP = 4
GDN_BATCH_BLOCK = 4
GDN_SAMPLE_BATCH_BLOCK = 8
SAMPLE_T_PAD = 8
VMEM_LIMIT = 56 * 1024 * 1024


def _resident(shape):
    nd = len(shape)
    return pl.BlockSpec(shape, lambda *_: (0,) * nd, pipeline_mode=pl.Buffered(1))


def _layernorm(y, g, b):
    mu = jnp.mean(y, axis=-1, keepdims=True)
    yc = y - mu
    var = jnp.mean(yc * yc, axis=-1, keepdims=True)
    return yc * lax.rsqrt(var + LN_EPS) * g + b


def _swiglu(xb, wg_ref, wu_ref, wd_ref, h_scr, side_jobs=()):
    d_ff = wg_ref.shape[1]
    step = 2 * LANES
    n_chunks = d_ff // step
    assert d_ff % step == 0 and len(side_jobs) <= n_chunks
    job_after = {(i + 1) * n_chunks // (len(side_jobs) + 1) - 1: job for i, job in enumerate(side_jobs)}
    for c in range(n_chunks):
        sl = slice(c * step, (c + 1) * step)
        gate = jnp.dot(xb, wg_ref[:, sl], preferred_element_type=F32)
        up = jnp.dot(xb, wu_ref[:, sl], preferred_element_type=F32)
        h_scr[:, sl] = (jax.nn.silu(gate) * up).astype(BF16)
        if c in job_after:
            job_after[c]()
    return jnp.dot(h_scr[...], wd_ref[...], preferred_element_type=F32)


PROJ_WIDTHS = (HEAD_WIDTH, HEAD_WIDTH, HEAD_WIDTH, 3 * HEAD_WIDTH, HEAD_WIDTH, LANES)
PROJ_EDGES = tuple(int(e) for e in np.cumsum((0,) + PROJ_WIDTHS))


def _first_half_step(x_ref, wg_ref, wu_ref, wd_ref, g_ref, b_ref, h_scr, alpha):
    x = x_ref[...]
    ff = _swiglu(x.astype(BF16), wg_ref, wu_ref, wd_ref, h_scr)
    return _layernorm(alpha * x + 0.5 * ff, g_ref[...], b_ref[...])


def _pre_kernel(x_ref, wg_ref, wu_ref, wd_ref, g_ref, b_ref, win_ref,
                x1_ref, aq_ref, ak_ref, av_ref, gq_ref, z_ref, ba_ref, h_scr, *, alpha):
    x1 = _first_half_step(x_ref, wg_ref, wu_ref, wd_ref, g_ref, b_ref, h_scr, alpha)
    x1_ref[...] = x1
    xb = x1.astype(BF16)
    for i, ref in enumerate((aq_ref, ak_ref, av_ref, gq_ref, z_ref, ba_ref)):
        ref[...] = jnp.dot(xb, win_ref[:, PROJ_EDGES[i]:PROJ_EDGES[i + 1]], preferred_element_type=F32)


def _gdn_gates(ba, hp_ref):
    beta = jax.nn.sigmoid(ba)
    g = -jnp.exp(hp_ref[0:1, :]) * jax.nn.softplus(ba + hp_ref[1:2, :])
    return beta, g


def _causal_conv_silu(ext_scr, x, cw_ref):
    tb = x.shape[0]
    hist = SUBLANES
    ext_scr[hist:hist + tb, :] = x
    ext = ext_scr[0:hist + tb, :]
    conv = jnp.zeros(x.shape, F32)
    for j in range(CONV_WIDTH):
        back = CONV_WIDTH - 1 - j
        rows = pltpu.roll(ext, back, axis=0)[hist:hist + tb, :] if back else x
        conv = conv + rows * cw_ref[j:j + 1, :]
    ext_scr[0:hist, :] = ext_scr[tb:tb + hist, :]
    return jax.nn.silu(conv)


def _pre_prompt_kernel(x_ref, wg_ref, wu_ref, wd_ref, g_ref, b_ref, win_ref, cw_ref, hp_ref, ones_ref,
                       x1_ref, aq_ref, ak_ref, av_ref, akt_ref, avt_ref, qkv_ref, z_ref, bt_ref, gg_ref, tail_ref,
                       h_scr, ext_scr, *, alpha, tiles_per_seq):
    x1 = _first_half_step(x_ref, wg_ref, wu_ref, wd_ref, g_ref, b_ref, h_scr, alpha)
    x1_ref[...] = x1
    xb = x1.astype(BF16)
    proj = lambda i: jnp.dot(xb, win_ref[:, PROJ_EDGES[i]:PROJ_EDGES[i + 1]], preferred_element_type=F32)

    @pl.when(pl.program_id(0) % tiles_per_seq == 0)
    def _():
        ext_scr[0:SUBLANES, :] = jnp.zeros((SUBLANES, ext_scr.shape[1]), F32)

    raw = proj(3)
    tail_ref[0] = raw[raw.shape[0] - SUBLANES:, :]
    act = _causal_conv_silu(ext_scr, raw, cw_ref)
    hw = HEAD_WIDTH
    head_sq = lambda x: jnp.dot((x * x).astype(BF16), ones_ref[...], preferred_element_type=F32)
    q = act[:, 0:hw]
    k = act[:, hw:2 * hw]
    qkv_ref[:, 0:hw] = q * lax.rsqrt(head_sq(q) + RMS_EPS) * HEAD_DIM ** -0.5
    qkv_ref[:, hw:2 * hw] = k * lax.rsqrt(head_sq(k) + RMS_EPS)
    qkv_ref[:, 2 * hw:3 * hw] = act[:, 2 * hw:3 * hw]
    bt_ref[...], gg_ref[...] = _gdn_gates(proj(5), hp_ref)
    aq_ref[...] = proj(0)
    for i, ref, ref_t in ((1, ak_ref, akt_ref), (2, av_ref, avt_ref)):
        kv = proj(i)
        ref[...] = kv
        ref_t[0] = kv.T
    z_ref[...] = proj(4)


def _pre_call(x2d, wg, wu, wd, g, b, win, alpha):
    m, d = x2d.shape
    d_ff = wg.shape[1]
    tm = TOKEN_TILE
    assert m % tm == 0 and sum(PROJ_WIDTHS) == win.shape[1]
    tile = lambda w: pl.BlockSpec((tm, w), lambda i: (i, 0))
    return pl.pallas_call(
        functools.partial(_pre_kernel, alpha=alpha),
        grid=(m // tm,),
        in_specs=[tile(d), _resident(wg.shape), _resident(wu.shape), _resident(wd.shape),
                  _resident(g.shape), _resident(b.shape), _resident(win.shape)],
        out_specs=[tile(d)] + [tile(w) for w in PROJ_WIDTHS],
        out_shape=[jax.ShapeDtypeStruct((m, d), F32)] + [jax.ShapeDtypeStruct((m, w), F32) for w in PROJ_WIDTHS],
        scratch_shapes=[pltpu.VMEM((tm, d_ff), BF16)],
        compiler_params=pltpu.CompilerParams(dimension_semantics=("parallel",), vmem_limit_bytes=VMEM_LIMIT),
        name="pre_ffn_proj",
    )(x2d, wg, wu, wd, g, b, win)


def _pre_prompt_call(x2d, wg, wu, wd, g, b, win, conv_w8, head_params, alpha, seq):
    m, d = x2d.shape
    d_ff = wg.shape[1]
    tm = TOKEN_TILE
    hw = HEAD_WIDTH
    assert seq % tm == 0 and m % seq == 0 and sum(PROJ_WIDTHS) == win.shape[1]
    per_seq = seq // tm
    tile = lambda w: pl.BlockSpec((tm, w), lambda i: (i, 0))
    t_spec = pl.BlockSpec((1, hw, tm), lambda i: (i // per_seq, 0, i % per_seq))
    t_shape = jax.ShapeDtypeStruct((m // seq, hw, seq), F32)
    rows = lambda w: jax.ShapeDtypeStruct((m, w), F32)
    consts = (wg, wu, wd, g, b, win, conv_w8, head_params, _head_ones())
    return pl.pallas_call(
        functools.partial(_pre_prompt_kernel, alpha=alpha, tiles_per_seq=per_seq),
        grid=(m // tm,),
        in_specs=[tile(d)] + [_resident(c.shape) for c in consts],
        out_specs=[tile(d), tile(hw), tile(hw), tile(hw), t_spec, t_spec, tile(3 * hw), tile(hw), tile(LANES),
                   tile(LANES), pl.BlockSpec((1, SUBLANES, 3 * hw), lambda i: (i // per_seq, 0, 0))],
        out_shape=[rows(d), rows(hw), rows(hw), rows(hw), t_shape, t_shape, rows(3 * hw), rows(hw), rows(LANES),
                   rows(LANES), jax.ShapeDtypeStruct((m // seq, SUBLANES, 3 * hw), F32)],
        scratch_shapes=[pltpu.VMEM((tm, d_ff), BF16), pltpu.VMEM((tm + 2 * SUBLANES, 3 * hw), F32)],
        compiler_params=pltpu.CompilerParams(dimension_semantics=("arbitrary",), vmem_limit_bytes=VMEM_LIMIT),
        name="pre_ffn_proj_gdnprep",
    )(x2d, *consts)


def _post_kernel(attn_ref, gdn_ref, x1_ref, wo_ref, g2_ref, b2_ref, wg_ref, wu_ref, wd_ref, g3_ref, b3_ref,
                 y_ref, h_scr, *, alpha, side_jobs=()):
    tm = y_ref.shape[0]
    n_split = 2 if tm >= 2 * TOKEN_TILE else 1
    halves = [pl.ds(i * (tm // n_split), tm // n_split) for i in range(n_split)]
    mix, x2, ff = {}, {}, {}
    for i, rows in enumerate(halves):
        mix[i] = (jnp.dot(attn_ref[rows, :].astype(BF16), wo_ref[0:HEAD_WIDTH, :], preferred_element_type=F32)
                  + jnp.dot(gdn_ref[rows, :].astype(BF16), wo_ref[HEAD_WIDTH:2 * HEAD_WIDTH, :],
                            preferred_element_type=F32))
    for i, rows in enumerate(halves):
        x2[i] = _layernorm(alpha * x1_ref[rows, :] + mix[i], g2_ref[...], b2_ref[...])
    for i, rows in enumerate(halves):
        ff[i] = _swiglu(x2[i].astype(BF16), wg_ref, wu_ref, wd_ref, h_scr.at[rows],
                        side_jobs=side_jobs if i == 0 else ())
    for i, rows in enumerate(halves):
        y_ref[rows, :] = _layernorm(alpha * x2[i] + 0.5 * ff[i], g3_ref[...], b3_ref[...])


def _post_call(attn, gdn, x1, wo, g2, b2, wg, wu, wd, g3, b3, alpha):
    m, d = x1.shape
    tm = POST_TOKEN_TILE
    assert m % tm == 0
    tile = lambda w: pl.BlockSpec((tm, w), lambda i: (i, 0))
    consts = (wo, g2, b2, wg, wu, wd, g3, b3)
    return pl.pallas_call(
        functools.partial(_post_kernel, alpha=alpha),
        grid=(m // tm,),
        in_specs=[tile(HEAD_WIDTH), tile(HEAD_WIDTH), tile(d)] + [_resident(c.shape) for c in consts],
        out_specs=tile(d),
        out_shape=jax.ShapeDtypeStruct((m, d), F32),
        scratch_shapes=[pltpu.VMEM((tm, wg.shape[1]), BF16)],
        compiler_params=pltpu.CompilerParams(dimension_semantics=("parallel",), vmem_limit_bytes=VMEM_LIMIT),
        name="post_out_ffn",
    )(attn, gdn, x1, *consts)


def _t5_bucket_np(dist):
    n = np.maximum(dist, 0)
    nf = np.maximum(n, 1).astype(np.float32)
    large = T5_MAX_EXACT + (np.log(nf / np.float32(T5_MAX_EXACT)) / np.float32(math.log(T5_MAX_DIST / T5_MAX_EXACT))
                            * np.float32(T5_BUCKETS - T5_MAX_EXACT)).astype(np.int32)
    large = np.minimum(large, T5_BUCKETS - 1)
    return np.where(n < T5_MAX_EXACT, n, large).astype(np.int32)


def _t5_table_kernel(rb_ref, bk_ref, out_ref):
    for h in range(N_HEADS):
        bk = bk_ref[0, h % bk_ref.shape[1]]
        acc = jnp.full(bk.shape, NEG_BIG, F32)
        for b in range(T5_BUCKETS):
            acc = jnp.where(bk == b, rb_ref[b, h], acc)
        out_ref[0, h] = acc


def _t5_table_call(rel_bias, buckets):
    p, hb, r, c = buckets.shape
    assert hb in (1, N_HEADS)
    return pl.pallas_call(
        _t5_table_kernel,
        grid=(p,),
        in_specs=[pl.BlockSpec(memory_space=pltpu.SMEM),
                  pl.BlockSpec((1, hb, r, c), lambda i: (i, 0, 0, 0))],
        out_specs=pl.BlockSpec((1, N_HEADS, r, c), lambda i: (i, 0, 0, 0)),
        out_shape=jax.ShapeDtypeStruct((p, N_HEADS, r, c), F32),
        compiler_params=pltpu.CompilerParams(dimension_semantics=("parallel",)),
        name="t5_bias_table",
    )(rel_bias, buckets)


def _prompt_buckets():
    nb = KEYS_PER_BLOCK
    qi = np.arange(nb)[:, None]
    ki = np.arange(2 * nb)[None, :]
    dist = qi + nb - ki
    valid = (dist >= 0) & (dist <= nb)
    out = []
    for window, dil in DILATED_PATTERNS:
        assert window // dil == nb
        out.append(np.where(valid, _t5_bucket_np(dist * dil), -1))
    return np.stack(out).astype(np.int32)[:, None]


def _nt_dot(a, b):
    return lax.dot_general(a, b, (((1,), (1,)), ((), ())), preferred_element_type=F32)


def _emit_skewed(stages, items, n_groups):
    n_groups = max(1, min(n_groups, len(items)))
    groups = [items[i::n_groups] for i in range(n_groups)]
    for step in range(len(stages) + n_groups - 1):
        for gi, group in enumerate(groups):
            s = step - gi
            if 0 <= s < len(stages):
                for item in group:
                    stages[s](item)


def _attn_prompt_kernel(q_ref, k_ref, v_ref, tab_ref, o_ref, og_scr, lg_scr, *, seq):
    nb = KEYS_PER_BLOCK
    hd = HEAD_DIM
    n_pat = len(DILATED_PATTERNS)
    heads = LANES // hd
    for g, (window, dil) in enumerate(DILATED_PATTERNS):
        n_blocks = seq // (dil * nb)

        def blocks(starts, with_prev, g=g, dil=dil):
            lane_head = lax.broadcasted_iota(jnp.int32, (nb, LANES), 1) // hd
            n_keys = 2 * nb if with_prev else nb
            row_sl, qs, ks, vs = [], [], [], []
            for start in starts:
                first_key = start - dil * (n_keys - nb)
                row_sl.append(pl.ds(start, nb, stride=dil) if dil > 1 else pl.ds(start, nb))
                key_rows = pl.ds(first_key, n_keys, stride=dil) if dil > 1 else pl.ds(first_key, n_keys)
                qs.append(q_ref[0, row_sl[-1], :] * hd ** -0.5)
                ks.append(k_ref[0, key_rows, :].astype(BF16))
                vs.append(v_ref[0, key_rows, :].astype(BF16))
            chains = [(u, hh) for u in range(len(starts)) for hh in range(heads)]
            s, m, p, den, acc = {}, {}, {}, {}, {}

            def scores(c):
                u, hh = c
                qb = jnp.where(lane_head == hh, qs[u], 0.0).astype(BF16)
                s[c] = _nt_dot(qb, ks[u]) + tab_ref[g, hh, :, 2 * nb - n_keys:2 * nb]

            def row_max(c):
                m[c] = jnp.max(s[c], axis=-1, keepdims=True)

            def probs(c):
                p[c] = jnp.exp(s[c] - m[c])
                den[c] = jnp.sum(p[c], axis=-1, keepdims=True)

            def values(c):
                u, hh = c
                acc[c] = jnp.dot(p[c].astype(BF16), vs[u], preferred_element_type=F32)

            def finish(c):
                acc[c] = acc[c] / den[c]
                den[c] = m[c] + jnp.log(den[c])

            _emit_skewed((scores, row_max, probs, values, finish), chains, ATTN_SKEW_GROUPS)
            for u in range(len(starts)):
                out, lse = acc[u, 0], den[u, 0]
                for hh in range(1, heads):
                    out = jnp.where(lane_head == hh, acc[u, hh], out)
                    lse = jnp.where(lane_head == hh, den[u, hh], lse)
                og_scr[g, row_sl[u], :] = out
                lg_scr[g, row_sl[u], :] = jnp.broadcast_to(lse, (nb, LANES))

        def unroll_of(count):
            return next(u for u in (8, 6, 5, 4, 3, 2, 1) if count % u == 0)

        u_first = unroll_of(dil)

        def first_body(i, carry, blocks=blocks, u_first=u_first):
            blocks([i * u_first + j for j in range(u_first)], False)
            return carry

        lax.fori_loop(0, dil // u_first, first_body, 0)
        if n_blocks > 1:
            n_rest = dil * (n_blocks - 1)
            u_rest = unroll_of(n_rest)

            def rest_body(i, carry, blocks=blocks, dil=dil, n_blocks=n_blocks, u_rest=u_rest):
                starts = []
                for j in range(u_rest):
                    idx = i * u_rest + j
                    starts.append(idx // (n_blocks - 1) + dil * nb * (idx % (n_blocks - 1) + 1))
                blocks(starts, True)
                return carry

            lax.fori_loop(0, n_rest // u_rest, rest_body, 0)

    rb = 2 * nb

    def merge_body(i, carry):
        rows = pl.ds(pl.multiple_of(i * rb, rb), rb)
        lses = [lg_scr[g, rows, :] for g in range(n_pat)]
        top = functools.reduce(jnp.maximum, lses)
        ws = [jnp.exp(l - top) for l in lses]
        num = functools.reduce(lambda a, b: a + b, [w * og_scr[g, rows, :] for g, w in enumerate(ws)])
        o_ref[0, rows, :] = num / functools.reduce(lambda a, b: a + b, ws)
        return carry

    lax.fori_loop(0, seq // rb, merge_body, 0)


def _attn_prompt_call(aq, ak, av, tab):
    bsz, seq, width = aq.shape
    assert width == HEAD_WIDTH and seq % (DILATED_PATTERNS[-1][1] * KEYS_PER_BLOCK) == 0
    n_pat = len(DILATED_PATTERNS)
    heads_per_step = LANES // HEAD_DIM
    qkv_spec = pl.BlockSpec((1, seq, LANES), lambda h, b: (b, 0, h))
    return pl.pallas_call(
        functools.partial(_attn_prompt_kernel, seq=seq),
        grid=(N_HEADS // heads_per_step, bsz),
        in_specs=[qkv_spec, qkv_spec, qkv_spec,
                  pl.BlockSpec((n_pat, heads_per_step, KEYS_PER_BLOCK, 2 * KEYS_PER_BLOCK),
                               lambda h, b: (0, h, 0, 0))],
        out_specs=pl.BlockSpec((1, seq, LANES), lambda h, b: (b, 0, h)),
        out_shape=jax.ShapeDtypeStruct((bsz, seq, width), F32),
        scratch_shapes=[pltpu.VMEM((n_pat, seq, LANES), F32)] * 2,
        compiler_params=pltpu.CompilerParams(dimension_semantics=("parallel", "parallel"),
                                             vmem_limit_bytes=VMEM_LIMIT),
        name="attn_prompt",
    )(aq, ak, av, tab)


def _sample_layout(w_buf, t_new):
    assert w_buf % LANES == 0 and t_new <= SAMPLE_T_PAD
    row_of_col = np.full((w_buf + LANES,), -1, np.int64)
    row_of_col[:w_buf + t_new] = np.arange(w_buf + t_new)
    buckets = np.full((len(DILATED_PATTERNS), 1, SAMPLE_T_PAD, w_buf + LANES), -1, np.int32)
    lows = []
    for g, (window, dil) in enumerate(DILATED_PATTERNS):
        lows.append(max(0, w_buf - window) // LANES * LANES)
        for t in range(t_new):
            dist = w_buf + t - row_of_col
            ok = (row_of_col >= 0) & (dist >= 0) & (dist <= window) & (dist % dil == 0)
            assert int(ok.sum()) == min(window, w_buf + t) // dil + 1 and not ok[:lows[g]].any()
            buckets[g, 0, t] = np.where(ok, _t5_bucket_np(dist), -1)
    return tuple(lows), buckets


def _attn_sample_stages(q_ref, kn_ref, vn_ref, kt_ref, vt_ref, tab_ref, o_ref, lows):
    tp = SAMPLE_T_PAD
    hd = HEAD_DIM
    w_buf = kt_ref.shape[3]
    n_pat = len(DILATED_PATTERNS)
    add = lambda x, y: x + y
    rowmax = lambda x: jnp.max(x, axis=-1, keepdims=True)
    rowsum = lambda x: jnp.sum(x, axis=-1, keepdims=True)
    group = N_HEADS // ATTN_SAMPLE_HEAD_GROUPS
    head_groups = [range(i * group, (i + 1) * group) for i in range(ATTN_SAMPLE_HEAD_GROUPS)]
    s_buf, s_new, p_buf, p_new, den, lse, outs = {}, {}, {}, {}, {}, {}, {}

    def scores(heads):
        for h in heads:
            q_h = q_ref[0, :, hd * h:hd * (h + 1)] * hd ** -0.5
            s_buf[h] = jnp.dot(q_h, kt_ref[0, h], preferred_element_type=F32)
            s_new[h] = _nt_dot(q_h, kn_ref[0, h])

    def statistics(heads):
        for h in heads:
            for g in range(n_pat):
                lo = lows[g]
                l_buf = s_buf[h][:, lo:] + tab_ref[g, h, :, lo:w_buf]
                l_new = s_new[h] + tab_ref[g, h, :, w_buf:w_buf + tp]
                m = jnp.maximum(rowmax(l_buf), rowmax(l_new))
                p_buf[h, g] = jnp.exp(l_buf - m)
                p_new[h, g] = jnp.exp(l_new - m)
                den[h, g] = rowsum(p_buf[h, g]) + rowsum(p_new[h, g])
                lse[h, g] = m + jnp.log(den[h, g])

    def values(heads):
        edges = sorted(set(lows)) + [w_buf]
        for h in heads:
            top = functools.reduce(jnp.maximum, [lse[h, g] for g in range(n_pat)])
            ws = [jnp.exp(lse[h, g] - top) for g in range(n_pat)]
            w_sum = functools.reduce(add, ws)
            coef = [ws[g] / (w_sum * den[h, g]) for g in range(n_pat)]
            acc = jnp.dot(functools.reduce(add, [coef[g] * p_new[h, g] for g in range(n_pat)]), vn_ref[0, h],
                          preferred_element_type=F32)
            for e0, e1 in zip(edges[:-1], edges[1:]):
                mix = functools.reduce(add, [coef[g] * p_buf[h, g][:, e0 - lows[g]:e1 - lows[g]]
                                             for g in range(n_pat) if lows[g] <= e0])
                acc = acc + _nt_dot(mix, vt_ref[0, h, :, e0:e1])
            outs[h] = acc

    def finish():
        o_ref[0] = jnp.concatenate([outs[h] for h in range(N_HEADS)], axis=1)

    return [functools.partial(stage, heads) for stage in (scores, statistics, values)
            for heads in head_groups] + [finish]


N_POST_INPUTS = 11
N_ATTN_SAMPLE_INPUTS = 6


def _post_with_sample_attn_kernel(*refs, alpha, lows):
    post_in = refs[:N_POST_INPUTS]
    attn_in = refs[N_POST_INPUTS:N_POST_INPUTS + N_ATTN_SAMPLE_INPUTS]
    y_ref, o_ref, h_scr = refs[N_POST_INPUTS + N_ATTN_SAMPLE_INPUTS:]
    _post_kernel(*post_in, y_ref, h_scr, alpha=alpha, side_jobs=_attn_sample_stages(*attn_in, o_ref, lows))


def _post_with_sample_attn_call(attn, gdn, x1, wo, g2, b2, wg, wu, wd, g3, b3, alpha,
                                aq8, k_new, v_new, cache_kt, cache_vt, tab, lows):
    m, d = x1.shape
    bsz, n_heads, hd, w_buf = cache_kt.shape
    assert m % bsz == 0
    tm = m // bsz
    assert tm % SUBLANES == 0
    tile = lambda w: pl.BlockSpec((tm, w), lambda i: (i, 0))
    consts = (wo, g2, b2, wg, wu, wd, g3, b3)
    q_spec = pl.BlockSpec((1, SAMPLE_T_PAD, n_heads * hd), lambda b: (b, 0, 0))
    new_spec = pl.BlockSpec((1, n_heads, SAMPLE_T_PAD, hd), lambda b: (b, 0, 0, 0))
    buf_spec = pl.BlockSpec((1, n_heads, hd, w_buf), lambda b: (b, 0, 0, 0))
    assert len(consts) + 3 == N_POST_INPUTS
    return pl.pallas_call(
        functools.partial(_post_with_sample_attn_kernel, alpha=alpha, lows=lows),
        grid=(bsz,),
        in_specs=[tile(HEAD_WIDTH), tile(HEAD_WIDTH), tile(d)] + [_resident(c.shape) for c in consts]
                 + [q_spec, new_spec, new_spec, buf_spec, buf_spec, _resident(tab.shape)],
        out_specs=[tile(d), q_spec],
        out_shape=[jax.ShapeDtypeStruct((m, d), F32),
                   jax.ShapeDtypeStruct((bsz, SAMPLE_T_PAD, n_heads * hd), F32)],
        scratch_shapes=[pltpu.VMEM((tm, wg.shape[1]), BF16)],
        compiler_params=pltpu.CompilerParams(dimension_semantics=("parallel",), vmem_limit_bytes=VMEM_LIMIT),
        name="post_with_sample_attn",
    )(attn, gdn, x1, *consts, aq8, k_new, v_new, cache_kt, cache_vt, tab)


def _aligned(index, size):
    return index * size if isinstance(index, int) else pl.multiple_of(index * size, size)


def _split3(x):
    h1 = x.astype(BF16)
    r1 = x - h1.astype(F32)
    h2 = r1.astype(BF16)
    h3 = (r1 - h2.astype(F32)).astype(BF16)
    return h1, h2, h3


def _head_sums(x, ones_ref):
    h1, h2, _ = _split3(x)
    return (jnp.dot(h1, ones_ref[...], preferred_element_type=F32)
            + jnp.dot(h2, ones_ref[...], preferred_element_type=F32))


def _store_normed_qk(act, q_dst, k_dst, ones_ref):
    hw = HEAD_WIDTH
    q = act[:, 0:hw]
    k = act[:, hw:2 * hw]
    q_dst[...] = q * lax.rsqrt(_head_sums(q * q, ones_ref) + RMS_EPS) * HEAD_DIM ** -0.5
    k_dst[...] = k * lax.rsqrt(_head_sums(k * k, ones_ref) + RMS_EPS)


def _gdn_kernel(x_ref, z_ref, ba_ref, cb_ref, s0_ref, cw_ref, hp_ref, nw_ref, ones_ref,
                o_ref, sfin_ref,
                ext_scr, q_scr, k_scr, v_scr, bt_scr, g_scr, *core_scr, chunk, n_valid, group):
    n_bb, tb = x_ref.shape[0], x_ref.shape[1]
    t = pl.program_id(1)

    @pl.when(t == 0)
    def _():
        ext_scr[:, 0:SUBLANES, :] = cb_ref[...]

    for bb in range(n_bb):
        act = _causal_conv_silu(ext_scr.at[bb], x_ref[bb], cw_ref)
        _store_normed_qk(act, q_scr.at[bb], k_scr.at[bb], ones_ref)
        v_scr[bb] = act[:, 2 * HEAD_WIDTH:3 * HEAD_WIDTH]
        ba = ba_ref[bb]
        live = t * tb + lax.broadcasted_iota(jnp.int32, ba.shape, 0) < n_valid
        beta, g = _gdn_gates(ba, hp_ref)
        bt_scr[bb] = jnp.where(live, beta, 0.0)
        g_scr[bb] = jnp.where(live, g, 0.0)
    _gdn_core(s0_ref, z_ref, nw_ref, ones_ref, o_ref, sfin_ref, q_scr, k_scr, v_scr, 0, bt_scr, g_scr,
              *core_scr, chunk=chunk, group=group)


def _gdn_pairs_kernel(qkv_ref, z_ref, bt_scr, g_scr, s0_ref, nw_ref, ones_ref, o_ref, sfin_ref,
                      oc_scr, s_scr, u_scr, w_scr, in_scr, qd_scr, kd_scr, el_scr, *, group):
    chunk = HEAD_DIM
    k_lane0, v_lane0 = HEAD_WIDTH, 2 * HEAD_WIDTH
    n_bb, tb = bt_scr.shape[0], bt_scr.shape[1]
    n_chunks = tb // chunk
    n_pairs = N_HEADS // 2
    hd = HEAD_DIM
    t = pl.program_id(1)
    dot = functools.partial(jnp.dot, preferred_element_type=F32)

    lane = lax.broadcasted_iota(jnp.int32, (chunk, LANES), 1)
    row = lax.broadcasted_iota(jnp.int32, (chunk, LANES), 0)
    second = lane >= hd
    col = lane - jnp.where(second, hd, 0)
    tril = row >= col
    strict = row > col
    eye2 = (row == col).astype(F32)
    tril_b = (lax.broadcasted_iota(jnp.int32, (chunk, chunk), 0)
              >= lax.broadcasted_iota(jnp.int32, (chunk, chunk), 1)).astype(BF16)
    n_doublings = int(math.log2(chunk)) - 1

    def blockdiag(x):
        return jnp.concatenate([jnp.where(second, 0.0, x), jnp.where(second, x, 0.0)], axis=0)

    def per_head(cols, j):
        return jnp.where(second, cols[:, 2 * j + 1:2 * j + 2], cols[:, 2 * j:2 * j + 1])

    @pl.when(t == 0)
    def _():
        for bb in range(n_bb):
            for j in range(n_pairs):
                s_scr[bb, j] = jnp.concatenate([s0_ref[bb, 2 * j], s0_ref[bb, 2 * j + 1]], axis=1)

    def state_free_part(gi, carry):
        probs, shared = [], {}
        for bb in range(n_bb):
            for cg in range(group):
                c_idx = gi * group + cg
                rows = pl.ds(_aligned(c_idx, chunk), chunk)
                gc = functools.reduce(lambda a, b: a + b,
                                      [dot(tril_b, p) for p in _split3(g_scr[bb, rows, :])])
                shared[bb, cg] = (rows, c_idx, gc[:, N_HEADS:2 * N_HEADS], gc.T[N_HEADS:2 * N_HEADS, :],
                                  bt_scr[bb, rows, :])
                probs += [(bb, cg, j) for j in range(n_pairs)]
        kk, qk, decay, a, t_inv, pw, vb, kbg = {}, {}, {}, {}, {}, {}, {}, {}
        for p in probs:
            bb, cg, j = p
            rows, c_idx, gc, gct, bt = shared[bb, cg]
            pl_ = pl.ds(LANES * j, LANES)
            q2 = qkv_ref[bb, rows, pl_]
            k2 = qkv_ref[bb, rows, pl.ds(k_lane0 + LANES * j, LANES)]
            beta2 = per_head(bt, j)
            gcc = per_head(gc, j)
            gcr = jnp.concatenate([gct[2 * j:2 * j + 1, :], gct[2 * j + 1:2 * j + 2, :]], axis=1)
            g_last = gcc[chunk - 1:chunk, :]
            e_gc = jnp.exp(gcc)
            kb2 = k2 * beta2
            k_bd = blockdiag(k2)
            both = _nt_dot(jnp.concatenate([kb2, q2], axis=0), k_bd)
            kk[p] = both[0:chunk]
            qk[p] = both[chunk:2 * chunk]
            vb[p] = qkv_ref[bb, rows, pl.ds(v_lane0 + LANES * j, LANES)] * beta2
            kbg[p] = kb2 * e_gc
            qd_scr[bb, rows, pl_] = q2 * e_gc
            kd_scr[bb, rows, pl_] = k2 * jnp.exp(g_last - gcc)
            el_scr[bb, pl.ds(_aligned(c_idx, SUBLANES), SUBLANES), pl_] = jnp.broadcast_to(
                jnp.exp(g_last), (SUBLANES, LANES))
            decay[p] = jnp.where(tril, jnp.exp(jnp.where(tril, gcc - gcr, 0.0)), 0.0)
        for p in probs:
            bb, cg, j = p
            rows = shared[bb, cg][0]
            a[p] = jnp.where(strict, kk[p] * decay[p], 0.0)
            in_scr[bb, rows, pl.ds(LANES * j, LANES)] = jnp.where(tril, qk[p] * decay[p], 0.0)
            t_inv[p] = eye2 - a[p]
        for p in probs:
            pw[p] = dot(a[p], blockdiag(a[p]))
        for _ in range(n_doublings - 1):
            nxt_pw, nxt_t = {}, {}
            for p in probs:
                nxt_pw[p] = dot(pw[p], blockdiag(pw[p]))
                nxt_t[p] = dot(t_inv[p], blockdiag(eye2 + pw[p]))
            pw, t_inv = nxt_pw, nxt_t
        for p in probs:
            t_inv[p] = dot(t_inv[p], blockdiag(eye2 + pw[p]))
        for p in probs:
            bb, cg, j = p
            rows = shared[bb, cg][0]
            uw = dot(t_inv[p], jnp.concatenate([blockdiag(vb[p]), blockdiag(kbg[p])], axis=1))
            u_scr[bb, rows, pl.ds(LANES * j, LANES)] = uw[:, 0:LANES]
            w_scr[bb, rows, pl.ds(LANES * j, LANES)] = uw[:, LANES:2 * LANES]
        return carry

    def state_part(ci_, carry):
        rows = pl.ds(_aligned(ci_, chunk), chunk)
        el_rows = pl.ds(_aligned(ci_, SUBLANES), SUBLANES)
        probs = [(bb, j) for bb in range(n_bb) for j in range(n_pairs)]
        s_old, both, v_new, o_in, s_add = {}, {}, {}, {}, {}
        for p in probs:
            bb, j = p
            pl_ = pl.ds(LANES * j, LANES)
            s_old[p] = s_scr[bb, j]
            lhs = jnp.concatenate([w_scr[bb, rows, pl_], qd_scr[bb, rows, pl_]], axis=0)
            both[p] = dot(lhs, blockdiag(s_old[p]))
        for p in probs:
            bb, j = p
            pl_ = pl.ds(LANES * j, LANES)
            v_new[p] = u_scr[bb, rows, pl_] - both[p][0:chunk]
            o_in[p] = dot(in_scr[bb, rows, pl_], blockdiag(v_new[p]))
            cross = lax.dot_general(kd_scr[bb, rows, pl_], v_new[p], (((0,), (0,)), ((), ())),
                                    preferred_element_type=F32)
            s_add[p] = jnp.where(second, cross[hd:2 * hd], cross[0:hd])
        for p in probs:
            bb, j = p
            pl_ = pl.ds(LANES * j, LANES)
            oc_scr[bb, rows, pl_] = both[p][chunk:2 * chunk] + o_in[p]
            s_scr[bb, j] = s_old[p] * el_scr[bb, el_rows, pl_][0:1, :] + s_add[p]
        return carry

    assert n_chunks % group == 0
    if n_chunks == group:
        state_free_part(0, 0)
    else:
        lax.fori_loop(0, n_chunks // group, state_free_part, 0)
    if n_chunks == 1:
        state_part(0, 0)
    else:
        lax.fori_loop(0, n_chunks, state_part, 0)

    for bb in range(n_bb):
        o = oc_scr[bb]
        o = o * lax.rsqrt(_head_sums(o * o, ones_ref) * (1.0 / hd) + RMS_EPS) * nw_ref[...]
        o_ref[bb] = o * jax.nn.silu(z_ref[bb])

    @pl.when(t == pl.num_programs(1) - 1)
    def _():
        for bb in range(n_bb):
            for j in range(n_pairs):
                sfin_ref[bb, 2 * j] = s_scr[bb, j][:, 0:hd]
                sfin_ref[bb, 2 * j + 1] = s_scr[bb, j][:, hd:2 * hd]


def _gdn_core(s0_ref, z_ref, nw_ref, ones_ref, o_ref, sfin_ref, q_scr, k_scr, v_scr, v_lane0, bt_scr, g_scr,
              oc_scr, s_scr, uw_scr, in_scr, qd_scr, kd_scr, el_scr, *, chunk, group):
    n_bb, tb = q_scr.shape[0], q_scr.shape[1]
    n_chunks = tb // chunk
    hd = HEAD_DIM
    t = pl.program_id(1)

    @pl.when(t == 0)
    def _():
        s_scr[...] = s0_ref[...]

    ri = lax.broadcasted_iota(jnp.int32, (chunk, chunk), 0)
    ci = lax.broadcasted_iota(jnp.int32, (chunk, chunk), 1)
    tril = ri >= ci
    strict = ri > ci
    eye_c = (ri == ci).astype(F32)
    tril_b = tril.astype(BF16)
    n_doublings = int(math.log2(chunk)) - 1
    assert 2 ** (n_doublings + 1) == chunk

    dot = functools.partial(jnp.dot, preferred_element_type=F32)

    def state_free_part(gi, carry):
        probs, shared = [], {}
        for bb in range(n_bb):
            for cg in range(group):
                c_idx = gi * group + cg
                rows = pl.ds(_aligned(c_idx, chunk), chunk)
                gc = functools.reduce(lambda a, b: a + b,
                                      [dot(tril_b, p) for p in _split3(g_scr[bb, rows, :])])
                gct = gc.T
                g_last = gc[chunk - 1:chunk, :]
                el_rows = pl.ds(_aligned(c_idx, SUBLANES), SUBLANES)
                el_scr[bb, el_rows, :] = jnp.broadcast_to(jnp.exp(g_last), (SUBLANES, LANES))
                shared[bb, cg] = (rows, gc, gct, jnp.exp(gc), jnp.exp(g_last - gc), bt_scr[bb, rows, :])
                probs += [(bb, cg, h) for h in range(N_HEADS)]
        kk, qk, rhs, decay, a, t_inv, pw = {}, {}, {}, {}, {}, {}, {}
        for p in probs:
            bb, cg, h = p
            rows, gc, gct, e_gc, e_rem, bt = shared[bb, cg]
            hl = pl.ds(hd * h, hd)
            gl = N_HEADS + h
            qh = q_scr[bb, rows, hl]
            kh = k_scr[bb, rows, hl]
            beta = bt[:, h:h + 1]
            kb = kh * beta
            kk[p] = _nt_dot(kb, kh)
            qk[p] = _nt_dot(qh, kh)
            v_h = v_scr[bb, rows, pl.ds(v_lane0 + hd * h, hd)]
            rhs[p] = jnp.concatenate([v_h * beta, kb * e_gc[:, gl:gl + 1]], axis=1)
            qd_scr[bb, rows, hl] = qh * e_gc[:, gl:gl + 1]
            kd_scr[bb, rows, hl] = kh * e_rem[:, gl:gl + 1]
            diff = gc[:, gl:gl + 1] - gct[gl:gl + 1, :]
            decay[p] = jnp.where(tril, jnp.exp(jnp.where(tril, diff, 0.0)), 0.0)
        for p in probs:
            bb, cg, h = p
            rows = shared[bb, cg][0]
            a[p] = jnp.where(strict, kk[p] * decay[p], 0.0)
            in_scr[bb, rows, LANES * h:LANES * h + chunk] = jnp.where(tril, qk[p] * decay[p], 0.0)
            t_inv[p] = eye_c - a[p]
        for p in probs:
            pw[p] = dot(a[p], a[p])
        for _ in range(n_doublings - 1):
            nxt_pw, nxt_t = {}, {}
            for p in probs:
                nxt_pw[p] = dot(pw[p], pw[p])
                nxt_t[p] = dot(t_inv[p], eye_c + pw[p])
            pw, t_inv = nxt_pw, nxt_t
        for p in probs:
            t_inv[p] = dot(t_inv[p], eye_c + pw[p])
        for p in probs:
            bb, cg, h = p
            rows = shared[bb, cg][0]
            uw_scr[bb, rows, LANES * h:LANES * (h + 1)] = dot(t_inv[p], rhs[p])
        return carry

    def state_part(ci_, carry):
        rows = pl.ds(_aligned(ci_, chunk), chunk)
        el_rows = pl.ds(_aligned(ci_, SUBLANES), SUBLANES)
        probs = [(bb, h) for bb in range(n_bb) for h in range(N_HEADS)]
        s_old, uw, both, v_new, o_in, s_add = {}, {}, {}, {}, {}, {}
        for p in probs:
            bb, h = p
            s_old[p] = s_scr[bb, h]
            uw[p] = uw_scr[bb, rows, LANES * h:LANES * (h + 1)]
            lhs = jnp.concatenate([uw[p][:, hd:2 * hd], qd_scr[bb, rows, pl.ds(hd * h, hd)]], axis=0)
            both[p] = dot(lhs, s_old[p])
        for p in probs:
            bb, h = p
            v_new[p] = uw[p][:, 0:hd] - both[p][0:chunk]
            o_in[p] = dot(in_scr[bb, rows, LANES * h:LANES * h + chunk], v_new[p])
            s_add[p] = lax.dot_general(kd_scr[bb, rows, pl.ds(hd * h, hd)], v_new[p],
                                       (((0,), (0,)), ((), ())), preferred_element_type=F32)
        for p in probs:
            bb, h = p
            e_last = el_scr[bb, el_rows, :][0:1, N_HEADS + h:N_HEADS + h + 1]
            oc_scr[bb, rows, pl.ds(hd * h, hd)] = both[p][chunk:2 * chunk] + o_in[p]
            s_scr[bb, h] = s_old[p] * e_last + s_add[p]
        return carry

    assert n_chunks % group == 0
    if n_chunks == group:
        state_free_part(0, 0)
    else:
        lax.fori_loop(0, n_chunks // group, state_free_part, 0)
    if n_chunks == 1:
        state_part(0, 0)
    else:
        lax.fori_loop(0, n_chunks, state_part, 0)

    for bb in range(n_bb):
        o = oc_scr[bb]
        o = o * lax.rsqrt(_head_sums(o * o, ones_ref) * (1.0 / hd) + RMS_EPS) * nw_ref[...]
        o_ref[bb] = o * jax.nn.silu(z_ref[bb])

    @pl.when(t == pl.num_programs(1) - 1)
    def _():
        sfin_ref[...] = s_scr[...]


def _gdn_core_scratch(nbb, tb, chunk):
    hw = HEAD_WIDTH
    return ([pltpu.VMEM((nbb, tb, hw), F32)]
            + [pltpu.VMEM((nbb, N_HEADS, HEAD_DIM, HEAD_DIM), F32)]
            + [pltpu.VMEM((nbb, tb, N_HEADS * LANES), F32)] * 2
            + [pltpu.VMEM((nbb, tb, hw), F32)] * 2
            + [pltpu.VMEM((nbb, (tb // chunk) * SUBLANES, LANES), F32)])


def _head_ones():
    ones_bd = np.kron(np.eye(N_HEADS, dtype=np.float32), np.ones((HEAD_DIM, HEAD_DIM), np.float32))
    return jnp.asarray(ones_bd, BF16)


def _gdn_call(gq, z, ba, conv_buf8, s0, conv_w8, head_params, norm_w_row, n_valid, token_block, chunk,
              batch_block, group):
    bsz, seq, w3 = gq.shape
    hw = HEAD_WIDTH
    assert w3 == 3 * hw and seq % token_block == 0 and token_block % chunk == 0 and bsz % batch_block == 0
    tb = token_block
    nbb = batch_block
    ones_bd = _head_ones()
    blk = lambda w: pl.BlockSpec((nbb, tb, w), lambda b, t: (b, t, 0))
    per_b3 = lambda s: pl.BlockSpec((nbb,) + s, lambda b, t: (b,) + (0,) * len(s))
    return pl.pallas_call(
        functools.partial(_gdn_kernel, chunk=chunk, n_valid=n_valid, group=group),
        grid=(bsz // nbb, seq // tb),
        in_specs=[blk(w3), blk(hw), blk(LANES), per_b3((SUBLANES, w3)), per_b3((N_HEADS, HEAD_DIM, HEAD_DIM)),
                  _resident(conv_w8.shape), _resident(head_params.shape), _resident(norm_w_row.shape),
                  _resident(ones_bd.shape)],
        out_specs=[blk(hw), per_b3((N_HEADS, HEAD_DIM, HEAD_DIM))],
        out_shape=[jax.ShapeDtypeStruct((bsz, seq, hw), F32),
                   jax.ShapeDtypeStruct((bsz, N_HEADS, HEAD_DIM, HEAD_DIM), F32)],
        scratch_shapes=[pltpu.VMEM((nbb, tb + 2 * SUBLANES, w3), F32)]
                       + [pltpu.VMEM((nbb, tb, hw), F32)] * 3
                       + [pltpu.VMEM((nbb, tb, LANES), F32)] * 2
                       + _gdn_core_scratch(nbb, tb, chunk),
        compiler_params=pltpu.CompilerParams(dimension_semantics=("parallel", "arbitrary"),
                                             vmem_limit_bytes=VMEM_LIMIT),
        name="gdn",
    )(gq, z, ba, conv_buf8, s0, conv_w8, head_params, norm_w_row, ones_bd)


def _gdn_pairs_call(qkv, z, beta, g, s0, norm_w_row, token_block, group):
    bsz, seq, w3 = qkv.shape
    hw = HEAD_WIDTH
    chunk = HEAD_DIM
    nbb = GDN_BATCH_BLOCK
    assert w3 == 3 * hw and seq % token_block == 0 and token_block % (chunk * group) == 0 and bsz % nbb == 0
    tb = token_block
    ones_bd = _head_ones()
    blk = lambda w: pl.BlockSpec((nbb, tb, w), lambda b, t: (b, t, 0))
    per_b3 = lambda s: pl.BlockSpec((nbb,) + s, lambda b, t: (b,) + (0,) * len(s))
    return pl.pallas_call(
        functools.partial(_gdn_pairs_kernel, group=group),
        grid=(bsz // nbb, seq // tb),
        in_specs=[blk(w3), blk(hw), blk(LANES), blk(LANES), per_b3((N_HEADS, HEAD_DIM, HEAD_DIM)),
                  _resident(norm_w_row.shape), _resident(ones_bd.shape)],
        out_specs=[blk(hw), per_b3((N_HEADS, HEAD_DIM, HEAD_DIM))],
        out_shape=[jax.ShapeDtypeStruct((bsz, seq, hw), F32),
                   jax.ShapeDtypeStruct((bsz, N_HEADS, HEAD_DIM, HEAD_DIM), F32)],
        scratch_shapes=[pltpu.VMEM((nbb, tb, hw), F32)]
                       + [pltpu.VMEM((nbb, N_HEADS // 2, HEAD_DIM, LANES), F32)]
                       + [pltpu.VMEM((nbb, tb, hw), F32)] * 5
                       + [pltpu.VMEM((nbb, (tb // chunk) * SUBLANES, hw), F32)],
        compiler_params=pltpu.CompilerParams(dimension_semantics=("parallel", "arbitrary"),
                                             vmem_limit_bytes=VMEM_LIMIT),
        name="gdn_pairs",
    )(qkv, z, beta, g, s0, norm_w_row, ones_bd)


def _gdn_step_kernel(xq_ref, xk_ref, xv_ref, cq_ref, ck_ref, cv_ref, wq_ref, wk_ref, wv_ref, z_ref, ba_ref, hp_ref,
                     nw_ref, s0_ref, o_ref, s_ref, k_scr, q_scr):
    n_tok, hd, _ = xq_ref.shape
    h = pl.program_id(0)
    col_sum = lambda x: jnp.sum(x, axis=0, keepdims=True)

    def conv_act(x_ref, c_ref, w_ref, t):
        planes = [c_ref[i] for i in range(CONV_WIDTH - 1)] + [x_ref[i] for i in range(t + 1)]
        taps = planes[t:t + CONV_WIDTH]
        acc = jnp.zeros(taps[0].shape, F32)
        for j in range(CONV_WIDTH):
            acc = acc + taps[j] * w_ref[j]
        return jax.nn.silu(acc)

    s_ref[...] = s0_ref[...]
    neg_rate = -jnp.exp(hp_ref[0, pl.ds(h, 1), :])
    dt_bias = hp_ref[1, pl.ds(h, 1), :]
    for t in range(n_tok):
        q = conv_act(xq_ref, cq_ref, wq_ref, t)
        k = conv_act(xk_ref, ck_ref, wk_ref, t)
        v = conv_act(xv_ref, cv_ref, wv_ref, t)
        q_scr[...] = q * lax.rsqrt(col_sum(q * q) + RMS_EPS) * hd ** -0.5
        k_scr[...] = k * lax.rsqrt(col_sum(k * k) + RMS_EPS)
        beta = jax.nn.sigmoid(ba_ref[t, pl.ds(h, 1), :])
        decay = jnp.exp(neg_rate * jax.nn.softplus(ba_ref[t, pl.ds(N_HEADS + h, 1), :] + dt_bias))

        def k_dot_state(d, acc):
            return acc + k_scr[pl.ds(d, 1), :] * s_ref[0, d]

        ks = lax.fori_loop(0, hd, k_dot_state, jnp.zeros((hd, xq_ref.shape[2]), F32), unroll=8)
        delta = beta * (v - decay * ks)

        def update(d, acc):
            s_new = decay * s_ref[0, d] + k_scr[pl.ds(d, 1), :] * delta
            s_ref[0, d] = s_new
            return acc + q_scr[pl.ds(d, 1), :] * s_new

        o = lax.fori_loop(0, hd, update, jnp.zeros((hd, xq_ref.shape[2]), F32), unroll=8)
        o = o * lax.rsqrt(col_sum(o * o) * (1.0 / hd) + RMS_EPS) * nw_ref[...]
        o_ref[t] = o * jax.nn.silu(z_ref[t])


def _gdn_step_call(x_t, conv_t, conv_w_t, z_t, ba_t, hp_t, nw_t, s0_t):
    n_tok, w3, bsz = x_t.shape
    hd = HEAD_DIM
    part = lambda rows, p: pl.BlockSpec((rows, hd, bsz), lambda h, p=p: (0, p * N_HEADS + h, 0))
    whole = lambda a: pl.BlockSpec(a.shape, lambda h: (0,) * a.ndim)
    state_spec = pl.BlockSpec((1, hd, hd, bsz), lambda h: (h, 0, 0, 0))
    return pl.pallas_call(
        _gdn_step_kernel,
        grid=(N_HEADS,),
        in_specs=[part(n_tok, 0), part(n_tok, 1), part(n_tok, 2),
                  part(CONV_WIDTH - 1, 0), part(CONV_WIDTH - 1, 1), part(CONV_WIDTH - 1, 2),
                  part(CONV_WIDTH, 0), part(CONV_WIDTH, 1), part(CONV_WIDTH, 2),
                  part(n_tok, 0), whole(ba_t), whole(hp_t), whole(nw_t), state_spec],
        out_specs=[part(n_tok, 0), state_spec],
        out_shape=[jax.ShapeDtypeStruct((n_tok, N_HEADS * hd, bsz), F32),
                   jax.ShapeDtypeStruct(s0_t.shape, F32)],
        scratch_shapes=[pltpu.VMEM((hd, bsz), F32)] * 2,
        compiler_params=pltpu.CompilerParams(dimension_semantics=("parallel",), vmem_limit_bytes=VMEM_LIMIT),
        name="gdn_step",
    )(x_t, x_t, x_t, conv_t, conv_t, conv_t, conv_w_t, conv_w_t, conv_w_t, z_t, ba_t, hp_t, nw_t, s0_t)


def _pad_rows(x, rows, front=False):
    extra = rows - x.shape[1]
    cfg = [(0, 0)] * x.ndim
    cfg[1] = (extra, 0) if front else (0, extra)
    return jnp.pad(x, cfg)


def kernel(x_prompt, x_sample, cache_attn_k, cache_attn_v, state_gdn, state_conv, rel_bias, ln1_g, ln1_b,
           ffn1_w_gate, ffn1_w_up, ffn1_w_down, w_in, w_out, gdn_conv_w, gdn_a_log, gdn_dt_bias, gdn_norm_w,
           ln2_g, ln2_b, ffn2_w_gate, ffn2_w_up, ffn2_w_down, ln3_g, ln3_b):
    depth = w_in.shape[0]
    alpha = (2.0 * depth) ** 0.25
    bsz, seq, d_model = x_prompt.shape
    dbsz, dseq, _ = x_sample.shape
    w_buf = cache_attn_k.shape[2]
    hw = HEAD_WIDTH
    in_cols = w_in.shape[2]
    assert in_cols == 7 * hw + 2 * N_HEADS and CONV_WIDTH - 1 <= dseq <= SAMPLE_T_PAD
    assert seq % GDN_TOKEN_BLOCK == 0 and GDN_TOKEN_BLOCK % (GDN_CHUNK * GDN_CHUNK_GROUP) == 0

    sample_lows, sample_buckets = _sample_layout(w_buf, dseq)
    tab_p = _t5_table_call(rel_bias, jnp.asarray(_prompt_buckets()))
    tab_s = _t5_table_call(rel_bias, jnp.asarray(sample_buckets))

    yp = x_prompt.reshape(bsz * seq, d_model)
    ys = x_sample.reshape(dbsz * dseq, d_model)
    collected = [[] for _ in range(8)]
    row = lambda v: v.reshape(1, -1)
    for layer in range(depth):
        wg1, wu1, wd1 = (w[layer].astype(BF16) for w in (ffn1_w_gate, ffn1_w_up, ffn1_w_down))
        wg2, wu2, wd2 = (w[layer].astype(BF16) for w in (ffn2_w_gate, ffn2_w_up, ffn2_w_down))
        win = jnp.pad(w_in[layer], ((0, 0), (0, 7 * hw + LANES - in_cols))).astype(BF16)
        wo = w_out[layer].astype(BF16)
        conv_w8 = jnp.pad(gdn_conv_w[layer], ((0, SUBLANES - CONV_WIDTH), (0, 0)))
        head_params = jnp.zeros((SUBLANES, LANES), F32)
        head_params = head_params.at[0, N_HEADS:2 * N_HEADS].set(gdn_a_log[layer])
        head_params = head_params.at[1, N_HEADS:2 * N_HEADS].set(gdn_dt_bias[layer])
        norm_w_row = jnp.tile(gdn_norm_w[layer], N_HEADS).reshape(1, hw)
        pre = functools.partial(_pre_call, wg=wg1, wu=wu1, wd=wd1, g=row(ln1_g[layer]), b=row(ln1_b[layer]),
                                win=win, alpha=alpha)
        post = functools.partial(_post_call, wo=wo, g2=row(ln2_g[layer]), b2=row(ln2_b[layer]), wg=wg2, wu=wu2,
                                 wd=wd2, g3=row(ln3_g[layer]), b3=row(ln3_b[layer]), alpha=alpha)

        x1, aq, ak, av, ak_t, av_t, gdn_qkv, z, beta, gate, gq_tail = _pre_prompt_call(
            yp, wg1, wu1, wd1, row(ln1_g[layer]), row(ln1_b[layer]), win, conv_w8, head_params, alpha, seq)
        shp = lambda a: a.reshape(bsz, seq, a.shape[1])
        attn = _attn_prompt_call(shp(aq), shp(ak), shp(av), tab_p)
        gdn, s_p = _gdn_pairs_call(shp(gdn_qkv), shp(z), shp(beta), shp(gate),
                                   jnp.zeros((bsz, N_HEADS, HEAD_DIM, HEAD_DIM), F32), norm_w_row,
                                   token_block=GDN_TOKEN_BLOCK, group=GDN_CHUNK_GROUP)
        wp = min(w_buf, seq)
        heads5 = lambda a_t: a_t.reshape(bsz, N_HEADS, HEAD_DIM, seq).transpose(0, 3, 1, 2)[:, seq - wp:]
        st_p = (heads5(ak_t), heads5(av_t), s_p, gq_tail[:, SUBLANES - (CONV_WIDTH - 1):])

        x1s, aqs, aks, avs, gqs, zs, bas = pre(ys)
        shs = lambda a: _pad_rows(a.reshape(dbsz, dseq, a.shape[1]), SAMPLE_T_PAD)
        heads5s = lambda a: a.reshape(dbsz, dseq, N_HEADS, HEAD_DIM)
        new_rows = lambda a: jnp.pad(heads5s(a).transpose(0, 2, 1, 3),
                                     ((0, 0), (0, 0), (0, SAMPLE_T_PAD - dseq), (0, 0)))
        by_head_t = lambda c: c.transpose(0, 2, 3, 1)
        yp, attn_s = _post_with_sample_attn_call(
            attn.reshape(bsz * seq, hw), gdn.reshape(bsz * seq, hw), x1, wo, row(ln2_g[layer]), row(ln2_b[layer]),
            wg2, wu2, wd2, row(ln3_g[layer]), row(ln3_b[layer]), alpha,
            shs(aqs), new_rows(aks), new_rows(avs), by_head_t(cache_attn_k[layer]), by_head_t(cache_attn_v[layer]),
            tab_s, sample_lows)
        to_lanes = lambda a: a.reshape(dbsz, dseq * a.shape[1]).T.reshape(dseq, a.shape[1], dbsz)
        on_lanes = lambda a: jnp.broadcast_to(a[..., None], a.shape + (dbsz,))
        gdn_t, s_t = _gdn_step_call(
            to_lanes(gqs), state_conv[layer].transpose(1, 2, 0), on_lanes(gdn_conv_w[layer]), to_lanes(zs),
            to_lanes(bas[:, :2 * N_HEADS]), on_lanes(jnp.stack([gdn_a_log[layer], gdn_dt_bias[layer]])),
            on_lanes(gdn_norm_w[layer]), state_gdn[layer].transpose(1, 2, 3, 0))
        gdn_s = gdn_t.reshape(dseq * hw, dbsz).T.reshape(dbsz * dseq, hw)
        s_s = s_t.transpose(3, 0, 1, 2)
        ys = post(attn_s[:, :dseq].reshape(dbsz * dseq, hw), gdn_s, x1s)
        st_s = (heads5s(aks), heads5s(avs), s_s,
                gqs.reshape(dbsz, dseq, 3 * hw)[:, dseq - (CONV_WIDTH - 1):])
        for lst, st in zip(collected, st_p + st_s):
            lst.append(st)
    outs = [jnp.stack(t, axis=0) for t in collected]
    return (yp.reshape(bsz, seq, d_model), ys.reshape(dbsz, dseq, d_model)) + tuple(outs)
```

```python
import functools
import math

import numpy as np
import jax
import jax.numpy as jnp
from jax import lax
from jax.experimental import pallas as pl
from jax.experimental.pallas import tpu as pltpu

F32 = jnp.float32
BF16 = jnp.bfloat16

HEAD_DIM = 64
N_HEADS = 8
HEAD_WIDTH = N_HEADS * HEAD_DIM
DILATED_PATTERNS = ((128, 1), (512, 4), (2048, 16))
KEYS_PER_BLOCK = 128
T5_BUCKETS = 32
T5_MAX_EXACT = 16
T5_MAX_DIST = 2048
CONV_WIDTH = 4
GDN_CHUNK = 64
LN_EPS = 1e-5
RMS_EPS = 1e-6
NEG_BIG = -1e30

LANES = 128
SUBLANES = 8
TOKEN_TILE = 256
POST_TOKEN_TILE = 512
GDN_TOKEN_BLOCK = 256
GDN_CHUNK_GROUP = 4
GDN_BATCH_BLOCK = 4
SAMPLE_T_PAD = 8
VMEM_LIMIT = 56 * 1024 * 1024


def _resident(shape):
    nd = len(shape)
    return pl.BlockSpec(shape, lambda *_: (0,) * nd, pipeline_mode=pl.Buffered(1))


def _layernorm(y, g, b):
    mu = jnp.mean(y, axis=-1, keepdims=True)
    yc = y - mu
    var = jnp.mean(yc * yc, axis=-1, keepdims=True)
    return yc * lax.rsqrt(var + LN_EPS) * g + b


def _swiglu(xb, wg_ref, wu_ref, wd_ref, h_scr):
    d_ff = wg_ref.shape[1]
    step = 2 * LANES
    assert d_ff % step == 0
    for c in range(d_ff // step):
        sl = slice(c * step, (c + 1) * step)
        gate = jnp.dot(xb, wg_ref[:, sl], preferred_element_type=F32)
        up = jnp.dot(xb, wu_ref[:, sl], preferred_element_type=F32)
        h_scr[:, sl] = (jax.nn.silu(gate) * up).astype(BF16)
    return jnp.dot(h_scr[...], wd_ref[...], preferred_element_type=F32)


PROJ_WIDTHS = (HEAD_WIDTH, HEAD_WIDTH, HEAD_WIDTH, 3 * HEAD_WIDTH, HEAD_WIDTH, LANES)
PROJ_EDGES = tuple(int(e) for e in np.cumsum((0,) + PROJ_WIDTHS))


def _first_half_step(x_ref, wg_ref, wu_ref, wd_ref, g_ref, b_ref, h_scr, alpha):
    x = x_ref[...]
    ff = _swiglu(x.astype(BF16), wg_ref, wu_ref, wd_ref, h_scr)
    return _layernorm(alpha * x + 0.5 * ff, g_ref[...], b_ref[...])


def _pre_kernel(x_ref, wg_ref, wu_ref, wd_ref, g_ref, b_ref, win_ref,
                x1_ref, aq_ref, ak_ref, av_ref, gq_ref, z_ref, ba_ref, h_scr, *, alpha):
    x1 = _first_half_step(x_ref, wg_ref, wu_ref, wd_ref, g_ref, b_ref, h_scr, alpha)
    x1_ref[...] = x1
    xb = x1.astype(BF16)
    for i, ref in enumerate((aq_ref, ak_ref, av_ref, gq_ref, z_ref, ba_ref)):
        ref[...] = jnp.dot(xb, win_ref[:, PROJ_EDGES[i]:PROJ_EDGES[i + 1]], preferred_element_type=F32)


def _gdn_gates(ba, hp_ref):
    beta = jax.nn.sigmoid(ba)
    g = -jnp.exp(hp_ref[0:1, :]) * jax.nn.softplus(ba + hp_ref[1:2, :])
    return beta, g


def _causal_conv_silu(ext_scr, x, cw_ref):
    tb = x.shape[0]
    hist = SUBLANES
    ext_scr[hist:hist + tb, :] = x
    ext = ext_scr[0:hist + tb, :]
    conv = jnp.zeros(x.shape, F32)
    for j in range(CONV_WIDTH):
        back = CONV_WIDTH - 1 - j
        rows = pltpu.roll(ext, back, axis=0)[hist:hist + tb, :] if back else x
        conv = conv + rows * cw_ref[j:j + 1, :]
    ext_scr[0:hist, :] = ext_scr[tb:tb + hist, :]
    return jax.nn.silu(conv)


def _pre_prompt_kernel(x_ref, wg_ref, wu_ref, wd_ref, g_ref, b_ref, win_ref, cw_ref, hp_ref, ones_ref,
                       x1_ref, aq_ref, ak_ref, av_ref, akt_ref, avt_ref, qkv_ref, z_ref, bt_ref, gg_ref, tail_ref,
                       h_scr, ext_scr, *, alpha, tiles_per_seq):
    x1 = _first_half_step(x_ref, wg_ref, wu_ref, wd_ref, g_ref, b_ref, h_scr, alpha)
    x1_ref[...] = x1
    xb = x1.astype(BF16)
    proj = lambda i: jnp.dot(xb, win_ref[:, PROJ_EDGES[i]:PROJ_EDGES[i + 1]], preferred_element_type=F32)

    @pl.when(pl.program_id(0) % tiles_per_seq == 0)
    def _():
        ext_scr[0:SUBLANES, :] = jnp.zeros((SUBLANES, ext_scr.shape[1]), F32)

    raw = proj(3)
    tail_ref[0] = raw[raw.shape[0] - SUBLANES:, :]
    act = _causal_conv_silu(ext_scr, raw, cw_ref)
    hw = HEAD_WIDTH
    head_sq = lambda x: jnp.dot((x * x).astype(BF16), ones_ref[...], preferred_element_type=F32)
    q = act[:, 0:hw]
    k = act[:, hw:2 * hw]
    qkv_ref[:, 0:hw] = q * lax.rsqrt(head_sq(q) + RMS_EPS) * HEAD_DIM ** -0.5
    qkv_ref[:, hw:2 * hw] = k * lax.rsqrt(head_sq(k) + RMS_EPS)
    qkv_ref[:, 2 * hw:3 * hw] = act[:, 2 * hw:3 * hw]
    bt_ref[...], gg_ref[...] = _gdn_gates(proj(5), hp_ref)
    aq_ref[...] = proj(0)
    for i, ref, ref_t in ((1, ak_ref, akt_ref), (2, av_ref, avt_ref)):
        kv = proj(i)
        ref[...] = kv
        ref_t[0] = kv.T
    z_ref[...] = proj(4)


def _pre_call(x2d, wg, wu, wd, g, b, win, alpha):
    m, d = x2d.shape
    d_ff = wg.shape[1]
    tm = TOKEN_TILE
    assert m % tm == 0 and sum(PROJ_WIDTHS) == win.shape[1]
    tile = lambda w: pl.BlockSpec((tm, w), lambda i: (i, 0))
    return pl.pallas_call(
        functools.partial(_pre_kernel, alpha=alpha),
        grid=(m // tm,),
        in_specs=[tile(d), _resident(wg.shape), _resident(wu.shape), _resident(wd.shape),
                  _resident(g.shape), _resident(b.shape), _resident(win.shape)],
        out_specs=[tile(d)] + [tile(w) for w in PROJ_WIDTHS],
        out_shape=[jax.ShapeDtypeStruct((m, d), F32)] + [jax.ShapeDtypeStruct((m, w), F32) for w in PROJ_WIDTHS],
        scratch_shapes=[pltpu.VMEM((tm, d_ff), BF16)],
        compiler_params=pltpu.CompilerParams(dimension_semantics=("parallel",), vmem_limit_bytes=VMEM_LIMIT),
        name="pre_ffn_proj",
    )(x2d, wg, wu, wd, g, b, win)


def _pre_prompt_call(x2d, wg, wu, wd, g, b, win, conv_w8, head_params, alpha, seq):
    m, d = x2d.shape
    d_ff = wg.shape[1]
    tm = TOKEN_TILE
    hw = HEAD_WIDTH
    assert seq % tm == 0 and m % seq == 0 and sum(PROJ_WIDTHS) == win.shape[1]
    per_seq = seq // tm
    tile = lambda w: pl.BlockSpec((tm, w), lambda i: (i, 0))
    t_spec = pl.BlockSpec((1, hw, tm), lambda i: (i // per_seq, 0, i % per_seq))
    t_shape = jax.ShapeDtypeStruct((m // seq, hw, seq), F32)
    rows = lambda w: jax.ShapeDtypeStruct((m, w), F32)
    consts = (wg, wu, wd, g, b, win, conv_w8, head_params, _head_ones())
    return pl.pallas_call(
        functools.partial(_pre_prompt_kernel, alpha=alpha, tiles_per_seq=per_seq),
        grid=(m // tm,),
        in_specs=[tile(d)] + [_resident(c.shape) for c in consts],
        out_specs=[tile(d), tile(hw), tile(hw), tile(hw), t_spec, t_spec, tile(3 * hw), tile(hw), tile(LANES),
                   tile(LANES), pl.BlockSpec((1, SUBLANES, 3 * hw), lambda i: (i // per_seq, 0, 0))],
        out_shape=[rows(d), rows(hw), rows(hw), rows(hw), t_shape, t_shape, rows(3 * hw), rows(hw), rows(LANES),
                   rows(LANES), jax.ShapeDtypeStruct((m // seq, SUBLANES, 3 * hw), F32)],
        scratch_shapes=[pltpu.VMEM((tm, d_ff), BF16), pltpu.VMEM((tm + 2 * SUBLANES, 3 * hw), F32)],
        compiler_params=pltpu.CompilerParams(dimension_semantics=("arbitrary",), vmem_limit_bytes=VMEM_LIMIT),
        name="pre_ffn_proj_gdnprep",
    )(x2d, *consts)


def _post_kernel(attn_ref, gdn_ref, x1_ref, wo_ref, g2_ref, b2_ref, wg_ref, wu_ref, wd_ref, g3_ref, b3_ref,
                 y_ref, h_scr, *, alpha):
    tm = y_ref.shape[0]
    n_split = 2 if tm >= 2 * TOKEN_TILE else 1
    halves = [pl.ds(i * (tm // n_split), tm // n_split) for i in range(n_split)]
    mix, x2, ff = {}, {}, {}
    for i, rows in enumerate(halves):
        mix[i] = (jnp.dot(attn_ref[rows, :].astype(BF16), wo_ref[0:HEAD_WIDTH, :], preferred_element_type=F32)
                  + jnp.dot(gdn_ref[rows, :].astype(BF16), wo_ref[HEAD_WIDTH:2 * HEAD_WIDTH, :],
                            preferred_element_type=F32))
    for i, rows in enumerate(halves):
        x2[i] = _layernorm(alpha * x1_ref[rows, :] + mix[i], g2_ref[...], b2_ref[...])
    for i, rows in enumerate(halves):
        ff[i] = _swiglu(x2[i].astype(BF16), wg_ref, wu_ref, wd_ref, h_scr.at[rows])
    for i, rows in enumerate(halves):
        y_ref[rows, :] = _layernorm(alpha * x2[i] + 0.5 * ff[i], g3_ref[...], b3_ref[...])


def _post_call(attn, gdn, x1, wo, g2, b2, wg, wu, wd, g3, b3, alpha):
    m, d = x1.shape
    tm = POST_TOKEN_TILE
    assert m % tm == 0
    tile = lambda w: pl.BlockSpec((tm, w), lambda i: (i, 0))
    consts = (wo, g2, b2, wg, wu, wd, g3, b3)
    return pl.pallas_call(
        functools.partial(_post_kernel, alpha=alpha),
        grid=(m // tm,),
        in_specs=[tile(HEAD_WIDTH), tile(HEAD_WIDTH), tile(d)] + [_resident(c.shape) for c in consts],
        out_specs=tile(d),
        out_shape=jax.ShapeDtypeStruct((m, d), F32),
        scratch_shapes=[pltpu.VMEM((tm, wg.shape[1]), BF16)],
        compiler_params=pltpu.CompilerParams(dimension_semantics=("parallel",), vmem_limit_bytes=VMEM_LIMIT),
        name="post_out_ffn",
    )(attn, gdn, x1, *consts)


def _t5_bucket_np(dist):
    n = np.maximum(dist, 0)
    nf = np.maximum(n, 1).astype(np.float32)
    large = T5_MAX_EXACT + (np.log(nf / np.float32(T5_MAX_EXACT)) / np.float32(math.log(T5_MAX_DIST / T5_MAX_EXACT))
                            * np.float32(T5_BUCKETS - T5_MAX_EXACT)).astype(np.int32)
    large = np.minimum(large, T5_BUCKETS - 1)
    return np.where(n < T5_MAX_EXACT, n, large).astype(np.int32)


def _t5_table_kernel(rb_ref, bk_ref, out_ref):
    for h in range(N_HEADS):
        bk = bk_ref[0, h % bk_ref.shape[1]]
        acc = jnp.full(bk.shape, NEG_BIG, F32)
        for b in range(T5_BUCKETS):
            acc = jnp.where(bk == b, rb_ref[b, h], acc)
        out_ref[0, h] = acc


def _t5_table_call(rel_bias, buckets):
    p, hb, r, c = buckets.shape
    assert hb in (1, N_HEADS)
    return pl.pallas_call(
        _t5_table_kernel,
        grid=(p,),
        in_specs=[pl.BlockSpec(memory_space=pltpu.SMEM),
                  pl.BlockSpec((1, hb, r, c), lambda i: (i, 0, 0, 0))],
        out_specs=pl.BlockSpec((1, N_HEADS, r, c), lambda i: (i, 0, 0, 0)),
        out_shape=jax.ShapeDtypeStruct((p, N_HEADS, r, c), F32),
        compiler_params=pltpu.CompilerParams(dimension_semantics=("parallel",)),
        name="t5_bias_table",
    )(rel_bias, buckets)


def _prompt_buckets():
    nb = KEYS_PER_BLOCK
    qi = np.arange(nb)[:, None]
    ki = np.arange(2 * nb)[None, :]
    dist = qi + nb - ki
    valid = (dist >= 0) & (dist <= nb)
    out = []
    for window, dil in DILATED_PATTERNS:
        assert window // dil == nb
        out.append(np.where(valid, _t5_bucket_np(dist * dil), -1))
    return np.stack(out).astype(np.int32)[:, None]


def _nt_dot(a, b):
    return lax.dot_general(a, b, (((1,), (1,)), ((), ())), preferred_element_type=F32)


def _attn_prompt_kernel(q_ref, k_ref, v_ref, tab_ref, o_ref, og_scr, lg_scr, *, seq):
    nb = KEYS_PER_BLOCK
    hd = HEAD_DIM
    n_pat = len(DILATED_PATTERNS)
    heads = LANES // hd
    for g, (window, dil) in enumerate(DILATED_PATTERNS):
        n_blocks = seq // (dil * nb)

        def blocks(starts, with_prev, g=g, dil=dil):
            lane_head = lax.broadcasted_iota(jnp.int32, (nb, LANES), 1) // hd
            n_keys = 2 * nb if with_prev else nb
            row_sl, qs, ks, vs = [], [], [], []
            for start in starts:
                first_key = start - dil * (n_keys - nb)
                row_sl.append(pl.ds(start, nb, stride=dil) if dil > 1 else pl.ds(start, nb))
                key_rows = pl.ds(first_key, n_keys, stride=dil) if dil > 1 else pl.ds(first_key, n_keys)
                qs.append(q_ref[0, row_sl[-1], :] * hd ** -0.5)
                ks.append(k_ref[0, key_rows, :].astype(BF16))
                vs.append(v_ref[0, key_rows, :].astype(BF16))
            chains = [(u, hh) for u in range(len(starts)) for hh in range(heads)]
            s, m, p, den, acc = {}, {}, {}, {}, {}

            def scores(c):
                u, hh = c
                qb = jnp.where(lane_head == hh, qs[u], 0.0).astype(BF16)
                s[c] = _nt_dot(qb, ks[u]) + tab_ref[g, hh, :, 2 * nb - n_keys:2 * nb]

            def row_max(c):
                m[c] = jnp.max(s[c], axis=-1, keepdims=True)

            def probs(c):
                p[c] = jnp.exp(s[c] - m[c])
                den[c] = jnp.sum(p[c], axis=-1, keepdims=True)

            def values(c):
                u, hh = c
                acc[c] = jnp.dot(p[c].astype(BF16), vs[u], preferred_element_type=F32)

            def finish(c):
                acc[c] = acc[c] / den[c]
                den[c] = m[c] + jnp.log(den[c])

            for stage in (scores, row_max, probs, values, finish):
                for c in chains:
                    stage(c)
            for u in range(len(starts)):
                out, lse = acc[u, 0], den[u, 0]
                for hh in range(1, heads):
                    out = jnp.where(lane_head == hh, acc[u, hh], out)
                    lse = jnp.where(lane_head == hh, den[u, hh], lse)
                og_scr[g, row_sl[u], :] = out
                lg_scr[g, row_sl[u], :] = jnp.broadcast_to(lse, (nb, LANES))

        def unroll_of(count):
            return next(u for u in (8, 6, 5, 4, 3, 2, 1) if count % u == 0)

        u_first = unroll_of(dil)

        def first_body(i, carry, blocks=blocks, u_first=u_first):
            blocks([i * u_first + j for j in range(u_first)], False)
            return carry

        lax.fori_loop(0, dil // u_first, first_body, 0)
        if n_blocks > 1:
            n_rest = dil * (n_blocks - 1)
            u_rest = unroll_of(n_rest)

            def rest_body(i, carry, blocks=blocks, dil=dil, n_blocks=n_blocks, u_rest=u_rest):
                starts = []
                for j in range(u_rest):
                    idx = i * u_rest + j
                    starts.append(idx // (n_blocks - 1) + dil * nb * (idx % (n_blocks - 1) + 1))
                blocks(starts, True)
                return carry

            lax.fori_loop(0, n_rest // u_rest, rest_body, 0)

    rb = 2 * nb

    def merge_body(i, carry):
        rows = pl.ds(pl.multiple_of(i * rb, rb), rb)
        lses = [lg_scr[g, rows, :] for g in range(n_pat)]
        top = functools.reduce(jnp.maximum, lses)
        ws = [jnp.exp(l - top) for l in lses]
        num = functools.reduce(lambda a, b: a + b, [w * og_scr[g, rows, :] for g, w in enumerate(ws)])
        o_ref[0, rows, :] = num / functools.reduce(lambda a, b: a + b, ws)
        return carry

    lax.fori_loop(0, seq // rb, merge_body, 0)


def _attn_prompt_call(aq, ak, av, tab):
    bsz, seq, width = aq.shape
    assert width == HEAD_WIDTH and seq % (DILATED_PATTERNS[-1][1] * KEYS_PER_BLOCK) == 0
    n_pat = len(DILATED_PATTERNS)
    heads_per_step = LANES // HEAD_DIM
    qkv_spec = pl.BlockSpec((1, seq, LANES), lambda h, b: (b, 0, h))
    return pl.pallas_call(
        functools.partial(_attn_prompt_kernel, seq=seq),
        grid=(N_HEADS // heads_per_step, bsz),
        in_specs=[qkv_spec, qkv_spec, qkv_spec,
                  pl.BlockSpec((n_pat, heads_per_step, KEYS_PER_BLOCK, 2 * KEYS_PER_BLOCK),
                               lambda h, b: (0, h, 0, 0))],
        out_specs=pl.BlockSpec((1, seq, LANES), lambda h, b: (b, 0, h)),
        out_shape=jax.ShapeDtypeStruct((bsz, seq, width), F32),
        scratch_shapes=[pltpu.VMEM((n_pat, seq, LANES), F32)] * 2,
        compiler_params=pltpu.CompilerParams(dimension_semantics=("parallel", "parallel"),
                                             vmem_limit_bytes=VMEM_LIMIT),
        name="attn_prompt",
    )(aq, ak, av, tab)


def _sample_layout(w_buf, t_new):
    assert w_buf % LANES == 0 and t_new <= SAMPLE_T_PAD
    row_of_col = np.full((w_buf + LANES,), -1, np.int64)
    row_of_col[:w_buf + t_new] = np.arange(w_buf + t_new)
    buckets = np.full((len(DILATED_PATTERNS), 1, SAMPLE_T_PAD, w_buf + LANES), -1, np.int32)
    lows = []
    for g, (window, dil) in enumerate(DILATED_PATTERNS):
        lows.append(max(0, w_buf - window) // LANES * LANES)
        for t in range(t_new):
            dist = w_buf + t - row_of_col
            ok = (row_of_col >= 0) & (dist >= 0) & (dist <= window) & (dist % dil == 0)
            assert int(ok.sum()) == min(window, w_buf + t) // dil + 1 and not ok[:lows[g]].any()
            buckets[g, 0, t] = np.where(ok, _t5_bucket_np(dist), -1)
    return tuple(lows), buckets


def _attn_sample_kernel(q_ref, kn_ref, vn_ref, kt_ref, vt_ref, tab_ref, o_ref, *, lows):
    tp = SAMPLE_T_PAD
    hd = HEAD_DIM
    w_buf = kt_ref.shape[3]
    n_pat = len(DILATED_PATTERNS)
    add = lambda x, y: x + y
    rowmax = lambda x: jnp.max(x, axis=-1, keepdims=True)
    rowsum = lambda x: jnp.sum(x, axis=-1, keepdims=True)
    heads = range(N_HEADS)
    q8 = q_ref[0] * hd ** -0.5
    s_buf = {h: jnp.dot(q8[:, hd * h:hd * (h + 1)], kt_ref[0, h], preferred_element_type=F32) for h in heads}
    s_new = {h: _nt_dot(q8[:, hd * h:hd * (h + 1)], kn_ref[0, h]) for h in heads}
    p_buf, p_new, den, lse = {}, {}, {}, {}
    for h in heads:
        for g in range(n_pat):
            lo = lows[g]
            l_buf = s_buf[h][:, lo:] + tab_ref[g, h, :, lo:w_buf]
            l_new = s_new[h] + tab_ref[g, h, :, w_buf:w_buf + tp]
            m = jnp.maximum(rowmax(l_buf), rowmax(l_new))
            p_buf[h, g] = jnp.exp(l_buf - m)
            p_new[h, g] = jnp.exp(l_new - m)
            den[h, g] = rowsum(p_buf[h, g]) + rowsum(p_new[h, g])
            lse[h, g] = m + jnp.log(den[h, g])
    edges = sorted(set(lows)) + [w_buf]
    outs = {}
    for h in heads:
        top = functools.reduce(jnp.maximum, [lse[h, g] for g in range(n_pat)])
        ws = [jnp.exp(lse[h, g] - top) for g in range(n_pat)]
        w_sum = functools.reduce(add, ws)
        coef = [ws[g] / (w_sum * den[h, g]) for g in range(n_pat)]
        acc = jnp.dot(functools.reduce(add, [coef[g] * p_new[h, g] for g in range(n_pat)]), vn_ref[0, h],
                      preferred_element_type=F32)
        for e0, e1 in zip(edges[:-1], edges[1:]):
            mix = functools.reduce(add, [coef[g] * p_buf[h, g][:, e0 - lows[g]:e1 - lows[g]]
                                         for g in range(n_pat) if lows[g] <= e0])
            acc = acc + _nt_dot(mix, vt_ref[0, h, :, e0:e1])
        outs[h] = acc
    o_ref[0] = jnp.concatenate([outs[h] for h in heads], axis=1)


N_POST_INPUTS = 11
N_ATTN_SAMPLE_INPUTS = 6


def _post_with_sample_attn_kernel(*refs, alpha, lows):
    post_in = refs[:N_POST_INPUTS]
    attn_in = refs[N_POST_INPUTS:N_POST_INPUTS + N_ATTN_SAMPLE_INPUTS]
    y_ref, o_ref, h_scr = refs[N_POST_INPUTS + N_ATTN_SAMPLE_INPUTS:]
    _post_kernel(*post_in, y_ref, h_scr, alpha=alpha)
    _attn_sample_kernel(*attn_in, o_ref, lows=lows)


def _post_with_sample_attn_call(attn, gdn, x1, wo, g2, b2, wg, wu, wd, g3, b3, alpha,
                                aq8, k_new, v_new, cache_kt, cache_vt, tab, lows):
    m, d = x1.shape
    bsz, n_heads, hd, w_buf = cache_kt.shape
    assert m % bsz == 0
    tm = m // bsz
    assert tm % SUBLANES == 0
    tile = lambda w: pl.BlockSpec((tm, w), lambda i: (i, 0))
    consts = (wo, g2, b2, wg, wu, wd, g3, b3)
    q_spec = pl.BlockSpec((1, SAMPLE_T_PAD, n_heads * hd), lambda b: (b, 0, 0))
    new_spec = pl.BlockSpec((1, n_heads, SAMPLE_T_PAD, hd), lambda b: (b, 0, 0, 0))
    buf_spec = pl.BlockSpec((1, n_heads, hd, w_buf), lambda b: (b, 0, 0, 0))
    assert len(consts) + 3 == N_POST_INPUTS
    return pl.pallas_call(
        functools.partial(_post_with_sample_attn_kernel, alpha=alpha, lows=lows),
        grid=(bsz,),
        in_specs=[tile(HEAD_WIDTH), tile(HEAD_WIDTH), tile(d)] + [_resident(c.shape) for c in consts]
                 + [q_spec, new_spec, new_spec, buf_spec, buf_spec, _resident(tab.shape)],
        out_specs=[tile(d), q_spec],
        out_shape=[jax.ShapeDtypeStruct((m, d), F32),
                   jax.ShapeDtypeStruct((bsz, SAMPLE_T_PAD, n_heads * hd), F32)],
        scratch_shapes=[pltpu.VMEM((tm, wg.shape[1]), BF16)],
        compiler_params=pltpu.CompilerParams(dimension_semantics=("parallel",), vmem_limit_bytes=VMEM_LIMIT),
        name="post_with_sample_attn",
    )(attn, gdn, x1, *consts, aq8, k_new, v_new, cache_kt, cache_vt, tab)


def _aligned(index, size):
    return index * size if isinstance(index, int) else pl.multiple_of(index * size, size)


def _split3(x):
    h1 = x.astype(BF16)
    r1 = x - h1.astype(F32)
    h2 = r1.astype(BF16)
    h3 = (r1 - h2.astype(F32)).astype(BF16)
    return h1, h2, h3


def _head_sums(x, ones_ref):
    h1, h2, _ = _split3(x)
    return (jnp.dot(h1, ones_ref[...], preferred_element_type=F32)
            + jnp.dot(h2, ones_ref[...], preferred_element_type=F32))


def _gdn_pairs_kernel(qkv_ref, z_ref, bt_scr, g_scr, s0_ref, nw_ref, ones_ref, o_ref, sfin_ref,
                      oc_scr, s_scr, u_scr, w_scr, in_scr, qd_scr, kd_scr, el_scr, *, group):
    chunk = HEAD_DIM
    k_lane0, v_lane0 = HEAD_WIDTH, 2 * HEAD_WIDTH
    n_bb, tb = bt_scr.shape[0], bt_scr.shape[1]
    n_chunks = tb // chunk
    n_pairs = N_HEADS // 2
    hd = HEAD_DIM
    t = pl.program_id(1)
    dot = functools.partial(jnp.dot, preferred_element_type=F32)

    lane = lax.broadcasted_iota(jnp.int32, (chunk, LANES), 1)
    row = lax.broadcasted_iota(jnp.int32, (chunk, LANES), 0)
    second = lane >= hd
    col = lane - jnp.where(second, hd, 0)
    tril = row >= col
    strict = row > col
    eye2 = (row == col).astype(F32)
    tril_b = (lax.broadcasted_iota(jnp.int32, (chunk, chunk), 0)
              >= lax.broadcasted_iota(jnp.int32, (chunk, chunk), 1)).astype(BF16)
    n_doublings = int(math.log2(chunk)) - 1

    def blockdiag(x):
        return jnp.concatenate([jnp.where(second, 0.0, x), jnp.where(second, x, 0.0)], axis=0)

    def per_head(cols, j):
        return jnp.where(second, cols[:, 2 * j + 1:2 * j + 2], cols[:, 2 * j:2 * j + 1])

    @pl.when(t == 0)
    def _():
        for bb in range(n_bb):
            for j in range(n_pairs):
                s_scr[bb, j] = jnp.concatenate([s0_ref[bb, 2 * j], s0_ref[bb, 2 * j + 1]], axis=1)

    def state_free_part(gi, carry):
        probs, shared = [], {}
        for bb in range(n_bb):
            for cg in range(group):
                c_idx = gi * group + cg
                rows = pl.ds(_aligned(c_idx, chunk), chunk)
                gc = functools.reduce(lambda a, b: a + b,
                                      [dot(tril_b, p) for p in _split3(g_scr[bb, rows, :])])
                shared[bb, cg] = (rows, c_idx, gc[:, N_HEADS:2 * N_HEADS], gc.T[N_HEADS:2 * N_HEADS, :],
                                  bt_scr[bb, rows, :])
                probs += [(bb, cg, j) for j in range(n_pairs)]
        kk, qk, decay, a, t_inv, pw, vb, kbg = {}, {}, {}, {}, {}, {}, {}, {}
        for p in probs:
            bb, cg, j = p
            rows, c_idx, gc, gct, bt = shared[bb, cg]
            pl_ = pl.ds(LANES * j, LANES)
            q2 = qkv_ref[bb, rows, pl_]
            k2 = qkv_ref[bb, rows, pl.ds(k_lane0 + LANES * j, LANES)]
            beta2 = per_head(bt, j)
            gcc = per_head(gc, j)
            gcr = jnp.concatenate([gct[2 * j:2 * j + 1, :], gct[2 * j + 1:2 * j + 2, :]], axis=1)
            g_last = gcc[chunk - 1:chunk, :]
            e_gc = jnp.exp(gcc)
            kb2 = k2 * beta2
            k_bd = blockdiag(k2)
            both = _nt_dot(jnp.concatenate([kb2, q2], axis=0), k_bd)
            kk[p] = both[0:chunk]
            qk[p] = both[chunk:2 * chunk]
            vb[p] = qkv_ref[bb, rows, pl.ds(v_lane0 + LANES * j, LANES)] * beta2
            kbg[p] = kb2 * e_gc
            qd_scr[bb, rows, pl_] = q2 * e_gc
            kd_scr[bb, rows, pl_] = k2 * jnp.exp(g_last - gcc)
            el_scr[bb, pl.ds(_aligned(c_idx, SUBLANES), SUBLANES), pl_] = jnp.broadcast_to(
                jnp.exp(g_last), (SUBLANES, LANES))
            decay[p] = jnp.where(tril, jnp.exp(jnp.where(tril, gcc - gcr, 0.0)), 0.0)
        for p in probs:
            bb, cg, j = p
            rows = shared[bb, cg][0]
            a[p] = jnp.where(strict, kk[p] * decay[p], 0.0)
            in_scr[bb, rows, pl.ds(LANES * j, LANES)] = jnp.where(tril, qk[p] * decay[p], 0.0)
            t_inv[p] = eye2 - a[p]
        for p in probs:
            pw[p] = dot(a[p], blockdiag(a[p]))
        for _ in range(n_doublings - 1):
            nxt_pw, nxt_t = {}, {}
            for p in probs:
                nxt_pw[p] = dot(pw[p], blockdiag(pw[p]))
                nxt_t[p] = dot(t_inv[p], blockdiag(eye2 + pw[p]))
            pw, t_inv = nxt_pw, nxt_t
        for p in probs:
            t_inv[p] = dot(t_inv[p], blockdiag(eye2 + pw[p]))
        for p in probs:
            bb, cg, j = p
            rows = shared[bb, cg][0]
            uw = dot(t_inv[p], jnp.concatenate([blockdiag(vb[p]), blockdiag(kbg[p])], axis=1))
            u_scr[bb, rows, pl.ds(LANES * j, LANES)] = uw[:, 0:LANES]
            w_scr[bb, rows, pl.ds(LANES * j, LANES)] = uw[:, LANES:2 * LANES]
        return carry

    def state_part(ci_, carry):
        rows = pl.ds(_aligned(ci_, chunk), chunk)
        el_rows = pl.ds(_aligned(ci_, SUBLANES), SUBLANES)
        probs = [(bb, j) for bb in range(n_bb) for j in range(n_pairs)]
        s_old, both, v_new, o_in, s_add = {}, {}, {}, {}, {}
        for p in probs:
            bb, j = p
            pl_ = pl.ds(LANES * j, LANES)
            s_old[p] = s_scr[bb, j]
            lhs = jnp.concatenate([w_scr[bb, rows, pl_], qd_scr[bb, rows, pl_]], axis=0)
            both[p] = dot(lhs, blockdiag(s_old[p]))
        for p in probs:
            bb, j = p
            pl_ = pl.ds(LANES * j, LANES)
            v_new[p] = u_scr[bb, rows, pl_] - both[p][0:chunk]
            o_in[p] = dot(in_scr[bb, rows, pl_], blockdiag(v_new[p]))
            cross = lax.dot_general(kd_scr[bb, rows, pl_], v_new[p], (((0,), (0,)), ((), ())),
                                    preferred_element_type=F32)
            s_add[p] = jnp.where(second, cross[hd:2 * hd], cross[0:hd])
        for p in probs:
            bb, j = p
            pl_ = pl.ds(LANES * j, LANES)
            oc_scr[bb, rows, pl_] = both[p][chunk:2 * chunk] + o_in[p]
            s_scr[bb, j] = s_old[p] * el_scr[bb, el_rows, pl_][0:1, :] + s_add[p]
        return carry

    assert n_chunks % group == 0
    if n_chunks == group:
        state_free_part(0, 0)
    else:
        lax.fori_loop(0, n_chunks // group, state_free_part, 0)
    if n_chunks == 1:
        state_part(0, 0)
    else:
        lax.fori_loop(0, n_chunks, state_part, 0)

    for bb in range(n_bb):
        o = oc_scr[bb]
        o = o * lax.rsqrt(_head_sums(o * o, ones_ref) * (1.0 / hd) + RMS_EPS) * nw_ref[...]
        o_ref[bb] = o * jax.nn.silu(z_ref[bb])

    @pl.when(t == pl.num_programs(1) - 1)
    def _():
        for bb in range(n_bb):
            for j in range(n_pairs):
                sfin_ref[bb, 2 * j] = s_scr[bb, j][:, 0:hd]
                sfin_ref[bb, 2 * j + 1] = s_scr[bb, j][:, hd:2 * hd]


def _head_ones():
    ones_bd = np.kron(np.eye(N_HEADS, dtype=np.float32), np.ones((HEAD_DIM, HEAD_DIM), np.float32))
    return jnp.asarray(ones_bd, BF16)


def _gdn_pairs_call(qkv, z, beta, g, s0, norm_w_row, token_block, group):
    bsz, seq, w3 = qkv.shape
    hw = HEAD_WIDTH
    chunk = HEAD_DIM
    nbb = GDN_BATCH_BLOCK
    assert w3 == 3 * hw and seq % token_block == 0 and token_block % (chunk * group) == 0 and bsz % nbb == 0
    tb = token_block
    ones_bd = _head_ones()
    blk = lambda w: pl.BlockSpec((nbb, tb, w), lambda b, t: (b, t, 0))
    per_b3 = lambda s: pl.BlockSpec((nbb,) + s, lambda b, t: (b,) + (0,) * len(s))
    return pl.pallas_call(
        functools.partial(_gdn_pairs_kernel, group=group),
        grid=(bsz // nbb, seq // tb),
        in_specs=[blk(w3), blk(hw), blk(LANES), blk(LANES), per_b3((N_HEADS, HEAD_DIM, HEAD_DIM)),
                  _resident(norm_w_row.shape), _resident(ones_bd.shape)],
        out_specs=[blk(hw), per_b3((N_HEADS, HEAD_DIM, HEAD_DIM))],
        out_shape=[jax.ShapeDtypeStruct((bsz, seq, hw), F32),
                   jax.ShapeDtypeStruct((bsz, N_HEADS, HEAD_DIM, HEAD_DIM), F32)],
        scratch_shapes=[pltpu.VMEM((nbb, tb, hw), F32)]
                       + [pltpu.VMEM((nbb, N_HEADS // 2, HEAD_DIM, LANES), F32)]
                       + [pltpu.VMEM((nbb, tb, hw), F32)] * 5
                       + [pltpu.VMEM((nbb, (tb // chunk) * SUBLANES, hw), F32)],
        compiler_params=pltpu.CompilerParams(dimension_semantics=("parallel", "arbitrary"),
                                             vmem_limit_bytes=VMEM_LIMIT),
        name="gdn_pairs",
    )(qkv, z, beta, g, s0, norm_w_row, ones_bd)


def _gdn_step_kernel(xq_ref, xk_ref, xv_ref, cq_ref, ck_ref, cv_ref, wq_ref, wk_ref, wv_ref, z_ref, ba_ref, hp_ref,
                     nw_ref, s0_ref, o_ref, s_ref, k_scr, q_scr):
    n_tok, hd, _ = xq_ref.shape
    h = pl.program_id(0)
    col_sum = lambda x: jnp.sum(x, axis=0, keepdims=True)

    def conv_act(x_ref, c_ref, w_ref, t):
        planes = [c_ref[i] for i in range(CONV_WIDTH - 1)] + [x_ref[i] for i in range(t + 1)]
        taps = planes[t:t + CONV_WIDTH]
        acc = jnp.zeros(taps[0].shape, F32)
        for j in range(CONV_WIDTH):
            acc = acc + taps[j] * w_ref[j]
        return jax.nn.silu(acc)

    s_ref[...] = s0_ref[...]
    neg_rate = -jnp.exp(hp_ref[0, pl.ds(h, 1), :])
    dt_bias = hp_ref[1, pl.ds(h, 1), :]
    for t in range(n_tok):
        q = conv_act(xq_ref, cq_ref, wq_ref, t)
        k = conv_act(xk_ref, ck_ref, wk_ref, t)
        v = conv_act(xv_ref, cv_ref, wv_ref, t)
        q_scr[...] = q * lax.rsqrt(col_sum(q * q) + RMS_EPS) * hd ** -0.5
        k_scr[...] = k * lax.rsqrt(col_sum(k * k) + RMS_EPS)
        beta = jax.nn.sigmoid(ba_ref[t, pl.ds(h, 1), :])
        decay = jnp.exp(neg_rate * jax.nn.softplus(ba_ref[t, pl.ds(N_HEADS + h, 1), :] + dt_bias))

        def k_dot_state(d, acc):
            return acc + k_scr[pl.ds(d, 1), :] * s_ref[0, d]

        ks = lax.fori_loop(0, hd, k_dot_state, jnp.zeros((hd, xq_ref.shape[2]), F32), unroll=8)
        delta = beta * (v - decay * ks)

        def update(d, acc):
            s_new = decay * s_ref[0, d] + k_scr[pl.ds(d, 1), :] * delta
            s_ref[0, d] = s_new
            return acc + q_scr[pl.ds(d, 1), :] * s_new

        o = lax.fori_loop(0, hd, update, jnp.zeros((hd, xq_ref.shape[2]), F32), unroll=8)
        o = o * lax.rsqrt(col_sum(o * o) * (1.0 / hd) + RMS_EPS) * nw_ref[...]
        o_ref[t] = o * jax.nn.silu(z_ref[t])


def _gdn_step_call(x_t, conv_t, conv_w_t, z_t, ba_t, hp_t, nw_t, s0_t):
    n_tok, w3, bsz = x_t.shape
    hd = HEAD_DIM
    part = lambda rows, p: pl.BlockSpec((rows, hd, bsz), lambda h, p=p: (0, p * N_HEADS + h, 0))
    whole = lambda a: pl.BlockSpec(a.shape, lambda h: (0,) * a.ndim)
    state_spec = pl.BlockSpec((1, hd, hd, bsz), lambda h: (h, 0, 0, 0))
    return pl.pallas_call(
        _gdn_step_kernel,
        grid=(N_HEADS,),
        in_specs=[part(n_tok, 0), part(n_tok, 1), part(n_tok, 2),
                  part(CONV_WIDTH - 1, 0), part(CONV_WIDTH - 1, 1), part(CONV_WIDTH - 1, 2),
                  part(CONV_WIDTH, 0), part(CONV_WIDTH, 1), part(CONV_WIDTH, 2),
                  part(n_tok, 0), whole(ba_t), whole(hp_t), whole(nw_t), state_spec],
        out_specs=[part(n_tok, 0), state_spec],
        out_shape=[jax.ShapeDtypeStruct((n_tok, N_HEADS * hd, bsz), F32),
                   jax.ShapeDtypeStruct(s0_t.shape, F32)],
        scratch_shapes=[pltpu.VMEM((hd, bsz), F32)] * 2,
        compiler_params=pltpu.CompilerParams(dimension_semantics=("parallel",), vmem_limit_bytes=VMEM_LIMIT),
        name="gdn_step",
    )(x_t, x_t, x_t, conv_t, conv_t, conv_t, conv_w_t, conv_w_t, conv_w_t, z_t, ba_t, hp_t, nw_t, s0_t)


def _pad_rows(x, rows):
    cfg = [(0, 0)] * x.ndim
    cfg[1] = (0, rows - x.shape[1])
    return jnp.pad(x, cfg)


def kernel(x_prompt, x_sample, cache_attn_k, cache_attn_v, state_gdn, state_conv, rel_bias, ln1_g, ln1_b,
           ffn1_w_gate, ffn1_w_up, ffn1_w_down, w_in, w_out, gdn_conv_w, gdn_a_log, gdn_dt_bias, gdn_norm_w,
           ln2_g, ln2_b, ffn2_w_gate, ffn2_w_up, ffn2_w_down, ln3_g, ln3_b):
    depth = w_in.shape[0]
    alpha = (2.0 * depth) ** 0.25
    bsz, seq, d_model = x_prompt.shape
    dbsz, dseq, _ = x_sample.shape
    w_buf = cache_attn_k.shape[2]
    hw = HEAD_WIDTH
    in_cols = w_in.shape[2]
    assert in_cols == 7 * hw + 2 * N_HEADS and CONV_WIDTH - 1 <= dseq <= SAMPLE_T_PAD
    assert seq % GDN_TOKEN_BLOCK == 0 and GDN_TOKEN_BLOCK % (GDN_CHUNK * GDN_CHUNK_GROUP) == 0

    sample_lows, sample_buckets = _sample_layout(w_buf, dseq)
    tab_p = _t5_table_call(rel_bias, jnp.asarray(_prompt_buckets()))
    tab_s = _t5_table_call(rel_bias, jnp.asarray(sample_buckets))

    yp = x_prompt.reshape(bsz * seq, d_model)
    ys = x_sample.reshape(dbsz * dseq, d_model)
    collected = [[] for _ in range(8)]
    row = lambda v: v.reshape(1, -1)
    for layer in range(depth):
        wg1, wu1, wd1 = (w[layer].astype(BF16) for w in (ffn1_w_gate, ffn1_w_up, ffn1_w_down))
        wg2, wu2, wd2 = (w[layer].astype(BF16) for w in (ffn2_w_gate, ffn2_w_up, ffn2_w_down))
        win = jnp.pad(w_in[layer], ((0, 0), (0, 7 * hw + LANES - in_cols))).astype(BF16)
        wo = w_out[layer].astype(BF16)
        conv_w8 = jnp.pad(gdn_conv_w[layer], ((0, SUBLANES - CONV_WIDTH), (0, 0)))
        head_params = jnp.zeros((SUBLANES, LANES), F32)
        head_params = head_params.at[0, N_HEADS:2 * N_HEADS].set(gdn_a_log[layer])
        head_params = head_params.at[1, N_HEADS:2 * N_HEADS].set(gdn_dt_bias[layer])
        norm_w_row = jnp.tile(gdn_norm_w[layer], N_HEADS).reshape(1, hw)
        pre = functools.partial(_pre_call, wg=wg1, wu=wu1, wd=wd1, g=row(ln1_g[layer]), b=row(ln1_b[layer]),
                                win=win, alpha=alpha)
        post = functools.partial(_post_call, wo=wo, g2=row(ln2_g[layer]), b2=row(ln2_b[layer]), wg=wg2, wu=wu2,
                                 wd=wd2, g3=row(ln3_g[layer]), b3=row(ln3_b[layer]), alpha=alpha)

        x1, aq, ak, av, ak_t, av_t, gdn_qkv, z, beta, gate, gq_tail = _pre_prompt_call(
            yp, wg1, wu1, wd1, row(ln1_g[layer]), row(ln1_b[layer]), win, conv_w8, head_params, alpha, seq)
        shp = lambda a: a.reshape(bsz, seq, a.shape[1])
        attn = _attn_prompt_call(shp(aq), shp(ak), shp(av), tab_p)
        gdn, s_p = _gdn_pairs_call(shp(gdn_qkv), shp(z), shp(beta), shp(gate),
                                   jnp.zeros((bsz, N_HEADS, HEAD_DIM, HEAD_DIM), F32), norm_w_row,
                                   token_block=GDN_TOKEN_BLOCK, group=GDN_CHUNK_GROUP)
        wp = min(w_buf, seq)
        heads5 = lambda a_t: a_t.reshape(bsz, N_HEADS, HEAD_DIM, seq).transpose(0, 3, 1, 2)[:, seq - wp:]
        st_p = (heads5(ak_t), heads5(av_t), s_p, gq_tail[:, SUBLANES - (CONV_WIDTH - 1):])

        x1s, aqs, aks, avs, gqs, zs, bas = pre(ys)
        shs = lambda a: _pad_rows(a.reshape(dbsz, dseq, a.shape[1]), SAMPLE_T_PAD)
        heads5s = lambda a: a.reshape(dbsz, dseq, N_HEADS, HEAD_DIM)
        new_rows = lambda a: jnp.pad(heads5s(a).transpose(0, 2, 1, 3),
                                     ((0, 0), (0, 0), (0, SAMPLE_T_PAD - dseq), (0, 0)))
        by_head_t = lambda c: c.transpose(0, 2, 3, 1)
        yp, attn_s = _post_with_sample_attn_call(
            attn.reshape(bsz * seq, hw), gdn.reshape(bsz * seq, hw), x1, wo, row(ln2_g[layer]), row(ln2_b[layer]),
            wg2, wu2, wd2, row(ln3_g[layer]), row(ln3_b[layer]), alpha,
            shs(aqs), new_rows(aks), new_rows(avs), by_head_t(cache_attn_k[layer]), by_head_t(cache_attn_v[layer]),
            tab_s, sample_lows)
        to_lanes = lambda a: a.reshape(dbsz, dseq * a.shape[1]).T.reshape(dseq, a.shape[1], dbsz)
        on_lanes = lambda a: jnp.broadcast_to(a[..., None], a.shape + (dbsz,))
        gdn_t, s_t = _gdn_step_call(
            to_lanes(gqs), state_conv[layer].transpose(1, 2, 0), on_lanes(gdn_conv_w[layer]), to_lanes(zs),
            to_lanes(bas[:, :2 * N_HEADS]), on_lanes(jnp.stack([gdn_a_log[layer], gdn_dt_bias[layer]])),
            on_lanes(gdn_norm_w[layer]), state_gdn[layer].transpose(1, 2, 3, 0))
        gdn_s = gdn_t.reshape(dseq * hw, dbsz).T.reshape(dbsz * dseq, hw)
        s_s = s_t.transpose(3, 0, 1, 2)
        ys = post(attn_s[:, :dseq].reshape(dbsz * dseq, hw), gdn_s, x1s)
        st_s = (heads5s(aks), heads5s(avs), s_s,
                gqs.reshape(dbsz, dseq, 3 * hw)[:, dseq - (CONV_WIDTH - 1):])
        for lst, st in zip(collected, st_p + st_s):
            lst.append(st)
    outs = [jnp.stack(t, axis=0) for t in collected]
    return (yp.reshape(bsz, seq, d_model), ys.reshape(dbsz, dseq, d_model)) + tuple(outs)
```

```python
import functools
import math

import numpy as np
import jax
import jax.numpy as jnp
from jax import lax
from jax.experimental import pallas as pl
from jax.experimental.pallas import tpu as pltpu

F32 = jnp.float32
BF16 = jnp.bfloat16

HEAD_DIM = 64
N_HEADS = 8
HEAD_WIDTH = N_HEADS * HEAD_DIM
DILATED_PATTERNS = ((128, 1), (512, 4), (2048, 16))
KEYS_PER_BLOCK = 128
T5_BUCKETS = 32
T5_MAX_EXACT = 16
T5_MAX_DIST = 2048
CONV_WIDTH = 4
GDN_CHUNK = 64
INV_BASE_BLOCK = 4
LN_EPS = 1e-5
RMS_EPS = 1e-6
NEG_BIG = -1e30

LANES = 128
SUBLANES = 8
TOKEN_TILE = 256
POST_TOKEN_TILE = 512
GDN_TOKEN_BLOCK = 256
GDN_CHUNK_GROUP = 4
GDN_BATCH_BLOCK = 4
SAMPLE_T_PAD = 8
VMEM_LIMIT = 56 * 1024 * 1024


def _resident(shape):
    nd = len(shape)
    return pl.BlockSpec(shape, lambda *_: (0,) * nd, pipeline_mode=pl.Buffered(1))


def _layernorm(y, g, b):
    mu = jnp.mean(y, axis=-1, keepdims=True)
    yc = y - mu
    var = jnp.mean(yc * yc, axis=-1, keepdims=True)
    return yc * lax.rsqrt(var + LN_EPS) * g + b


def _swiglu(xb, wg_ref, wu_ref, wd_ref, h_scr):
    d_ff = wg_ref.shape[1]
    step = 2 * LANES
    assert d_ff % step == 0
    for c in range(d_ff // step):
        sl = slice(c * step, (c + 1) * step)
        gate = jnp.dot(xb, wg_ref[:, sl], preferred_element_type=F32)
        up = jnp.dot(xb, wu_ref[:, sl], preferred_element_type=F32)
        h_scr[:, sl] = (jax.nn.silu(gate) * up).astype(BF16)
    return jnp.dot(h_scr[...], wd_ref[...], preferred_element_type=F32)


PROJ_WIDTHS = (HEAD_WIDTH, HEAD_WIDTH, HEAD_WIDTH, 3 * HEAD_WIDTH, HEAD_WIDTH, LANES)
PROJ_EDGES = tuple(int(e) for e in np.cumsum((0,) + PROJ_WIDTHS))


def _first_half_step(x_ref, wg_ref, wu_ref, wd_ref, g_ref, b_ref, h_scr, alpha):
    x = x_ref[...]
    ff = _swiglu(x.astype(BF16), wg_ref, wu_ref, wd_ref, h_scr)
    return _layernorm(alpha * x + 0.5 * ff, g_ref[...], b_ref[...])


def _pre_kernel(x_ref, wg_ref, wu_ref, wd_ref, g_ref, b_ref, win_ref,
                x1_ref, aq_ref, ak_ref, av_ref, gq_ref, z_ref, ba_ref, h_scr, *, alpha):
    x1 = _first_half_step(x_ref, wg_ref, wu_ref, wd_ref, g_ref, b_ref, h_scr, alpha)
    x1_ref[...] = x1
    xb = x1.astype(BF16)
    for i, ref in enumerate((aq_ref, ak_ref, av_ref, gq_ref, z_ref, ba_ref)):
        ref[...] = jnp.dot(xb, win_ref[:, PROJ_EDGES[i]:PROJ_EDGES[i + 1]], preferred_element_type=F32)


def _gdn_gates(ba, hp_ref):
    beta = jax.nn.sigmoid(ba)
    g = -jnp.exp(hp_ref[0:1, :]) * jax.nn.softplus(ba + hp_ref[1:2, :])
    return beta, g


def _causal_conv_silu(ext_scr, x, cw_ref):
    tb = x.shape[0]
    hist = SUBLANES
    ext_scr[hist:hist + tb, :] = x
    ext = ext_scr[0:hist + tb, :]
    conv = jnp.zeros(x.shape, F32)
    for j in range(CONV_WIDTH):
        back = CONV_WIDTH - 1 - j
        rows = pltpu.roll(ext, back, axis=0)[hist:hist + tb, :] if back else x
        conv = conv + rows * cw_ref[j:j + 1, :]
    ext_scr[0:hist, :] = ext_scr[tb:tb + hist, :]
    return jax.nn.silu(conv)


def _pre_prompt_kernel(x_ref, wg_ref, wu_ref, wd_ref, g_ref, b_ref, win_ref, cw_ref, hp_ref, ones_ref,
                       x1_ref, aq_ref, ak_ref, av_ref, akt_ref, avt_ref, qkv_ref, z_ref, bt_ref, gg_ref, tail_ref,
                       h_scr, ext_scr, *, alpha, tiles_per_seq):
    x1 = _first_half_step(x_ref, wg_ref, wu_ref, wd_ref, g_ref, b_ref, h_scr, alpha)
    x1_ref[...] = x1
    xb = x1.astype(BF16)
    proj = lambda i: jnp.dot(xb, win_ref[:, PROJ_EDGES[i]:PROJ_EDGES[i + 1]], preferred_element_type=F32)

    @pl.when(pl.program_id(0) % tiles_per_seq == 0)
    def _():
        ext_scr[0:SUBLANES, :] = jnp.zeros((SUBLANES, ext_scr.shape[1]), F32)

    raw = proj(3)
    tail_ref[0] = raw[raw.shape[0] - SUBLANES:, :]
    act = _causal_conv_silu(ext_scr, raw, cw_ref)
    hw = HEAD_WIDTH
    head_sq = lambda x: jnp.dot((x * x).astype(BF16), ones_ref[...], preferred_element_type=F32)
    q = act[:, 0:hw]
    k = act[:, hw:2 * hw]
    qkv_ref[:, 0:hw] = q * lax.rsqrt(head_sq(q) + RMS_EPS) * HEAD_DIM ** -0.5
    qkv_ref[:, hw:2 * hw] = k * lax.rsqrt(head_sq(k) + RMS_EPS)
    qkv_ref[:, 2 * hw:3 * hw] = act[:, 2 * hw:3 * hw]
    bt_ref[...], gg_ref[...] = _gdn_gates(proj(5), hp_ref)
    aq_ref[...] = proj(0)
    for i, ref, ref_t in ((1, ak_ref, akt_ref), (2, av_ref, avt_ref)):
        kv = proj(i)
        ref[...] = kv
        ref_t[0] = kv.T
    z_ref[...] = proj(4)


def _pre_call(x2d, wg, wu, wd, g, b, win, alpha):
    m, d = x2d.shape
    d_ff = wg.shape[1]
    tm = TOKEN_TILE
    assert m % tm == 0 and sum(PROJ_WIDTHS) == win.shape[1]
    tile = lambda w: pl.BlockSpec((tm, w), lambda i: (i, 0))
    return pl.pallas_call(
        functools.partial(_pre_kernel, alpha=alpha),
        grid=(m // tm,),
        in_specs=[tile(d), _resident(wg.shape), _resident(wu.shape), _resident(wd.shape),
                  _resident(g.shape), _resident(b.shape), _resident(win.shape)],
        out_specs=[tile(d)] + [tile(w) for w in PROJ_WIDTHS],
        out_shape=[jax.ShapeDtypeStruct((m, d), F32)] + [jax.ShapeDtypeStruct((m, w), F32) for w in PROJ_WIDTHS],
        scratch_shapes=[pltpu.VMEM((tm, d_ff), BF16)],
        compiler_params=pltpu.CompilerParams(dimension_semantics=("parallel",), vmem_limit_bytes=VMEM_LIMIT),
        name="pre_ffn_proj",
    )(x2d, wg, wu, wd, g, b, win)


def _pre_prompt_call(x2d, wg, wu, wd, g, b, win, conv_w8, head_params, alpha, seq):
    m, d = x2d.shape
    d_ff = wg.shape[1]
    tm = TOKEN_TILE
    hw = HEAD_WIDTH
    assert seq % tm == 0 and m % seq == 0 and sum(PROJ_WIDTHS) == win.shape[1]
    per_seq = seq // tm
    tile = lambda w: pl.BlockSpec((tm, w), lambda i: (i, 0))
    t_spec = pl.BlockSpec((1, hw, tm), lambda i: (i // per_seq, 0, i % per_seq))
    t_shape = jax.ShapeDtypeStruct((m // seq, hw, seq), F32)
    rows = lambda w: jax.ShapeDtypeStruct((m, w), F32)
    consts = (wg, wu, wd, g, b, win, conv_w8, head_params, _head_ones())
    return pl.pallas_call(
        functools.partial(_pre_prompt_kernel, alpha=alpha, tiles_per_seq=per_seq),
        grid=(m // tm,),
        in_specs=[tile(d)] + [_resident(c.shape) for c in consts],
        out_specs=[tile(d), tile(hw), tile(hw), tile(hw), t_spec, t_spec, tile(3 * hw), tile(hw), tile(LANES),
                   tile(LANES), pl.BlockSpec((1, SUBLANES, 3 * hw), lambda i: (i // per_seq, 0, 0))],
        out_shape=[rows(d), rows(hw), rows(hw), rows(hw), t_shape, t_shape, rows(3 * hw), rows(hw), rows(LANES),
                   rows(LANES), jax.ShapeDtypeStruct((m // seq, SUBLANES, 3 * hw), F32)],
        scratch_shapes=[pltpu.VMEM((tm, d_ff), BF16), pltpu.VMEM((tm + 2 * SUBLANES, 3 * hw), F32)],
        compiler_params=pltpu.CompilerParams(dimension_semantics=("arbitrary",), vmem_limit_bytes=VMEM_LIMIT),
        name="pre_ffn_proj_gdnprep",
    )(x2d, *consts)


def _post_kernel(attn_ref, gdn_ref, x1_ref, wo_ref, g2_ref, b2_ref, wg_ref, wu_ref, wd_ref, g3_ref, b3_ref,
                 y_ref, h_scr, *, alpha):
    tm = y_ref.shape[0]
    n_split = 2 if tm >= 2 * TOKEN_TILE else 1
    halves = [pl.ds(i * (tm // n_split), tm // n_split) for i in range(n_split)]
    mix, x2, ff = {}, {}, {}
    for i, rows in enumerate(halves):
        mix[i] = (jnp.dot(attn_ref[rows, :].astype(BF16), wo_ref[0:HEAD_WIDTH, :], preferred_element_type=F32)
                  + jnp.dot(gdn_ref[rows, :].astype(BF16), wo_ref[HEAD_WIDTH:2 * HEAD_WIDTH, :],
                            preferred_element_type=F32))
    for i, rows in enumerate(halves):
        x2[i] = _layernorm(alpha * x1_ref[rows, :] + mix[i], g2_ref[...], b2_ref[...])
    for i, rows in enumerate(halves):
        ff[i] = _swiglu(x2[i].astype(BF16), wg_ref, wu_ref, wd_ref, h_scr.at[rows])
    for i, rows in enumerate(halves):
        y_ref[rows, :] = _layernorm(alpha * x2[i] + 0.5 * ff[i], g3_ref[...], b3_ref[...])


def _post_call(attn, gdn, x1, wo, g2, b2, wg, wu, wd, g3, b3, alpha):
    m, d = x1.shape
    tm = POST_TOKEN_TILE
    assert m % tm == 0
    tile = lambda w: pl.BlockSpec((tm, w), lambda i: (i, 0))
    consts = (wo, g2, b2, wg, wu, wd, g3, b3)
    return pl.pallas_call(
        functools.partial(_post_kernel, alpha=alpha),
        grid=(m // tm,),
        in_specs=[tile(HEAD_WIDTH), tile(HEAD_WIDTH), tile(d)] + [_resident(c.shape) for c in consts],
        out_specs=tile(d),
        out_shape=jax.ShapeDtypeStruct((m, d), F32),
        scratch_shapes=[pltpu.VMEM((tm, wg.shape[1]), BF16)],
        compiler_params=pltpu.CompilerParams(dimension_semantics=("parallel",), vmem_limit_bytes=VMEM_LIMIT),
        name="post_out_ffn",
    )(attn, gdn, x1, *consts)


def _t5_bucket_np(dist):
    n = np.maximum(dist, 0)
    nf = np.maximum(n, 1).astype(np.float32)
    large = T5_MAX_EXACT + (np.log(nf / np.float32(T5_MAX_EXACT)) / np.float32(math.log(T5_MAX_DIST / T5_MAX_EXACT))
                            * np.float32(T5_BUCKETS - T5_MAX_EXACT)).astype(np.int32)
    large = np.minimum(large, T5_BUCKETS - 1)
    return np.where(n < T5_MAX_EXACT, n, large).astype(np.int32)


def _t5_table_kernel(rb_ref, bk_ref, out_ref):
    for h in range(N_HEADS):
        bk = bk_ref[0, h % bk_ref.shape[1]]
        acc = jnp.full(bk.shape, NEG_BIG, F32)
        for b in range(T5_BUCKETS):
            acc = jnp.where(bk == b, rb_ref[b, h], acc)
        out_ref[0, h] = acc


def _t5_table_call(rel_bias, buckets):
    p, hb, r, c = buckets.shape
    assert hb in (1, N_HEADS)
    return pl.pallas_call(
        _t5_table_kernel,
        grid=(p,),
        in_specs=[pl.BlockSpec(memory_space=pltpu.SMEM),
                  pl.BlockSpec((1, hb, r, c), lambda i: (i, 0, 0, 0))],
        out_specs=pl.BlockSpec((1, N_HEADS, r, c), lambda i: (i, 0, 0, 0)),
        out_shape=jax.ShapeDtypeStruct((p, N_HEADS, r, c), F32),
        compiler_params=pltpu.CompilerParams(dimension_semantics=("parallel",)),
        name="t5_bias_table",
    )(rel_bias, buckets)


def _prompt_buckets():
    nb = KEYS_PER_BLOCK
    qi = np.arange(nb)[:, None]
    ki = np.arange(2 * nb)[None, :]
    dist = qi + nb - ki
    valid = (dist >= 0) & (dist <= nb)
    out = []
    for window, dil in DILATED_PATTERNS:
        assert window // dil == nb
        out.append(np.where(valid, _t5_bucket_np(dist * dil), -1))
    return np.stack(out).astype(np.int32)[:, None]


def _nt_dot(a, b):
    return lax.dot_general(a, b, (((1,), (1,)), ((), ())), preferred_element_type=F32)


def _attn_prompt_kernel(q_ref, k_ref, v_ref, tab_ref, o_ref, og_scr, lg_scr, *, seq):
    nb = KEYS_PER_BLOCK
    hd = HEAD_DIM
    n_pat = len(DILATED_PATTERNS)
    heads = LANES // hd
    for g, (window, dil) in enumerate(DILATED_PATTERNS):
        n_blocks = seq // (dil * nb)

        def blocks(starts, with_prev, g=g, dil=dil):
            lane_head = lax.broadcasted_iota(jnp.int32, (nb, LANES), 1) // hd
            n_keys = 2 * nb if with_prev else nb
            row_sl, qs, ks, vs = [], [], [], []
            for start in starts:
                first_key = start - dil * (n_keys - nb)
                row_sl.append(pl.ds(start, nb, stride=dil) if dil > 1 else pl.ds(start, nb))
                key_rows = pl.ds(first_key, n_keys, stride=dil) if dil > 1 else pl.ds(first_key, n_keys)
                qs.append(q_ref[0, row_sl[-1], :] * hd ** -0.5)
                ks.append(k_ref[0, key_rows, :].astype(BF16))
                vs.append(v_ref[0, key_rows, :].astype(BF16))
            chains = [(u, hh) for u in range(len(starts)) for hh in range(heads)]
            s, m, p, den, acc = {}, {}, {}, {}, {}

            def scores(c):
                u, hh = c
                qb = jnp.where(lane_head == hh, qs[u], 0.0).astype(BF16)
                s[c] = _nt_dot(qb, ks[u]) + tab_ref[g, hh, :, 2 * nb - n_keys:2 * nb]

            def row_max(c):
                m[c] = jnp.max(s[c], axis=-1, keepdims=True)

            def probs(c):
                p[c] = jnp.exp(s[c] - m[c])
                den[c] = jnp.sum(p[c], axis=-1, keepdims=True)

            def values(c):
                u, hh = c
                acc[c] = jnp.dot(p[c].astype(BF16), vs[u], preferred_element_type=F32)

            def finish(c):
                acc[c] = acc[c] / den[c]
                den[c] = m[c] + jnp.log(den[c])

            for stage in (scores, row_max, probs, values, finish):
                for c in chains:
                    stage(c)
            for u in range(len(starts)):
                out, lse = acc[u, 0], den[u, 0]
                for hh in range(1, heads):
                    out = jnp.where(lane_head == hh, acc[u, hh], out)
                    lse = jnp.where(lane_head == hh, den[u, hh], lse)
                og_scr[g, row_sl[u], :] = out
                lg_scr[g, row_sl[u], :] = jnp.broadcast_to(lse, (nb, LANES))

        def unroll_of(count):
            return next(u for u in (8, 6, 5, 4, 3, 2, 1) if count % u == 0)

        u_first = unroll_of(dil)

        def first_body(i, carry, blocks=blocks, u_first=u_first):
            blocks([i * u_first + j for j in range(u_first)], False)
            return carry

        lax.fori_loop(0, dil // u_first, first_body, 0)
        if n_blocks > 1:
            n_rest = dil * (n_blocks - 1)
            u_rest = unroll_of(n_rest)

            def rest_body(i, carry, blocks=blocks, dil=dil, n_blocks=n_blocks, u_rest=u_rest):
                starts = []
                for j in range(u_rest):
                    idx = i * u_rest + j
                    starts.append(idx // (n_blocks - 1) + dil * nb * (idx % (n_blocks - 1) + 1))
                blocks(starts, True)
                return carry

            lax.fori_loop(0, n_rest // u_rest, rest_body, 0)

    rb = 2 * nb

    def merge_body(i, carry):
        rows = pl.ds(pl.multiple_of(i * rb, rb), rb)
        lses = [lg_scr[g, rows, :] for g in range(n_pat)]
        top = functools.reduce(jnp.maximum, lses)
        ws = [jnp.exp(l - top) for l in lses]
        num = functools.reduce(lambda a, b: a + b, [w * og_scr[g, rows, :] for g, w in enumerate(ws)])
        o_ref[0, rows, :] = num / functools.reduce(lambda a, b: a + b, ws)
        return carry

    lax.fori_loop(0, seq // rb, merge_body, 0)


def _attn_prompt_call(aq, ak, av, tab):
    bsz, seq, width = aq.shape
    assert width == HEAD_WIDTH and seq % (DILATED_PATTERNS[-1][1] * KEYS_PER_BLOCK) == 0
    n_pat = len(DILATED_PATTERNS)
    heads_per_step = LANES // HEAD_DIM
    qkv_spec = pl.BlockSpec((1, seq, LANES), lambda h, b: (b, 0, h))
    return pl.pallas_call(
        functools.partial(_attn_prompt_kernel, seq=seq),
        grid=(N_HEADS // heads_per_step, bsz),
        in_specs=[qkv_spec, qkv_spec, qkv_spec,
                  pl.BlockSpec((n_pat, heads_per_step, KEYS_PER_BLOCK, 2 * KEYS_PER_BLOCK),
                               lambda h, b: (0, h, 0, 0))],
        out_specs=pl.BlockSpec((1, seq, LANES), lambda h, b: (b, 0, h)),
        out_shape=jax.ShapeDtypeStruct((bsz, seq, width), F32),
        scratch_shapes=[pltpu.VMEM((n_pat, seq, LANES), F32)] * 2,
        compiler_params=pltpu.CompilerParams(dimension_semantics=("parallel", "parallel"),
                                             vmem_limit_bytes=VMEM_LIMIT),
        name="attn_prompt",
    )(aq, ak, av, tab)


def _sample_layout(w_buf, t_new):
    assert w_buf % LANES == 0 and t_new <= SAMPLE_T_PAD
    row_of_col = np.full((w_buf + LANES,), -1, np.int64)
    row_of_col[:w_buf + t_new] = np.arange(w_buf + t_new)
    buckets = np.full((len(DILATED_PATTERNS), 1, SAMPLE_T_PAD, w_buf + LANES), -1, np.int32)
    lows = []
    for g, (window, dil) in enumerate(DILATED_PATTERNS):
        lows.append(max(0, w_buf - window) // LANES * LANES)
        for t in range(t_new):
            dist = w_buf + t - row_of_col
            ok = (row_of_col >= 0) & (dist >= 0) & (dist <= window) & (dist % dil == 0)
            assert int(ok.sum()) == min(window, w_buf + t) // dil + 1 and not ok[:lows[g]].any()
            buckets[g, 0, t] = np.where(ok, _t5_bucket_np(dist), -1)
    return tuple(lows), buckets


def _attn_sample_kernel(q_ref, kn_ref, vn_ref, kt_ref, vt_ref, tab_ref, o_ref, *, lows):
    tp = SAMPLE_T_PAD
    hd = HEAD_DIM
    w_buf = kt_ref.shape[3]
    n_pat = len(DILATED_PATTERNS)
    add = lambda x, y: x + y
    rowmax = lambda x: jnp.max(x, axis=-1, keepdims=True)
    rowsum = lambda x: jnp.sum(x, axis=-1, keepdims=True)
    heads = range(N_HEADS)
    q8 = q_ref[0] * hd ** -0.5
    s_buf = {h: jnp.dot(q8[:, hd * h:hd * (h + 1)], kt_ref[0, h], preferred_element_type=F32) for h in heads}
    s_new = {h: _nt_dot(q8[:, hd * h:hd * (h + 1)], kn_ref[0, h]) for h in heads}
    p_buf, p_new, den, lse = {}, {}, {}, {}
    for h in heads:
        for g in range(n_pat):
            lo = lows[g]
            l_buf = s_buf[h][:, lo:] + tab_ref[g, h, :, lo:w_buf]
            l_new = s_new[h] + tab_ref[g, h, :, w_buf:w_buf + tp]
            m = jnp.maximum(rowmax(l_buf), rowmax(l_new))
            p_buf[h, g] = jnp.exp(l_buf - m)
            p_new[h, g] = jnp.exp(l_new - m)
            den[h, g] = rowsum(p_buf[h, g]) + rowsum(p_new[h, g])
            lse[h, g] = m + jnp.log(den[h, g])
    edges = sorted(set(lows)) + [w_buf]
    outs = {}
    for h in heads:
        top = functools.reduce(jnp.maximum, [lse[h, g] for g in range(n_pat)])
        ws = [jnp.exp(lse[h, g] - top) for g in range(n_pat)]
        w_sum = functools.reduce(add, ws)
        coef = [ws[g] / (w_sum * den[h, g]) for g in range(n_pat)]
        acc = jnp.dot(functools.reduce(add, [coef[g] * p_new[h, g] for g in range(n_pat)]), vn_ref[0, h],
                      preferred_element_type=F32)
        for e0, e1 in zip(edges[:-1], edges[1:]):
            mix = functools.reduce(add, [coef[g] * p_buf[h, g][:, e0 - lows[g]:e1 - lows[g]]
                                         for g in range(n_pat) if lows[g] <= e0])
            acc = acc + _nt_dot(mix, vt_ref[0, h, :, e0:e1])
        outs[h] = acc
    o_ref[0] = jnp.concatenate([outs[h] for h in heads], axis=1)


N_POST_INPUTS = 11
N_ATTN_SAMPLE_INPUTS = 6


def _post_with_sample_attn_kernel(*refs, alpha, lows):
    post_in = refs[:N_POST_INPUTS]
    attn_in = refs[N_POST_INPUTS:N_POST_INPUTS + N_ATTN_SAMPLE_INPUTS]
    y_ref, o_ref, h_scr = refs[N_POST_INPUTS + N_ATTN_SAMPLE_INPUTS:]
    _post_kernel(*post_in, y_ref, h_scr, alpha=alpha)
    _attn_sample_kernel(*attn_in, o_ref, lows=lows)


def _post_with_sample_attn_call(attn, gdn, x1, wo, g2, b2, wg, wu, wd, g3, b3, alpha,
                                aq8, k_new, v_new, cache_kt, cache_vt, tab, lows):
    m, d = x1.shape
    bsz, n_heads, hd, w_buf = cache_kt.shape
    assert m % bsz == 0
    tm = m // bsz
    assert tm % SUBLANES == 0
    tile = lambda w: pl.BlockSpec((tm, w), lambda i: (i, 0))
    consts = (wo, g2, b2, wg, wu, wd, g3, b3)
    q_spec = pl.BlockSpec((1, SAMPLE_T_PAD, n_heads * hd), lambda b: (b, 0, 0))
    new_spec = pl.BlockSpec((1, n_heads, SAMPLE_T_PAD, hd), lambda b: (b, 0, 0, 0))
    buf_spec = pl.BlockSpec((1, n_heads, hd, w_buf), lambda b: (b, 0, 0, 0))
    assert len(consts) + 3 == N_POST_INPUTS
    return pl.pallas_call(
        functools.partial(_post_with_sample_attn_kernel, alpha=alpha, lows=lows),
        grid=(bsz,),
        in_specs=[tile(HEAD_WIDTH), tile(HEAD_WIDTH), tile(d)] + [_resident(c.shape) for c in consts]
                 + [q_spec, new_spec, new_spec, buf_spec, buf_spec, _resident(tab.shape)],
        out_specs=[tile(d), q_spec],
        out_shape=[jax.ShapeDtypeStruct((m, d), F32),
                   jax.ShapeDtypeStruct((bsz, SAMPLE_T_PAD, n_heads * hd), F32)],
        scratch_shapes=[pltpu.VMEM((tm, wg.shape[1]), BF16)],
        compiler_params=pltpu.CompilerParams(dimension_semantics=("parallel",), vmem_limit_bytes=VMEM_LIMIT),
        name="post_with_sample_attn",
    )(attn, gdn, x1, *consts, aq8, k_new, v_new, cache_kt, cache_vt, tab)


def _aligned(index, size):
    return index * size if isinstance(index, int) else pl.multiple_of(index * size, size)


def _split3(x):
    h1 = x.astype(BF16)
    r1 = x - h1.astype(F32)
    h2 = r1.astype(BF16)
    h3 = (r1 - h2.astype(F32)).astype(BF16)
    return h1, h2, h3


def _head_sums(x, ones_ref):
    h1, h2, _ = _split3(x)
    return (jnp.dot(h1, ones_ref[...], preferred_element_type=F32)
            + jnp.dot(h2, ones_ref[...], preferred_element_type=F32))


def _gdn_pairs_kernel(qkv_ref, z_ref, bt_scr, g_scr, s0_ref, nw_ref, ones_ref, o_ref, sfin_ref,
                      oc_scr, s_scr, u_scr, w_scr, in_scr, qd_scr, kd_scr, el_scr, *, group):
    chunk = HEAD_DIM
    k_lane0, v_lane0 = HEAD_WIDTH, 2 * HEAD_WIDTH
    n_bb, tb = bt_scr.shape[0], bt_scr.shape[1]
    n_chunks = tb // chunk
    n_pairs = N_HEADS // 2
    hd = HEAD_DIM
    t = pl.program_id(1)
    dot = functools.partial(jnp.dot, preferred_element_type=F32)

    lane = lax.broadcasted_iota(jnp.int32, (chunk, LANES), 1)
    row = lax.broadcasted_iota(jnp.int32, (chunk, LANES), 0)
    second = lane >= hd
    col = lane - jnp.where(second, hd, 0)
    tril = row >= col
    strict = row > col
    eye2 = (row == col).astype(F32)
    tril_b = (lax.broadcasted_iota(jnp.int32, (chunk, chunk), 0)
              >= lax.broadcasted_iota(jnp.int32, (chunk, chunk), 1)).astype(BF16)
    base_blocks = (row // INV_BASE_BLOCK) == (col // INV_BASE_BLOCK)
    level_blocks = []
    m = INV_BASE_BLOCK
    while m < chunk:
        level_blocks.append(((row // m) % 2 == 1) & ((col // m) == (row // m) - 1))
        m *= 2

    def blockdiag(x):
        return jnp.concatenate([jnp.where(second, 0.0, x), jnp.where(second, x, 0.0)], axis=0)

    def per_head(cols, j):
        return jnp.where(second, cols[:, 2 * j + 1:2 * j + 2], cols[:, 2 * j:2 * j + 1])

    @pl.when(t == 0)
    def _():
        for bb in range(n_bb):
            for j in range(n_pairs):
                s_scr[bb, j] = jnp.concatenate([s0_ref[bb, 2 * j], s0_ref[bb, 2 * j + 1]], axis=1)

    def state_free_part(gi, carry):
        probs, shared = [], {}
        for bb in range(n_bb):
            for cg in range(group):
                c_idx = gi * group + cg
                rows = pl.ds(_aligned(c_idx, chunk), chunk)
                gc = functools.reduce(lambda a, b: a + b,
                                      [dot(tril_b, p) for p in _split3(g_scr[bb, rows, :])])
                shared[bb, cg] = (rows, c_idx, gc[:, N_HEADS:2 * N_HEADS], gc.T[N_HEADS:2 * N_HEADS, :],
                                  bt_scr[bb, rows, :])
                probs += [(bb, cg, j) for j in range(n_pairs)]
        kk, qk, decay, a, t_inv, pw, vb, kbg = {}, {}, {}, {}, {}, {}, {}, {}
        for p in probs:
            bb, cg, j = p
            rows, c_idx, gc, gct, bt = shared[bb, cg]
            pl_ = pl.ds(LANES * j, LANES)
            q2 = qkv_ref[bb, rows, pl_]
            k2 = qkv_ref[bb, rows, pl.ds(k_lane0 + LANES * j, LANES)]
            beta2 = per_head(bt, j)
            gcc = per_head(gc, j)
            gcr = jnp.concatenate([gct[2 * j:2 * j + 1, :], gct[2 * j + 1:2 * j + 2, :]], axis=1)
            g_last = gcc[chunk - 1:chunk, :]
            e_gc = jnp.exp(gcc)
            kb2 = k2 * beta2
            k_bd = blockdiag(k2)
            both = _nt_dot(jnp.concatenate([kb2, q2], axis=0), k_bd)
            kk[p] = both[0:chunk]
            qk[p] = both[chunk:2 * chunk]
            vb[p] = qkv_ref[bb, rows, pl.ds(v_lane0 + LANES * j, LANES)] * beta2
            kbg[p] = kb2 * e_gc
            qd_scr[bb, rows, pl_] = q2 * e_gc
            kd_scr[bb, rows, pl_] = k2 * jnp.exp(g_last - gcc)
            el_scr[bb, pl.ds(_aligned(c_idx, SUBLANES), SUBLANES), pl_] = jnp.broadcast_to(
                jnp.exp(g_last), (SUBLANES, LANES))
            decay[p] = jnp.where(tril, jnp.exp(jnp.where(tril, gcc - gcr, 0.0)), 0.0)
        for p in probs:
            bb, cg, j = p
            rows = shared[bb, cg][0]
            a[p] = jnp.where(strict, kk[p] * decay[p], 0.0)
            in_scr[bb, rows, pl.ds(LANES * j, LANES)] = jnp.where(tril, qk[p] * decay[p], 0.0)
        for p in probs:
            d = jnp.where(base_blocks, a[p], 0.0)
            pw[p] = d
            t_inv[p] = eye2 - d
        for p in probs:
            pw[p] = dot(pw[p], blockdiag(pw[p]))
        for p in probs:
            t_inv[p] = dot(t_inv[p], blockdiag(eye2 + pw[p]))
        for lower_left in level_blocks:
            for p in probs:
                pw[p] = dot(t_inv[p], blockdiag(jnp.where(lower_left, a[p], 0.0)))
            for p in probs:
                t_inv[p] = t_inv[p] - dot(pw[p], blockdiag(t_inv[p]))
        for p in probs:
            bb, cg, j = p
            rows = shared[bb, cg][0]
            uw = dot(t_inv[p], jnp.concatenate([blockdiag(vb[p]), blockdiag(kbg[p])], axis=1))
            u_scr[bb, rows, pl.ds(LANES * j, LANES)] = uw[:, 0:LANES]
            w_scr[bb, rows, pl.ds(LANES * j, LANES)] = uw[:, LANES:2 * LANES]
        return carry

    def state_part(ci_, carry):
        rows = pl.ds(_aligned(ci_, chunk), chunk)
        el_rows = pl.ds(_aligned(ci_, SUBLANES), SUBLANES)
        probs = [(bb, j) for bb in range(n_bb) for j in range(n_pairs)]
        s_old, both, v_new, o_in, s_add = {}, {}, {}, {}, {}
        for p in probs:
            bb, j = p
            pl_ = pl.ds(LANES * j, LANES)
            s_old[p] = s_scr[bb, j]
            lhs = jnp.concatenate([w_scr[bb, rows, pl_], qd_scr[bb, rows, pl_]], axis=0)
            both[p] = dot(lhs, blockdiag(s_old[p]))
        for p in probs:
            bb, j = p
            pl_ = pl.ds(LANES * j, LANES)
            v_new[p] = u_scr[bb, rows, pl_] - both[p][0:chunk]
            o_in[p] = dot(in_scr[bb, rows, pl_], blockdiag(v_new[p]))
            cross = lax.dot_general(kd_scr[bb, rows, pl_], v_new[p], (((0,), (0,)), ((), ())),
                                    preferred_element_type=F32)
            s_add[p] = jnp.where(second, cross[hd:2 * hd], cross[0:hd])
        for p in probs:
            bb, j = p
            pl_ = pl.ds(LANES * j, LANES)
            oc_scr[bb, rows, pl_] = both[p][chunk:2 * chunk] + o_in[p]
            s_scr[bb, j] = s_old[p] * el_scr[bb, el_rows, pl_][0:1, :] + s_add[p]
        return carry

    assert n_chunks % group == 0
    if n_chunks == group:
        state_free_part(0, 0)
    else:
        lax.fori_loop(0, n_chunks // group, state_free_part, 0)
    if n_chunks == 1:
        state_part(0, 0)
    else:
        lax.fori_loop(0, n_chunks, state_part, 0)

    for bb in range(n_bb):
        o = oc_scr[bb]
        o = o * lax.rsqrt(_head_sums(o * o, ones_ref) * (1.0 / hd) + RMS_EPS) * nw_ref[...]
        o_ref[bb] = o * jax.nn.silu(z_ref[bb])

    @pl.when(t == pl.num_programs(1) - 1)
    def _():
        for bb in range(n_bb):
            for j in range(n_pairs):
                sfin_ref[bb, 2 * j] = s_scr[bb, j][:, 0:hd]
                sfin_ref[bb, 2 * j + 1] = s_scr[bb, j][:, hd:2 * hd]


def _head_ones():
    ones_bd = np.kron(np.eye(N_HEADS, dtype=np.float32), np.ones((HEAD_DIM, HEAD_DIM), np.float32))
    return jnp.asarray(ones_bd, BF16)


def _gdn_pairs_call(qkv, z, beta, g, s0, norm_w_row, token_block, group):
    bsz, seq, w3 = qkv.shape
    hw = HEAD_WIDTH
    chunk = HEAD_DIM
    nbb = GDN_BATCH_BLOCK
    assert w3 == 3 * hw and seq % token_block == 0 and token_block % (chunk * group) == 0 and bsz % nbb == 0
    tb = token_block
    ones_bd = _head_ones()
    blk = lambda w: pl.BlockSpec((nbb, tb, w), lambda b, t: (b, t, 0))
    per_b3 = lambda s: pl.BlockSpec((nbb,) + s, lambda b, t: (b,) + (0,) * len(s))
    return pl.pallas_call(
        functools.partial(_gdn_pairs_kernel, group=group),
        grid=(bsz // nbb, seq // tb),
        in_specs=[blk(w3), blk(hw), blk(LANES), blk(LANES), per_b3((N_HEADS, HEAD_DIM, HEAD_DIM)),
                  _resident(norm_w_row.shape), _resident(ones_bd.shape)],
        out_specs=[blk(hw), per_b3((N_HEADS, HEAD_DIM, HEAD_DIM))],
        out_shape=[jax.ShapeDtypeStruct((bsz, seq, hw), F32),
                   jax.ShapeDtypeStruct((bsz, N_HEADS, HEAD_DIM, HEAD_DIM), F32)],
        scratch_shapes=[pltpu.VMEM((nbb, tb, hw), F32)]
                       + [pltpu.VMEM((nbb, N_HEADS // 2, HEAD_DIM, LANES), F32)]
                       + [pltpu.VMEM((nbb, tb, hw), F32)] * 5
                       + [pltpu.VMEM((nbb, (tb // chunk) * SUBLANES, hw), F32)],
        compiler_params=pltpu.CompilerParams(dimension_semantics=("parallel", "arbitrary"),
                                             vmem_limit_bytes=VMEM_LIMIT),
        name="gdn_pairs",
    )(qkv, z, beta, g, s0, norm_w_row, ones_bd)


def _gdn_step_kernel(xq_ref, xk_ref, xv_ref, cq_ref, ck_ref, cv_ref, wq_ref, wk_ref, wv_ref, z_ref, ba_ref, hp_ref,
                     nw_ref, s0_ref, o_ref, s_ref, k_scr, q_scr):
    n_tok, hd, _ = xq_ref.shape
    h = pl.program_id(0)
    col_sum = lambda x: jnp.sum(x, axis=0, keepdims=True)

    def conv_act(x_ref, c_ref, w_ref, t):
        planes = [c_ref[i] for i in range(CONV_WIDTH - 1)] + [x_ref[i] for i in range(t + 1)]
        taps = planes[t:t + CONV_WIDTH]
        acc = jnp.zeros(taps[0].shape, F32)
        for j in range(CONV_WIDTH):
            acc = acc + taps[j] * w_ref[j]
        return jax.nn.silu(acc)

    s_ref[...] = s0_ref[...]
    neg_rate = -jnp.exp(hp_ref[0, pl.ds(h, 1), :])
    dt_bias = hp_ref[1, pl.ds(h, 1), :]
    for t in range(n_tok):
        q = conv_act(xq_ref, cq_ref, wq_ref, t)
        k = conv_act(xk_ref, ck_ref, wk_ref, t)
        v = conv_act(xv_ref, cv_ref, wv_ref, t)
        q_scr[...] = q * lax.rsqrt(col_sum(q * q) + RMS_EPS) * hd ** -0.5
        k_scr[...] = k * lax.rsqrt(col_sum(k * k) + RMS_EPS)
        beta = jax.nn.sigmoid(ba_ref[t, pl.ds(h, 1), :])
        decay = jnp.exp(neg_rate * jax.nn.softplus(ba_ref[t, pl.ds(N_HEADS + h, 1), :] + dt_bias))

        def k_dot_state(d, acc):
            return acc + k_scr[pl.ds(d, 1), :] * s_ref[0, d]

        ks = lax.fori_loop(0, hd, k_dot_state, jnp.zeros((hd, xq_ref.shape[2]), F32), unroll=8)
        delta = beta * (v - decay * ks)

        def update(d, acc):
            s_new = decay * s_ref[0, d] + k_scr[pl.ds(d, 1), :] * delta
            s_ref[0, d] = s_new
            return acc + q_scr[pl.ds(d, 1), :] * s_new

        o = lax.fori_loop(0, hd, update, jnp.zeros((hd, xq_ref.shape[2]), F32), unroll=8)
        o = o * lax.rsqrt(col_sum(o * o) * (1.0 / hd) + RMS_EPS) * nw_ref[...]
        o_ref[t] = o * jax.nn.silu(z_ref[t])


def _gdn_step_call(x_t, conv_t, conv_w_t, z_t, ba_t, hp_t, nw_t, s0_t):
    n_tok, w3, bsz = x_t.shape
    hd = HEAD_DIM
    part = lambda rows, p: pl.BlockSpec((rows, hd, bsz), lambda h, p=p: (0, p * N_HEADS + h, 0))
    whole = lambda a: pl.BlockSpec(a.shape, lambda h: (0,) * a.ndim)
    state_spec = pl.BlockSpec((1, hd, hd, bsz), lambda h: (h, 0, 0, 0))
    return pl.pallas_call(
        _gdn_step_kernel,
        grid=(N_HEADS,),
        in_specs=[part(n_tok, 0), part(n_tok, 1), part(n_tok, 2),
                  part(CONV_WIDTH - 1, 0), part(CONV_WIDTH - 1, 1), part(CONV_WIDTH - 1, 2),
                  part(CONV_WIDTH, 0), part(CONV_WIDTH, 1), part(CONV_WIDTH, 2),
                  part(n_tok, 0), whole(ba_t), whole(hp_t), whole(nw_t), state_spec],
        out_specs=[part(n_tok, 0), state_spec],
        out_shape=[jax.ShapeDtypeStruct((n_tok, N_HEADS * hd, bsz), F32),
                   jax.ShapeDtypeStruct(s0_t.shape, F32)],
        scratch_shapes=[pltpu.VMEM((hd, bsz), F32)] * 2,
        compiler_params=pltpu.CompilerParams(dimension_semantics=("parallel",), vmem_limit_bytes=VMEM_LIMIT),
        name="gdn_step",
    )(x_t, x_t, x_t, conv_t, conv_t, conv_t, conv_w_t, conv_w_t, conv_w_t, z_t, ba_t, hp_t, nw_t, s0_t)


def _pad_rows(x, rows):
    cfg = [(0, 0)] * x.ndim
    cfg[1] = (0, rows - x.shape[1])
    return jnp.pad(x, cfg)


def kernel(x_prompt, x_sample, cache_attn_k, cache_attn_v, state_gdn, state_conv, rel_bias, ln1_g, ln1_b,
           ffn1_w_gate, ffn1_w_up, ffn1_w_down, w_in, w_out, gdn_conv_w, gdn_a_log, gdn_dt_bias, gdn_norm_w,
           ln2_g, ln2_b, ffn2_w_gate, ffn2_w_up, ffn2_w_down, ln3_g, ln3_b):
    depth = w_in.shape[0]
    alpha = (2.0 * depth) ** 0.25
    bsz, seq, d_model = x_prompt.shape
    dbsz, dseq, _ = x_sample.shape
    w_buf = cache_attn_k.shape[2]
    hw = HEAD_WIDTH
    in_cols = w_in.shape[2]
    assert in_cols == 7 * hw + 2 * N_HEADS and CONV_WIDTH - 1 <= dseq <= SAMPLE_T_PAD
    assert seq % GDN_TOKEN_BLOCK == 0 and GDN_TOKEN_BLOCK % (GDN_CHUNK * GDN_CHUNK_GROUP) == 0

    sample_lows, sample_buckets = _sample_layout(w_buf, dseq)
    tab_p = _t5_table_call(rel_bias, jnp.asarray(_prompt_buckets()))
    tab_s = _t5_table_call(rel_bias, jnp.asarray(sample_buckets))

    yp = x_prompt.reshape(bsz * seq, d_model)
    ys = x_sample.reshape(dbsz * dseq, d_model)
    collected = [[] for _ in range(8)]
    row = lambda v: v.reshape(1, -1)
    for layer in range(depth):
        wg1, wu1, wd1 = (w[layer].astype(BF16) for w in (ffn1_w_gate, ffn1_w_up, ffn1_w_down))
        wg2, wu2, wd2 = (w[layer].astype(BF16) for w in (ffn2_w_gate, ffn2_w_up, ffn2_w_down))
        win = jnp.pad(w_in[layer], ((0, 0), (0, 7 * hw + LANES - in_cols))).astype(BF16)
        wo = w_out[layer].astype(BF16)
        conv_w8 = jnp.pad(gdn_conv_w[layer], ((0, SUBLANES - CONV_WIDTH), (0, 0)))
        head_params = jnp.zeros((SUBLANES, LANES), F32)
        head_params = head_params.at[0, N_HEADS:2 * N_HEADS].set(gdn_a_log[layer])
        head_params = head_params.at[1, N_HEADS:2 * N_HEADS].set(gdn_dt_bias[layer])
        norm_w_row = jnp.tile(gdn_norm_w[layer], N_HEADS).reshape(1, hw)
        pre = functools.partial(_pre_call, wg=wg1, wu=wu1, wd=wd1, g=row(ln1_g[layer]), b=row(ln1_b[layer]),
                                win=win, alpha=alpha)
        post = functools.partial(_post_call, wo=wo, g2=row(ln2_g[layer]), b2=row(ln2_b[layer]), wg=wg2, wu=wu2,
                                 wd=wd2, g3=row(ln3_g[layer]), b3=row(ln3_b[layer]), alpha=alpha)

        x1, aq, ak, av, ak_t, av_t, gdn_qkv, z, beta, gate, gq_tail = _pre_prompt_call(
            yp, wg1, wu1, wd1, row(ln1_g[layer]), row(ln1_b[layer]), win, conv_w8, head_params, alpha, seq)
        shp = lambda a: a.reshape(bsz, seq, a.shape[1])
        attn = _attn_prompt_call(shp(aq), shp(ak), shp(av), tab_p)
        gdn, s_p = _gdn_pairs_call(shp(gdn_qkv), shp(z), shp(beta), shp(gate),
                                   jnp.zeros((bsz, N_HEADS, HEAD_DIM, HEAD_DIM), F32), norm_w_row,
                                   token_block=GDN_TOKEN_BLOCK, group=GDN_CHUNK_GROUP)
        wp = min(w_buf, seq)
        heads5 = lambda a_t: a_t.reshape(bsz, N_HEADS, HEAD_DIM, seq).transpose(0, 3, 1, 2)[:, seq - wp:]
        st_p = (heads5(ak_t), heads5(av_t), s_p, gq_tail[:, SUBLANES - (CONV_WIDTH - 1):])

        x1s, aqs, aks, avs, gqs, zs, bas = pre(ys)
        shs = lambda a: _pad_rows(a.reshape(dbsz, dseq, a.shape[1]), SAMPLE_T_PAD)
        heads5s = lambda a: a.reshape(dbsz, dseq, N_HEADS, HEAD_DIM)
        new_rows = lambda a: jnp.pad(heads5s(a).transpose(0, 2, 1, 3),
                                     ((0, 0), (0, 0), (0, SAMPLE_T_PAD - dseq), (0, 0)))
        by_head_t = lambda c: c.transpose(0, 2, 3, 1)
        yp, attn_s = _post_with_sample_attn_call(
            attn.reshape(bsz * seq, hw), gdn.reshape(bsz * seq, hw), x1, wo, row(ln2_g[layer]), row(ln2_b[layer]),
            wg2, wu2, wd2, row(ln3_g[layer]), row(ln3_b[layer]), alpha,
            shs(aqs), new_rows(aks), new_rows(avs), by_head_t(cache_attn_k[layer]), by_head_t(cache_attn_v[layer]),
            tab_s, sample_lows)
        to_lanes = lambda a: a.reshape(dbsz, dseq * a.shape[1]).T.reshape(dseq, a.shape[1], dbsz)
        on_lanes = lambda a: jnp.broadcast_to(a[..., None], a.shape + (dbsz,))
        gdn_t, s_t = _gdn_step_call(
            to_lanes(gqs), state_conv[layer].transpose(1, 2, 0), on_lanes(gdn_conv_w[layer]), to_lanes(zs),
            to_lanes(bas[:, :2 * N_HEADS]), on_lanes(jnp.stack([gdn_a_log[layer], gdn_dt_bias[layer]])),
            on_lanes(gdn_norm_w[layer]), state_gdn[layer].transpose(1, 2, 3, 0))
        gdn_s = gdn_t.reshape(dseq * hw, dbsz).T.reshape(dbsz * dseq, hw)
        s_s = s_t.transpose(3, 0, 1, 2)
        ys = post(attn_s[:, :dseq].reshape(dbsz * dseq, hw), gdn_s, x1s)
        st_s = (heads5s(aks), heads5s(avs), s_s,
                gqs.reshape(dbsz, dseq, 3 * hw)[:, dseq - (CONV_WIDTH - 1):])
        for lst, st in zip(collected, st_p + st_s):
            lst.append(st)
    outs = [jnp.stack(t, axis=0) for t in collected]
    return (yp.reshape(bsz, seq, d_model), ys.reshape(dbsz, dseq, d_model)) + tuple(outs)
```

```python
import functools
import math

import numpy as np
import jax
import jax.numpy as jnp
from jax import lax
from jax.experimental import pallas as pl
from jax.experimental.pallas import tpu as pltpu

F32 = jnp.float32
BF16 = jnp.bfloat16

HEAD_DIM = 64
N_HEADS = 8
HEAD_WIDTH = N_HEADS * HEAD_DIM
DILATED_PATTERNS = ((128, 1), (512, 4), (2048, 16))
KEYS_PER_BLOCK = 128
T5_BUCKETS = 32
T5_MAX_EXACT = 16
T5_MAX_DIST = 2048
CONV_WIDTH = 4
GDN_CHUNK = 64
INV_BASE_BLOCK = 4
LN_EPS = 1e-5
RMS_EPS = 1e-6
NEG_BIG = -1e30

LANES = 128
SUBLANES = 8
TOKEN_TILE = 256
POST_TOKEN_TILE = 512
GDN_TOKEN_BLOCK = 256
GDN_CHUNK_GROUP = 4
GDN_BATCH_BLOCK = 4
SAMPLE_T_PAD = 8
VMEM_LIMIT = 56 * 1024 * 1024


def _resident(shape):
    nd = len(shape)
    return pl.BlockSpec(shape, lambda *_: (0,) * nd, pipeline_mode=pl.Buffered(1))


def _layernorm(y, g, b):
    mu = jnp.mean(y, axis=-1, keepdims=True)
    yc = y - mu
    var = jnp.mean(yc * yc, axis=-1, keepdims=True)
    return yc * lax.rsqrt(var + LN_EPS) * g + b


def _swiglu(xb, wg_ref, wu_ref, wd_ref, h_scr):
    d_ff = wg_ref.shape[1]
    step = 2 * LANES
    assert d_ff % step == 0
    for c in range(d_ff // step):
        sl = slice(c * step, (c + 1) * step)
        gate = jnp.dot(xb, wg_ref[:, sl], preferred_element_type=F32)
        up = jnp.dot(xb, wu_ref[:, sl], preferred_element_type=F32)
        h_scr[:, sl] = (jax.nn.silu(gate) * up).astype(BF16)
    return jnp.dot(h_scr[...], wd_ref[...], preferred_element_type=F32)


PROJ_WIDTHS = (HEAD_WIDTH, HEAD_WIDTH, HEAD_WIDTH, 3 * HEAD_WIDTH, HEAD_WIDTH, LANES)
PROJ_EDGES = tuple(int(e) for e in np.cumsum((0,) + PROJ_WIDTHS))


def _first_half_step(x_ref, wg_ref, wu_ref, wd_ref, g_ref, b_ref, h_scr, alpha):
    x = x_ref[...]
    ff = _swiglu(x.astype(BF16), wg_ref, wu_ref, wd_ref, h_scr)
    return _layernorm(alpha * x + 0.5 * ff, g_ref[...], b_ref[...])


def _pre_kernel(x_ref, wg_ref, wu_ref, wd_ref, g_ref, b_ref, win_ref,
                x1_ref, aq_ref, ak_ref, av_ref, gq_ref, z_ref, ba_ref, h_scr, *, alpha):
    x1 = _first_half_step(x_ref, wg_ref, wu_ref, wd_ref, g_ref, b_ref, h_scr, alpha)
    x1_ref[...] = x1
    xb = x1.astype(BF16)
    for i, ref in enumerate((aq_ref, ak_ref, av_ref, gq_ref, z_ref, ba_ref)):
        ref[...] = jnp.dot(xb, win_ref[:, PROJ_EDGES[i]:PROJ_EDGES[i + 1]], preferred_element_type=F32)


def _gdn_gates(ba, hp_ref):
    beta = jax.nn.sigmoid(ba)
    g = -jnp.exp(hp_ref[0:1, :]) * jax.nn.softplus(ba + hp_ref[1:2, :])
    return beta, g


def _causal_conv_silu(ext_scr, x, cw_ref):
    tb = x.shape[0]
    hist = SUBLANES
    ext_scr[hist:hist + tb, :] = x
    ext = ext_scr[0:hist + tb, :]
    conv = jnp.zeros(x.shape, F32)
    for j in range(CONV_WIDTH):
        back = CONV_WIDTH - 1 - j
        rows = pltpu.roll(ext, back, axis=0)[hist:hist + tb, :] if back else x
        conv = conv + rows * cw_ref[j:j + 1, :]
    ext_scr[0:hist, :] = ext_scr[tb:tb + hist, :]
    half = 0.5 * conv
    return half + half * jnp.tanh(half)


def _pre_prompt_kernel(x_ref, wg_ref, wu_ref, wd_ref, g_ref, b_ref, win_ref, cw_ref, hp_ref, ones_ref,
                       x1_ref, aq_ref, ak_ref, av_ref, akt_ref, avt_ref, qkv_ref, z_ref, bt_ref, gg_ref, tail_ref,
                       h_scr, ext_scr, *, alpha, tiles_per_seq):
    x1 = _first_half_step(x_ref, wg_ref, wu_ref, wd_ref, g_ref, b_ref, h_scr, alpha)
    x1_ref[...] = x1
    xb = x1.astype(BF16)
    proj = lambda i: jnp.dot(xb, win_ref[:, PROJ_EDGES[i]:PROJ_EDGES[i + 1]], preferred_element_type=F32)

    @pl.when(pl.program_id(0) % tiles_per_seq == 0)
    def _():
        ext_scr[0:SUBLANES, :] = jnp.zeros((SUBLANES, ext_scr.shape[1]), F32)

    raw = proj(3)
    tail_ref[0] = raw[raw.shape[0] - SUBLANES:, :]
    act = _causal_conv_silu(ext_scr, raw, cw_ref)
    hw = HEAD_WIDTH
    head_sq = lambda x: jnp.dot((x * x).astype(BF16), ones_ref[...], preferred_element_type=F32)
    q = act[:, 0:hw]
    k = act[:, hw:2 * hw]
    qkv_ref[:, 0:hw] = q * lax.rsqrt(head_sq(q) + RMS_EPS) * HEAD_DIM ** -0.5
    qkv_ref[:, hw:2 * hw] = k * lax.rsqrt(head_sq(k) + RMS_EPS)
    qkv_ref[:, 2 * hw:3 * hw] = act[:, 2 * hw:3 * hw]
    bt_ref[...], gg_ref[...] = _gdn_gates(proj(5), hp_ref)
    aq_ref[...] = proj(0)
    for i, ref, ref_t in ((1, ak_ref, akt_ref), (2, av_ref, avt_ref)):
        kv = proj(i)
        ref[...] = kv
        ref_t[0] = kv.T
    z_ref[...] = proj(4)


def _pre_call(x2d, wg, wu, wd, g, b, win, alpha):
    m, d = x2d.shape
    d_ff = wg.shape[1]
    tm = TOKEN_TILE
    assert m % tm == 0 and sum(PROJ_WIDTHS) == win.shape[1]
    tile = lambda w: pl.BlockSpec((tm, w), lambda i: (i, 0))
    return pl.pallas_call(
        functools.partial(_pre_kernel, alpha=alpha),
        grid=(m // tm,),
        in_specs=[tile(d), _resident(wg.shape), _resident(wu.shape), _resident(wd.shape),
                  _resident(g.shape), _resident(b.shape), _resident(win.shape)],
        out_specs=[tile(d)] + [tile(w) for w in PROJ_WIDTHS],
        out_shape=[jax.ShapeDtypeStruct((m, d), F32)] + [jax.ShapeDtypeStruct((m, w), F32) for w in PROJ_WIDTHS],
        scratch_shapes=[pltpu.VMEM((tm, d_ff), BF16)],
        compiler_params=pltpu.CompilerParams(dimension_semantics=("parallel",), vmem_limit_bytes=VMEM_LIMIT),
        name="pre_ffn_proj",
    )(x2d, wg, wu, wd, g, b, win)


def _pre_prompt_call(x2d, wg, wu, wd, g, b, win, conv_w8, head_params, alpha, seq):
    m, d = x2d.shape
    d_ff = wg.shape[1]
    tm = TOKEN_TILE
    hw = HEAD_WIDTH
    assert seq % tm == 0 and m % seq == 0 and sum(PROJ_WIDTHS) == win.shape[1]
    per_seq = seq // tm
    tile = lambda w: pl.BlockSpec((tm, w), lambda i: (i, 0))
    t_spec = pl.BlockSpec((1, hw, tm), lambda i: (i // per_seq, 0, i % per_seq))
    t_shape = jax.ShapeDtypeStruct((m // seq, hw, seq), F32)
    rows = lambda w: jax.ShapeDtypeStruct((m, w), F32)
    consts = (wg, wu, wd, g, b, win, conv_w8, head_params, _head_ones())
    return pl.pallas_call(
        functools.partial(_pre_prompt_kernel, alpha=alpha, tiles_per_seq=per_seq),
        grid=(m // tm,),
        in_specs=[tile(d)] + [_resident(c.shape) for c in consts],
        out_specs=[tile(d), tile(hw), tile(hw), tile(hw), t_spec, t_spec, tile(3 * hw), tile(hw), tile(LANES),
                   tile(LANES), pl.BlockSpec((1, SUBLANES, 3 * hw), lambda i: (i // per_seq, 0, 0))],
        out_shape=[rows(d), rows(hw), rows(hw), rows(hw), t_shape, t_shape, rows(3 * hw), rows(hw), rows(LANES),
                   rows(LANES), jax.ShapeDtypeStruct((m // seq, SUBLANES, 3 * hw), F32)],
        scratch_shapes=[pltpu.VMEM((tm, d_ff), BF16), pltpu.VMEM((tm + 2 * SUBLANES, 3 * hw), F32)],
        compiler_params=pltpu.CompilerParams(dimension_semantics=("arbitrary",), vmem_limit_bytes=VMEM_LIMIT),
        name="pre_ffn_proj_gdnprep",
    )(x2d, *consts)


def _post_kernel(attn_ref, gdn_ref, x1_ref, wo_ref, g2_ref, b2_ref, wg_ref, wu_ref, wd_ref, g3_ref, b3_ref,
                 y_ref, h_scr, *, alpha):
    tm = y_ref.shape[0]
    n_split = 2 if tm >= 2 * TOKEN_TILE else 1
    halves = [pl.ds(i * (tm // n_split), tm // n_split) for i in range(n_split)]
    mix, x2, ff = {}, {}, {}
    for i, rows in enumerate(halves):
        mix[i] = (jnp.dot(attn_ref[rows, :].astype(BF16), wo_ref[0:HEAD_WIDTH, :], preferred_element_type=F32)
                  + jnp.dot(gdn_ref[rows, :].astype(BF16), wo_ref[HEAD_WIDTH:2 * HEAD_WIDTH, :],
                            preferred_element_type=F32))
    for i, rows in enumerate(halves):
        x2[i] = _layernorm(alpha * x1_ref[rows, :] + mix[i], g2_ref[...], b2_ref[...])
    for i, rows in enumerate(halves):
        ff[i] = _swiglu(x2[i].astype(BF16), wg_ref, wu_ref, wd_ref, h_scr.at[rows])
    for i, rows in enumerate(halves):
        y_ref[rows, :] = _layernorm(alpha * x2[i] + 0.5 * ff[i], g3_ref[...], b3_ref[...])


def _post_call(attn, gdn, x1, wo, g2, b2, wg, wu, wd, g3, b3, alpha):
    m, d = x1.shape
    tm = POST_TOKEN_TILE
    assert m % tm == 0
    tile = lambda w: pl.BlockSpec((tm, w), lambda i: (i, 0))
    consts = (wo, g2, b2, wg, wu, wd, g3, b3)
    return pl.pallas_call(
        functools.partial(_post_kernel, alpha=alpha),
        grid=(m // tm,),
        in_specs=[tile(HEAD_WIDTH), tile(HEAD_WIDTH), tile(d)] + [_resident(c.shape) for c in consts],
        out_specs=tile(d),
        out_shape=jax.ShapeDtypeStruct((m, d), F32),
        scratch_shapes=[pltpu.VMEM((tm, wg.shape[1]), BF16)],
        compiler_params=pltpu.CompilerParams(dimension_semantics=("parallel",), vmem_limit_bytes=VMEM_LIMIT),
        name="post_out_ffn",
    )(attn, gdn, x1, *consts)


def _t5_bucket_np(dist):
    n = np.maximum(dist, 0)
    nf = np.maximum(n, 1).astype(np.float32)
    large = T5_MAX_EXACT + (np.log(nf / np.float32(T5_MAX_EXACT)) / np.float32(math.log(T5_MAX_DIST / T5_MAX_EXACT))
                            * np.float32(T5_BUCKETS - T5_MAX_EXACT)).astype(np.int32)
    large = np.minimum(large, T5_BUCKETS - 1)
    return np.where(n < T5_MAX_EXACT, n, large).astype(np.int32)


def _t5_table_kernel(rb_ref, bk_ref, out_ref):
    for h in range(N_HEADS):
        bk = bk_ref[0, h % bk_ref.shape[1]]
        acc = jnp.full(bk.shape, NEG_BIG, F32)
        for b in range(T5_BUCKETS):
            acc = jnp.where(bk == b, rb_ref[b, h], acc)
        out_ref[0, h] = acc


def _t5_table_call(rel_bias, buckets):
    p, hb, r, c = buckets.shape
    assert hb in (1, N_HEADS)
    return pl.pallas_call(
        _t5_table_kernel,
        grid=(p,),
        in_specs=[pl.BlockSpec(memory_space=pltpu.SMEM),
                  pl.BlockSpec((1, hb, r, c), lambda i: (i, 0, 0, 0))],
        out_specs=pl.BlockSpec((1, N_HEADS, r, c), lambda i: (i, 0, 0, 0)),
        out_shape=jax.ShapeDtypeStruct((p, N_HEADS, r, c), F32),
        compiler_params=pltpu.CompilerParams(dimension_semantics=("parallel",)),
        name="t5_bias_table",
    )(rel_bias, buckets)


def _prompt_buckets():
    nb = KEYS_PER_BLOCK
    qi = np.arange(nb)[:, None]
    ki = np.arange(2 * nb)[None, :]
    dist = qi + nb - ki
    valid = (dist >= 0) & (dist <= nb)
    out = []
    for window, dil in DILATED_PATTERNS:
        assert window // dil == nb
        out.append(np.where(valid, _t5_bucket_np(dist * dil), -1))
    return np.stack(out).astype(np.int32)[:, None]


def _nt_dot(a, b):
    return lax.dot_general(a, b, (((1,), (1,)), ((), ())), preferred_element_type=F32)


def _attn_prompt_kernel(q_ref, k_ref, v_ref, tab_ref, o_ref, og_scr, lg_scr, *, seq):
    nb = KEYS_PER_BLOCK
    hd = HEAD_DIM
    n_pat = len(DILATED_PATTERNS)
    heads = LANES // hd
    for g, (window, dil) in enumerate(DILATED_PATTERNS):
        n_blocks = seq // (dil * nb)

        def blocks(starts, with_prev, g=g, dil=dil):
            lane_head = lax.broadcasted_iota(jnp.int32, (nb, LANES), 1) // hd
            n_keys = 2 * nb if with_prev else nb
            row_sl, qs, ks, vs = [], [], [], []
            for start in starts:
                first_key = start - dil * (n_keys - nb)
                row_sl.append(pl.ds(start, nb, stride=dil) if dil > 1 else pl.ds(start, nb))
                key_rows = pl.ds(first_key, n_keys, stride=dil) if dil > 1 else pl.ds(first_key, n_keys)
                qs.append(q_ref[0, row_sl[-1], :] * hd ** -0.5)
                ks.append(k_ref[0, key_rows, :].astype(BF16))
                vs.append(v_ref[0, key_rows, :].astype(BF16))
            chains = [(u, hh) for u in range(len(starts)) for hh in range(heads)]
            s, m, p, den, acc = {}, {}, {}, {}, {}

            def scores(c):
                u, hh = c
                qb = jnp.where(lane_head == hh, qs[u], 0.0).astype(BF16)
                s[c] = _nt_dot(qb, ks[u]) + tab_ref[g, hh, :, 2 * nb - n_keys:2 * nb]

            def row_max(c):
                m[c] = jnp.max(s[c], axis=-1, keepdims=True)

            def probs(c):
                p[c] = jnp.exp(s[c] - m[c])
                den[c] = jnp.sum(p[c], axis=-1, keepdims=True)

            def values(c):
                u, hh = c
                acc[c] = jnp.dot(p[c].astype(BF16), vs[u], preferred_element_type=F32)

            def finish(c):
                acc[c] = acc[c] / den[c]
                den[c] = m[c] + jnp.log(den[c])

            for stage in (scores, row_max, probs, values, finish):
                for c in chains:
                    stage(c)
            for u in range(len(starts)):
                out, lse = acc[u, 0], den[u, 0]
                for hh in range(1, heads):
                    out = jnp.where(lane_head == hh, acc[u, hh], out)
                    lse = jnp.where(lane_head == hh, den[u, hh], lse)
                og_scr[g, row_sl[u], :] = out
                lg_scr[g, row_sl[u], :] = jnp.broadcast_to(lse, (nb, LANES))

        def unroll_of(count):
            return next(u for u in (16, 12, 8, 6, 5, 4, 3, 2, 1) if count % u == 0)

        u_first = unroll_of(dil)

        def first_body(i, carry, blocks=blocks, u_first=u_first):
            blocks([i * u_first + j for j in range(u_first)], False)
            return carry

        lax.fori_loop(0, dil // u_first, first_body, 0)
        if n_blocks > 1:
            n_rest = dil * (n_blocks - 1)
            u_rest = unroll_of(n_rest)

            def rest_body(i, carry, blocks=blocks, dil=dil, n_blocks=n_blocks, u_rest=u_rest):
                starts = []
                for j in range(u_rest):
                    idx = i * u_rest + j
                    starts.append(idx // (n_blocks - 1) + dil * nb * (idx % (n_blocks - 1) + 1))
                blocks(starts, True)
                return carry

            lax.fori_loop(0, n_rest // u_rest, rest_body, 0)

    rb = 2 * nb

    def merge_body(i, carry):
        rows = pl.ds(pl.multiple_of(i * rb, rb), rb)
        lses = [lg_scr[g, rows, :] for g in range(n_pat)]
        top = functools.reduce(jnp.maximum, lses)
        ws = [jnp.exp(l - top) for l in lses]
        num = functools.reduce(lambda a, b: a + b, [w * og_scr[g, rows, :] for g, w in enumerate(ws)])
        o_ref[0, rows, :] = num / functools.reduce(lambda a, b: a + b, ws)
        return carry

    lax.fori_loop(0, seq // rb, merge_body, 0)


def _attn_prompt_call(aq, ak, av, tab):
    bsz, seq, width = aq.shape
    assert width == HEAD_WIDTH and seq % (DILATED_PATTERNS[-1][1] * KEYS_PER_BLOCK) == 0
    n_pat = len(DILATED_PATTERNS)
    heads_per_step = LANES // HEAD_DIM
    qkv_spec = pl.BlockSpec((1, seq, LANES), lambda h, b: (b, 0, h))
    return pl.pallas_call(
        functools.partial(_attn_prompt_kernel, seq=seq),
        grid=(N_HEADS // heads_per_step, bsz),
        in_specs=[qkv_spec, qkv_spec, qkv_spec,
                  pl.BlockSpec((n_pat, heads_per_step, KEYS_PER_BLOCK, 2 * KEYS_PER_BLOCK),
                               lambda h, b: (0, h, 0, 0))],
        out_specs=pl.BlockSpec((1, seq, LANES), lambda h, b: (b, 0, h)),
        out_shape=jax.ShapeDtypeStruct((bsz, seq, width), F32),
        scratch_shapes=[pltpu.VMEM((n_pat, seq, LANES), F32)] * 2,
        compiler_params=pltpu.CompilerParams(dimension_semantics=("parallel", "parallel"),
                                             vmem_limit_bytes=VMEM_LIMIT),
        name="attn_prompt",
    )(aq, ak, av, tab)


def _sample_layout(w_buf, t_new):
    assert w_buf % LANES == 0 and t_new <= SAMPLE_T_PAD
    row_of_col = np.full((w_buf + LANES,), -1, np.int64)
    row_of_col[:w_buf + t_new] = np.arange(w_buf + t_new)
    buckets = np.full((len(DILATED_PATTERNS), 1, SAMPLE_T_PAD, w_buf + LANES), -1, np.int32)
    lows = []
    for g, (window, dil) in enumerate(DILATED_PATTERNS):
        lows.append(max(0, w_buf - window) // LANES * LANES)
        for t in range(t_new):
            dist = w_buf + t - row_of_col
            ok = (row_of_col >= 0) & (dist >= 0) & (dist <= window) & (dist % dil == 0)
            assert int(ok.sum()) == min(window, w_buf + t) // dil + 1 and not ok[:lows[g]].any()
            buckets[g, 0, t] = np.where(ok, _t5_bucket_np(dist), -1)
    return tuple(lows), buckets


def _attn_sample_kernel(q_ref, kn_ref, vn_ref, kt_ref, vt_ref, tab_ref, o_ref, *, lows):
    tp = SAMPLE_T_PAD
    hd = HEAD_DIM
    w_buf = kt_ref.shape[3]
    n_pat = len(DILATED_PATTERNS)
    add = lambda x, y: x + y
    rowmax = lambda x: jnp.max(x, axis=-1, keepdims=True)
    rowsum = lambda x: jnp.sum(x, axis=-1, keepdims=True)
    heads = range(N_HEADS)
    q8 = q_ref[0] * hd ** -0.5
    s_buf = {h: jnp.dot(q8[:, hd * h:hd * (h + 1)], kt_ref[0, h], preferred_element_type=F32) for h in heads}
    s_new = {h: _nt_dot(q8[:, hd * h:hd * (h + 1)], kn_ref[0, h]) for h in heads}
    p_buf, p_new, den, lse = {}, {}, {}, {}
    for h in heads:
        for g in range(n_pat):
            lo = lows[g]
            l_buf = s_buf[h][:, lo:] + tab_ref[g, h, :, lo:w_buf]
            l_new = s_new[h] + tab_ref[g, h, :, w_buf:w_buf + tp]
            m = jnp.maximum(rowmax(l_buf), rowmax(l_new))
            p_buf[h, g] = jnp.exp(l_buf - m)
            p_new[h, g] = jnp.exp(l_new - m)
            den[h, g] = rowsum(p_buf[h, g]) + rowsum(p_new[h, g])
            lse[h, g] = m + jnp.log(den[h, g])
    edges = sorted(set(lows)) + [w_buf]
    outs = {}
    for h in heads:
        top = functools.reduce(jnp.maximum, [lse[h, g] for g in range(n_pat)])
        ws = [jnp.exp(lse[h, g] - top) for g in range(n_pat)]
        w_sum = functools.reduce(add, ws)
        coef = [ws[g] / (w_sum * den[h, g]) for g in range(n_pat)]
        acc = jnp.dot(functools.reduce(add, [coef[g] * p_new[h, g] for g in range(n_pat)]), vn_ref[0, h],
                      preferred_element_type=F32)
        for e0, e1 in zip(edges[:-1], edges[1:]):
            mix = functools.reduce(add, [coef[g] * p_buf[h, g][:, e0 - lows[g]:e1 - lows[g]]
                                         for g in range(n_pat) if lows[g] <= e0])
            acc = acc + _nt_dot(mix, vt_ref[0, h, :, e0:e1])
        outs[h] = acc
    o_ref[0] = jnp.concatenate([outs[h] for h in heads], axis=1)


N_POST_INPUTS = 11
N_ATTN_SAMPLE_INPUTS = 6


def _post_with_sample_attn_kernel(*refs, alpha, lows):
    post_in = refs[:N_POST_INPUTS]
    attn_in = refs[N_POST_INPUTS:N_POST_INPUTS + N_ATTN_SAMPLE_INPUTS]
    y_ref, o_ref, h_scr = refs[N_POST_INPUTS + N_ATTN_SAMPLE_INPUTS:]
    _post_kernel(*post_in, y_ref, h_scr, alpha=alpha)
    _attn_sample_kernel(*attn_in, o_ref, lows=lows)


def _post_with_sample_attn_call(attn, gdn, x1, wo, g2, b2, wg, wu, wd, g3, b3, alpha,
                                aq8, k_new, v_new, cache_kt, cache_vt, tab, lows):
    m, d = x1.shape
    bsz, n_heads, hd, w_buf = cache_kt.shape
    assert m % bsz == 0
    tm = m // bsz
    assert tm % SUBLANES == 0
    tile = lambda w: pl.BlockSpec((tm, w), lambda i: (i, 0))
    consts = (wo, g2, b2, wg, wu, wd, g3, b3)
    q_spec = pl.BlockSpec((1, SAMPLE_T_PAD, n_heads * hd), lambda b: (b, 0, 0))
    new_spec = pl.BlockSpec((1, n_heads, SAMPLE_T_PAD, hd), lambda b: (b, 0, 0, 0))
    buf_spec = pl.BlockSpec((1, n_heads, hd, w_buf), lambda b: (b, 0, 0, 0))
    assert len(consts) + 3 == N_POST_INPUTS
    return pl.pallas_call(
        functools.partial(_post_with_sample_attn_kernel, alpha=alpha, lows=lows),
        grid=(bsz,),
        in_specs=[tile(HEAD_WIDTH), tile(HEAD_WIDTH), tile(d)] + [_resident(c.shape) for c in consts]
                 + [q_spec, new_spec, new_spec, buf_spec, buf_spec, _resident(tab.shape)],
        out_specs=[tile(d), q_spec],
        out_shape=[jax.ShapeDtypeStruct((m, d), F32),
                   jax.ShapeDtypeStruct((bsz, SAMPLE_T_PAD, n_heads * hd), F32)],
        scratch_shapes=[pltpu.VMEM((tm, wg.shape[1]), BF16)],
        compiler_params=pltpu.CompilerParams(dimension_semantics=("parallel",), vmem_limit_bytes=VMEM_LIMIT),
        name="post_with_sample_attn",
    )(attn, gdn, x1, *consts, aq8, k_new, v_new, cache_kt, cache_vt, tab)


def _aligned(index, size):
    return index * size if isinstance(index, int) else pl.multiple_of(index * size, size)


def _split3(x):
    h1 = x.astype(BF16)
    r1 = x - h1.astype(F32)
    h2 = r1.astype(BF16)
    h3 = (r1 - h2.astype(F32)).astype(BF16)
    return h1, h2, h3


def _head_sums(x, ones_ref):
    h1, h2, _ = _split3(x)
    return (jnp.dot(h1, ones_ref[...], preferred_element_type=F32)
            + jnp.dot(h2, ones_ref[...], preferred_element_type=F32))


def _gdn_pairs_kernel(qkv_ref, z_ref, bt_scr, g_scr, s0_ref, nw_ref, ones_ref, o_ref, sfin_ref,
                      oc_scr, s_scr, u_scr, w_scr, in_scr, qd_scr, kd_scr, el_scr, *, group):
    chunk = HEAD_DIM
    k_lane0, v_lane0 = HEAD_WIDTH, 2 * HEAD_WIDTH
    n_bb, tb = bt_scr.shape[0], bt_scr.shape[1]
    n_chunks = tb // chunk
    n_pairs = N_HEADS // 2
    hd = HEAD_DIM
    t = pl.program_id(1)
    dot = functools.partial(jnp.dot, preferred_element_type=F32)

    lane = lax.broadcasted_iota(jnp.int32, (chunk, LANES), 1)
    row = lax.broadcasted_iota(jnp.int32, (chunk, LANES), 0)
    second = lane >= hd
    col = lane - jnp.where(second, hd, 0)
    tril = row >= col
    strict = row > col
    eye2 = (row == col).astype(F32)
    tril_b = (lax.broadcasted_iota(jnp.int32, (chunk, chunk), 0)
              >= lax.broadcasted_iota(jnp.int32, (chunk, chunk), 1)).astype(BF16)
    base_blocks = (row // INV_BASE_BLOCK) == (col // INV_BASE_BLOCK)
    level_blocks = []
    m = INV_BASE_BLOCK
    while m < chunk:
        level_blocks.append(((row // m) % 2 == 1) & ((col // m) == (row // m) - 1))
        m *= 2

    def blockdiag(x):
        return jnp.concatenate([jnp.where(second, 0.0, x), jnp.where(second, x, 0.0)], axis=0)

    def per_head(cols, j):
        return jnp.where(second, cols[:, 2 * j + 1:2 * j + 2], cols[:, 2 * j:2 * j + 1])

    @pl.when(t == 0)
    def _():
        for bb in range(n_bb):
            for j in range(n_pairs):
                s_scr[bb, j] = jnp.concatenate([s0_ref[bb, 2 * j], s0_ref[bb, 2 * j + 1]], axis=1)

    def state_free_part(gi, carry):
        probs, shared = [], {}
        for bb in range(n_bb):
            for cg in range(group):
                c_idx = gi * group + cg
                rows = pl.ds(_aligned(c_idx, chunk), chunk)
                gc = functools.reduce(lambda a, b: a + b,
                                      [dot(tril_b, p) for p in _split3(g_scr[bb, rows, :])])
                shared[bb, cg] = (rows, c_idx, gc[:, N_HEADS:2 * N_HEADS], gc.T[N_HEADS:2 * N_HEADS, :],
                                  bt_scr[bb, rows, :])
                probs += [(bb, cg, j) for j in range(n_pairs)]
        kk, qk, decay, a, t_inv, pw, vb, kbg = {}, {}, {}, {}, {}, {}, {}, {}
        for p in probs:
            bb, cg, j = p
            rows, c_idx, gc, gct, bt = shared[bb, cg]
            pl_ = pl.ds(LANES * j, LANES)
            q2 = qkv_ref[bb, rows, pl_]
            k2 = qkv_ref[bb, rows, pl.ds(k_lane0 + LANES * j, LANES)]
            beta2 = per_head(bt, j)
            gcc = per_head(gc, j)
            gcr = jnp.concatenate([gct[2 * j:2 * j + 1, :], gct[2 * j + 1:2 * j + 2, :]], axis=1)
            g_last = gcc[chunk - 1:chunk, :]
            e_gc = jnp.exp(gcc)
            kb2 = k2 * beta2
            k_bd = blockdiag(k2)
            both = _nt_dot(jnp.concatenate([kb2, q2], axis=0), k_bd)
            kk[p] = both[0:chunk]
            qk[p] = both[chunk:2 * chunk]
            vb[p] = qkv_ref[bb, rows, pl.ds(v_lane0 + LANES * j, LANES)] * beta2
            kbg[p] = kb2 * e_gc
            qd_scr[bb, rows, pl_] = q2 * e_gc
            kd_scr[bb, rows, pl_] = k2 * jnp.exp(g_last - gcc)
            el_scr[bb, pl.ds(_aligned(c_idx, SUBLANES), SUBLANES), pl_] = jnp.broadcast_to(
                jnp.exp(g_last), (SUBLANES, LANES))
            decay[p] = jnp.where(tril, jnp.exp(jnp.where(tril, gcc - gcr, 0.0)), 0.0)
        for p in probs:
            bb, cg, j = p
            rows = shared[bb, cg][0]
            a[p] = jnp.where(strict, kk[p] * decay[p], 0.0)
            in_scr[bb, rows, pl.ds(LANES * j, LANES)] = jnp.where(tril, qk[p] * decay[p], 0.0)
        for p in probs:
            d = jnp.where(base_blocks, a[p], 0.0)
            pw[p] = d
            t_inv[p] = eye2 - d
        for p in probs:
            pw[p] = dot(pw[p], blockdiag(pw[p]))
        for p in probs:
            t_inv[p] = dot(t_inv[p], blockdiag(eye2 + pw[p]))
        for lower_left in level_blocks:
            for p in probs:
                pw[p] = dot(t_inv[p], blockdiag(jnp.where(lower_left, a[p], 0.0)))
            for p in probs:
                t_inv[p] = t_inv[p] - dot(pw[p], blockdiag(t_inv[p]))
        for p in probs:
            bb, cg, j = p
            rows = shared[bb, cg][0]
            uw = dot(t_inv[p], jnp.concatenate([blockdiag(vb[p]), blockdiag(kbg[p])], axis=1))
            u_scr[bb, rows, pl.ds(LANES * j, LANES)] = uw[:, 0:LANES]
            w_scr[bb, rows, pl.ds(LANES * j, LANES)] = uw[:, LANES:2 * LANES]
        return carry

    def state_part(ci_, carry):
        rows = pl.ds(_aligned(ci_, chunk), chunk)
        el_rows = pl.ds(_aligned(ci_, SUBLANES), SUBLANES)
        probs = [(bb, j) for bb in range(n_bb) for j in range(n_pairs)]
        s_old, both, v_new, o_in, s_add = {}, {}, {}, {}, {}
        for p in probs:
            bb, j = p
            pl_ = pl.ds(LANES * j, LANES)
            s_old[p] = s_scr[bb, j]
            lhs = jnp.concatenate([w_scr[bb, rows, pl_], qd_scr[bb, rows, pl_]], axis=0)
            both[p] = dot(lhs, blockdiag(s_old[p]))
        for p in probs:
            bb, j = p
            pl_ = pl.ds(LANES * j, LANES)
            v_new[p] = u_scr[bb, rows, pl_] - both[p][0:chunk]
            o_in[p] = dot(in_scr[bb, rows, pl_], blockdiag(v_new[p]))
            cross = lax.dot_general(kd_scr[bb, rows, pl_], v_new[p], (((0,), (0,)), ((), ())),
                                    preferred_element_type=F32)
            s_add[p] = jnp.where(second, cross[hd:2 * hd], cross[0:hd])
        for p in probs:
            bb, j = p
            pl_ = pl.ds(LANES * j, LANES)
            oc_scr[bb, rows, pl_] = both[p][chunk:2 * chunk] + o_in[p]
            s_scr[bb, j] = s_old[p] * el_scr[bb, el_rows, pl_][0:1, :] + s_add[p]
        return carry

    assert n_chunks % group == 0
    if n_chunks == group:
        state_free_part(0, 0)
    else:
        lax.fori_loop(0, n_chunks // group, state_free_part, 0)
    if n_chunks == 1:
        state_part(0, 0)
    else:
        lax.fori_loop(0, n_chunks, state_part, 0)

    for bb in range(n_bb):
        o = oc_scr[bb]
        o = o * lax.rsqrt(_head_sums(o * o, ones_ref) * (1.0 / hd) + RMS_EPS) * nw_ref[...]
        o_ref[bb] = o * jax.nn.silu(z_ref[bb])

    @pl.when(t == pl.num_programs(1) - 1)
    def _():
        for bb in range(n_bb):
            for j in range(n_pairs):
                sfin_ref[bb, 2 * j] = s_scr[bb, j][:, 0:hd]
                sfin_ref[bb, 2 * j + 1] = s_scr[bb, j][:, hd:2 * hd]


def _head_ones():
    ones_bd = np.kron(np.eye(N_HEADS, dtype=np.float32), np.ones((HEAD_DIM, HEAD_DIM), np.float32))
    return jnp.asarray(ones_bd, BF16)


def _gdn_pairs_call(qkv, z, beta, g, s0, norm_w_row, token_block, group):
    bsz, seq, w3 = qkv.shape
    hw = HEAD_WIDTH
    chunk = HEAD_DIM
    nbb = GDN_BATCH_BLOCK
    assert w3 == 3 * hw and seq % token_block == 0 and token_block % (chunk * group) == 0 and bsz % nbb == 0
    tb = token_block
    ones_bd = _head_ones()
    blk = lambda w: pl.BlockSpec((nbb, tb, w), lambda b, t: (b, t, 0))
    per_b3 = lambda s: pl.BlockSpec((nbb,) + s, lambda b, t: (b,) + (0,) * len(s))
    return pl.pallas_call(
        functools.partial(_gdn_pairs_kernel, group=group),
        grid=(bsz // nbb, seq // tb),
        in_specs=[blk(w3), blk(hw), blk(LANES), blk(LANES), per_b3((N_HEADS, HEAD_DIM, HEAD_DIM)),
                  _resident(norm_w_row.shape), _resident(ones_bd.shape)],
        out_specs=[blk(hw), per_b3((N_HEADS, HEAD_DIM, HEAD_DIM))],
        out_shape=[jax.ShapeDtypeStruct((bsz, seq, hw), F32),
                   jax.ShapeDtypeStruct((bsz, N_HEADS, HEAD_DIM, HEAD_DIM), F32)],
        scratch_shapes=[pltpu.VMEM((nbb, tb, hw), F32)]
                       + [pltpu.VMEM((nbb, N_HEADS // 2, HEAD_DIM, LANES), F32)]
                       + [pltpu.VMEM((nbb, tb, hw), F32)] * 5
                       + [pltpu.VMEM((nbb, (tb // chunk) * SUBLANES, hw), F32)],
        compiler_params=pltpu.CompilerParams(dimension_semantics=("parallel", "arbitrary"),
                                             vmem_limit_bytes=VMEM_LIMIT),
        name="gdn_pairs",
    )(qkv, z, beta, g, s0, norm_w_row, ones_bd)


def _gdn_step_kernel(xq_ref, xk_ref, xv_ref, cq_ref, ck_ref, cv_ref, wq_ref, wk_ref, wv_ref, z_ref, ba_ref, hp_ref,
                     nw_ref, s0_ref, o_ref, s_ref, k_scr, q_scr):
    n_tok, hd, _ = xq_ref.shape
    h = pl.program_id(0)
    col_sum = lambda x: jnp.sum(x, axis=0, keepdims=True)

    def conv_act(x_ref, c_ref, w_ref, t):
        planes = [c_ref[i] for i in range(CONV_WIDTH - 1)] + [x_ref[i] for i in range(t + 1)]
        taps = planes[t:t + CONV_WIDTH]
        acc = jnp.zeros(taps[0].shape, F32)
        for j in range(CONV_WIDTH):
            acc = acc + taps[j] * w_ref[j]
        return jax.nn.silu(acc)

    s_ref[...] = s0_ref[...]
    neg_rate = -jnp.exp(hp_ref[0, pl.ds(h, 1), :])
    dt_bias = hp_ref[1, pl.ds(h, 1), :]
    for t in range(n_tok):
        q = conv_act(xq_ref, cq_ref, wq_ref, t)
        k = conv_act(xk_ref, ck_ref, wk_ref, t)
        v = conv_act(xv_ref, cv_ref, wv_ref, t)
        q_scr[...] = q * lax.rsqrt(col_sum(q * q) + RMS_EPS) * hd ** -0.5
        k_scr[...] = k * lax.rsqrt(col_sum(k * k) + RMS_EPS)
        beta = jax.nn.sigmoid(ba_ref[t, pl.ds(h, 1), :])
        decay = jnp.exp(neg_rate * jax.nn.softplus(ba_ref[t, pl.ds(N_HEADS + h, 1), :] + dt_bias))

        def k_dot_state(d, acc):
            return acc + k_scr[pl.ds(d, 1), :] * s_ref[0, d]

        ks = lax.fori_loop(0, hd, k_dot_state, jnp.zeros((hd, xq_ref.shape[2]), F32), unroll=8)
        delta = beta * (v - decay * ks)

        def update(d, acc):
            s_new = decay * s_ref[0, d] + k_scr[pl.ds(d, 1), :] * delta
            s_ref[0, d] = s_new
            return acc + q_scr[pl.ds(d, 1), :] * s_new

        o = lax.fori_loop(0, hd, update, jnp.zeros((hd, xq_ref.shape[2]), F32), unroll=8)
        o = o * lax.rsqrt(col_sum(o * o) * (1.0 / hd) + RMS_EPS) * nw_ref[...]
        o_ref[t] = o * jax.nn.silu(z_ref[t])


def _gdn_step_call(x_t, conv_t, conv_w_t, z_t, ba_t, hp_t, nw_t, s0_t):
    n_tok, w3, bsz = x_t.shape
    hd = HEAD_DIM
    part = lambda rows, p: pl.BlockSpec((rows, hd, bsz), lambda h, p=p: (0, p * N_HEADS + h, 0))
    whole = lambda a: pl.BlockSpec(a.shape, lambda h: (0,) * a.ndim)
    state_spec = pl.BlockSpec((1, hd, hd, bsz), lambda h: (h, 0, 0, 0))
    return pl.pallas_call(
        _gdn_step_kernel,
        grid=(N_HEADS,),
        in_specs=[part(n_tok, 0), part(n_tok, 1), part(n_tok, 2),
                  part(CONV_WIDTH - 1, 0), part(CONV_WIDTH - 1, 1), part(CONV_WIDTH - 1, 2),
                  part(CONV_WIDTH, 0), part(CONV_WIDTH, 1), part(CONV_WIDTH, 2),
                  part(n_tok, 0), whole(ba_t), whole(hp_t), whole(nw_t), state_spec],
        out_specs=[part(n_tok, 0), state_spec],
        out_shape=[jax.ShapeDtypeStruct((n_tok, N_HEADS * hd, bsz), F32),
                   jax.ShapeDtypeStruct(s0_t.shape, F32)],
        scratch_shapes=[pltpu.VMEM((hd, bsz), F32)] * 2,
        compiler_params=pltpu.CompilerParams(dimension_semantics=("parallel",), vmem_limit_bytes=VMEM_LIMIT),
        name="gdn_step",
    )(x_t, x_t, x_t, conv_t, conv_t, conv_t, conv_w_t, conv_w_t, conv_w_t, z_t, ba_t, hp_t, nw_t, s0_t)


def _pad_rows(x, rows):
    cfg = [(0, 0)] * x.ndim
    cfg[1] = (0, rows - x.shape[1])
    return jnp.pad(x, cfg)


def kernel(x_prompt, x_sample, cache_attn_k, cache_attn_v, state_gdn, state_conv, rel_bias, ln1_g, ln1_b,
           ffn1_w_gate, ffn1_w_up, ffn1_w_down, w_in, w_out, gdn_conv_w, gdn_a_log, gdn_dt_bias, gdn_norm_w,
           ln2_g, ln2_b, ffn2_w_gate, ffn2_w_up, ffn2_w_down, ln3_g, ln3_b):
    depth = w_in.shape[0]
    alpha = (2.0 * depth) ** 0.25
    bsz, seq, d_model = x_prompt.shape
    dbsz, dseq, _ = x_sample.shape
    w_buf = cache_attn_k.shape[2]
    hw = HEAD_WIDTH
    in_cols = w_in.shape[2]
    assert in_cols == 7 * hw + 2 * N_HEADS and CONV_WIDTH - 1 <= dseq <= SAMPLE_T_PAD
    assert seq % GDN_TOKEN_BLOCK == 0 and GDN_TOKEN_BLOCK % (GDN_CHUNK * GDN_CHUNK_GROUP) == 0

    sample_lows, sample_buckets = _sample_layout(w_buf, dseq)
    tab_p = _t5_table_call(rel_bias, jnp.asarray(_prompt_buckets()))
    tab_s = _t5_table_call(rel_bias, jnp.asarray(sample_buckets))

    yp = x_prompt.reshape(bsz * seq, d_model)
    ys = x_sample.reshape(dbsz * dseq, d_model)
    collected = [[] for _ in range(8)]
    row = lambda v: v.reshape(1, -1)
    for layer in range(depth):
        wg1, wu1, wd1 = (w[layer].astype(BF16) for w in (ffn1_w_gate, ffn1_w_up, ffn1_w_down))
        wg2, wu2, wd2 = (w[layer].astype(BF16) for w in (ffn2_w_gate, ffn2_w_up, ffn2_w_down))
        win = jnp.pad(w_in[layer], ((0, 0), (0, 7 * hw + LANES - in_cols))).astype(BF16)
        wo = w_out[layer].astype(BF16)
        conv_w8 = jnp.pad(gdn_conv_w[layer], ((0, SUBLANES - CONV_WIDTH), (0, 0)))
        head_params = jnp.zeros((SUBLANES, LANES), F32)
        head_params = head_params.at[0, N_HEADS:2 * N_HEADS].set(gdn_a_log[layer])
        head_params = head_params.at[1, N_HEADS:2 * N_HEADS].set(gdn_dt_bias[layer])
        norm_w_row = jnp.tile(gdn_norm_w[layer], N_HEADS).reshape(1, hw)
        pre = functools.partial(_pre_call, wg=wg1, wu=wu1, wd=wd1, g=row(ln1_g[layer]), b=row(ln1_b[layer]),
                                win=win, alpha=alpha)
        post = functools.partial(_post_call, wo=wo, g2=row(ln2_g[layer]), b2=row(ln2_b[layer]), wg=wg2, wu=wu2,
                                 wd=wd2, g3=row(ln3_g[layer]), b3=row(ln3_b[layer]), alpha=alpha)

        x1, aq, ak, av, ak_t, av_t, gdn_qkv, z, beta, gate, gq_tail = _pre_prompt_call(
            yp, wg1, wu1, wd1, row(ln1_g[layer]), row(ln1_b[layer]), win, conv_w8, head_params, alpha, seq)
        shp = lambda a: a.reshape(bsz, seq, a.shape[1])
        attn = _attn_prompt_call(shp(aq), shp(ak), shp(av), tab_p)
        gdn, s_p = _gdn_pairs_call(shp(gdn_qkv), shp(z), shp(beta), shp(gate),
                                   jnp.zeros((bsz, N_HEADS, HEAD_DIM, HEAD_DIM), F32), norm_w_row,
                                   token_block=GDN_TOKEN_BLOCK, group=GDN_CHUNK_GROUP)
        wp = min(w_buf, seq)
        heads5 = lambda a_t: a_t.reshape(bsz, N_HEADS, HEAD_DIM, seq).transpose(0, 3, 1, 2)[:, seq - wp:]
        st_p = (heads5(ak_t), heads5(av_t), s_p, gq_tail[:, SUBLANES - (CONV_WIDTH - 1):])

        x1s, aqs, aks, avs, gqs, zs, bas = pre(ys)
        shs = lambda a: _pad_rows(a.reshape(dbsz, dseq, a.shape[1]), SAMPLE_T_PAD)
        heads5s = lambda a: a.reshape(dbsz, dseq, N_HEADS, HEAD_DIM)
        new_rows = lambda a: jnp.pad(heads5s(a).transpose(0, 2, 1, 3),
                                     ((0, 0), (0, 0), (0, SAMPLE_T_PAD - dseq), (0, 0)))
        by_head_t = lambda c: c.transpose(0, 2, 3, 1)
        yp, attn_s = _post_with_sample_attn_call(
            attn.reshape(bsz * seq, hw), gdn.reshape(bsz * seq, hw), x1, wo, row(ln2_g[layer]), row(ln2_b[layer]),
            wg2, wu2, wd2, row(ln3_g[layer]), row(ln3_b[layer]), alpha,
            shs(aqs), new_rows(aks), new_rows(avs), by_head_t(cache_attn_k[layer]), by_head_t(cache_attn_v[layer]),
            tab_s, sample_lows)
        to_lanes = lambda a: a.reshape(dbsz, dseq * a.shape[1]).T.reshape(dseq, a.shape[1], dbsz)
        on_lanes = lambda a: jnp.broadcast_to(a[..., None], a.shape + (dbsz,))
        gdn_t, s_t = _gdn_step_call(
            to_lanes(gqs), state_conv[layer].transpose(1, 2, 0), on_lanes(gdn_conv_w[layer]), to_lanes(zs),
            to_lanes(bas[:, :2 * N_HEADS]), on_lanes(jnp.stack([gdn_a_log[layer], gdn_dt_bias[layer]])),
            on_lanes(gdn_norm_w[layer]), state_gdn[layer].transpose(1, 2, 3, 0))
        gdn_s = gdn_t.reshape(dseq * hw, dbsz).T.reshape(dbsz * dseq, hw)
        s_s = s_t.transpose(3, 0, 1, 2)
        ys = post(attn_s[:, :dseq].reshape(dbsz * dseq, hw), gdn_s, x1s)
        st_s = (heads5s(aks), heads5s(avs), s_s,
                gqs.reshape(dbsz, dseq, 3 * hw)[:, dseq - (CONV_WIDTH - 1):])
        for lst, st in zip(collected, st_p + st_s):
            lst.append(st)
    outs = [jnp.stack(t, axis=0) for t in collected]
    return (yp.reshape(bsz, seq, d_model), ys.reshape(dbsz, dseq, d_model)) + tuple(outs)
```

```python
import functools
import math

import numpy as np
import jax
import jax.numpy as jnp
from jax import lax
from jax.experimental import pallas as pl
from jax.experimental.pallas import tpu as pltpu

F32 = jnp.float32
BF16 = jnp.bfloat16

HEAD_DIM = 64
N_HEADS = 8
HEAD_WIDTH = N_HEADS * HEAD_DIM
DILATED_PATTERNS = ((128, 1), (512, 4), (2048, 16))
KEYS_PER_BLOCK = 128
T5_BUCKETS = 32
T5_MAX_EXACT = 16
T5_MAX_DIST = 2048
CONV_WIDTH = 4
GDN_CHUNK = 64
INV_BASE_BLOCK = 4
LN_EPS = 1e-5
RMS_EPS = 1e-6
NEG_BIG = -1e30

LANES = 128
SUBLANES = 8
TOKEN_TILE = 256
POST_TOKEN_TILE = 512
GDN_TOKEN_BLOCK = 256
GDN_CHUNK_GROUP = 4
GDN_BATCH_BLOCK = 4
SAMPLE_T_PAD = 8
VMEM_LIMIT = 56 * 1024 * 1024


def _resident(shape):
    nd = len(shape)
    return pl.BlockSpec(shape, lambda *_: (0,) * nd, pipeline_mode=pl.Buffered(1))


def _layernorm(y, g, b):
    mu = jnp.mean(y, axis=-1, keepdims=True)
    yc = y - mu
    var = jnp.mean(yc * yc, axis=-1, keepdims=True)
    return yc * lax.rsqrt(var + LN_EPS) * g + b


def _swiglu(xb, wg_ref, wu_ref, wd_ref, h_scr):
    d_ff = wg_ref.shape[1]
    step = 2 * LANES
    assert d_ff % step == 0
    for c in range(d_ff // step):
        sl = slice(c * step, (c + 1) * step)
        gate = jnp.dot(xb, wg_ref[:, sl], preferred_element_type=F32)
        up = jnp.dot(xb, wu_ref[:, sl], preferred_element_type=F32)
        h_scr[:, sl] = (jax.nn.silu(gate) * up).astype(BF16)
    return jnp.dot(h_scr[...], wd_ref[...], preferred_element_type=F32)


PROJ_WIDTHS = (HEAD_WIDTH, HEAD_WIDTH, HEAD_WIDTH, 3 * HEAD_WIDTH, HEAD_WIDTH, LANES)
PROJ_EDGES = tuple(int(e) for e in np.cumsum((0,) + PROJ_WIDTHS))


def _first_half_step(x_ref, wg_ref, wu_ref, wd_ref, g_ref, b_ref, h_scr, alpha):
    x = x_ref[...]
    ff = _swiglu(x.astype(BF16), wg_ref, wu_ref, wd_ref, h_scr)
    return _layernorm(alpha * x + 0.5 * ff, g_ref[...], b_ref[...])


def _pre_kernel(x_ref, wg_ref, wu_ref, wd_ref, g_ref, b_ref, win_ref,
                x1_ref, aq_ref, ak_ref, av_ref, gq_ref, z_ref, ba_ref, h_scr, *, alpha):
    x1 = _first_half_step(x_ref, wg_ref, wu_ref, wd_ref, g_ref, b_ref, h_scr, alpha)
    x1_ref[...] = x1
    xb = x1.astype(BF16)
    for i, ref in enumerate((aq_ref, ak_ref, av_ref, gq_ref, z_ref, ba_ref)):
        ref[...] = jnp.dot(xb, win_ref[:, PROJ_EDGES[i]:PROJ_EDGES[i + 1]], preferred_element_type=F32)


def _gdn_gates(ba, hp_ref):
    beta = jax.nn.sigmoid(ba)
    g = -jnp.exp(hp_ref[0:1, :]) * jax.nn.softplus(ba + hp_ref[1:2, :])
    return beta, g


def _causal_conv_silu(ext_scr, x, cw_ref):
    tb = x.shape[0]
    hist = SUBLANES
    ext_scr[hist:hist + tb, :] = x
    ext = ext_scr[0:hist + tb, :]
    conv = jnp.zeros(x.shape, F32)
    for j in range(CONV_WIDTH):
        back = CONV_WIDTH - 1 - j
        rows = pltpu.roll(ext, back, axis=0)[hist:hist + tb, :] if back else x
        conv = conv + rows * cw_ref[j:j + 1, :]
    ext_scr[0:hist, :] = ext_scr[tb:tb + hist, :]
    half = 0.5 * conv
    return half + half * jnp.tanh(half)


def _pre_prompt_kernel(x_ref, wg_ref, wu_ref, wd_ref, g_ref, b_ref, win_ref, cw_ref, hp_ref, ones_ref,
                       x1_ref, aq_ref, ak_ref, av_ref, akt_ref, avt_ref, qkv_ref, z_ref, bt_ref, gg_ref, tail_ref,
                       h_scr, ext_scr, *, alpha, tiles_per_seq):
    x1 = _first_half_step(x_ref, wg_ref, wu_ref, wd_ref, g_ref, b_ref, h_scr, alpha)
    x1_ref[...] = x1
    xb = x1.astype(BF16)
    proj = lambda i: jnp.dot(xb, win_ref[:, PROJ_EDGES[i]:PROJ_EDGES[i + 1]], preferred_element_type=F32)

    @pl.when(pl.program_id(0) % tiles_per_seq == 0)
    def _():
        ext_scr[0:SUBLANES, :] = jnp.zeros((SUBLANES, ext_scr.shape[1]), F32)

    raw = proj(3)
    tail_ref[0] = raw[raw.shape[0] - SUBLANES:, :]
    act = _causal_conv_silu(ext_scr, raw, cw_ref)
    hw = HEAD_WIDTH
    head_sq = lambda x: jnp.dot((x * x).astype(BF16), ones_ref[...], preferred_element_type=F32)
    q = act[:, 0:hw]
    k = act[:, hw:2 * hw]
    qkv_ref[:, 0:hw] = q * lax.rsqrt(head_sq(q) + RMS_EPS) * HEAD_DIM ** -0.5
    qkv_ref[:, hw:2 * hw] = k * lax.rsqrt(head_sq(k) + RMS_EPS)
    qkv_ref[:, 2 * hw:3 * hw] = act[:, 2 * hw:3 * hw]
    bt_ref[...], gg_ref[...] = _gdn_gates(proj(5), hp_ref)
    aq_ref[...] = proj(0)
    for i, ref, ref_t in ((1, ak_ref, akt_ref), (2, av_ref, avt_ref)):
        kv = proj(i)
        ref[...] = kv
        ref_t[0] = kv.T
    z_ref[...] = proj(4)


def _pre_call(x2d, wg, wu, wd, g, b, win, alpha):
    m, d = x2d.shape
    d_ff = wg.shape[1]
    tm = TOKEN_TILE
    assert m % tm == 0 and sum(PROJ_WIDTHS) == win.shape[1]
    tile = lambda w: pl.BlockSpec((tm, w), lambda i: (i, 0))
    return pl.pallas_call(
        functools.partial(_pre_kernel, alpha=alpha),
        grid=(m // tm,),
        in_specs=[tile(d), _resident(wg.shape), _resident(wu.shape), _resident(wd.shape),
                  _resident(g.shape), _resident(b.shape), _resident(win.shape)],
        out_specs=[tile(d)] + [tile(w) for w in PROJ_WIDTHS],
        out_shape=[jax.ShapeDtypeStruct((m, d), F32)] + [jax.ShapeDtypeStruct((m, w), F32) for w in PROJ_WIDTHS],
        scratch_shapes=[pltpu.VMEM((tm, d_ff), BF16)],
        compiler_params=pltpu.CompilerParams(dimension_semantics=("parallel",), vmem_limit_bytes=VMEM_LIMIT),
        name="pre_ffn_proj",
    )(x2d, wg, wu, wd, g, b, win)


def _pre_prompt_call(x2d, wg, wu, wd, g, b, win, conv_w8, head_params, alpha, seq):
    m, d = x2d.shape
    d_ff = wg.shape[1]
    tm = TOKEN_TILE
    hw = HEAD_WIDTH
    assert seq % tm == 0 and m % seq == 0 and sum(PROJ_WIDTHS) == win.shape[1]
    per_seq = seq // tm
    tile = lambda w: pl.BlockSpec((tm, w), lambda i: (i, 0))
    t_spec = pl.BlockSpec((1, hw, tm), lambda i: (i // per_seq, 0, i % per_seq))
    t_shape = jax.ShapeDtypeStruct((m // seq, hw, seq), F32)
    rows = lambda w: jax.ShapeDtypeStruct((m, w), F32)
    consts = (wg, wu, wd, g, b, win, conv_w8, head_params, _head_ones())
    return pl.pallas_call(
        functools.partial(_pre_prompt_kernel, alpha=alpha, tiles_per_seq=per_seq),
        grid=(m // tm,),
        in_specs=[tile(d)] + [_resident(c.shape) for c in consts],
        out_specs=[tile(d), tile(hw), tile(hw), tile(hw), t_spec, t_spec, tile(3 * hw), tile(hw), tile(LANES),
                   tile(LANES), pl.BlockSpec((1, SUBLANES, 3 * hw), lambda i: (i // per_seq, 0, 0))],
        out_shape=[rows(d), rows(hw), rows(hw), rows(hw), t_shape, t_shape, rows(3 * hw), rows(hw), rows(LANES),
                   rows(LANES), jax.ShapeDtypeStruct((m // seq, SUBLANES, 3 * hw), F32)],
        scratch_shapes=[pltpu.VMEM((tm, d_ff), BF16), pltpu.VMEM((tm + 2 * SUBLANES, 3 * hw), F32)],
        compiler_params=pltpu.CompilerParams(dimension_semantics=("arbitrary",), vmem_limit_bytes=VMEM_LIMIT),
        name="pre_ffn_proj_gdnprep",
    )(x2d, *consts)


def _post_kernel(attn_ref, gdn_ref, x1_ref, wo_ref, g2_ref, b2_ref, wg_ref, wu_ref, wd_ref, g3_ref, b3_ref,
                 y_ref, h_scr, *, alpha):
    tm = y_ref.shape[0]
    n_split = 2 if tm >= 2 * TOKEN_TILE else 1
    halves = [pl.ds(i * (tm // n_split), tm // n_split) for i in range(n_split)]
    mix, x2, ff = {}, {}, {}
    for i, rows in enumerate(halves):
        mix[i] = (jnp.dot(attn_ref[rows, :].astype(BF16), wo_ref[0:HEAD_WIDTH, :], preferred_element_type=F32)
                  + jnp.dot(gdn_ref[rows, :].astype(BF16), wo_ref[HEAD_WIDTH:2 * HEAD_WIDTH, :],
                            preferred_element_type=F32))
    for i, rows in enumerate(halves):
        x2[i] = _layernorm(alpha * x1_ref[rows, :] + mix[i], g2_ref[...], b2_ref[...])
    for i, rows in enumerate(halves):
        ff[i] = _swiglu(x2[i].astype(BF16), wg_ref, wu_ref, wd_ref, h_scr.at[rows])
    for i, rows in enumerate(halves):
        y_ref[rows, :] = _layernorm(alpha * x2[i] + 0.5 * ff[i], g3_ref[...], b3_ref[...])


def _post_call(attn, gdn, x1, wo, g2, b2, wg, wu, wd, g3, b3, alpha):
    m, d = x1.shape
    tm = POST_TOKEN_TILE
    assert m % tm == 0
    tile = lambda w: pl.BlockSpec((tm, w), lambda i: (i, 0))
    consts = (wo, g2, b2, wg, wu, wd, g3, b3)
    return pl.pallas_call(
        functools.partial(_post_kernel, alpha=alpha),
        grid=(m // tm,),
        in_specs=[tile(HEAD_WIDTH), tile(HEAD_WIDTH), tile(d)] + [_resident(c.shape) for c in consts],
        out_specs=tile(d),
        out_shape=jax.ShapeDtypeStruct((m, d), F32),
        scratch_shapes=[pltpu.VMEM((tm, wg.shape[1]), BF16)],
        compiler_params=pltpu.CompilerParams(dimension_semantics=("parallel",), vmem_limit_bytes=VMEM_LIMIT),
        name="post_out_ffn",
    )(attn, gdn, x1, *consts)


def _t5_bucket_np(dist):
    n = np.maximum(dist, 0)
    nf = np.maximum(n, 1).astype(np.float32)
    large = T5_MAX_EXACT + (np.log(nf / np.float32(T5_MAX_EXACT)) / np.float32(math.log(T5_MAX_DIST / T5_MAX_EXACT))
                            * np.float32(T5_BUCKETS - T5_MAX_EXACT)).astype(np.int32)
    large = np.minimum(large, T5_BUCKETS - 1)
    return np.where(n < T5_MAX_EXACT, n, large).astype(np.int32)


def _t5_table_kernel(rb_ref, bk_ref, out_ref):
    for h in range(N_HEADS):
        bk = bk_ref[0, h % bk_ref.shape[1]]
        acc = jnp.full(bk.shape, NEG_BIG, F32)
        for b in range(T5_BUCKETS):
            acc = jnp.where(bk == b, rb_ref[b, h], acc)
        out_ref[0, h] = acc


def _t5_table_call(rel_bias, buckets):
    p, hb, r, c = buckets.shape
    assert hb in (1, N_HEADS)
    return pl.pallas_call(
        _t5_table_kernel,
        grid=(p,),
        in_specs=[pl.BlockSpec(memory_space=pltpu.SMEM),
                  pl.BlockSpec((1, hb, r, c), lambda i: (i, 0, 0, 0))],
        out_specs=pl.BlockSpec((1, N_HEADS, r, c), lambda i: (i, 0, 0, 0)),
        out_shape=jax.ShapeDtypeStruct((p, N_HEADS, r, c), F32),
        compiler_params=pltpu.CompilerParams(dimension_semantics=("parallel",)),
        name="t5_bias_table",
    )(rel_bias, buckets)


def _prompt_buckets():
    nb = KEYS_PER_BLOCK
    qi = np.arange(nb)[:, None]
    ki = np.arange(2 * nb)[None, :]
    dist = qi + nb - ki
    valid = (dist >= 0) & (dist <= nb)
    out = []
    for window, dil in DILATED_PATTERNS:
        assert window // dil == nb
        out.append(np.where(valid, _t5_bucket_np(dist * dil), -1))
    return np.stack(out).astype(np.int32)[:, None]


def _nt_dot(a, b):
    return lax.dot_general(a, b, (((1,), (1,)), ((), ())), preferred_element_type=F32)


def _attn_prompt_kernel(q_ref, k_ref, v_ref, tab_ref, o_ref, og_scr, lg_scr, *, seq):
    nb = KEYS_PER_BLOCK
    hd = HEAD_DIM
    n_pat = len(DILATED_PATTERNS)
    heads = LANES // hd
    for g, (window, dil) in enumerate(DILATED_PATTERNS):
        n_blocks = seq // (dil * nb)

        def blocks(starts, with_prev, g=g, dil=dil):
            lane_head = lax.broadcasted_iota(jnp.int32, (nb, LANES), 1) // hd
            n_keys = 2 * nb if with_prev else nb
            row_sl, qs, ks, vs = [], [], [], []
            for start in starts:
                first_key = start - dil * (n_keys - nb)
                row_sl.append(pl.ds(start, nb, stride=dil) if dil > 1 else pl.ds(start, nb))
                key_rows = pl.ds(first_key, n_keys, stride=dil) if dil > 1 else pl.ds(first_key, n_keys)
                qs.append(q_ref[0, row_sl[-1], :] * hd ** -0.5)
                ks.append(k_ref[0, key_rows, :].astype(BF16))
                vs.append(v_ref[0, key_rows, :].astype(BF16))
            chains = [(u, hh) for u in range(len(starts)) for hh in range(heads)]
            s, m, p, den, acc = {}, {}, {}, {}, {}

            def scores(c):
                u, hh = c
                qb = jnp.where(lane_head == hh, qs[u], 0.0).astype(BF16)
                s[c] = _nt_dot(qb, ks[u]) + tab_ref[g, hh, :, 2 * nb - n_keys:2 * nb]

            def row_max(c):
                m[c] = jnp.max(s[c], axis=-1, keepdims=True)

            def probs(c):
                p[c] = jnp.exp(s[c] - m[c])
                den[c] = jnp.sum(p[c], axis=-1, keepdims=True)

            def values(c):
                u, hh = c
                acc[c] = jnp.dot(p[c].astype(BF16), vs[u], preferred_element_type=F32)

            def finish(c):
                acc[c] = acc[c] / den[c]
                den[c] = m[c] + jnp.log(den[c])

            for stage in (scores, row_max, probs, values, finish):
                for c in chains:
                    stage(c)
            for u in range(len(starts)):
                out, lse = acc[u, 0], den[u, 0]
                for hh in range(1, heads):
                    out = jnp.where(lane_head == hh, acc[u, hh], out)
                    lse = jnp.where(lane_head == hh, den[u, hh], lse)
                og_scr[g, row_sl[u], :] = out
                lg_scr[g, row_sl[u], :] = jnp.broadcast_to(lse, (nb, LANES))

        def unroll_of(count):
            return next(u for u in (16, 12, 8, 6, 5, 4, 3, 2, 1) if count % u == 0)

        u_first = unroll_of(dil)

        def first_body(i, carry, blocks=blocks, u_first=u_first):
            blocks([i * u_first + j for j in range(u_first)], False)
            return carry

        lax.fori_loop(0, dil // u_first, first_body, 0)
        if n_blocks > 1:
            n_rest = dil * (n_blocks - 1)
            u_rest = unroll_of(n_rest)

            def rest_body(i, carry, blocks=blocks, dil=dil, n_blocks=n_blocks, u_rest=u_rest):
                starts = []
                for j in range(u_rest):
                    idx = i * u_rest + j
                    starts.append(idx // (n_blocks - 1) + dil * nb * (idx % (n_blocks - 1) + 1))
                blocks(starts, True)
                return carry

            lax.fori_loop(0, n_rest // u_rest, rest_body, 0)

    rb = 2 * nb

    def merge_body(i, carry):
        rows = pl.ds(pl.multiple_of(i * rb, rb), rb)
        lses = [lg_scr[g, rows, :] for g in range(n_pat)]
        top = functools.reduce(jnp.maximum, lses)
        ws = [jnp.exp(l - top) for l in lses]
        num = functools.reduce(lambda a, b: a + b, [w * og_scr[g, rows, :] for g, w in enumerate(ws)])
        o_ref[0, rows, :] = num / functools.reduce(lambda a, b: a + b, ws)
        return carry

    lax.fori_loop(0, seq // rb, merge_body, 0)


def _attn_prompt_call(aq, ak, av, tab):
    bsz, seq, width = aq.shape
    assert width == HEAD_WIDTH and seq % (DILATED_PATTERNS[-1][1] * KEYS_PER_BLOCK) == 0
    n_pat = len(DILATED_PATTERNS)
    heads_per_step = LANES // HEAD_DIM
    qkv_spec = pl.BlockSpec((1, seq, LANES), lambda h, b: (b, 0, h))
    return pl.pallas_call(
        functools.partial(_attn_prompt_kernel, seq=seq),
        grid=(N_HEADS // heads_per_step, bsz),
        in_specs=[qkv_spec, qkv_spec, qkv_spec,
                  pl.BlockSpec((n_pat, heads_per_step, KEYS_PER_BLOCK, 2 * KEYS_PER_BLOCK),
                               lambda h, b: (0, h, 0, 0))],
        out_specs=pl.BlockSpec((1, seq, LANES), lambda h, b: (b, 0, h)),
        out_shape=jax.ShapeDtypeStruct((bsz, seq, width), F32),
        scratch_shapes=[pltpu.VMEM((n_pat, seq, LANES), F32)] * 2,
        compiler_params=pltpu.CompilerParams(dimension_semantics=("parallel", "parallel"),
                                             vmem_limit_bytes=VMEM_LIMIT),
        name="attn_prompt",
    )(aq, ak, av, tab)


def _sample_layout(w_buf, t_new):
    assert w_buf % LANES == 0 and t_new <= SAMPLE_T_PAD
    row_of_col = np.full((w_buf + LANES,), -1, np.int64)
    row_of_col[:w_buf + t_new] = np.arange(w_buf + t_new)
    buckets = np.full((len(DILATED_PATTERNS), 1, SAMPLE_T_PAD, w_buf + LANES), -1, np.int32)
    lows = []
    for g, (window, dil) in enumerate(DILATED_PATTERNS):
        lows.append(max(0, w_buf - window) // LANES * LANES)
        for t in range(t_new):
            dist = w_buf + t - row_of_col
            ok = (row_of_col >= 0) & (dist >= 0) & (dist <= window) & (dist % dil == 0)
            assert int(ok.sum()) == min(window, w_buf + t) // dil + 1 and not ok[:lows[g]].any()
            buckets[g, 0, t] = np.where(ok, _t5_bucket_np(dist), -1)
    return tuple(lows), buckets


def _attn_sample_kernel(q_ref, kn_ref, vn_ref, kt_ref, vt_ref, tab_ref, o_ref, *, lows):
    tp = SAMPLE_T_PAD
    hd = HEAD_DIM
    w_buf = kt_ref.shape[3]
    n_pat = len(DILATED_PATTERNS)
    add = lambda x, y: x + y
    rowmax = lambda x: jnp.max(x, axis=-1, keepdims=True)
    rowsum = lambda x: jnp.sum(x, axis=-1, keepdims=True)
    heads = range(N_HEADS)
    q8 = q_ref[0] * hd ** -0.5
    s_buf = {h: jnp.dot(q8[:, hd * h:hd * (h + 1)], kt_ref[0, h], preferred_element_type=F32) for h in heads}
    s_new = {h: _nt_dot(q8[:, hd * h:hd * (h + 1)], kn_ref[0, h]) for h in heads}
    p_buf, p_new, den, lse = {}, {}, {}, {}
    for h in heads:
        for g in range(n_pat):
            lo = lows[g]
            l_buf = s_buf[h][:, lo:] + tab_ref[g, h, :, lo:w_buf]
            l_new = s_new[h] + tab_ref[g, h, :, w_buf:w_buf + tp]
            m = jnp.maximum(rowmax(l_buf), rowmax(l_new))
            p_buf[h, g] = jnp.exp(l_buf - m)
            p_new[h, g] = jnp.exp(l_new - m)
            den[h, g] = rowsum(p_buf[h, g]) + rowsum(p_new[h, g])
            lse[h, g] = m + jnp.log(den[h, g])
    edges = sorted(set(lows)) + [w_buf]
    outs = {}
    for h in heads:
        top = functools.reduce(jnp.maximum, [lse[h, g] for g in range(n_pat)])
        ws = [jnp.exp(lse[h, g] - top) for g in range(n_pat)]
        w_sum = functools.reduce(add, ws)
        coef = [ws[g] / (w_sum * den[h, g]) for g in range(n_pat)]
        acc = jnp.dot(functools.reduce(add, [coef[g] * p_new[h, g] for g in range(n_pat)]), vn_ref[0, h],
                      preferred_element_type=F32)
        for e0, e1 in zip(edges[:-1], edges[1:]):
            mix = functools.reduce(add, [coef[g] * p_buf[h, g][:, e0 - lows[g]:e1 - lows[g]]
                                         for g in range(n_pat) if lows[g] <= e0])
            acc = acc + _nt_dot(mix, vt_ref[0, h, :, e0:e1])
        outs[h] = acc
    o_ref[0] = jnp.concatenate([outs[h] for h in heads], axis=1)


N_POST_INPUTS = 11
N_ATTN_SAMPLE_INPUTS = 6


def _post_with_sample_attn_kernel(*refs, alpha, lows):
    post_in = refs[:N_POST_INPUTS]
    attn_in = refs[N_POST_INPUTS:N_POST_INPUTS + N_ATTN_SAMPLE_INPUTS]
    y_ref, o_ref, h_scr = refs[N_POST_INPUTS + N_ATTN_SAMPLE_INPUTS:]
    _post_kernel(*post_in, y_ref, h_scr, alpha=alpha)
    _attn_sample_kernel(*attn_in, o_ref, lows=lows)


def _post_with_sample_attn_call(attn, gdn, x1, wo, g2, b2, wg, wu, wd, g3, b3, alpha,
                                aq8, k_new, v_new, cache_kt, cache_vt, tab, lows):
    m, d = x1.shape
    bsz, n_heads, hd, w_buf = cache_kt.shape
    assert m % bsz == 0
    tm = m // bsz
    assert tm % SUBLANES == 0
    tile = lambda w: pl.BlockSpec((tm, w), lambda i: (i, 0))
    consts = (wo, g2, b2, wg, wu, wd, g3, b3)
    q_spec = pl.BlockSpec((1, SAMPLE_T_PAD, n_heads * hd), lambda b: (b, 0, 0))
    new_spec = pl.BlockSpec((1, n_heads, SAMPLE_T_PAD, hd), lambda b: (b, 0, 0, 0))
    buf_spec = pl.BlockSpec((1, n_heads, hd, w_buf), lambda b: (b, 0, 0, 0))
    assert len(consts) + 3 == N_POST_INPUTS
    return pl.pallas_call(
        functools.partial(_post_with_sample_attn_kernel, alpha=alpha, lows=lows),
        grid=(bsz,),
        in_specs=[tile(HEAD_WIDTH), tile(HEAD_WIDTH), tile(d)] + [_resident(c.shape) for c in consts]
                 + [q_spec, new_spec, new_spec, buf_spec, buf_spec, _resident(tab.shape)],
        out_specs=[tile(d), q_spec],
        out_shape=[jax.ShapeDtypeStruct((m, d), F32),
                   jax.ShapeDtypeStruct((bsz, SAMPLE_T_PAD, n_heads * hd), F32)],
        scratch_shapes=[pltpu.VMEM((tm, wg.shape[1]), BF16)],
        compiler_params=pltpu.CompilerParams(dimension_semantics=("parallel",), vmem_limit_bytes=VMEM_LIMIT),
        name="post_with_sample_attn",
    )(attn, gdn, x1, *consts, aq8, k_new, v_new, cache_kt, cache_vt, tab)


def _aligned(index, size):
    return index * size if isinstance(index, int) else pl.multiple_of(index * size, size)


def _split3(x):
    h1 = x.astype(BF16)
    r1 = x - h1.astype(F32)
    h2 = r1.astype(BF16)
    h3 = (r1 - h2.astype(F32)).astype(BF16)
    return h1, h2, h3


def _head_sums(x, ones_ref):
    h1, h2, _ = _split3(x)
    return (jnp.dot(h1, ones_ref[...], preferred_element_type=F32)
            + jnp.dot(h2, ones_ref[...], preferred_element_type=F32))


def _gdn_pairs_kernel(qkv_ref, z_ref, bt_scr, g_scr, s0_ref, nw_ref, ones_ref, o_ref, sfin_ref,
                      oc_scr, s_scr, u_scr, w_scr, in_scr, qd_scr, kd_scr, el_scr, *, group):
    chunk = HEAD_DIM
    k_lane0, v_lane0 = HEAD_WIDTH, 2 * HEAD_WIDTH
    n_bb, tb = bt_scr.shape[0], bt_scr.shape[1]
    n_chunks = tb // chunk
    n_pairs = N_HEADS // 2
    hd = HEAD_DIM
    t = pl.program_id(1)
    dot = lambda a, b: jnp.dot(a.astype(BF16), b.astype(BF16), preferred_element_type=F32)

    lane = lax.broadcasted_iota(jnp.int32, (chunk, LANES), 1)
    row = lax.broadcasted_iota(jnp.int32, (chunk, LANES), 0)
    second = lane >= hd
    col = lane - jnp.where(second, hd, 0)
    tril = row >= col
    strict = row > col
    eye2 = (row == col).astype(F32)
    tril_b = (lax.broadcasted_iota(jnp.int32, (chunk, chunk), 0)
              >= lax.broadcasted_iota(jnp.int32, (chunk, chunk), 1)).astype(BF16)
    base_blocks = (row // INV_BASE_BLOCK) == (col // INV_BASE_BLOCK)
    level_blocks = []
    m = INV_BASE_BLOCK
    while m < chunk:
        level_blocks.append(((row // m) % 2 == 1) & ((col // m) == (row // m) - 1))
        m *= 2

    def blockdiag(x):
        xb = x.astype(BF16)
        zero = jnp.zeros_like(xb)
        return jnp.concatenate([jnp.where(second, zero, xb), jnp.where(second, xb, zero)], axis=0)

    def per_head(cols, j):
        return jnp.where(second, cols[:, 2 * j + 1:2 * j + 2], cols[:, 2 * j:2 * j + 1])

    @pl.when(t == 0)
    def _():
        for bb in range(n_bb):
            for j in range(n_pairs):
                s_scr[bb, j] = jnp.concatenate([s0_ref[bb, 2 * j], s0_ref[bb, 2 * j + 1]], axis=1)

    def state_free_part(gi, carry):
        probs, shared = [], {}
        for bb in range(n_bb):
            for cg in range(group):
                c_idx = gi * group + cg
                rows = pl.ds(_aligned(c_idx, chunk), chunk)
                gc = functools.reduce(lambda a, b: a + b,
                                      [dot(tril_b, p) for p in _split3(g_scr[bb, rows, :])])
                shared[bb, cg] = (rows, c_idx, gc[:, N_HEADS:2 * N_HEADS], gc.T[N_HEADS:2 * N_HEADS, :],
                                  bt_scr[bb, rows, :])
                probs += [(bb, cg, j) for j in range(n_pairs)]
        kk, qk, decay, a, t_inv, pw, vb, kbg = {}, {}, {}, {}, {}, {}, {}, {}
        for p in probs:
            bb, cg, j = p
            rows, c_idx, gc, gct, bt = shared[bb, cg]
            pl_ = pl.ds(LANES * j, LANES)
            q2 = qkv_ref[bb, rows, pl_]
            k2 = qkv_ref[bb, rows, pl.ds(k_lane0 + LANES * j, LANES)]
            beta2 = per_head(bt, j)
            gcc = per_head(gc, j)
            gcr = jnp.concatenate([gct[2 * j:2 * j + 1, :], gct[2 * j + 1:2 * j + 2, :]], axis=1)
            g_last = gcc[chunk - 1:chunk, :]
            e_gc = jnp.exp(gcc)
            kb2 = k2 * beta2
            k_bd = blockdiag(k2)
            both = _nt_dot(jnp.concatenate([kb2, q2], axis=0).astype(BF16), k_bd)
            kk[p] = both[0:chunk]
            qk[p] = both[chunk:2 * chunk]
            vb[p] = qkv_ref[bb, rows, pl.ds(v_lane0 + LANES * j, LANES)] * beta2
            kbg[p] = kb2 * e_gc
            qd_scr[bb, rows, pl_] = q2 * e_gc
            kd_scr[bb, rows, pl_] = k2 * jnp.exp(g_last - gcc)
            el_scr[bb, pl.ds(_aligned(c_idx, SUBLANES), SUBLANES), pl_] = jnp.broadcast_to(
                jnp.exp(g_last), (SUBLANES, LANES))
            decay[p] = jnp.where(tril, jnp.exp(jnp.where(tril, gcc - gcr, 0.0)), 0.0)
        for p in probs:
            bb, cg, j = p
            rows = shared[bb, cg][0]
            a[p] = jnp.where(strict, kk[p] * decay[p], 0.0)
            in_scr[bb, rows, pl.ds(LANES * j, LANES)] = jnp.where(tril, qk[p] * decay[p], 0.0)
        for p in probs:
            d = jnp.where(base_blocks, a[p], 0.0)
            pw[p] = d
            t_inv[p] = eye2 - d
        for p in probs:
            pw[p] = dot(pw[p], blockdiag(pw[p]))
        for p in probs:
            t_inv[p] = dot(t_inv[p], blockdiag(eye2 + pw[p]))
        for lower_left in level_blocks:
            for p in probs:
                pw[p] = dot(t_inv[p], blockdiag(jnp.where(lower_left, a[p], 0.0)))
            for p in probs:
                t_inv[p] = t_inv[p] - dot(pw[p], blockdiag(t_inv[p]))
        for p in probs:
            bb, cg, j = p
            rows = shared[bb, cg][0]
            uw = dot(t_inv[p], jnp.concatenate([blockdiag(vb[p]), blockdiag(kbg[p])], axis=1))
            u_scr[bb, rows, pl.ds(LANES * j, LANES)] = uw[:, 0:LANES]
            w_scr[bb, rows, pl.ds(LANES * j, LANES)] = uw[:, LANES:2 * LANES]
        return carry

    def state_part(ci_, carry):
        rows = pl.ds(_aligned(ci_, chunk), chunk)
        el_rows = pl.ds(_aligned(ci_, SUBLANES), SUBLANES)
        probs = [(bb, j) for bb in range(n_bb) for j in range(n_pairs)]
        s_old, both, v_new, o_in, s_add = {}, {}, {}, {}, {}
        for p in probs:
            bb, j = p
            pl_ = pl.ds(LANES * j, LANES)
            s_old[p] = s_scr[bb, j]
            lhs = jnp.concatenate([w_scr[bb, rows, pl_], qd_scr[bb, rows, pl_]], axis=0)
            both[p] = dot(lhs, blockdiag(s_old[p]))
        for p in probs:
            bb, j = p
            pl_ = pl.ds(LANES * j, LANES)
            v_new[p] = u_scr[bb, rows, pl_] - both[p][0:chunk]
            o_in[p] = dot(in_scr[bb, rows, pl_], blockdiag(v_new[p]))
            cross = lax.dot_general(kd_scr[bb, rows, pl_], v_new[p], (((0,), (0,)), ((), ())),
                                    preferred_element_type=F32)
            s_add[p] = jnp.where(second, cross[hd:2 * hd], cross[0:hd])
        for p in probs:
            bb, j = p
            pl_ = pl.ds(LANES * j, LANES)
            oc_scr[bb, rows, pl_] = both[p][chunk:2 * chunk] + o_in[p]
            s_scr[bb, j] = s_old[p] * el_scr[bb, el_rows, pl_][0:1, :] + s_add[p]
        return carry

    assert n_chunks % group == 0
    if n_chunks == group:
        state_free_part(0, 0)
    else:
        lax.fori_loop(0, n_chunks // group, state_free_part, 0)
    if n_chunks == 1:
        state_part(0, 0)
    else:
        lax.fori_loop(0, n_chunks, state_part, 0)

    for bb in range(n_bb):
        o = oc_scr[bb]
        o = o * lax.rsqrt(_head_sums(o * o, ones_ref) * (1.0 / hd) + RMS_EPS) * nw_ref[...]
        o_ref[bb] = o * jax.nn.silu(z_ref[bb])

    @pl.when(t == pl.num_programs(1) - 1)
    def _():
        for bb in range(n_bb):
            for j in range(n_pairs):
                sfin_ref[bb, 2 * j] = s_scr[bb, j][:, 0:hd]
                sfin_ref[bb, 2 * j + 1] = s_scr[bb, j][:, hd:2 * hd]


def _head_ones():
    ones_bd = np.kron(np.eye(N_HEADS, dtype=np.float32), np.ones((HEAD_DIM, HEAD_DIM), np.float32))
    return jnp.asarray(ones_bd, BF16)


def _gdn_pairs_call(qkv, z, beta, g, s0, norm_w_row, token_block, group):
    bsz, seq, w3 = qkv.shape
    hw = HEAD_WIDTH
    chunk = HEAD_DIM
    nbb = GDN_BATCH_BLOCK
    assert w3 == 3 * hw and seq % token_block == 0 and token_block % (chunk * group) == 0 and bsz % nbb == 0
    tb = token_block
    ones_bd = _head_ones()
    blk = lambda w: pl.BlockSpec((nbb, tb, w), lambda b, t: (b, t, 0))
    per_b3 = lambda s: pl.BlockSpec((nbb,) + s, lambda b, t: (b,) + (0,) * len(s))
    return pl.pallas_call(
        functools.partial(_gdn_pairs_kernel, group=group),
        grid=(bsz // nbb, seq // tb),
        in_specs=[blk(w3), blk(hw), blk(LANES), blk(LANES), per_b3((N_HEADS, HEAD_DIM, HEAD_DIM)),
                  _resident(norm_w_row.shape), _resident(ones_bd.shape)],
        out_specs=[blk(hw), per_b3((N_HEADS, HEAD_DIM, HEAD_DIM))],
        out_shape=[jax.ShapeDtypeStruct((bsz, seq, hw), F32),
                   jax.ShapeDtypeStruct((bsz, N_HEADS, HEAD_DIM, HEAD_DIM), F32)],
        scratch_shapes=[pltpu.VMEM((nbb, tb, hw), F32)]
                       + [pltpu.VMEM((nbb, N_HEADS // 2, HEAD_DIM, LANES), F32)]
                       + [pltpu.VMEM((nbb, tb, hw), F32)] * 5
                       + [pltpu.VMEM((nbb, (tb // chunk) * SUBLANES, hw), F32)],
        compiler_params=pltpu.CompilerParams(dimension_semantics=("parallel", "arbitrary"),
                                             vmem_limit_bytes=VMEM_LIMIT),
        name="gdn_pairs",
    )(qkv, z, beta, g, s0, norm_w_row, ones_bd)


def _gdn_step_kernel(xq_ref, xk_ref, xv_ref, cq_ref, ck_ref, cv_ref, wq_ref, wk_ref, wv_ref, z_ref, ba_ref, hp_ref,
                     nw_ref, s0_ref, o_ref, s_ref, k_scr, q_scr):
    n_tok, hd, _ = xq_ref.shape
    h = pl.program_id(0)
    col_sum = lambda x: jnp.sum(x, axis=0, keepdims=True)

    def conv_act(x_ref, c_ref, w_ref, t):
        planes = [c_ref[i] for i in range(CONV_WIDTH - 1)] + [x_ref[i] for i in range(t + 1)]
        taps = planes[t:t + CONV_WIDTH]
        acc = jnp.zeros(taps[0].shape, F32)
        for j in range(CONV_WIDTH):
            acc = acc + taps[j] * w_ref[j]
        return jax.nn.silu(acc)

    s_ref[...] = s0_ref[...]
    neg_rate = -jnp.exp(hp_ref[0, pl.ds(h, 1), :])
    dt_bias = hp_ref[1, pl.ds(h, 1), :]
    for t in range(n_tok):
        q = conv_act(xq_ref, cq_ref, wq_ref, t)
        k = conv_act(xk_ref, ck_ref, wk_ref, t)
        v = conv_act(xv_ref, cv_ref, wv_ref, t)
        q_scr[...] = q * lax.rsqrt(col_sum(q * q) + RMS_EPS) * hd ** -0.5
        k_scr[...] = k * lax.rsqrt(col_sum(k * k) + RMS_EPS)
        beta = jax.nn.sigmoid(ba_ref[t, pl.ds(h, 1), :])
        decay = jnp.exp(neg_rate * jax.nn.softplus(ba_ref[t, pl.ds(N_HEADS + h, 1), :] + dt_bias))

        def k_dot_state(d, acc):
            return acc + k_scr[pl.ds(d, 1), :] * s_ref[0, d]

        ks = lax.fori_loop(0, hd, k_dot_state, jnp.zeros((hd, xq_ref.shape[2]), F32), unroll=8)
        delta = beta * (v - decay * ks)

        def update(d, acc):
            s_new = decay * s_ref[0, d] + k_scr[pl.ds(d, 1), :] * delta
            s_ref[0, d] = s_new
            return acc + q_scr[pl.ds(d, 1), :] * s_new

        o = lax.fori_loop(0, hd, update, jnp.zeros((hd, xq_ref.shape[2]), F32), unroll=8)
        o = o * lax.rsqrt(col_sum(o * o) * (1.0 / hd) + RMS_EPS) * nw_ref[...]
        o_ref[t] = o * jax.nn.silu(z_ref[t])


def _gdn_step_call(x_t, conv_t, conv_w_t, z_t, ba_t, hp_t, nw_t, s0_t):
    n_tok, w3, bsz = x_t.shape
    hd = HEAD_DIM
    part = lambda rows, p: pl.BlockSpec((rows, hd, bsz), lambda h, p=p: (0, p * N_HEADS + h, 0))
    whole = lambda a: pl.BlockSpec(a.shape, lambda h: (0,) * a.ndim)
    state_spec = pl.BlockSpec((1, hd, hd, bsz), lambda h: (h, 0, 0, 0))
    return pl.pallas_call(
        _gdn_step_kernel,
        grid=(N_HEADS,),
        in_specs=[part(n_tok, 0), part(n_tok, 1), part(n_tok, 2),
                  part(CONV_WIDTH - 1, 0), part(CONV_WIDTH - 1, 1), part(CONV_WIDTH - 1, 2),
                  part(CONV_WIDTH, 0), part(CONV_WIDTH, 1), part(CONV_WIDTH, 2),
                  part(n_tok, 0), whole(ba_t), whole(hp_t), whole(nw_t), state_spec],
        out_specs=[part(n_tok, 0), state_spec],
        out_shape=[jax.ShapeDtypeStruct((n_tok, N_HEADS * hd, bsz), F32),
                   jax.ShapeDtypeStruct(s0_t.shape, F32)],
        scratch_shapes=[pltpu.VMEM((hd, bsz), F32)] * 2,
        compiler_params=pltpu.CompilerParams(dimension_semantics=("parallel",), vmem_limit_bytes=VMEM_LIMIT),
        name="gdn_step",
    )(x_t, x_t, x_t, conv_t, conv_t, conv_t, conv_w_t, conv_w_t, conv_w_t, z_t, ba_t, hp_t, nw_t, s0_t)


def _pad_rows(x, rows):
    cfg = [(0, 0)] * x.ndim
    cfg[1] = (0, rows - x.shape[1])
    return jnp.pad(x, cfg)


def kernel(x_prompt, x_sample, cache_attn_k, cache_attn_v, state_gdn, state_conv, rel_bias, ln1_g, ln1_b,
           ffn1_w_gate, ffn1_w_up, ffn1_w_down, w_in, w_out, gdn_conv_w, gdn_a_log, gdn_dt_bias, gdn_norm_w,
           ln2_g, ln2_b, ffn2_w_gate, ffn2_w_up, ffn2_w_down, ln3_g, ln3_b):
    depth = w_in.shape[0]
    alpha = (2.0 * depth) ** 0.25
    bsz, seq, d_model = x_prompt.shape
    dbsz, dseq, _ = x_sample.shape
    w_buf = cache_attn_k.shape[2]
    hw = HEAD_WIDTH
    in_cols = w_in.shape[2]
    assert in_cols == 7 * hw + 2 * N_HEADS and CONV_WIDTH - 1 <= dseq <= SAMPLE_T_PAD
    assert seq % GDN_TOKEN_BLOCK == 0 and GDN_TOKEN_BLOCK % (GDN_CHUNK * GDN_CHUNK_GROUP) == 0

    sample_lows, sample_buckets = _sample_layout(w_buf, dseq)
    tab_p = _t5_table_call(rel_bias, jnp.asarray(_prompt_buckets()))
    tab_s = _t5_table_call(rel_bias, jnp.asarray(sample_buckets))

    yp = x_prompt.reshape(bsz * seq, d_model)
    ys = x_sample.reshape(dbsz * dseq, d_model)
    collected = [[] for _ in range(8)]
    row = lambda v: v.reshape(1, -1)
    for layer in range(depth):
        wg1, wu1, wd1 = (w[layer].astype(BF16) for w in (ffn1_w_gate, ffn1_w_up, ffn1_w_down))
        wg2, wu2, wd2 = (w[layer].astype(BF16) for w in (ffn2_w_gate, ffn2_w_up, ffn2_w_down))
        win = jnp.pad(w_in[layer], ((0, 0), (0, 7 * hw + LANES - in_cols))).astype(BF16)
        wo = w_out[layer].astype(BF16)
        conv_w8 = jnp.pad(gdn_conv_w[layer], ((0, SUBLANES - CONV_WIDTH), (0, 0)))
        head_params = jnp.zeros((SUBLANES, LANES), F32)
        head_params = head_params.at[0, N_HEADS:2 * N_HEADS].set(gdn_a_log[layer])
        head_params = head_params.at[1, N_HEADS:2 * N_HEADS].set(gdn_dt_bias[layer])
        norm_w_row = jnp.tile(gdn_norm_w[layer], N_HEADS).reshape(1, hw)
        pre = functools.partial(_pre_call, wg=wg1, wu=wu1, wd=wd1, g=row(ln1_g[layer]), b=row(ln1_b[layer]),
                                win=win, alpha=alpha)
        post = functools.partial(_post_call, wo=wo, g2=row(ln2_g[layer]), b2=row(ln2_b[layer]), wg=wg2, wu=wu2,
                                 wd=wd2, g3=row(ln3_g[layer]), b3=row(ln3_b[layer]), alpha=alpha)

        x1, aq, ak, av, ak_t, av_t, gdn_qkv, z, beta, gate, gq_tail = _pre_prompt_call(
            yp, wg1, wu1, wd1, row(ln1_g[layer]), row(ln1_b[layer]), win, conv_w8, head_params, alpha, seq)
        shp = lambda a: a.reshape(bsz, seq, a.shape[1])
        attn = _attn_prompt_call(shp(aq), shp(ak), shp(av), tab_p)
        gdn, s_p = _gdn_pairs_call(shp(gdn_qkv), shp(z), shp(beta), shp(gate),
                                   jnp.zeros((bsz, N_HEADS, HEAD_DIM, HEAD_DIM), F32), norm_w_row,
                                   token_block=GDN_TOKEN_BLOCK, group=GDN_CHUNK_GROUP)
        wp = min(w_buf, seq)
        heads5 = lambda a_t: a_t.reshape(bsz, N_HEADS, HEAD_DIM, seq).transpose(0, 3, 1, 2)[:, seq - wp:]
        st_p = (heads5(ak_t), heads5(av_t), s_p, gq_tail[:, SUBLANES - (CONV_WIDTH - 1):])

        x1s, aqs, aks, avs, gqs, zs, bas = pre(ys)
        shs = lambda a: _pad_rows(a.reshape(dbsz, dseq, a.shape[1]), SAMPLE_T_PAD)
        heads5s = lambda a: a.reshape(dbsz, dseq, N_HEADS, HEAD_DIM)
        new_rows = lambda a: jnp.pad(heads5s(a).transpose(0, 2, 1, 3),
                                     ((0, 0), (0, 0), (0, SAMPLE_T_PAD - dseq), (0, 0)))
        by_head_t = lambda c: c.transpose(0, 2, 3, 1)
        yp, attn_s = _post_with_sample_attn_call(
            attn.reshape(bsz * seq, hw), gdn.reshape(bsz * seq, hw), x1, wo, row(ln2_g[layer]), row(ln2_b[layer]),
            wg2, wu2, wd2, row(ln3_g[layer]), row(ln3_b[layer]), alpha,
            shs(aqs), new_rows(aks), new_rows(avs), by_head_t(cache_attn_k[layer]), by_head_t(cache_attn_v[layer]),
            tab_s, sample_lows)
        to_lanes = lambda a: a.reshape(dbsz, dseq * a.shape[1]).T.reshape(dseq, a.shape[1], dbsz)
        on_lanes = lambda a: jnp.broadcast_to(a[..., None], a.shape + (dbsz,))
        gdn_t, s_t = _gdn_step_call(
            to_lanes(gqs), state_conv[layer].transpose(1, 2, 0), on_lanes(gdn_conv_w[layer]), to_lanes(zs),
            to_lanes(bas[:, :2 * N_HEADS]), on_lanes(jnp.stack([gdn_a_log[layer], gdn_dt_bias[layer]])),
            on_lanes(gdn_norm_w[layer]), state_gdn[layer].transpose(1, 2, 3, 0))
        gdn_s = gdn_t.reshape(dseq * hw, dbsz).T.reshape(dbsz * dseq, hw)
        s_s = s_t.transpose(3, 0, 1, 2)
        ys = post(attn_s[:, :dseq].reshape(dbsz * dseq, hw), gdn_s, x1s)
        st_s = (heads5s(aks), heads5s(avs), s_s,
                gqs.reshape(dbsz, dseq, 3 * hw)[:, dseq - (CONV_WIDTH - 1):])
        for lst, st in zip(collected, st_p + st_s):
            lst.append(st)
    outs = [jnp.stack(t, axis=0) for t in collected]
    return (yp.reshape(bsz, seq, d_model), ys.reshape(dbsz, dseq, d_model)) + tuple(outs)
```
